```python
import jax, jax.numpy as jnp
from jax import lax
import numpy as np

D_MODEL = 1024
BATCH = 8
SEQ = 2048
DEPTH = 4

N_A_LAYERS = DEPTH // 2
N_B_LAYERS = DEPTH - N_A_LAYERS

HG_HEADS = 8
HG_DK = 128
HG_DV = D_MODEL // HG_HEADS
HG_KEY_WIDTH = HG_HEADS * HG_DK
HG_VAL_WIDTH = HG_HEADS * HG_DV
HG_CHUNK = 64

SWA_Q_HEADS = 16
SWA_KV_HEADS = 4
SWA_GROUP = SWA_Q_HEADS // SWA_KV_HEADS
SWA_HEAD_DIM = 64
SWA_WIDTH = SWA_Q_HEADS * SWA_HEAD_DIM
SWA_KV_WIDTH = SWA_KV_HEADS * SWA_HEAD_DIM
WINDOW = 128
SWA_BLOCK = WINDOW

MEM_TOKENS = 256
MEM_HEADS = 4
MEM_HEAD_DIM = 128
MEM_WIDTH = MEM_HEADS * MEM_HEAD_DIM

ROPE_THETA = 10000.0
NORM_EPS = 1e-6

A_IN_SPLITS = (HG_KEY_WIDTH, HG_KEY_WIDTH, HG_VAL_WIDTH, HG_VAL_WIDTH, MEM_WIDTH, MEM_WIDTH)
B_IN_SPLITS = (SWA_WIDTH, SWA_WIDTH, MEM_WIDTH, MEM_WIDTH)
A_IN_WIDTH = sum(A_IN_SPLITS)
B_IN_WIDTH = sum(B_IN_SPLITS)
A_OUT_WIDTH = HG_VAL_WIDTH + MEM_WIDTH
B_OUT_WIDTH = SWA_WIDTH + MEM_WIDTH

kernel_name = "yoco_hgrn2_swa_sink_memory_trunk"


def rms_norm(x, g):
    xf = x.astype(jnp.float32)
    y = xf * lax.rsqrt(jnp.mean(xf * xf, axis=-1, keepdims=True) + NORM_EPS)
    return (y * g.astype(jnp.float32)).astype(x.dtype)


def split_cols(z, sizes):
    return jnp.split(z, list(np.cumsum(sizes)[:-1]), axis=-1)


def rope_tables(positions, dim):
    inv_freq = ROPE_THETA ** (-jnp.arange(0, dim, 2, dtype=jnp.float32) / dim)
    ang = positions.astype(jnp.float32)[..., None] * inv_freq
    return jnp.cos(ang), jnp.sin(ang)


def apply_rope(x, cos, sin):
    xf = x.astype(jnp.float32)
    x1, x2 = jnp.split(xf, 2, axis=-1)
    c = cos[:, :, None, :]
    s = sin[:, :, None, :]
    return jnp.concatenate([x1 * c - x2 * s, x2 * c + x1 * s], axis=-1).astype(x.dtype)


def memory_attention(mq, mem_n, w_mem_kv):
    b, s, _ = mq.shape
    q = mq.reshape(b, s, MEM_HEADS, MEM_HEAD_DIM)
    kv = jnp.einsum('bmd,de->bme', mem_n, w_mem_kv)
    k, v = jnp.split(kv, 2, axis=-1)
    k = k.reshape(b, -1, MEM_HEADS, MEM_HEAD_DIM)
    v = v.reshape(b, -1, MEM_HEADS, MEM_HEAD_DIM)
    scores = jnp.einsum('bshd,bmhd->bhsm', q, k).astype(jnp.float32) * (MEM_HEAD_DIM ** -0.5)
    p = jax.nn.softmax(scores, axis=-1)
    o = jnp.einsum('bhsm,bmhd->bshd', p.astype(v.dtype), v)
    return o.reshape(b, s, MEM_WIDTH)


def hgrn2_mixer(q_pre, f_pre, i_in, lb, out_norm_g):
    b, s, _ = q_pre.shape
    n_chunks = s // HG_CHUNK
    lbf = lb.astype(jnp.float32).reshape(HG_HEADS, HG_DK)
    fp = f_pre.astype(jnp.float32).reshape(b, s, HG_HEADS, HG_DK)
    log_f = jnp.logaddexp(jnp.log(lbf), jnp.log1p(-lbf) + jax.nn.log_sigmoid(fp))
    k = (1.0 - lbf) * jax.nn.sigmoid(-fp)
    q = jax.nn.silu(q_pre.astype(jnp.float32)).reshape(b, s, HG_HEADS, HG_DK)
    v = i_in.astype(jnp.float32).reshape(b, s, HG_HEADS, HG_DV)

    def to_chunks(t):
        return t.reshape(b, n_chunks, HG_CHUNK, HG_HEADS, -1).transpose(1, 0, 3, 2, 4)

    causal = jnp.tril(jnp.ones((HG_CHUNK, HG_CHUNK), dtype=bool))

    def step(state, xs):
        qc, kc, vc, lfc = xs
        cum = jnp.cumsum(lfc, axis=2)
        inter = jnp.einsum('bhtk,bhkv->bhtv', qc * jnp.exp(cum), state)
        diff = cum[:, :, :, None, :] - cum[:, :, None, :, :]
        decay = jnp.exp(jnp.where(causal[:, :, None], diff, -jnp.inf))
        scores = jnp.einsum('bhtsk,bhsk->bhts', qc[:, :, :, None, :] * decay, kc)
        intra = jnp.einsum('bhts,bhsv->bhtv', scores, vc)
        last = cum[:, :, -1, :]
        k_dec = kc * jnp.exp(last[:, :, None, :] - cum)
        state = jnp.exp(last)[..., None] * state + jnp.einsum('bhsk,bhsv->bhkv', k_dec, vc)
        return state, inter + intra

    state0 = jnp.zeros((b, HG_HEADS, HG_DK, HG_DV), jnp.float32)
    _, o = lax.scan(step, state0, (to_chunks(q), to_chunks(k), to_chunks(v), to_chunks(log_f)))
    o = o.transpose(1, 0, 3, 2, 4).reshape(b, s, HG_HEADS, HG_DV)
    o = o * lax.rsqrt(jnp.mean(o * o, axis=-1, keepdims=True) + NORM_EPS)
    o = o * out_norm_g.astype(jnp.float32).reshape(HG_HEADS, HG_DV)
    return o.reshape(b, s, HG_VAL_WIDTH).astype(q_pre.dtype)


def sliding_window_sink_attention(q, k, v, sinks):
    b, s = q.shape[:2]
    nb = s // SWA_BLOCK
    qb = q.reshape(b, nb, SWA_BLOCK, SWA_KV_HEADS, SWA_GROUP, SWA_HEAD_DIM)
    kb = k.reshape(b, nb, SWA_BLOCK, SWA_KV_HEADS, SWA_HEAD_DIM)
    vb = v.reshape(b, nb, SWA_BLOCK, SWA_KV_HEADS, SWA_HEAD_DIM)
    pad_k = jnp.zeros_like(kb[:, :1])
    pad_v = jnp.zeros_like(vb[:, :1])
    k2 = jnp.concatenate([jnp.concatenate([pad_k, kb[:, :-1]], axis=1), kb], axis=2)
    v2 = jnp.concatenate([jnp.concatenate([pad_v, vb[:, :-1]], axis=1), vb], axis=2)
    scores = jnp.einsum('bnqhgd,bnkhd->bnhgqk', qb, k2).astype(jnp.float32) * (SWA_HEAD_DIM ** -0.5)
    blk = jnp.arange(nb)[:, None, None]
    qpos = blk * SWA_BLOCK + jnp.arange(SWA_BLOCK)[None, :, None]
    kpos = (blk - 1) * SWA_BLOCK + jnp.arange(2 * SWA_BLOCK)[None, None, :]
    dist = qpos - kpos
    mask = (dist >= 0) & (dist < WINDOW) & (kpos >= 0)
    scores = jnp.where(mask[None, :, None, None], scores, -jnp.inf)
    sink = sinks.astype(jnp.float32).reshape(SWA_KV_HEADS, SWA_GROUP)[None, None, :, :, None, None]
    m = jnp.maximum(jnp.max(scores, axis=-1, keepdims=True), sink)
    p = jnp.exp(scores - m)
    p = p / (jnp.sum(p, axis=-1, keepdims=True) + jnp.exp(sink - m))
    o = jnp.einsum('bnhgqk,bnkhd->bnqhgd', p.astype(v2.dtype), v2)
    return o.reshape(b, s, SWA_WIDTH)


def shared_kv(h, kv_norm_g, w_kv_shared, cos, sin):
    b, s, _ = h.shape
    kv = jnp.einsum('bsd,de->bse', rms_norm(h, kv_norm_g), w_kv_shared)
    k, v = jnp.split(kv, 2, axis=-1)
    k = apply_rope(k.reshape(b, s, SWA_KV_HEADS, SWA_HEAD_DIM), cos, sin)
    v = v.reshape(b, s, SWA_KV_HEADS, SWA_HEAD_DIM)
    return k, v


def setup_inputs(seed: int = 0) -> dict:
    key = jax.random.key(seed)
    ks = jax.random.split(key, 20)

    def nrm(k, shape, scale):
        return jax.random.normal(k, shape, jnp.float32) * scale

    x = nrm(ks[0], (BATCH, SEQ, D_MODEL), 1.0)
    mem = nrm(ks[1], (BATCH, MEM_TOKENS, D_MODEL), 1.0)
    offsets = jax.random.randint(ks[2], (BATCH, 1), 0, 4096, dtype=jnp.int32)
    positions = offsets + jnp.arange(SEQ, dtype=jnp.int32)[None, :]
    return {
        "x": x,
        "mem": mem,
        "positions": positions,
        "pre_norm_g": 1.0 + nrm(ks[3], (DEPTH, D_MODEL), 0.02),
        "post_norm_g": 1.0 + nrm(ks[4], (DEPTH, D_MODEL), 0.02),
        "mem_norm_g": 1.0 + nrm(ks[5], (DEPTH, D_MODEL), 0.02),
        "w_mem_kv": nrm(ks[6], (DEPTH, D_MODEL, 2 * MEM_WIDTH), D_MODEL ** -0.5),
        "a_w_in": nrm(ks[7], (N_A_LAYERS, D_MODEL, A_IN_WIDTH), D_MODEL ** -0.5),
        "a_lb_logits": nrm(ks[8], (N_A_LAYERS + 1, HG_KEY_WIDTH), 0.5),
        "a_out_norm_g": 1.0 + nrm(ks[9], (N_A_LAYERS, HG_VAL_WIDTH), 0.02),
        "a_w_out": nrm(ks[10], (N_A_LAYERS, A_OUT_WIDTH, D_MODEL), A_OUT_WIDTH ** -0.5),
        "kv_norm_g": 1.0 + nrm(ks[11], (D_MODEL,), 0.02),
        "w_kv_shared": nrm(ks[12], (D_MODEL, 2 * SWA_KV_WIDTH), D_MODEL ** -0.5),
        "b_w_in": nrm(ks[13], (N_B_LAYERS, D_MODEL, B_IN_WIDTH), D_MODEL ** -0.5),
        "b_sinks": nrm(ks[14], (N_B_LAYERS, SWA_Q_HEADS), 0.5),
        "b_w_out": nrm(ks[15], (N_B_LAYERS, B_OUT_WIDTH, D_MODEL), B_OUT_WIDTH ** -0.5),
    }


def reference(x, mem, positions, pre_norm_g, post_norm_g, mem_norm_g, w_mem_kv,
              a_w_in, a_lb_logits, a_out_norm_g, a_w_out,
              kv_norm_g, w_kv_shared, b_w_in, b_sinks, b_w_out):
    b, s, _ = x.shape
    cos, sin = rope_tables(positions, SWA_HEAD_DIM)
    lb_all = jnp.cumsum(jax.nn.softmax(a_lb_logits.astype(jnp.float32), axis=0), axis=0)
    h = x
    k_sh = None
    v_sh = None
    for layer in range(DEPTH):
        xn = rms_norm(h, pre_norm_g[layer])
        mem_n = rms_norm(mem, mem_norm_g[layer])
        if layer < N_A_LAYERS:
            z = jnp.einsum('bsd,de->bse', xn, a_w_in[layer])
            q_pre, f_pre, i_in, g_main, mq, g_mem = split_cols(z, A_IN_SPLITS)
            o_main = hgrn2_mixer(q_pre, f_pre, i_in, lb_all[layer], a_out_norm_g[layer])
            o_main = o_main * jax.nn.silu(g_main)
            o_mem = memory_attention(mq, mem_n, w_mem_kv[layer]) * jax.nn.silu(g_mem)
            y = jnp.einsum('bse,ed->bsd', jnp.concatenate([o_main, o_mem], axis=-1), a_w_out[layer])
        else:
            if layer == N_A_LAYERS:
                k_sh, v_sh = shared_kv(h, kv_norm_g, w_kv_shared, cos, sin)
            j = layer - N_A_LAYERS
            z = jnp.einsum('bsd,de->bse', xn, b_w_in[j])
            q, g_main, mq, g_mem = split_cols(z, B_IN_SPLITS)
            q = apply_rope(q.reshape(b, s, SWA_Q_HEADS, SWA_HEAD_DIM), cos, sin)
            o_main = sliding_window_sink_attention(q, k_sh, v_sh, b_sinks[j]) * jax.nn.silu(g_main)
            o_mem = memory_attention(mq, mem_n, w_mem_kv[layer]) * jax.nn.silu(g_mem)
            y = jnp.einsum('bse,ed->bsd', jnp.concatenate([o_main, o_mem], axis=-1), b_w_out[j])
        h = h + rms_norm(y, post_norm_g[layer])
    return h
```

```python
import functools

import jax
import jax.numpy as jnp
from jax import lax
from jax.experimental import pallas as pl
from jax.experimental.pallas import tpu as pltpu

D_MODEL = 1024
DEPTH = 4
N_A_LAYERS = DEPTH // 2

HG_HEADS = 8
HG_DK = 128
HG_DV = D_MODEL // HG_HEADS
HG_KEY_WIDTH = HG_HEADS * HG_DK
HG_VAL_WIDTH = HG_HEADS * HG_DV
HG_CHUNK = 64

SWA_Q_HEADS = 16
SWA_KV_HEADS = 4
SWA_GROUP = SWA_Q_HEADS // SWA_KV_HEADS
SWA_HEAD_DIM = 64
SWA_WIDTH = SWA_Q_HEADS * SWA_HEAD_DIM
SWA_KV_WIDTH = SWA_KV_HEADS * SWA_HEAD_DIM
WINDOW = 128

MEM_TOKENS = 256
MEM_HEADS = 4
MEM_HEAD_DIM = 128
MEM_WIDTH = MEM_HEADS * MEM_HEAD_DIM

ROPE_THETA = 10000.0
NORM_EPS = 1e-6

A_IN_WIDTH = 2 * HG_KEY_WIDTH + 2 * HG_VAL_WIDTH + 2 * MEM_WIDTH
B_IN_WIDTH = 2 * SWA_WIDTH + 2 * MEM_WIDTH
OUT_WIDTH = HG_VAL_WIDTH + MEM_WIDTH

LANES = 128
SUBLANES = 8
MASK_VALUE = -1e30

IN_TILE = 256
OUT_TILE = 512
HG_TILE = 256
SWA_TILE = 512
VMEM_LIMIT = 48 * 1024 * 1024

F32 = jnp.float32
BF16 = jnp.bfloat16

_NT = (((1,), (1,)), ((), ()))
_TN = (((0,), (0,)), ((), ()))


def _dot(a, b):
    return jnp.dot(a, b, preferred_element_type=F32)


def _dot_nt(a, b):
    return lax.dot_general(a, b, _NT, preferred_element_type=F32)


def _dot_tn(a, b):
    return lax.dot_general(a, b, _TN, preferred_element_type=F32)


def _rms_norm(x, g):
    ms = jnp.mean(x * x, axis=-1, keepdims=True)
    return x * lax.rsqrt(ms + NORM_EPS) * g


def _sigmoid(x):
    return 1.0 / (1.0 + jnp.exp(-x))


def _silu(x):
    return x * _sigmoid(x)


def _params(*semantics):
    return pltpu.CompilerParams(dimension_semantics=semantics, vmem_limit_bytes=VMEM_LIMIT)


def _rope_table_kernel(pos_ref, invf_ref, sign_ref, cos_ref, sin_ref):
    ang = pos_ref[...].astype(F32) * invf_ref[...]
    cos_ref[...] = jnp.cos(ang)
    sin_ref[...] = jnp.sin(ang) * sign_ref[...]


def _rope_tables(positions):
    n = positions.size
    half = SWA_HEAD_DIM // 2
    inv_freq = ROPE_THETA ** (-jnp.arange(0, SWA_HEAD_DIM, 2, dtype=F32) / SWA_HEAD_DIM)
    invf = jnp.tile(inv_freq, LANES // half).reshape(1, LANES)
    sign = jnp.tile(jnp.concatenate([-jnp.ones((half,), F32), jnp.ones((half,), F32)]),
                    LANES // SWA_HEAD_DIM).reshape(1, LANES)
    pos = jnp.broadcast_to(positions.reshape(n, 1), (n, LANES))
    tile = 2048
    row = pl.BlockSpec((tile, LANES), lambda i: (i, 0))
    vec = pl.BlockSpec((1, LANES), lambda i: (0, 0))
    return pl.pallas_call(
        _rope_table_kernel,
        grid=(n // tile,),
        in_specs=[row, vec, vec],
        out_specs=[row, row],
        out_shape=[jax.ShapeDtypeStruct((n, LANES), F32)] * 2,
        compiler_params=_params("parallel"),
        name="rope_tables",
    )(pos, invf, sign)


def _rope_block(xc, cos, sin, first_half):
    swapped = jnp.where(first_half,
                        pltpu.roll(xc, LANES - SWA_HEAD_DIM // 2, 1),
                        pltpu.roll(xc, SWA_HEAD_DIM // 2, 1))
    return xc * cos + swapped * sin


def _first_half_mask(rows):
    lane = lax.broadcasted_iota(jnp.int32, (rows, LANES), 1)
    return (lane & (SWA_HEAD_DIM // 2)) == 0


def _mem_kv_kernel(mem_ref, g_ref, w_ref, k_ref, v_ref):
    mn = _rms_norm(mem_ref[...], g_ref[...]).astype(BF16)
    kv = _dot(mn, w_ref[...])
    k_ref[...] = kv[:, :MEM_WIDTH].astype(BF16)
    v_ref[...] = kv[:, MEM_WIDTH:].astype(BF16)


def _mem_kv(mem, mem_norm_g, w_mem_kv):
    b = mem.shape[0]
    out = pl.BlockSpec((None, None, MEM_TOKENS, MEM_WIDTH), lambda l, i: (l, i, 0, 0))
    return pl.pallas_call(
        _mem_kv_kernel,
        grid=(DEPTH, b),
        in_specs=[
            pl.BlockSpec((None, MEM_TOKENS, D_MODEL), lambda l, i: (i, 0, 0)),
            pl.BlockSpec((None, 1, D_MODEL), lambda l, i: (l, 0, 0)),
            pl.BlockSpec((None, D_MODEL, 2 * MEM_WIDTH), lambda l, i: (l, 0, 0)),
        ],
        out_specs=[out, out],
        out_shape=[jax.ShapeDtypeStruct((DEPTH, b, MEM_TOKENS, MEM_WIDTH), BF16)] * 2,
        compiler_params=_params("arbitrary", "arbitrary"),
        name="mem_kv",
    )(mem, mem_norm_g.reshape(DEPTH, 1, D_MODEL), w_mem_kv)


def _memory_attention(mq, g_mem, mk_ref, mv_ref, out_ref):
    mq = (mq * (MEM_HEAD_DIM ** -0.5)).astype(BF16)
    gate = _silu(g_mem)
    for hh in range(MEM_HEADS):
        hs = slice(hh * MEM_HEAD_DIM, (hh + 1) * MEM_HEAD_DIM)
        s = _dot_nt(mq[:, hs], mk_ref[:, hs])
        p = jnp.exp(s - jnp.max(s, axis=-1, keepdims=True))
        o = _dot(p.astype(BF16), mv_ref[:, hs]) / jnp.sum(p, axis=-1, keepdims=True)
        out_ref[:, hs] = (o * gate[:, hs]).astype(BF16)


def _a_in_kernel(layer, x_ref, g_ref, w_ref, lbl_ref, mk_ref, mv_ref,
                 qs_ref, lf_ref, kk_ref, v_ref, gm_ref, om_ref):
    xn = _rms_norm(x_ref[...], g_ref[...]).astype(BF16)

    def proj(lo, width):
        return _dot(xn, w_ref[:, lo:lo + width])

    logits = lbl_ref[...]
    e = jnp.exp(logits - jnp.max(logits, axis=0, keepdims=True))
    lb = jnp.sum(e[:layer + 1], axis=0, keepdims=True) / jnp.sum(e, axis=0, keepdims=True)

    qs_ref[...] = _silu(proj(0, HG_KEY_WIDTH)).astype(BF16)

    fp = proj(HG_KEY_WIDTH, HG_KEY_WIDTH)
    t = jnp.exp(-jnp.abs(fp))
    r = 1.0 / (1.0 + t)
    pos = fp >= 0
    sig = jnp.where(pos, r, t * r)
    sig_neg = jnp.where(pos, t * r, r)
    lf_ref[...] = jnp.log(lb + (1.0 - lb) * sig)
    kk_ref[...] = ((1.0 - lb) * sig_neg).astype(BF16)

    v_ref[...] = proj(2 * HG_KEY_WIDTH, HG_VAL_WIDTH).astype(BF16)
    gm_ref[...] = _silu(proj(2 * HG_KEY_WIDTH + HG_VAL_WIDTH, HG_VAL_WIDTH)).astype(BF16)

    base = 2 * HG_KEY_WIDTH + 2 * HG_VAL_WIDTH
    _memory_attention(proj(base, MEM_WIDTH), proj(base + MEM_WIDTH, MEM_WIDTH), mk_ref, mv_ref, om_ref)


def _a_in_proj(layer, h, g, w, lb_logits, mem_k, mem_v, tiles_per_seq):
    n = h.shape[0]
    row = lambda width: pl.BlockSpec((IN_TILE, width), lambda i: (i, 0))
    full = lambda a: pl.BlockSpec(a.shape, lambda i: (0,) * a.ndim)
    mem = pl.BlockSpec((None, None, MEM_TOKENS, MEM_WIDTH), lambda i: (layer, i // tiles_per_seq, 0, 0))
    widths = (HG_KEY_WIDTH, HG_KEY_WIDTH, HG_KEY_WIDTH, HG_VAL_WIDTH, HG_VAL_WIDTH, MEM_WIDTH)
    dtypes = (BF16, F32, BF16, BF16, BF16, BF16)
    return pl.pallas_call(
        functools.partial(_a_in_kernel, layer),
        grid=(n // IN_TILE,),
        in_specs=[row(D_MODEL), full(g), full(w), full(lb_logits), mem, mem],
        out_specs=[row(wd) for wd in widths],
        out_shape=[jax.ShapeDtypeStruct((n, wd), dt) for wd, dt in zip(widths, dtypes)],
        compiler_params=_params("parallel"),
        name=f"a_in_proj_{layer}",
    )(h, g, w, lb_logits, mem_k, mem_v)


_LEVELS = (32, 16, 8, 4, 2, 1)


def _level_reference(cum, m):
    c, w = cum.shape
    if m >= SUBLANES:
        parts = [jnp.broadcast_to(cum[blk * 2 * m + m - 1:blk * 2 * m + m, :], (2 * m, w))
                 for blk in range(c // (2 * m))]
        return parts[0] if len(parts) == 1 else jnp.concatenate(parts, axis=0)
    cum3 = cum.reshape(c // SUBLANES, SUBLANES, w)
    parts = [jnp.broadcast_to(cum3[:, blk * 2 * m + m - 1:blk * 2 * m + m, :], (c // SUBLANES, 2 * m, w))
             for blk in range(SUBLANES // (2 * m))]
    ref3 = parts[0] if len(parts) == 1 else jnp.concatenate(parts, axis=1)
    return ref3.reshape(c, w)


def _hgrn_kernel(qs_ref, lf_ref, kk_ref, v_ref, gm_ref, gn_ref, o_ref, st_ref):
    c = HG_CHUNK

    @pl.when(pl.program_id(1) == 0)
    def _():
        st_ref[...] = jnp.zeros_like(st_ref)

    row = lax.broadcasted_iota(jnp.int32, (c, c), 0)
    col = lax.broadcasted_iota(jnp.int32, (c, c), 1)
    tri = (col <= row).astype(BF16)
    same_block = {m: (row // (2 * m)) == (col // (2 * m)) for m in _LEVELS if 2 * m < c}
    diag = row == col
    trow = lax.broadcasted_iota(jnp.int32, (c, HG_KEY_WIDTH), 0)
    gn = gn_ref[...]

    def chunk(ci, carry):
        rows = pl.ds(pl.multiple_of(ci * c, c), c)
        lf = lf_ref[rows, :]
        qs = qs_ref[rows, :].astype(F32)
        kk = kk_ref[rows, :].astype(F32)
        v = v_ref[rows, :]
        gm = gm_ref[rows, :].astype(F32)

        hi = lf.astype(BF16)
        rem = lf - hi.astype(F32)
        mid = rem.astype(BF16)
        lo = (rem - mid.astype(F32)).astype(BF16)
        cum = _dot(tri, hi) + _dot(tri, mid) + _dot(tri, lo)

        pairs = []
        for m in _LEVELS:
            upper = (trow & m) != 0
            if m == 1:
                qd = jnp.where(upper, qs * jnp.exp(lf), 0.0)
                kd = jnp.where(upper, 0.0, kk)
            else:
                x = cum - _level_reference(cum, m)
                e = jnp.exp(jnp.where(upper, x, -x))
                qd = jnp.where(upper, qs * e, 0.0)
                kd = jnp.where(upper, 0.0, kk * e)
            pairs.append((qd.astype(BF16), kd.astype(BF16), same_block.get(m)))
        pairs.append((qs.astype(BF16), kk.astype(BF16), diag))

        last = cum[c - 1:c, :]
        q_in = (qs * jnp.exp(cum)).astype(BF16)
        k_out = (kk * jnp.exp(last - cum)).astype(BF16)
        e_last = jnp.exp(last)

        for h in range(HG_HEADS):
            hs = slice(h * HG_DK, (h + 1) * HG_DK)
            scores = None
            for qd, kd, mask in pairs:
                s = _dot_nt(qd[:, hs], kd[:, hs])
                if mask is not None:
                    s = jnp.where(mask, s, 0.0)
                scores = s if scores is None else scores + s
            state = st_ref[h]
            o = _dot_nt(q_in[:, hs], state.astype(BF16)) + _dot(scores.astype(BF16), v[:, hs])
            st_ref[h] = state * e_last[:, hs] + _dot_tn(v[:, hs], k_out[:, hs])
            o = o * lax.rsqrt(jnp.mean(o * o, axis=-1, keepdims=True) + NORM_EPS)
            o_ref[rows, hs] = (o * gn[:, hs] * gm[:, hs]).astype(BF16)
        return carry

    lax.fori_loop(0, HG_TILE // c, chunk, 0)


def _hgrn(qs, lf, kk, v, gm, gn, batch):
    n = qs.shape[0]
    tiles = n // batch // HG_TILE
    row = pl.BlockSpec((HG_TILE, HG_KEY_WIDTH), lambda b, t: (b * tiles + t, 0))
    return pl.pallas_call(
        _hgrn_kernel,
        grid=(batch, tiles),
        in_specs=[row, row, row, row, row, pl.BlockSpec((1, HG_VAL_WIDTH), lambda b, t: (0, 0))],
        out_specs=row,
        out_shape=jax.ShapeDtypeStruct((n, HG_VAL_WIDTH), BF16),
        scratch_shapes=[pltpu.VMEM((HG_HEADS, HG_DV, HG_DK), F32)],
        compiler_params=_params("arbitrary", "arbitrary"),
        name="hgrn2",
    )(qs, lf, kk, v, gm, gn)


def _out_kernel(om_ref, ome_ref, w_ref, g_ref, h_ref, o_ref):
    y = _dot(om_ref[...], w_ref[:HG_VAL_WIDTH, :]) + _dot(ome_ref[...], w_ref[HG_VAL_WIDTH:, :])
    o_ref[...] = h_ref[...] + _rms_norm(y, g_ref[...])


def _out_proj(o_main, o_mem, w, g, h, name):
    n = h.shape[0]
    row = lambda width: pl.BlockSpec((OUT_TILE, width), lambda i: (i, 0))
    full = lambda a: pl.BlockSpec(a.shape, lambda i: (0,) * a.ndim)
    return pl.pallas_call(
        _out_kernel,
        grid=(n // OUT_TILE,),
        in_specs=[row(HG_VAL_WIDTH), row(MEM_WIDTH), full(w), full(g), row(D_MODEL)],
        out_specs=row(D_MODEL),
        out_shape=jax.ShapeDtypeStruct((n, D_MODEL), F32),
        compiler_params=_params("parallel"),
        name=name,
    )(o_main, o_mem, w, g, h)


def _shared_kv_kernel(x_ref, g_ref, w_ref, cos_ref, sin_ref, k_ref, v_ref):
    xn = _rms_norm(x_ref[...], g_ref[...]).astype(BF16)
    kv = _dot(xn, w_ref[...])
    cos = cos_ref[...]
    sin = sin_ref[...]
    first = _first_half_mask(IN_TILE)
    for cb in range(SWA_KV_WIDTH // LANES):
        cs = slice(cb * LANES, (cb + 1) * LANES)
        k_ref[:, cs] = _rope_block(kv[:, cs], cos, sin, first).astype(BF16)
    v_ref[...] = kv[:, SWA_KV_WIDTH:].astype(BF16)


def _shared_kv(h, g, w, cos, sin):
    n = h.shape[0]
    row = lambda width: pl.BlockSpec((IN_TILE, width), lambda i: (i, 0))
    full = lambda a: pl.BlockSpec(a.shape, lambda i: (0,) * a.ndim)
    return pl.pallas_call(
        _shared_kv_kernel,
        grid=(n // IN_TILE,),
        in_specs=[row(D_MODEL), full(g), full(w), row(LANES), row(LANES)],
        out_specs=[row(SWA_KV_WIDTH), row(SWA_KV_WIDTH)],
        out_shape=[jax.ShapeDtypeStruct((n, SWA_KV_WIDTH), BF16)] * 2,
        compiler_params=_params("parallel"),
        name="shared_kv",
    )(h, g, w, cos, sin)


def _b_in_kernel(x_ref, g_ref, w_ref, cos_ref, sin_ref, mk_ref, mv_ref, q_ref, gm_ref, om_ref):
    xn = _rms_norm(x_ref[...], g_ref[...]).astype(BF16)

    def proj(lo, width):
        return _dot(xn, w_ref[:, lo:lo + width])

    q = proj(0, SWA_WIDTH)
    cos = cos_ref[...]
    sin = sin_ref[...]
    first = _first_half_mask(IN_TILE)
    scale = SWA_HEAD_DIM ** -0.5
    for cb in range(SWA_WIDTH // LANES):
        cs = slice(cb * LANES, (cb + 1) * LANES)
        q_ref[:, cs] = (_rope_block(q[:, cs], cos, sin, first) * scale).astype(BF16)

    gm_ref[...] = _silu(proj(SWA_WIDTH, SWA_WIDTH)).astype(BF16)
    base = 2 * SWA_WIDTH
    _memory_attention(proj(base, MEM_WIDTH), proj(base + MEM_WIDTH, MEM_WIDTH), mk_ref, mv_ref, om_ref)


def _b_in_proj(layer, h, g, w, cos, sin, mem_k, mem_v, tiles_per_seq):
    n = h.shape[0]
    row = lambda width: pl.BlockSpec((IN_TILE, width), lambda i: (i, 0))
    full = lambda a: pl.BlockSpec(a.shape, lambda i: (0,) * a.ndim)
    mem = pl.BlockSpec((None, None, MEM_TOKENS, MEM_WIDTH), lambda i: (layer, i // tiles_per_seq, 0, 0))
    widths = (SWA_WIDTH, SWA_WIDTH, MEM_WIDTH)
    return pl.pallas_call(
        _b_in_kernel,
        grid=(n // IN_TILE,),
        in_specs=[row(D_MODEL), full(g), full(w), row(LANES), row(LANES), mem, mem],
        out_specs=[row(wd) for wd in widths],
        out_shape=[jax.ShapeDtypeStruct((n, wd), BF16) for wd in widths],
        compiler_params=_params("parallel"),
        name=f"b_in_proj_{layer}",
    )(h, g, w, cos, sin, mem_k, mem_v)


def _swa_kernel(sink_ref, q_ref, kc_ref, kp_ref, vc_ref, vp_ref, gm_ref, o_ref):
    blk = WINDOW
    qi = lax.broadcasted_iota(jnp.int32, (blk, 2 * blk), 0)
    ki = lax.broadcasted_iota(jnp.int32, (blk, 2 * blk), 1)
    d = ki - qi
    band = (d > 0) & (d <= blk)
    first_key = jnp.where(pl.program_id(1) > 0, 0, blk)

    for sb in range(SWA_TILE // blk):
        rows = slice(sb * blk, (sb + 1) * blk)
        if sb == 0:
            k2 = jnp.concatenate([kp_ref[...], kc_ref[rows, :]], axis=0)
            v2 = jnp.concatenate([vp_ref[...], vc_ref[rows, :]], axis=0)
            mask = band & (ki >= first_key)
        else:
            k2 = kc_ref[(sb - 1) * blk:(sb + 1) * blk, :]
            v2 = vc_ref[(sb - 1) * blk:(sb + 1) * blk, :]
            mask = band
        for h in range(SWA_Q_HEADS):
            kvs = slice((h // SWA_GROUP) * SWA_HEAD_DIM, (h // SWA_GROUP + 1) * SWA_HEAD_DIM)
            hs = slice(h * SWA_HEAD_DIM, (h + 1) * SWA_HEAD_DIM)
            s = jnp.where(mask, _dot_nt(q_ref[rows, hs], k2[:, kvs]), MASK_VALUE)
            sink = sink_ref[h]
            m = jnp.maximum(jnp.max(s, axis=-1, keepdims=True), sink)
            p = jnp.exp(s - m)
            den = jnp.sum(p, axis=-1, keepdims=True) + jnp.exp(sink - m)
            o = _dot(p.astype(BF16), v2[:, kvs]) / den
            o_ref[rows, hs] = (o * gm_ref[rows, hs].astype(F32)).astype(BF16)


def _swa(sinks, q, k, v, gm, batch):
    n = q.shape[0]
    tiles = n // batch // SWA_TILE
    sub = SWA_TILE // WINDOW
    cur = lambda width: pl.BlockSpec((SWA_TILE, width), lambda b, t: (b * tiles + t, 0))
    prev = pl.BlockSpec((WINDOW, SWA_KV_WIDTH), lambda b, t: (jnp.maximum((b * tiles + t) * sub - 1, 0), 0))
    return pl.pallas_call(
        _swa_kernel,
        grid=(batch, tiles),
        in_specs=[pl.BlockSpec(memory_space=pltpu.SMEM),
                  cur(SWA_WIDTH), cur(SWA_KV_WIDTH), prev, cur(SWA_KV_WIDTH), prev, cur(SWA_WIDTH)],
        out_specs=cur(SWA_WIDTH),
        out_shape=jax.ShapeDtypeStruct((n, SWA_WIDTH), BF16),
        compiler_params=_params("parallel", "arbitrary"),
        name="swa",
    )(sinks, q, k, k, v, v, gm)


def kernel(x, mem, positions, pre_norm_g, post_norm_g, mem_norm_g, w_mem_kv, a_w_in, a_lb_logits, a_out_norm_g, a_w_out, kv_norm_g, w_kv_shared, b_w_in, b_sinks, b_w_out):
    batch, seq, _ = x.shape
    n = batch * seq
    assert seq % max(IN_TILE, OUT_TILE, HG_TILE, SWA_TILE) == 0
    tiles_per_seq = seq // IN_TILE

    h = x.reshape(n, D_MODEL)
    mem_k, mem_v = _mem_kv(mem, mem_norm_g, w_mem_kv.astype(BF16))
    cos, sin = _rope_tables(positions)
    a_w_in = a_w_in.astype(BF16)
    a_w_out = a_w_out.astype(BF16)
    b_w_in = b_w_in.astype(BF16)
    b_w_out = b_w_out.astype(BF16)
    k_sh = v_sh = None

    for layer in range(DEPTH):
        pre_g = pre_norm_g[layer].reshape(1, D_MODEL)
        post_g = post_norm_g[layer].reshape(1, D_MODEL)
        if layer < N_A_LAYERS:
            qs, lf, kk, v, gm, o_mem = _a_in_proj(layer, h, pre_g, a_w_in[layer], a_lb_logits,
                                                  mem_k, mem_v, tiles_per_seq)
            o_main = _hgrn(qs, lf, kk, v, gm, a_out_norm_g[layer].reshape(1, HG_VAL_WIDTH), batch)
            h = _out_proj(o_main, o_mem, a_w_out[layer], post_g, h, f"a_out_proj_{layer}")
        else:
            j = layer - N_A_LAYERS
            if k_sh is None:
                k_sh, v_sh = _shared_kv(h, kv_norm_g.reshape(1, D_MODEL), w_kv_shared.astype(BF16), cos, sin)
            q, gm, o_mem = _b_in_proj(layer, h, pre_g, b_w_in[j], cos, sin, mem_k, mem_v, tiles_per_seq)
            o_main = _swa(b_sinks[j], q, k_sh, v_sh, gm, batch)
            h = _out_proj(o_main, o_mem, b_w_out[j], post_g, h, f"b_out_proj_{layer}")
    return h.reshape(batch, seq, D_MODEL)
```

```python
import functools

import jax
import jax.numpy as jnp
from jax import lax
from jax.experimental import pallas as pl
from jax.experimental.pallas import tpu as pltpu

D_MODEL = 1024
DEPTH = 4
N_A_LAYERS = DEPTH // 2

HG_HEADS = 8
HG_DK = 128
HG_DV = D_MODEL // HG_HEADS
HG_KEY_WIDTH = HG_HEADS * HG_DK
HG_VAL_WIDTH = HG_HEADS * HG_DV
HG_CHUNK = 64

SWA_Q_HEADS = 16
SWA_KV_HEADS = 4
SWA_GROUP = SWA_Q_HEADS // SWA_KV_HEADS
SWA_HEAD_DIM = 64
SWA_WIDTH = SWA_Q_HEADS * SWA_HEAD_DIM
SWA_KV_WIDTH = SWA_KV_HEADS * SWA_HEAD_DIM
WINDOW = 128

MEM_TOKENS = 256
MEM_HEADS = 4
MEM_HEAD_DIM = 128
MEM_WIDTH = MEM_HEADS * MEM_HEAD_DIM

ROPE_THETA = 10000.0
NORM_EPS = 1e-6

A_IN_WIDTH = 2 * HG_KEY_WIDTH + 2 * HG_VAL_WIDTH + 2 * MEM_WIDTH
B_IN_WIDTH = 2 * SWA_WIDTH + 2 * MEM_WIDTH
OUT_WIDTH = HG_VAL_WIDTH + MEM_WIDTH

LANES = 128
SUBLANES = 8
MASK_VALUE = -1e30

IN_TILE = 256
OUT_TILE = 512
HG_TILE = 256
SWA_TILE = 512
VMEM_LIMIT = 48 * 1024 * 1024

F32 = jnp.float32
BF16 = jnp.bfloat16

_NT = (((1,), (1,)), ((), ()))
_TN = (((0,), (0,)), ((), ()))


def _dot(a, b):
    return jnp.dot(a, b, preferred_element_type=F32)


def _dot_nt(a, b):
    return lax.dot_general(a, b, _NT, preferred_element_type=F32)


def _dot_tn(a, b):
    return lax.dot_general(a, b, _TN, preferred_element_type=F32)


def _rms_norm(x, g):
    ms = jnp.mean(x * x, axis=-1, keepdims=True)
    return x * lax.rsqrt(ms + NORM_EPS) * g


def _sigmoid(x):
    return 1.0 / (1.0 + jnp.exp(-x))


def _silu(x):
    return x * _sigmoid(x)


def _params(*semantics):
    return pltpu.CompilerParams(dimension_semantics=semantics, vmem_limit_bytes=VMEM_LIMIT)


def _rope_table_kernel(pos_ref, invf_ref, sign_ref, cos_ref, sin_ref):
    ang = pos_ref[...].astype(F32) * invf_ref[...]
    cos_ref[...] = jnp.cos(ang)
    sin_ref[...] = jnp.sin(ang) * sign_ref[...]


def _rope_tables(positions):
    n = positions.size
    half = SWA_HEAD_DIM // 2
    inv_freq = ROPE_THETA ** (-jnp.arange(0, SWA_HEAD_DIM, 2, dtype=F32) / SWA_HEAD_DIM)
    invf = jnp.tile(inv_freq, LANES // half).reshape(1, LANES)
    sign = jnp.tile(jnp.concatenate([-jnp.ones((half,), F32), jnp.ones((half,), F32)]),
                    LANES // SWA_HEAD_DIM).reshape(1, LANES)
    pos = jnp.broadcast_to(positions.reshape(n, 1), (n, LANES))
    tile = 2048
    row = pl.BlockSpec((tile, LANES), lambda i: (i, 0))
    vec = pl.BlockSpec((1, LANES), lambda i: (0, 0))
    return pl.pallas_call(
        _rope_table_kernel,
        grid=(n // tile,),
        in_specs=[row, vec, vec],
        out_specs=[row, row],
        out_shape=[jax.ShapeDtypeStruct((n, LANES), F32)] * 2,
        compiler_params=_params("parallel"),
        name="rope_tables",
    )(pos, invf, sign)


def _rope_block(xc, cos, sin, first_half):
    swapped = jnp.where(first_half,
                        pltpu.roll(xc, LANES - SWA_HEAD_DIM // 2, 1),
                        pltpu.roll(xc, SWA_HEAD_DIM // 2, 1))
    return xc * cos + swapped * sin


def _first_half_mask(rows):
    lane = lax.broadcasted_iota(jnp.int32, (rows, LANES), 1)
    return (lane & (SWA_HEAD_DIM // 2)) == 0


def _mem_kv_kernel(mem_ref, g_ref, w_ref, k_ref, v_ref):
    mn = _rms_norm(mem_ref[...], g_ref[...]).astype(BF16)
    kv = _dot(mn, w_ref[...])
    k_ref[...] = kv[:, :MEM_WIDTH].astype(BF16)
    v_ref[...] = kv[:, MEM_WIDTH:].astype(BF16)


def _mem_kv(mem, mem_norm_g, w_mem_kv):
    b = mem.shape[0]
    out = pl.BlockSpec((None, None, MEM_TOKENS, MEM_WIDTH), lambda l, i: (l, i, 0, 0))
    return pl.pallas_call(
        _mem_kv_kernel,
        grid=(DEPTH, b),
        in_specs=[
            pl.BlockSpec((None, MEM_TOKENS, D_MODEL), lambda l, i: (i, 0, 0)),
            pl.BlockSpec((None, 1, D_MODEL), lambda l, i: (l, 0, 0)),
            pl.BlockSpec((None, D_MODEL, 2 * MEM_WIDTH), lambda l, i: (l, 0, 0)),
        ],
        out_specs=[out, out],
        out_shape=[jax.ShapeDtypeStruct((DEPTH, b, MEM_TOKENS, MEM_WIDTH), BF16)] * 2,
        compiler_params=_params("arbitrary", "arbitrary"),
        name="mem_kv",
    )(mem, mem_norm_g.reshape(DEPTH, 1, D_MODEL), w_mem_kv)


def _memory_attention(mq, g_mem, mk_ref, mv_ref, out_ref):
    mq = (mq * (MEM_HEAD_DIM ** -0.5)).astype(BF16)
    gate = _silu(g_mem)
    for hh in range(MEM_HEADS):
        hs = slice(hh * MEM_HEAD_DIM, (hh + 1) * MEM_HEAD_DIM)
        s = _dot_nt(mq[:, hs], mk_ref[:, hs])
        p = jnp.exp(s - jnp.max(s, axis=-1, keepdims=True))
        o = _dot(p.astype(BF16), mv_ref[:, hs]) / jnp.sum(p, axis=-1, keepdims=True)
        out_ref[:, hs] = (o * gate[:, hs]).astype(BF16)


def _a_in_kernel(layer, x_ref, g_ref, w_ref, lbl_ref, mk_ref, mv_ref,
                 qs_ref, lf_ref, kk_ref, v_ref, gm_ref, om_ref):
    xn = _rms_norm(x_ref[...], g_ref[...]).astype(BF16)

    def proj(lo, width):
        return _dot(xn, w_ref[:, lo:lo + width])

    logits = lbl_ref[...]
    e = jnp.exp(logits - jnp.max(logits, axis=0, keepdims=True))
    lb = jnp.sum(e[:layer + 1], axis=0, keepdims=True) / jnp.sum(e, axis=0, keepdims=True)

    qs_ref[...] = _silu(proj(0, HG_KEY_WIDTH)).astype(BF16)

    fp = proj(HG_KEY_WIDTH, HG_KEY_WIDTH)
    t = jnp.exp(-jnp.abs(fp))
    r = 1.0 / (1.0 + t)
    pos = fp >= 0
    sig = jnp.where(pos, r, t * r)
    sig_neg = jnp.where(pos, t * r, r)
    lf_ref[...] = jnp.log2(lb + (1.0 - lb) * sig)
    kk_ref[...] = ((1.0 - lb) * sig_neg).astype(BF16)

    v_ref[...] = proj(2 * HG_KEY_WIDTH, HG_VAL_WIDTH).astype(BF16)
    gm_ref[...] = _silu(proj(2 * HG_KEY_WIDTH + HG_VAL_WIDTH, HG_VAL_WIDTH)).astype(BF16)

    base = 2 * HG_KEY_WIDTH + 2 * HG_VAL_WIDTH
    _memory_attention(proj(base, MEM_WIDTH), proj(base + MEM_WIDTH, MEM_WIDTH), mk_ref, mv_ref, om_ref)


def _a_in_proj(layer, h, g, w, lb_logits, mem_k, mem_v, tiles_per_seq):
    n = h.shape[0]
    row = lambda width: pl.BlockSpec((IN_TILE, width), lambda i: (i, 0))
    full = lambda a: pl.BlockSpec(a.shape, lambda i: (0,) * a.ndim)
    mem = pl.BlockSpec((None, None, MEM_TOKENS, MEM_WIDTH), lambda i: (layer, i // tiles_per_seq, 0, 0))
    widths = (HG_KEY_WIDTH, HG_KEY_WIDTH, HG_KEY_WIDTH, HG_VAL_WIDTH, HG_VAL_WIDTH, MEM_WIDTH)
    dtypes = (BF16, F32, BF16, BF16, BF16, BF16)
    return pl.pallas_call(
        functools.partial(_a_in_kernel, layer),
        grid=(n // IN_TILE,),
        in_specs=[row(D_MODEL), full(g), full(w), full(lb_logits), mem, mem],
        out_specs=[row(wd) for wd in widths],
        out_shape=[jax.ShapeDtypeStruct((n, wd), dt) for wd, dt in zip(widths, dtypes)],
        compiler_params=_params("parallel"),
        name=f"a_in_proj_{layer}",
    )(h, g, w, lb_logits, mem_k, mem_v)


_LEVELS = (32, 16, 8, 4, 2, 1)


def _coarse_level(cum, qs, kk, m):
    c, w = cum.shape
    zeros = jnp.zeros((m, w), F32)
    qd, kd = [], []
    for blk in range(c // (2 * m)):
        lo, mid, hi = blk * 2 * m, blk * 2 * m + m, (blk + 1) * 2 * m
        ref = cum[mid - 1:mid, :]
        kd += [kk[lo:mid] * jnp.exp2(ref - cum[lo:mid]), zeros]
        qd += [zeros, qs[mid:hi] * jnp.exp2(cum[mid:hi] - ref)]
    return jnp.concatenate(qd, axis=0).astype(BF16), jnp.concatenate(kd, axis=0).astype(BF16)


def _fine_reference(cum, m):
    c, w = cum.shape
    cum3 = cum.reshape(c // SUBLANES, SUBLANES, w)
    parts = [jnp.broadcast_to(cum3[:, blk * 2 * m + m - 1:blk * 2 * m + m, :], (c // SUBLANES, 2 * m, w))
             for blk in range(SUBLANES // (2 * m))]
    ref3 = parts[0] if len(parts) == 1 else jnp.concatenate(parts, axis=1)
    return ref3.reshape(c, w)


def _hgrn_kernel(qs_ref, lf_ref, kk_ref, v_ref, gm_ref, gn_ref, o_ref, st_ref):
    c = HG_CHUNK

    @pl.when(pl.program_id(1) == 0)
    def _():
        st_ref[...] = jnp.zeros_like(st_ref)

    row = lax.broadcasted_iota(jnp.int32, (c, c), 0)
    col = lax.broadcasted_iota(jnp.int32, (c, c), 1)
    tri = (col <= row).astype(BF16)
    same_block = {m: (row // (2 * m)) == (col // (2 * m)) for m in _LEVELS if 2 * m < c}
    diag = row == col
    trow = lax.broadcasted_iota(jnp.int32, (c, HG_DK), 0)
    upper = {m: (trow & m) != 0 for m in _LEVELS if m < SUBLANES}
    gn = gn_ref[...]

    def chunk(ci, carry):
        rows = pl.ds(pl.multiple_of(ci * c, c), c)

        lf_all = lf_ref[rows, :]
        hi = lf_all.astype(BF16)
        lo = (lf_all - hi.astype(F32)).astype(BF16)
        cum_all = _dot(tri, hi) + _dot(tri, lo)

        heads = [slice(h * HG_DK, (h + 1) * HG_DK) for h in range(HG_HEADS)]
        scores_bf, q_ins, k_outs, lasts = [], [], [], []
        for hs in heads:
            cum = cum_all[:, hs]
            qs = qs_ref[rows, hs].astype(F32)
            kk = kk_ref[rows, hs].astype(F32)

            pairs = []
            for m in _LEVELS:
                if m >= SUBLANES:
                    qd, kd = _coarse_level(cum, qs, kk, m)
                elif m == 1:
                    qd = jnp.where(upper[m], qs * jnp.exp2(lf_ref[rows, hs]), 0.0).astype(BF16)
                    kd = jnp.where(upper[m], 0.0, kk).astype(BF16)
                else:
                    x = cum - _fine_reference(cum, m)
                    e = jnp.exp2(jnp.where(upper[m], x, -x))
                    qd = jnp.where(upper[m], qs * e, 0.0).astype(BF16)
                    kd = jnp.where(upper[m], 0.0, kk * e).astype(BF16)
                pairs.append((qd, kd, same_block.get(m)))
            pairs.append((qs.astype(BF16), kk.astype(BF16), diag))

            scores = None
            for qd, kd, mask in pairs:
                s = _dot_nt(qd, kd)
                if mask is not None:
                    s = jnp.where(mask, s, 0.0)
                scores = s if scores is None else scores + s

            last = cum[c - 1:c, :]
            scores_bf.append(scores.astype(BF16))
            q_ins.append((qs * jnp.exp2(cum)).astype(BF16))
            k_outs.append((kk * jnp.exp2(last - cum)).astype(BF16))
            lasts.append(last)

        for h, hs in enumerate(heads):
            o = _dot_nt(q_ins[h], st_ref[h].astype(BF16)) + _dot(scores_bf[h], v_ref[rows, hs])
            o = o * lax.rsqrt(jnp.mean(o * o, axis=-1, keepdims=True) + NORM_EPS)
            o_ref[rows, hs] = (o * gn[:, hs] * gm_ref[rows, hs].astype(F32)).astype(BF16)

        for h, hs in enumerate(heads):
            st_ref[h] = st_ref[h] * jnp.exp2(lasts[h]) + _dot_tn(v_ref[rows, hs], k_outs[h])
        return carry

    lax.fori_loop(0, HG_TILE // c, chunk, 0)


def _hgrn(qs, lf, kk, v, gm, gn, batch):
    n = qs.shape[0]
    tiles = n // batch // HG_TILE
    row = pl.BlockSpec((HG_TILE, HG_KEY_WIDTH), lambda b, t: (b * tiles + t, 0))
    return pl.pallas_call(
        _hgrn_kernel,
        grid=(batch, tiles),
        in_specs=[row, row, row, row, row, pl.BlockSpec((1, HG_VAL_WIDTH), lambda b, t: (0, 0))],
        out_specs=row,
        out_shape=jax.ShapeDtypeStruct((n, HG_VAL_WIDTH), BF16),
        scratch_shapes=[pltpu.VMEM((HG_HEADS, HG_DV, HG_DK), F32)],
        compiler_params=_params("arbitrary", "arbitrary"),
        name="hgrn2",
    )(qs, lf, kk, v, gm, gn)


def _out_kernel(om_ref, ome_ref, w_ref, g_ref, h_ref, o_ref):
    y = _dot(om_ref[...], w_ref[:HG_VAL_WIDTH, :]) + _dot(ome_ref[...], w_ref[HG_VAL_WIDTH:, :])
    o_ref[...] = h_ref[...] + _rms_norm(y, g_ref[...])


def _out_proj(o_main, o_mem, w, g, h, name):
    n = h.shape[0]
    row = lambda width: pl.BlockSpec((OUT_TILE, width), lambda i: (i, 0))
    full = lambda a: pl.BlockSpec(a.shape, lambda i: (0,) * a.ndim)
    return pl.pallas_call(
        _out_kernel,
        grid=(n // OUT_TILE,),
        in_specs=[row(HG_VAL_WIDTH), row(MEM_WIDTH), full(w), full(g), row(D_MODEL)],
        out_specs=row(D_MODEL),
        out_shape=jax.ShapeDtypeStruct((n, D_MODEL), F32),
        compiler_params=_params("parallel"),
        name=name,
    )(o_main, o_mem, w, g, h)


def _shared_kv_kernel(x_ref, g_ref, w_ref, cos_ref, sin_ref, k_ref, v_ref):
    xn = _rms_norm(x_ref[...], g_ref[...]).astype(BF16)
    kv = _dot(xn, w_ref[...])
    cos = cos_ref[...]
    sin = sin_ref[...]
    first = _first_half_mask(IN_TILE)
    for cb in range(SWA_KV_WIDTH // LANES):
        cs = slice(cb * LANES, (cb + 1) * LANES)
        k_ref[:, cs] = _rope_block(kv[:, cs], cos, sin, first).astype(BF16)
    v_ref[...] = kv[:, SWA_KV_WIDTH:].astype(BF16)


def _shared_kv(h, g, w, cos, sin):
    n = h.shape[0]
    row = lambda width: pl.BlockSpec((IN_TILE, width), lambda i: (i, 0))
    full = lambda a: pl.BlockSpec(a.shape, lambda i: (0,) * a.ndim)
    return pl.pallas_call(
        _shared_kv_kernel,
        grid=(n // IN_TILE,),
        in_specs=[row(D_MODEL), full(g), full(w), row(LANES), row(LANES)],
        out_specs=[row(SWA_KV_WIDTH), row(SWA_KV_WIDTH)],
        out_shape=[jax.ShapeDtypeStruct((n, SWA_KV_WIDTH), BF16)] * 2,
        compiler_params=_params("parallel"),
        name="shared_kv",
    )(h, g, w, cos, sin)


def _b_in_kernel(x_ref, g_ref, w_ref, cos_ref, sin_ref, mk_ref, mv_ref, q_ref, gm_ref, om_ref):
    xn = _rms_norm(x_ref[...], g_ref[...]).astype(BF16)

    def proj(lo, width):
        return _dot(xn, w_ref[:, lo:lo + width])

    q = proj(0, SWA_WIDTH)
    cos = cos_ref[...]
    sin = sin_ref[...]
    first = _first_half_mask(IN_TILE)
    scale = SWA_HEAD_DIM ** -0.5
    for cb in range(SWA_WIDTH // LANES):
        cs = slice(cb * LANES, (cb + 1) * LANES)
        q_ref[:, cs] = (_rope_block(q[:, cs], cos, sin, first) * scale).astype(BF16)

    gm_ref[...] = _silu(proj(SWA_WIDTH, SWA_WIDTH)).astype(BF16)
    base = 2 * SWA_WIDTH
    _memory_attention(proj(base, MEM_WIDTH), proj(base + MEM_WIDTH, MEM_WIDTH), mk_ref, mv_ref, om_ref)


def _b_in_proj(layer, h, g, w, cos, sin, mem_k, mem_v, tiles_per_seq):
    n = h.shape[0]
    row = lambda width: pl.BlockSpec((IN_TILE, width), lambda i: (i, 0))
    full = lambda a: pl.BlockSpec(a.shape, lambda i: (0,) * a.ndim)
    mem = pl.BlockSpec((None, None, MEM_TOKENS, MEM_WIDTH), lambda i: (layer, i // tiles_per_seq, 0, 0))
    widths = (SWA_WIDTH, SWA_WIDTH, MEM_WIDTH)
    return pl.pallas_call(
        _b_in_kernel,
        grid=(n // IN_TILE,),
        in_specs=[row(D_MODEL), full(g), full(w), row(LANES), row(LANES), mem, mem],
        out_specs=[row(wd) for wd in widths],
        out_shape=[jax.ShapeDtypeStruct((n, wd), BF16) for wd in widths],
        compiler_params=_params("parallel"),
        name=f"b_in_proj_{layer}",
    )(h, g, w, cos, sin, mem_k, mem_v)


def _swa_kernel(sink_ref, q_ref, kc_ref, kp_ref, vc_ref, vp_ref, gm_ref, o_ref):
    blk = WINDOW
    qi = lax.broadcasted_iota(jnp.int32, (blk, 2 * blk), 0)
    ki = lax.broadcasted_iota(jnp.int32, (blk, 2 * blk), 1)
    d = ki - qi
    band = (d > 0) & (d <= blk)
    first_key = jnp.where(pl.program_id(1) > 0, 0, blk)

    for sb in range(SWA_TILE // blk):
        rows = slice(sb * blk, (sb + 1) * blk)
        if sb == 0:
            k2 = jnp.concatenate([kp_ref[...], kc_ref[rows, :]], axis=0)
            v2 = jnp.concatenate([vp_ref[...], vc_ref[rows, :]], axis=0)
            mask = band & (ki >= first_key)
        else:
            k2 = kc_ref[(sb - 1) * blk:(sb + 1) * blk, :]
            v2 = vc_ref[(sb - 1) * blk:(sb + 1) * blk, :]
            mask = band
        for h in range(SWA_Q_HEADS):
            kvs = slice((h // SWA_GROUP) * SWA_HEAD_DIM, (h // SWA_GROUP + 1) * SWA_HEAD_DIM)
            hs = slice(h * SWA_HEAD_DIM, (h + 1) * SWA_HEAD_DIM)
            s = jnp.where(mask, _dot_nt(q_ref[rows, hs], k2[:, kvs]), MASK_VALUE)
            sink = sink_ref[h]
            m = jnp.maximum(jnp.max(s, axis=-1, keepdims=True), sink)
            p = jnp.exp(s - m)
            den = jnp.sum(p, axis=-1, keepdims=True) + jnp.exp(sink - m)
            o = _dot(p.astype(BF16), v2[:, kvs]) / den
            o_ref[rows, hs] = (o * gm_ref[rows, hs].astype(F32)).astype(BF16)


def _swa(sinks, q, k, v, gm, batch):
    n = q.shape[0]
    tiles = n // batch // SWA_TILE
    sub = SWA_TILE // WINDOW
    cur = lambda width: pl.BlockSpec((SWA_TILE, width), lambda b, t: (b * tiles + t, 0))
    prev = pl.BlockSpec((WINDOW, SWA_KV_WIDTH), lambda b, t: (jnp.maximum((b * tiles + t) * sub - 1, 0), 0))
    return pl.pallas_call(
        _swa_kernel,
        grid=(batch, tiles),
        in_specs=[pl.BlockSpec(memory_space=pltpu.SMEM),
                  cur(SWA_WIDTH), cur(SWA_KV_WIDTH), prev, cur(SWA_KV_WIDTH), prev, cur(SWA_WIDTH)],
        out_specs=cur(SWA_WIDTH),
        out_shape=jax.ShapeDtypeStruct((n, SWA_WIDTH), BF16),
        compiler_params=_params("parallel", "arbitrary"),
        name="swa",
    )(sinks, q, k, k, v, v, gm)


def kernel(x, mem, positions, pre_norm_g, post_norm_g, mem_norm_g, w_mem_kv, a_w_in, a_lb_logits, a_out_norm_g, a_w_out, kv_norm_g, w_kv_shared, b_w_in, b_sinks, b_w_out):
    batch, seq, _ = x.shape
    n = batch * seq
    assert seq % max(IN_TILE, OUT_TILE, HG_TILE, SWA_TILE) == 0
    tiles_per_seq = seq // IN_TILE

    h = x.reshape(n, D_MODEL)
    mem_k, mem_v = _mem_kv(mem, mem_norm_g, w_mem_kv.astype(BF16))
    cos, sin = _rope_tables(positions)
    a_w_in = a_w_in.astype(BF16)
    a_w_out = a_w_out.astype(BF16)
    b_w_in = b_w_in.astype(BF16)
    b_w_out = b_w_out.astype(BF16)
    k_sh = v_sh = None

    for layer in range(DEPTH):
        pre_g = pre_norm_g[layer].reshape(1, D_MODEL)
        post_g = post_norm_g[layer].reshape(1, D_MODEL)
        if layer < N_A_LAYERS:
            qs, lf, kk, v, gm, o_mem = _a_in_proj(layer, h, pre_g, a_w_in[layer], a_lb_logits,
                                                  mem_k, mem_v, tiles_per_seq)
            o_main = _hgrn(qs, lf, kk, v, gm, a_out_norm_g[layer].reshape(1, HG_VAL_WIDTH), batch)
            h = _out_proj(o_main, o_mem, a_w_out[layer], post_g, h, f"a_out_proj_{layer}")
        else:
            j = layer - N_A_LAYERS
            if k_sh is None:
                k_sh, v_sh = _shared_kv(h, kv_norm_g.reshape(1, D_MODEL), w_kv_shared.astype(BF16), cos, sin)
            q, gm, o_mem = _b_in_proj(layer, h, pre_g, b_w_in[j], cos, sin, mem_k, mem_v, tiles_per_seq)
            o_main = _swa(b_sinks[j], q, k_sh, v_sh, gm, batch)
            h = _out_proj(o_main, o_mem, b_w_out[j], post_g, h, f"b_out_proj_{layer}")
    return h.reshape(batch, seq, D_MODEL)
```

```python
import functools

import jax
import jax.numpy as jnp
from jax import lax
from jax.experimental import pallas as pl
from jax.experimental.pallas import tpu as pltpu

D_MODEL = 1024
DEPTH = 4
N_A_LAYERS = DEPTH // 2

HG_HEADS = 8
HG_DK = 128
HG_DV = D_MODEL // HG_HEADS
HG_KEY_WIDTH = HG_HEADS * HG_DK
HG_VAL_WIDTH = HG_HEADS * HG_DV
HG_CHUNK = 64

SWA_Q_HEADS = 16
SWA_KV_HEADS = 4
SWA_GROUP = SWA_Q_HEADS // SWA_KV_HEADS
SWA_HEAD_DIM = 64
SWA_WIDTH = SWA_Q_HEADS * SWA_HEAD_DIM
SWA_KV_WIDTH = SWA_KV_HEADS * SWA_HEAD_DIM
WINDOW = 128

MEM_TOKENS = 256
MEM_HEADS = 4
MEM_HEAD_DIM = 128
MEM_WIDTH = MEM_HEADS * MEM_HEAD_DIM

ROPE_THETA = 10000.0
NORM_EPS = 1e-6

A_IN_WIDTH = 2 * HG_KEY_WIDTH + 2 * HG_VAL_WIDTH + 2 * MEM_WIDTH
B_IN_WIDTH = 2 * SWA_WIDTH + 2 * MEM_WIDTH
OUT_WIDTH = HG_VAL_WIDTH + MEM_WIDTH

LANES = 128
SUBLANES = 8
MASK_VALUE = -1e30
LOG2E = 1.4426950408889634

IN_TILE = 256
OUT_TILE = 512
HG_TILE = 256
SWA_TILE = 512
VMEM_LIMIT = 48 * 1024 * 1024

F32 = jnp.float32
BF16 = jnp.bfloat16

_NT = (((1,), (1,)), ((), ()))
_TN = (((0,), (0,)), ((), ()))


def _dot(a, b):
    return jnp.dot(a, b, preferred_element_type=F32)


def _dot_nt(a, b):
    return lax.dot_general(a, b, _NT, preferred_element_type=F32)


def _dot_tn(a, b):
    return lax.dot_general(a, b, _TN, preferred_element_type=F32)


def _rms_norm(x, g):
    ms = jnp.mean(x * x, axis=-1, keepdims=True)
    return x * lax.rsqrt(ms + NORM_EPS) * g


def _sigmoid(x):
    return 1.0 / (1.0 + jnp.exp(-x))


def _silu(x):
    return x * _sigmoid(x)


def _params(*semantics):
    return pltpu.CompilerParams(dimension_semantics=semantics, vmem_limit_bytes=VMEM_LIMIT)


def _rope_table_kernel(pos_ref, invf_ref, sign_ref, cos_ref, sin_ref):
    ang = pos_ref[...].astype(F32) * invf_ref[...]
    cos_ref[...] = jnp.cos(ang)
    sin_ref[...] = jnp.sin(ang) * sign_ref[...]


def _rope_tables(positions):
    n = positions.size
    half = SWA_HEAD_DIM // 2
    inv_freq = ROPE_THETA ** (-jnp.arange(0, SWA_HEAD_DIM, 2, dtype=F32) / SWA_HEAD_DIM)
    invf = jnp.tile(inv_freq, LANES // half).reshape(1, LANES)
    sign = jnp.tile(jnp.concatenate([-jnp.ones((half,), F32), jnp.ones((half,), F32)]),
                    LANES // SWA_HEAD_DIM).reshape(1, LANES)
    pos = jnp.broadcast_to(positions.reshape(n, 1), (n, LANES))
    tile = 2048
    row = pl.BlockSpec((tile, LANES), lambda i: (i, 0))
    vec = pl.BlockSpec((1, LANES), lambda i: (0, 0))
    return pl.pallas_call(
        _rope_table_kernel,
        grid=(n // tile,),
        in_specs=[row, vec, vec],
        out_specs=[row, row],
        out_shape=[jax.ShapeDtypeStruct((n, LANES), F32)] * 2,
        compiler_params=_params("parallel"),
        name="rope_tables",
    )(pos, invf, sign)


def _rope_block(xc, cos, sin, first_half):
    swapped = jnp.where(first_half,
                        pltpu.roll(xc, LANES - SWA_HEAD_DIM // 2, 1),
                        pltpu.roll(xc, SWA_HEAD_DIM // 2, 1))
    return xc * cos + swapped * sin


def _first_half_mask(rows):
    lane = lax.broadcasted_iota(jnp.int32, (rows, LANES), 1)
    return (lane & (SWA_HEAD_DIM // 2)) == 0


def _mem_kv_kernel(mem_ref, g_ref, w_ref, k_ref, v_ref):
    mn = _rms_norm(mem_ref[...], g_ref[...]).astype(BF16)
    kv = _dot(mn, w_ref[...])
    k_ref[...] = kv[:, :MEM_WIDTH].astype(BF16)
    v_ref[...] = kv[:, MEM_WIDTH:].astype(BF16)


def _mem_kv(mem, mem_norm_g, w_mem_kv):
    b = mem.shape[0]
    out = pl.BlockSpec((None, None, MEM_TOKENS, MEM_WIDTH), lambda l, i: (l, i, 0, 0))
    return pl.pallas_call(
        _mem_kv_kernel,
        grid=(DEPTH, b),
        in_specs=[
            pl.BlockSpec((None, MEM_TOKENS, D_MODEL), lambda l, i: (i, 0, 0)),
            pl.BlockSpec((None, 1, D_MODEL), lambda l, i: (l, 0, 0)),
            pl.BlockSpec((None, D_MODEL, 2 * MEM_WIDTH), lambda l, i: (l, 0, 0)),
        ],
        out_specs=[out, out],
        out_shape=[jax.ShapeDtypeStruct((DEPTH, b, MEM_TOKENS, MEM_WIDTH), BF16)] * 2,
        compiler_params=_params("arbitrary", "arbitrary"),
        name="mem_kv",
    )(mem, mem_norm_g.reshape(DEPTH, 1, D_MODEL), w_mem_kv)


def _memory_attention(mq, g_mem, mk_ref, mv_ref, out_ref):
    mq = (mq * (MEM_HEAD_DIM ** -0.5)).astype(BF16)
    gate = _silu(g_mem)
    for hh in range(MEM_HEADS):
        hs = slice(hh * MEM_HEAD_DIM, (hh + 1) * MEM_HEAD_DIM)
        s = _dot_nt(mq[:, hs], mk_ref[:, hs])
        p = jnp.exp(s - jnp.max(s, axis=-1, keepdims=True))
        o = _dot(p.astype(BF16), mv_ref[:, hs]) / jnp.sum(p, axis=-1, keepdims=True)
        out_ref[:, hs] = (o * gate[:, hs]).astype(BF16)


def _a_in_kernel(layer, x_ref, g_ref, w_ref, lbl_ref, mk_ref, mv_ref,
                 qs_ref, lf_ref, kk_ref, v_ref, gm_ref, om_ref):
    xn = _rms_norm(x_ref[...], g_ref[...]).astype(BF16)

    def proj(lo, width):
        return _dot(xn, w_ref[:, lo:lo + width])

    logits = lbl_ref[...]
    e = jnp.exp(logits - jnp.max(logits, axis=0, keepdims=True))
    lb = jnp.sum(e[:layer + 1], axis=0, keepdims=True) / jnp.sum(e, axis=0, keepdims=True)

    qs_ref[...] = _silu(proj(0, HG_KEY_WIDTH)).astype(BF16)

    fp = proj(HG_KEY_WIDTH, HG_KEY_WIDTH)
    t = jnp.exp(-jnp.abs(fp))
    r = 1.0 / (1.0 + t)
    pos = fp >= 0
    sig = jnp.where(pos, r, t * r)
    sig_neg = jnp.where(pos, t * r, r)
    lf_ref[...] = jnp.log2(lb + (1.0 - lb) * sig)
    kk_ref[...] = ((1.0 - lb) * sig_neg).astype(BF16)

    v_ref[...] = proj(2 * HG_KEY_WIDTH, HG_VAL_WIDTH).astype(BF16)
    gm_ref[...] = _silu(proj(2 * HG_KEY_WIDTH + HG_VAL_WIDTH, HG_VAL_WIDTH)).astype(BF16)

    base = 2 * HG_KEY_WIDTH + 2 * HG_VAL_WIDTH
    _memory_attention(proj(base, MEM_WIDTH), proj(base + MEM_WIDTH, MEM_WIDTH), mk_ref, mv_ref, om_ref)


def _a_in_proj(layer, h, g, w, lb_logits, mem_k, mem_v, tiles_per_seq):
    n = h.shape[0]
    row = lambda width: pl.BlockSpec((IN_TILE, width), lambda i: (i, 0))
    full = lambda a: pl.BlockSpec(a.shape, lambda i: (0,) * a.ndim)
    mem = pl.BlockSpec((None, None, MEM_TOKENS, MEM_WIDTH), lambda i: (layer, i // tiles_per_seq, 0, 0))
    widths = (HG_KEY_WIDTH, HG_KEY_WIDTH, HG_KEY_WIDTH, HG_VAL_WIDTH, HG_VAL_WIDTH, MEM_WIDTH)
    dtypes = (BF16, F32, BF16, BF16, BF16, BF16)
    return pl.pallas_call(
        functools.partial(_a_in_kernel, layer),
        grid=(n // IN_TILE,),
        in_specs=[row(D_MODEL), full(g), full(w), full(lb_logits), mem, mem],
        out_specs=[row(wd) for wd in widths],
        out_shape=[jax.ShapeDtypeStruct((n, wd), dt) for wd, dt in zip(widths, dtypes)],
        compiler_params=_params("parallel"),
        name=f"a_in_proj_{layer}",
    )(h, g, w, lb_logits, mem_k, mem_v)


_LEVELS = (32, 16, 8, 4, 2, 1)


def _coarse_level(cum, qs, kk, m):
    c, w = cum.shape
    zeros = jnp.zeros((m, w), F32)
    qd, kd = [], []
    for blk in range(c // (2 * m)):
        lo, mid, hi = blk * 2 * m, blk * 2 * m + m, (blk + 1) * 2 * m
        ref = cum[mid - 1:mid, :]
        kd += [kk[lo:mid] * jnp.exp2(ref - cum[lo:mid]), zeros]
        qd += [zeros, qs[mid:hi] * jnp.exp2(cum[mid:hi] - ref)]
    return jnp.concatenate(qd, axis=0).astype(BF16), jnp.concatenate(kd, axis=0).astype(BF16)


def _fine_reference(cum, m):
    c, w = cum.shape
    cum3 = cum.reshape(c // SUBLANES, SUBLANES, w)
    parts = [jnp.broadcast_to(cum3[:, blk * 2 * m + m - 1:blk * 2 * m + m, :], (c // SUBLANES, 2 * m, w))
             for blk in range(SUBLANES // (2 * m))]
    ref3 = parts[0] if len(parts) == 1 else jnp.concatenate(parts, axis=1)
    return ref3.reshape(c, w)


def _hgrn_kernel(qs_ref, lf_ref, kk_ref, v_ref, gm_ref, gn_ref, o_ref, st_ref):
    c = HG_CHUNK

    @pl.when(pl.program_id(1) == 0)
    def _():
        st_ref[...] = jnp.zeros_like(st_ref)

    row = lax.broadcasted_iota(jnp.int32, (c, c), 0)
    col = lax.broadcasted_iota(jnp.int32, (c, c), 1)
    tri = (col <= row).astype(BF16)
    same_block = {m: (row // (2 * m)) == (col // (2 * m)) for m in _LEVELS if 2 * m < c}
    diag = row == col
    trow = lax.broadcasted_iota(jnp.int32, (c, HG_DK), 0)
    upper = {m: (trow & m) != 0 for m in _LEVELS if m < SUBLANES}
    gn = gn_ref[...]

    def chunk(ci, carry):
        rows = pl.ds(pl.multiple_of(ci * c, c), c)

        lf_all = lf_ref[rows, :]
        hi = lf_all.astype(BF16)
        lo = (lf_all - hi.astype(F32)).astype(BF16)
        cum_all = _dot(tri, hi) + _dot(tri, lo)

        heads = [slice(h * HG_DK, (h + 1) * HG_DK) for h in range(HG_HEADS)]
        scores_bf, q_ins, k_outs, lasts = [], [], [], []
        for hs in heads:
            cum = cum_all[:, hs]
            qs = qs_ref[rows, hs].astype(F32)
            kk = kk_ref[rows, hs].astype(F32)

            pairs = []
            for m in _LEVELS:
                if m >= SUBLANES:
                    qd, kd = _coarse_level(cum, qs, kk, m)
                elif m == 1:
                    qd = jnp.where(upper[m], qs * jnp.exp2(lf_ref[rows, hs]), 0.0).astype(BF16)
                    kd = jnp.where(upper[m], 0.0, kk).astype(BF16)
                else:
                    x = cum - _fine_reference(cum, m)
                    e = jnp.exp2(jnp.where(upper[m], x, -x))
                    qd = jnp.where(upper[m], qs * e, 0.0).astype(BF16)
                    kd = jnp.where(upper[m], 0.0, kk * e).astype(BF16)
                pairs.append((qd, kd, same_block.get(m)))
            pairs.append((qs.astype(BF16), kk.astype(BF16), diag))

            scores = None
            for qd, kd, mask in pairs:
                s = _dot_nt(qd, kd)
                if mask is not None:
                    s = jnp.where(mask, s, 0.0)
                scores = s if scores is None else scores + s

            last = cum[c - 1:c, :]
            scores_bf.append(scores.astype(BF16))
            q_ins.append((qs * jnp.exp2(cum)).astype(BF16))
            k_outs.append((kk * jnp.exp2(last - cum)).astype(BF16))
            lasts.append(last)

        for h, hs in enumerate(heads):
            o = _dot_nt(q_ins[h], st_ref[h].astype(BF16)) + _dot(scores_bf[h], v_ref[rows, hs])
            o = o * lax.rsqrt(jnp.mean(o * o, axis=-1, keepdims=True) + NORM_EPS)
            o_ref[rows, hs] = (o * gn[:, hs] * gm_ref[rows, hs].astype(F32)).astype(BF16)

        for h, hs in enumerate(heads):
            st_ref[h] = st_ref[h] * jnp.exp2(lasts[h]) + _dot_tn(v_ref[rows, hs], k_outs[h])
        return carry

    lax.fori_loop(0, HG_TILE // c, chunk, 0)


def _hgrn(qs, lf, kk, v, gm, gn, batch):
    n = qs.shape[0]
    tiles = n // batch // HG_TILE
    row = pl.BlockSpec((HG_TILE, HG_KEY_WIDTH), lambda b, t: (b * tiles + t, 0))
    return pl.pallas_call(
        _hgrn_kernel,
        grid=(batch, tiles),
        in_specs=[row, row, row, row, row, pl.BlockSpec((1, HG_VAL_WIDTH), lambda b, t: (0, 0))],
        out_specs=row,
        out_shape=jax.ShapeDtypeStruct((n, HG_VAL_WIDTH), BF16),
        scratch_shapes=[pltpu.VMEM((HG_HEADS, HG_DV, HG_DK), F32)],
        compiler_params=_params("arbitrary", "arbitrary"),
        name="hgrn2",
    )(qs, lf, kk, v, gm, gn)


def _out_kernel(main_transposed, om_ref, ome_ref, w_ref, g_ref, h_ref, o_ref):
    main_dot = _dot_tn if main_transposed else _dot
    y = main_dot(om_ref[...], w_ref[:HG_VAL_WIDTH, :]) + _dot(ome_ref[...], w_ref[HG_VAL_WIDTH:, :])
    o_ref[...] = h_ref[...] + _rms_norm(y, g_ref[...])


def _out_proj(o_main, o_mem, w, g, h, name, main_transposed=False):
    n = h.shape[0]
    row = lambda width: pl.BlockSpec((OUT_TILE, width), lambda i: (i, 0))
    col = lambda width: pl.BlockSpec((width, OUT_TILE), lambda i: (0, i))
    full = lambda a: pl.BlockSpec(a.shape, lambda i: (0,) * a.ndim)
    main = col(HG_VAL_WIDTH) if main_transposed else row(HG_VAL_WIDTH)
    return pl.pallas_call(
        functools.partial(_out_kernel, main_transposed),
        grid=(n // OUT_TILE,),
        in_specs=[main, row(MEM_WIDTH), full(w), full(g), row(D_MODEL)],
        out_specs=row(D_MODEL),
        out_shape=jax.ShapeDtypeStruct((n, D_MODEL), F32),
        compiler_params=_params("parallel"),
        name=name,
    )(o_main, o_mem, w, g, h)


def _shared_kv_kernel(x_ref, g_ref, w_ref, cos_ref, sin_ref, k_ref, v_ref):
    xn = _rms_norm(x_ref[...], g_ref[...]).astype(BF16)
    kv = _dot(xn, w_ref[...])
    cos = cos_ref[...]
    sin = sin_ref[...]
    first = _first_half_mask(IN_TILE)
    heads_per_block = LANES // SWA_HEAD_DIM
    for cb in range(SWA_KV_WIDTH // LANES):
        kr = _rope_block(kv[:, cb * LANES:(cb + 1) * LANES], cos, sin, first).astype(BF16)
        for j in range(heads_per_block):
            k_ref[cb * heads_per_block + j] = kr[:, j * SWA_HEAD_DIM:(j + 1) * SWA_HEAD_DIM]
    v_ref[...] = kv[:, SWA_KV_WIDTH:].T.astype(BF16)


def _shared_kv(h, g, w, cos, sin):
    n = h.shape[0]
    row = lambda width: pl.BlockSpec((IN_TILE, width), lambda i: (i, 0))
    full = lambda a: pl.BlockSpec(a.shape, lambda i: (0,) * a.ndim)
    return pl.pallas_call(
        _shared_kv_kernel,
        grid=(n // IN_TILE,),
        in_specs=[row(D_MODEL), full(g), full(w), row(LANES), row(LANES)],
        out_specs=[pl.BlockSpec((SWA_KV_HEADS, IN_TILE, SWA_HEAD_DIM), lambda i: (0, i, 0)),
                   pl.BlockSpec((SWA_KV_WIDTH, IN_TILE), lambda i: (0, i))],
        out_shape=[jax.ShapeDtypeStruct((SWA_KV_HEADS, n, SWA_HEAD_DIM), BF16),
                   jax.ShapeDtypeStruct((SWA_KV_WIDTH, n), BF16)],
        compiler_params=_params("parallel"),
        name="shared_kv",
    )(h, g, w, cos, sin)


def _b_in_kernel(x_ref, g_ref, w_ref, cos_ref, sin_ref, mk_ref, mv_ref, qt_ref, gmt_ref, om_ref):
    xn = _rms_norm(x_ref[...], g_ref[...]).astype(BF16)

    def proj(lo, width):
        return _dot(xn, w_ref[:, lo:lo + width])

    q = proj(0, SWA_WIDTH)
    gate = _silu(proj(SWA_WIDTH, SWA_WIDTH))
    cos = cos_ref[...]
    sin = sin_ref[...]
    first = _first_half_mask(IN_TILE)
    scale = SWA_HEAD_DIM ** -0.5 * LOG2E
    for cb in range(SWA_WIDTH // LANES):
        cs = slice(cb * LANES, (cb + 1) * LANES)
        qt_ref[cs, :] = (_rope_block(q[:, cs], cos, sin, first) * scale).T.astype(BF16)
        gmt_ref[cs, :] = gate[:, cs].T.astype(BF16)

    base = 2 * SWA_WIDTH
    _memory_attention(proj(base, MEM_WIDTH), proj(base + MEM_WIDTH, MEM_WIDTH), mk_ref, mv_ref, om_ref)


def _b_in_proj(layer, h, g, w, cos, sin, mem_k, mem_v, tiles_per_seq):
    n = h.shape[0]
    row = lambda width: pl.BlockSpec((IN_TILE, width), lambda i: (i, 0))
    col = pl.BlockSpec((SWA_WIDTH, IN_TILE), lambda i: (0, i))
    full = lambda a: pl.BlockSpec(a.shape, lambda i: (0,) * a.ndim)
    mem = pl.BlockSpec((None, None, MEM_TOKENS, MEM_WIDTH), lambda i: (layer, i // tiles_per_seq, 0, 0))
    return pl.pallas_call(
        _b_in_kernel,
        grid=(n // IN_TILE,),
        in_specs=[row(D_MODEL), full(g), full(w), row(LANES), row(LANES), mem, mem],
        out_specs=[col, col, row(MEM_WIDTH)],
        out_shape=[jax.ShapeDtypeStruct((SWA_WIDTH, n), BF16), jax.ShapeDtypeStruct((SWA_WIDTH, n), BF16),
                   jax.ShapeDtypeStruct((n, MEM_WIDTH), BF16)],
        compiler_params=_params("parallel"),
        name=f"b_in_proj_{layer}",
    )(h, g, w, cos, sin, mem_k, mem_v)


def _swa_kernel(sink_ref, qt_ref, kc_ref, kp_ref, vtc_ref, vtp_ref, gmt_ref, ot_ref):
    blk = WINDOW
    dh = SWA_HEAD_DIM
    ki = lax.broadcasted_iota(jnp.int32, (2 * blk, blk), 0)
    qi = lax.broadcasted_iota(jnp.int32, (2 * blk, blk), 1)
    d = ki - qi
    band = (d > 0) & (d <= blk)
    first_key = jnp.where(pl.program_id(1) > 0, 0, blk)

    for sb in range(SWA_TILE // blk):
        cols = slice(sb * blk, (sb + 1) * blk)
        two = slice((sb - 1) * blk, (sb + 1) * blk)
        mask = band & (ki >= first_key) if sb == 0 else band

        probs, inv_den = [], []
        for g in range(SWA_KV_HEADS):
            k2 = jnp.concatenate([kp_ref[g], kc_ref[g, cols, :]], axis=0) if sb == 0 else kc_ref[g, two, :]
            heads = range(g * SWA_GROUP, (g + 1) * SWA_GROUP)
            qt = jnp.concatenate([qt_ref[h * dh:(h + 1) * dh, cols] for h in heads], axis=1)
            st = _dot(k2, qt)
            ps = []
            for j, h in enumerate(heads):
                s = jnp.where(mask, st[:, j * blk:(j + 1) * blk], MASK_VALUE)
                sink = sink_ref[h] * LOG2E
                m = jnp.maximum(jnp.max(s, axis=0, keepdims=True), sink)
                p = jnp.exp2(s - m)
                inv_den.append(1.0 / (jnp.sum(p, axis=0, keepdims=True) + jnp.exp2(sink - m)))
                ps.append(p.astype(BF16))
            probs.append(jnp.concatenate(ps, axis=1))

        for g in range(SWA_KV_HEADS):
            vrows = slice(g * dh, (g + 1) * dh)
            vt2 = (jnp.concatenate([vtp_ref[vrows, :], vtc_ref[vrows, cols]], axis=1) if sb == 0
                   else vtc_ref[vrows, two])
            ot = _dot(vt2, probs[g])
            for j, h in enumerate(range(g * SWA_GROUP, (g + 1) * SWA_GROUP)):
                rows = slice(h * dh, (h + 1) * dh)
                o = ot[:, j * blk:(j + 1) * blk] * inv_den[h]
                ot_ref[rows, cols] = (o * gmt_ref[rows, cols].astype(F32)).astype(BF16)


def _swa(sinks, qt, k, vt, gmt, batch):
    n = qt.shape[1]
    tiles = n // batch // SWA_TILE
    sub = SWA_TILE // WINDOW
    tile = lambda b, t: b * tiles + t
    prev = lambda b, t: jnp.maximum(tile(b, t) * sub - 1, 0)
    wide = pl.BlockSpec((SWA_WIDTH, SWA_TILE), lambda b, t: (0, tile(b, t)))
    return pl.pallas_call(
        _swa_kernel,
        grid=(batch, tiles),
        in_specs=[pl.BlockSpec(memory_space=pltpu.SMEM),
                  wide,
                  pl.BlockSpec((SWA_KV_HEADS, SWA_TILE, SWA_HEAD_DIM), lambda b, t: (0, tile(b, t), 0)),
                  pl.BlockSpec((SWA_KV_HEADS, WINDOW, SWA_HEAD_DIM), lambda b, t: (0, prev(b, t), 0)),
                  pl.BlockSpec((SWA_KV_WIDTH, SWA_TILE), lambda b, t: (0, tile(b, t))),
                  pl.BlockSpec((SWA_KV_WIDTH, WINDOW), lambda b, t: (0, prev(b, t))),
                  wide],
        out_specs=wide,
        out_shape=jax.ShapeDtypeStruct((SWA_WIDTH, n), BF16),
        compiler_params=_params("parallel", "arbitrary"),
        name="swa",
    )(sinks, qt, k, k, vt, vt, gmt)


def kernel(x, mem, positions, pre_norm_g, post_norm_g, mem_norm_g, w_mem_kv, a_w_in, a_lb_logits, a_out_norm_g, a_w_out, kv_norm_g, w_kv_shared, b_w_in, b_sinks, b_w_out):
    batch, seq, _ = x.shape
    n = batch * seq
    assert seq % max(IN_TILE, OUT_TILE, HG_TILE, SWA_TILE) == 0
    tiles_per_seq = seq // IN_TILE

    h = x.reshape(n, D_MODEL)
    mem_k, mem_v = _mem_kv(mem, mem_norm_g, w_mem_kv.astype(BF16))
    cos, sin = _rope_tables(positions)
    a_w_in = a_w_in.astype(BF16)
    a_w_out = a_w_out.astype(BF16)
    b_w_in = b_w_in.astype(BF16)
    b_w_out = b_w_out.astype(BF16)
    k_sh = v_sh = None

    for layer in range(DEPTH):
        pre_g = pre_norm_g[layer].reshape(1, D_MODEL)
        post_g = post_norm_g[layer].reshape(1, D_MODEL)
        if layer < N_A_LAYERS:
            qs, lf, kk, v, gm, o_mem = _a_in_proj(layer, h, pre_g, a_w_in[layer], a_lb_logits,
                                                  mem_k, mem_v, tiles_per_seq)
            o_main = _hgrn(qs, lf, kk, v, gm, a_out_norm_g[layer].reshape(1, HG_VAL_WIDTH), batch)
            h = _out_proj(o_main, o_mem, a_w_out[layer], post_g, h, f"a_out_proj_{layer}")
        else:
            j = layer - N_A_LAYERS
            if k_sh is None:
                k_sh, v_sh = _shared_kv(h, kv_norm_g.reshape(1, D_MODEL), w_kv_shared.astype(BF16), cos, sin)
            qt, gmt, o_mem = _b_in_proj(layer, h, pre_g, b_w_in[j], cos, sin, mem_k, mem_v, tiles_per_seq)
            o_main_t = _swa(b_sinks[j], qt, k_sh, v_sh, gmt, batch)
            h = _out_proj(o_main_t, o_mem, b_w_out[j], post_g, h, f"b_out_proj_{layer}", main_transposed=True)
    return h.reshape(batch, seq, D_MODEL)
```

```python
import functools

import jax
import jax.numpy as jnp
from jax import lax
from jax.experimental import pallas as pl
from jax.experimental.pallas import tpu as pltpu

D_MODEL = 1024
DEPTH = 4
N_A_LAYERS = DEPTH // 2

HG_HEADS = 8
HG_DK = 128
HG_DV = D_MODEL // HG_HEADS
HG_KEY_WIDTH = HG_HEADS * HG_DK
HG_VAL_WIDTH = HG_HEADS * HG_DV
HG_CHUNK = 64

SWA_Q_HEADS = 16
SWA_KV_HEADS = 4
SWA_GROUP = SWA_Q_HEADS // SWA_KV_HEADS
SWA_HEAD_DIM = 64
SWA_WIDTH = SWA_Q_HEADS * SWA_HEAD_DIM
SWA_KV_WIDTH = SWA_KV_HEADS * SWA_HEAD_DIM
WINDOW = 128

MEM_TOKENS = 256
MEM_HEADS = 4
MEM_HEAD_DIM = 128
MEM_WIDTH = MEM_HEADS * MEM_HEAD_DIM

ROPE_THETA = 10000.0
NORM_EPS = 1e-6

A_IN_WIDTH = 2 * HG_KEY_WIDTH + 2 * HG_VAL_WIDTH + 2 * MEM_WIDTH
B_IN_WIDTH = 2 * SWA_WIDTH + 2 * MEM_WIDTH
OUT_WIDTH = HG_VAL_WIDTH + MEM_WIDTH

LANES = 128
SUBLANES = 8
MASK_VALUE = -1e30
LOG2E = 1.4426950408889634

IN_TILE = 512
IN_SUB_TILE = 256
OUT_TILE = 512
HG_TILE = 256
SWA_TILE = 512
VMEM_LIMIT = 56 * 1024 * 1024

F32 = jnp.float32
BF16 = jnp.bfloat16

_NT = (((1,), (1,)), ((), ()))
_TN = (((0,), (0,)), ((), ()))


def _dot(a, b):
    return jnp.dot(a, b, preferred_element_type=F32)


def _dot_nt(a, b):
    return lax.dot_general(a, b, _NT, preferred_element_type=F32)


def _dot_tn(a, b):
    return lax.dot_general(a, b, _TN, preferred_element_type=F32)


def _rms_norm(x, g):
    ms = jnp.mean(x * x, axis=-1, keepdims=True)
    return x * lax.rsqrt(ms + NORM_EPS) * g


def _sigmoid(x):
    return 1.0 / (1.0 + jnp.exp(-x))


def _silu(x):
    return x * _sigmoid(x)


def _params(*semantics):
    return pltpu.CompilerParams(dimension_semantics=semantics, vmem_limit_bytes=VMEM_LIMIT)


def _rope_table_kernel(pos_ref, invf_ref, sign_ref, cos_ref, sin_ref):
    ang = pos_ref[...].astype(F32) * invf_ref[...]
    cos_ref[...] = jnp.cos(ang)
    sin_ref[...] = jnp.sin(ang) * sign_ref[...]


def _rope_tables(positions):
    n = positions.size
    half = SWA_HEAD_DIM // 2
    inv_freq = ROPE_THETA ** (-jnp.arange(0, SWA_HEAD_DIM, 2, dtype=F32) / SWA_HEAD_DIM)
    invf = jnp.tile(inv_freq, LANES // half).reshape(1, LANES)
    sign = jnp.tile(jnp.concatenate([-jnp.ones((half,), F32), jnp.ones((half,), F32)]),
                    LANES // SWA_HEAD_DIM).reshape(1, LANES)
    pos = jnp.broadcast_to(positions.reshape(n, 1), (n, LANES))
    tile = 2048
    row = pl.BlockSpec((tile, LANES), lambda i: (i, 0))
    vec = pl.BlockSpec((1, LANES), lambda i: (0, 0))
    return pl.pallas_call(
        _rope_table_kernel,
        grid=(n // tile,),
        in_specs=[row, vec, vec],
        out_specs=[row, row],
        out_shape=[jax.ShapeDtypeStruct((n, LANES), F32)] * 2,
        compiler_params=_params("parallel"),
        name="rope_tables",
    )(pos, invf, sign)


def _rope_block(xc, cos, sin, first_half):
    swapped = jnp.where(first_half,
                        pltpu.roll(xc, LANES - SWA_HEAD_DIM // 2, 1),
                        pltpu.roll(xc, SWA_HEAD_DIM // 2, 1))
    return xc * cos + swapped * sin


def _first_half_mask(rows):
    lane = lax.broadcasted_iota(jnp.int32, (rows, LANES), 1)
    return (lane & (SWA_HEAD_DIM // 2)) == 0


def _mem_kv_kernel(mem_ref, g_ref, w_ref, k_ref, v_ref):
    mn = _rms_norm(mem_ref[...], g_ref[...]).astype(BF16)
    kv = _dot(mn, w_ref[...])
    k_ref[...] = kv[:, :MEM_WIDTH].astype(BF16)
    v_ref[...] = kv[:, MEM_WIDTH:].astype(BF16)


def _mem_kv(mem, mem_norm_g, w_mem_kv):
    b = mem.shape[0]
    out = pl.BlockSpec((None, None, MEM_TOKENS, MEM_WIDTH), lambda l, i: (l, i, 0, 0))
    return pl.pallas_call(
        _mem_kv_kernel,
        grid=(DEPTH, b),
        in_specs=[
            pl.BlockSpec((None, MEM_TOKENS, D_MODEL), lambda l, i: (i, 0, 0)),
            pl.BlockSpec((None, 1, D_MODEL), lambda l, i: (l, 0, 0)),
            pl.BlockSpec((None, D_MODEL, 2 * MEM_WIDTH), lambda l, i: (l, 0, 0)),
        ],
        out_specs=[out, out],
        out_shape=[jax.ShapeDtypeStruct((DEPTH, b, MEM_TOKENS, MEM_WIDTH), BF16)] * 2,
        compiler_params=_params("arbitrary", "arbitrary"),
        name="mem_kv",
    )(mem, mem_norm_g.reshape(DEPTH, 1, D_MODEL), w_mem_kv)


def _memory_probs(mq, mk_ref):
    mq = (mq * (MEM_HEAD_DIM ** -0.5)).astype(BF16)
    probs, inv_sums = [], []
    for hh in range(MEM_HEADS):
        hs = slice(hh * MEM_HEAD_DIM, (hh + 1) * MEM_HEAD_DIM)
        s = _dot_nt(mq[:, hs], mk_ref[:, hs])
        p = jnp.exp(s - jnp.max(s, axis=-1, keepdims=True))
        probs.append(p.astype(BF16))
        inv_sums.append(1.0 / jnp.sum(p, axis=-1, keepdims=True))
    return probs, inv_sums


def _memory_readout(probs, inv_sums, gate, mv_ref, out_ref, rows):
    for hh in range(MEM_HEADS):
        hs = slice(hh * MEM_HEAD_DIM, (hh + 1) * MEM_HEAD_DIM)
        o = _dot(probs[hh], mv_ref[:, hs]) * inv_sums[hh]
        out_ref[rows, hs] = (o * gate[:, hs]).astype(BF16)


def _a_in_kernel(layer, x_ref, g_ref, w_ref, lbl_ref, mk_ref, mv_ref,
                 qs_ref, lf_ref, kk_ref, v_ref, gm_ref, om_ref):
    logits = lbl_ref[...]
    e = jnp.exp(logits - jnp.max(logits, axis=0, keepdims=True))
    lb = jnp.sum(e[:layer + 1], axis=0, keepdims=True) / jnp.sum(e, axis=0, keepdims=True)

    col_q, col_f, col_v = 0, HG_KEY_WIDTH, 2 * HG_KEY_WIDTH
    col_g = col_v + HG_VAL_WIDTH
    col_mq = col_g + HG_VAL_WIDTH
    col_mg = col_mq + MEM_WIDTH

    for sub in range(IN_TILE // IN_SUB_TILE):
        rows = slice(sub * IN_SUB_TILE, (sub + 1) * IN_SUB_TILE)
        xn = _rms_norm(x_ref[rows, :], g_ref[...]).astype(BF16)

        def proj(lo, width):
            return _dot(xn, w_ref[:, lo:lo + width])

        mq = proj(col_mq, MEM_WIDTH)
        mem_gate = _silu(proj(col_mg, MEM_WIDTH))
        qs_ref[rows, :] = _silu(proj(col_q, HG_KEY_WIDTH)).astype(BF16)
        probs, inv_sums = _memory_probs(mq, mk_ref)

        fp = proj(col_f, HG_KEY_WIDTH)
        t = jnp.exp(-jnp.abs(fp))
        r = 1.0 / (1.0 + t)
        pos = fp >= 0
        sig = jnp.where(pos, r, t * r)
        sig_neg = jnp.where(pos, t * r, r)
        lf_ref[rows, :] = jnp.log2(lb + (1.0 - lb) * sig)
        kk_ref[rows, :] = ((1.0 - lb) * sig_neg).astype(BF16)

        _memory_readout(probs, inv_sums, mem_gate, mv_ref, om_ref, rows)
        v_ref[rows, :] = proj(col_v, HG_VAL_WIDTH).astype(BF16)
        gm_ref[rows, :] = _silu(proj(col_g, HG_VAL_WIDTH)).astype(BF16)


def _layer_weight(w, layer):
    return pl.BlockSpec((None,) + w.shape[1:], lambda i: (layer, 0, 0))


def _a_in_proj(layer, h, g, w, lb_logits, mem_k, mem_v, tiles_per_seq):
    n = h.shape[0]
    row = lambda width: pl.BlockSpec((IN_TILE, width), lambda i: (i, 0))
    full = lambda a: pl.BlockSpec(a.shape, lambda i: (0,) * a.ndim)
    mem = pl.BlockSpec((None, None, MEM_TOKENS, MEM_WIDTH), lambda i: (layer, i // tiles_per_seq, 0, 0))
    widths = (HG_KEY_WIDTH, HG_KEY_WIDTH, HG_KEY_WIDTH, HG_VAL_WIDTH, HG_VAL_WIDTH, MEM_WIDTH)
    dtypes = (BF16, F32, BF16, BF16, BF16, BF16)
    return pl.pallas_call(
        functools.partial(_a_in_kernel, layer),
        grid=(n // IN_TILE,),
        in_specs=[row(D_MODEL), full(g), _layer_weight(w, layer), full(lb_logits), mem, mem],
        out_specs=[row(wd) for wd in widths],
        out_shape=[jax.ShapeDtypeStruct((n, wd), dt) for wd, dt in zip(widths, dtypes)],
        compiler_params=_params("parallel"),
        name=f"a_in_proj_{layer}",
    )(h, g, w, lb_logits, mem_k, mem_v)


_LEVELS = (32, 16, 8, 4, 2, 1)


def _coarse_level(cum, qs, kk, m):
    c, w = cum.shape
    zeros = jnp.zeros((m, w), F32)
    qd, kd = [], []
    for blk in range(c // (2 * m)):
        lo, mid, hi = blk * 2 * m, blk * 2 * m + m, (blk + 1) * 2 * m
        ref = cum[mid - 1:mid, :]
        kd += [kk[lo:mid] * jnp.exp2(ref - cum[lo:mid]), zeros]
        qd += [zeros, qs[mid:hi] * jnp.exp2(cum[mid:hi] - ref)]
    return jnp.concatenate(qd, axis=0).astype(BF16), jnp.concatenate(kd, axis=0).astype(BF16)


def _fine_reference(cum, m):
    c, w = cum.shape
    cum3 = cum.reshape(c // SUBLANES, SUBLANES, w)
    parts = [jnp.broadcast_to(cum3[:, blk * 2 * m + m - 1:blk * 2 * m + m, :], (c // SUBLANES, 2 * m, w))
             for blk in range(SUBLANES // (2 * m))]
    ref3 = parts[0] if len(parts) == 1 else jnp.concatenate(parts, axis=1)
    return ref3.reshape(c, w)


def _hgrn_kernel(qs_ref, lf_ref, kk_ref, v_ref, gm_ref, gn_ref, o_ref, st_ref):
    c = HG_CHUNK

    @pl.when(pl.program_id(1) == 0)
    def _():
        st_ref[...] = jnp.zeros_like(st_ref)

    row = lax.broadcasted_iota(jnp.int32, (c, c), 0)
    col = lax.broadcasted_iota(jnp.int32, (c, c), 1)
    tri = (col <= row).astype(BF16)
    same_block = {m: (row // (2 * m)) == (col // (2 * m)) for m in _LEVELS if 2 * m < c}
    diag = row == col
    trow = lax.broadcasted_iota(jnp.int32, (c, HG_DK), 0)
    upper = {m: (trow & m) != 0 for m in _LEVELS if m < SUBLANES}
    gn = gn_ref[...]

    def chunk(ci, carry):
        rows = pl.ds(pl.multiple_of(ci * c, c), c)

        lf_all = lf_ref[rows, :]
        hi = lf_all.astype(BF16)
        lo = (lf_all - hi.astype(F32)).astype(BF16)
        cum_all = _dot(tri, hi) + _dot(tri, lo)

        heads = [slice(h * HG_DK, (h + 1) * HG_DK) for h in range(HG_HEADS)]
        scores_bf, q_ins, k_outs, lasts = [], [], [], []
        for hs in heads:
            cum = cum_all[:, hs]
            qs = qs_ref[rows, hs].astype(F32)
            kk = kk_ref[rows, hs].astype(F32)

            pairs = []
            for m in _LEVELS:
                if m >= SUBLANES:
                    qd, kd = _coarse_level(cum, qs, kk, m)
                elif m == 1:
                    qd = jnp.where(upper[m], qs * jnp.exp2(lf_ref[rows, hs]), 0.0).astype(BF16)
                    kd = jnp.where(upper[m], 0.0, kk).astype(BF16)
                else:
                    x = cum - _fine_reference(cum, m)
                    e = jnp.exp2(jnp.where(upper[m], x, -x))
                    qd = jnp.where(upper[m], qs * e, 0.0).astype(BF16)
                    kd = jnp.where(upper[m], 0.0, kk * e).astype(BF16)
                pairs.append((qd, kd, same_block.get(m)))
            pairs.append((qs.astype(BF16), kk.astype(BF16), diag))

            scores = None
            for qd, kd, mask in pairs:
                s = _dot_nt(qd, kd)
                if mask is not None:
                    s = jnp.where(mask, s, 0.0)
                scores = s if scores is None else scores + s

            last = cum[c - 1:c, :]
            scores_bf.append(scores.astype(BF16))
            q_ins.append((qs * jnp.exp2(cum)).astype(BF16))
            k_outs.append((kk * jnp.exp2(last - cum)).astype(BF16))
            lasts.append(last)

        for h, hs in enumerate(heads):
            o = _dot_nt(q_ins[h], st_ref[h].astype(BF16)) + _dot(scores_bf[h], v_ref[rows, hs])
            o = o * lax.rsqrt(jnp.mean(o * o, axis=-1, keepdims=True) + NORM_EPS)
            o_ref[rows, hs] = (o * gn[:, hs] * gm_ref[rows, hs].astype(F32)).astype(BF16)

        for h, hs in enumerate(heads):
            st_ref[h] = st_ref[h] * jnp.exp2(lasts[h]) + _dot_tn(v_ref[rows, hs], k_outs[h])
        return carry

    lax.fori_loop(0, HG_TILE // c, chunk, 0)


def _hgrn(qs, lf, kk, v, gm, gn, batch):
    n = qs.shape[0]
    tiles = n // batch // HG_TILE
    row = pl.BlockSpec((HG_TILE, HG_KEY_WIDTH), lambda b, t: (b * tiles + t, 0))
    return pl.pallas_call(
        _hgrn_kernel,
        grid=(batch, tiles),
        in_specs=[row, row, row, row, row, pl.BlockSpec((1, HG_VAL_WIDTH), lambda b, t: (0, 0))],
        out_specs=row,
        out_shape=jax.ShapeDtypeStruct((n, HG_VAL_WIDTH), BF16),
        scratch_shapes=[pltpu.VMEM((HG_HEADS, HG_DV, HG_DK), F32)],
        compiler_params=_params("arbitrary", "arbitrary"),
        name="hgrn2",
    )(qs, lf, kk, v, gm, gn)


def _out_kernel(main_transposed, om_ref, ome_ref, w_ref, g_ref, h_ref, o_ref):
    main_dot = _dot_tn if main_transposed else _dot
    y = main_dot(om_ref[...], w_ref[:HG_VAL_WIDTH, :]) + _dot(ome_ref[...], w_ref[HG_VAL_WIDTH:, :])
    o_ref[...] = h_ref[...] + _rms_norm(y, g_ref[...])


def _out_proj(o_main, o_mem, w, layer, g, h, name, main_transposed=False):
    n = h.shape[0]
    row = lambda width: pl.BlockSpec((OUT_TILE, width), lambda i: (i, 0))
    col = lambda width: pl.BlockSpec((width, OUT_TILE), lambda i: (0, i))
    full = lambda a: pl.BlockSpec(a.shape, lambda i: (0,) * a.ndim)
    main = col(HG_VAL_WIDTH) if main_transposed else row(HG_VAL_WIDTH)
    return pl.pallas_call(
        functools.partial(_out_kernel, main_transposed),
        grid=(n // OUT_TILE,),
        in_specs=[main, row(MEM_WIDTH), _layer_weight(w, layer), full(g), row(D_MODEL)],
        out_specs=row(D_MODEL),
        out_shape=jax.ShapeDtypeStruct((n, D_MODEL), F32),
        compiler_params=_params("parallel"),
        name=name,
    )(o_main, o_mem, w, g, h)


def _shared_kv_kernel(x_ref, g_ref, w_ref, cos_ref, sin_ref, k_ref, v_ref):
    xn = _rms_norm(x_ref[...], g_ref[...]).astype(BF16)
    kv = _dot(xn, w_ref[...])
    cos = cos_ref[...]
    sin = sin_ref[...]
    first = _first_half_mask(IN_TILE)
    heads_per_block = LANES // SWA_HEAD_DIM
    for cb in range(SWA_KV_WIDTH // LANES):
        kr = _rope_block(kv[:, cb * LANES:(cb + 1) * LANES], cos, sin, first).astype(BF16)
        for j in range(heads_per_block):
            k_ref[cb * heads_per_block + j] = kr[:, j * SWA_HEAD_DIM:(j + 1) * SWA_HEAD_DIM]
    v_ref[...] = kv[:, SWA_KV_WIDTH:].T.astype(BF16)


def _shared_kv(h, g, w, cos, sin):
    n = h.shape[0]
    row = lambda width: pl.BlockSpec((IN_TILE, width), lambda i: (i, 0))
    full = lambda a: pl.BlockSpec(a.shape, lambda i: (0,) * a.ndim)
    return pl.pallas_call(
        _shared_kv_kernel,
        grid=(n // IN_TILE,),
        in_specs=[row(D_MODEL), full(g), full(w), row(LANES), row(LANES)],
        out_specs=[pl.BlockSpec((SWA_KV_HEADS, IN_TILE, SWA_HEAD_DIM), lambda i: (0, i, 0)),
                   pl.BlockSpec((SWA_KV_WIDTH, IN_TILE), lambda i: (0, i))],
        out_shape=[jax.ShapeDtypeStruct((SWA_KV_HEADS, n, SWA_HEAD_DIM), BF16),
                   jax.ShapeDtypeStruct((SWA_KV_WIDTH, n), BF16)],
        compiler_params=_params("parallel"),
        name="shared_kv",
    )(h, g, w, cos, sin)


def _b_in_kernel(x_ref, g_ref, w_ref, cos_ref, sin_ref, mk_ref, mv_ref, qt_ref, gmt_ref, om_ref):
    first = _first_half_mask(IN_SUB_TILE)
    scale = SWA_HEAD_DIM ** -0.5 * LOG2E
    col_mq = 2 * SWA_WIDTH
    col_mg = col_mq + MEM_WIDTH

    for sub in range(IN_TILE // IN_SUB_TILE):
        rows = slice(sub * IN_SUB_TILE, (sub + 1) * IN_SUB_TILE)
        xn = _rms_norm(x_ref[rows, :], g_ref[...]).astype(BF16)

        def proj(lo, width):
            return _dot(xn, w_ref[:, lo:lo + width])

        mq = proj(col_mq, MEM_WIDTH)
        mem_gate = _silu(proj(col_mg, MEM_WIDTH))

        q = proj(0, SWA_WIDTH)
        probs, inv_sums = _memory_probs(mq, mk_ref)
        cos = cos_ref[rows, :]
        sin = sin_ref[rows, :]
        for cb in range(SWA_WIDTH // LANES):
            cs = slice(cb * LANES, (cb + 1) * LANES)
            qt_ref[cs, rows] = (_rope_block(q[:, cs], cos, sin, first) * scale).T.astype(BF16)

        gate = _silu(proj(SWA_WIDTH, SWA_WIDTH))
        _memory_readout(probs, inv_sums, mem_gate, mv_ref, om_ref, rows)
        for cb in range(SWA_WIDTH // LANES):
            cs = slice(cb * LANES, (cb + 1) * LANES)
            gmt_ref[cs, rows] = gate[:, cs].T.astype(BF16)


def _b_in_proj(layer, h, g, w, cos, sin, mem_k, mem_v, tiles_per_seq):
    n = h.shape[0]
    row = lambda width: pl.BlockSpec((IN_TILE, width), lambda i: (i, 0))
    col = pl.BlockSpec((SWA_WIDTH, IN_TILE), lambda i: (0, i))
    full = lambda a: pl.BlockSpec(a.shape, lambda i: (0,) * a.ndim)
    mem = pl.BlockSpec((None, None, MEM_TOKENS, MEM_WIDTH), lambda i: (layer, i // tiles_per_seq, 0, 0))
    return pl.pallas_call(
        _b_in_kernel,
        grid=(n // IN_TILE,),
        in_specs=[row(D_MODEL), full(g), _layer_weight(w, layer - N_A_LAYERS), row(LANES), row(LANES), mem, mem],
        out_specs=[col, col, row(MEM_WIDTH)],
        out_shape=[jax.ShapeDtypeStruct((SWA_WIDTH, n), BF16), jax.ShapeDtypeStruct((SWA_WIDTH, n), BF16),
                   jax.ShapeDtypeStruct((n, MEM_WIDTH), BF16)],
        compiler_params=_params("parallel"),
        name=f"b_in_proj_{layer}",
    )(h, g, w, cos, sin, mem_k, mem_v)


def _swa_kernel(sink_ref, qt_ref, kc_ref, kp_ref, vtc_ref, vtp_ref, gmt_ref, ot_ref):
    blk = WINDOW
    dh = SWA_HEAD_DIM
    ki = lax.broadcasted_iota(jnp.int32, (2 * blk, blk), 0)
    qi = lax.broadcasted_iota(jnp.int32, (2 * blk, blk), 1)
    d = ki - qi
    band = (d > 0) & (d <= blk)
    first_key = jnp.where(pl.program_id(1) > 0, 0, blk)

    for sb in range(SWA_TILE // blk):
        cols = slice(sb * blk, (sb + 1) * blk)
        two = slice((sb - 1) * blk, (sb + 1) * blk)
        mask = band & (ki >= first_key) if sb == 0 else band

        probs, inv_den = [], []
        for g in range(SWA_KV_HEADS):
            k2 = jnp.concatenate([kp_ref[g], kc_ref[g, cols, :]], axis=0) if sb == 0 else kc_ref[g, two, :]
            heads = range(g * SWA_GROUP, (g + 1) * SWA_GROUP)
            qt = jnp.concatenate([qt_ref[h * dh:(h + 1) * dh, cols] for h in heads], axis=1)
            st = _dot(k2, qt)
            ps = []
            for j, h in enumerate(heads):
                s = jnp.where(mask, st[:, j * blk:(j + 1) * blk], MASK_VALUE)
                sink = sink_ref[h] * LOG2E
                m = jnp.maximum(jnp.max(s, axis=0, keepdims=True), sink)
                p = jnp.exp2(s - m)
                inv_den.append(1.0 / (jnp.sum(p, axis=0, keepdims=True) + jnp.exp2(sink - m)))
                ps.append(p.astype(BF16))
            probs.append(jnp.concatenate(ps, axis=1))

        for g in range(SWA_KV_HEADS):
            vrows = slice(g * dh, (g + 1) * dh)
            vt2 = (jnp.concatenate([vtp_ref[vrows, :], vtc_ref[vrows, cols]], axis=1) if sb == 0
                   else vtc_ref[vrows, two])
            ot = _dot(vt2, probs[g])
            for j, h in enumerate(range(g * SWA_GROUP, (g + 1) * SWA_GROUP)):
                rows = slice(h * dh, (h + 1) * dh)
                o = ot[:, j * blk:(j + 1) * blk] * inv_den[h]
                ot_ref[rows, cols] = (o * gmt_ref[rows, cols].astype(F32)).astype(BF16)


def _swa(sinks, qt, k, vt, gmt, batch):
    n = qt.shape[1]
    tiles = n // batch // SWA_TILE
    sub = SWA_TILE // WINDOW
    tile = lambda b, t: b * tiles + t
    prev = lambda b, t: jnp.maximum(tile(b, t) * sub - 1, 0)
    wide = pl.BlockSpec((SWA_WIDTH, SWA_TILE), lambda b, t: (0, tile(b, t)))
    return pl.pallas_call(
        _swa_kernel,
        grid=(batch, tiles),
        in_specs=[pl.BlockSpec(memory_space=pltpu.SMEM),
                  wide,
                  pl.BlockSpec((SWA_KV_HEADS, SWA_TILE, SWA_HEAD_DIM), lambda b, t: (0, tile(b, t), 0)),
                  pl.BlockSpec((SWA_KV_HEADS, WINDOW, SWA_HEAD_DIM), lambda b, t: (0, prev(b, t), 0)),
                  pl.BlockSpec((SWA_KV_WIDTH, SWA_TILE), lambda b, t: (0, tile(b, t))),
                  pl.BlockSpec((SWA_KV_WIDTH, WINDOW), lambda b, t: (0, prev(b, t))),
                  wide],
        out_specs=wide,
        out_shape=jax.ShapeDtypeStruct((SWA_WIDTH, n), BF16),
        compiler_params=_params("parallel", "arbitrary"),
        name="swa",
    )(sinks, qt, k, k, vt, vt, gmt)


def kernel(x, mem, positions, pre_norm_g, post_norm_g, mem_norm_g, w_mem_kv, a_w_in, a_lb_logits, a_out_norm_g, a_w_out, kv_norm_g, w_kv_shared, b_w_in, b_sinks, b_w_out):
    batch, seq, _ = x.shape
    n = batch * seq
    assert seq % max(IN_TILE, OUT_TILE, HG_TILE, SWA_TILE) == 0
    tiles_per_seq = seq // IN_TILE

    h = x.reshape(n, D_MODEL)
    mem_k, mem_v = _mem_kv(mem, mem_norm_g, w_mem_kv.astype(BF16))
    cos, sin = _rope_tables(positions)
    a_w_in = a_w_in.astype(BF16)
    a_w_out = a_w_out.astype(BF16)
    b_w_in = b_w_in.astype(BF16)
    b_w_out = b_w_out.astype(BF16)
    k_sh = v_sh = None

    for layer in range(DEPTH):
        pre_g = pre_norm_g[layer].reshape(1, D_MODEL)
        post_g = post_norm_g[layer].reshape(1, D_MODEL)
        if layer < N_A_LAYERS:
            qs, lf, kk, v, gm, o_mem = _a_in_proj(layer, h, pre_g, a_w_in, a_lb_logits,
                                                  mem_k, mem_v, tiles_per_seq)
            o_main = _hgrn(qs, lf, kk, v, gm, a_out_norm_g[layer].reshape(1, HG_VAL_WIDTH), batch)
            h = _out_proj(o_main, o_mem, a_w_out, layer, post_g, h, f"a_out_proj_{layer}")
        else:
            j = layer - N_A_LAYERS
            if k_sh is None:
                k_sh, v_sh = _shared_kv(h, kv_norm_g.reshape(1, D_MODEL), w_kv_shared.astype(BF16), cos, sin)
            qt, gmt, o_mem = _b_in_proj(layer, h, pre_g, b_w_in, cos, sin, mem_k, mem_v, tiles_per_seq)
            o_main_t = _swa(b_sinks[j], qt, k_sh, v_sh, gmt, batch)
            h = _out_proj(o_main_t, o_mem, b_w_out, j, post_g, h, f"b_out_proj_{layer}", main_transposed=True)
    return h.reshape(batch, seq, D_MODEL)
```

```python
import functools

import jax
import jax.numpy as jnp
from jax import lax
from jax.experimental import pallas as pl
from jax.experimental.pallas import tpu as pltpu

D_MODEL = 1024
DEPTH = 4
N_A_LAYERS = DEPTH // 2

HG_HEADS = 8
HG_DK = 128
HG_DV = D_MODEL // HG_HEADS
HG_KEY_WIDTH = HG_HEADS * HG_DK
HG_VAL_WIDTH = HG_HEADS * HG_DV
HG_CHUNK = 64

SWA_Q_HEADS = 16
SWA_KV_HEADS = 4
SWA_GROUP = SWA_Q_HEADS // SWA_KV_HEADS
SWA_HEAD_DIM = 64
SWA_WIDTH = SWA_Q_HEADS * SWA_HEAD_DIM
SWA_KV_WIDTH = SWA_KV_HEADS * SWA_HEAD_DIM
WINDOW = 128

MEM_TOKENS = 256
MEM_HEADS = 4
MEM_HEAD_DIM = 128
MEM_WIDTH = MEM_HEADS * MEM_HEAD_DIM

ROPE_THETA = 10000.0
NORM_EPS = 1e-6

A_IN_WIDTH = 2 * HG_KEY_WIDTH + 2 * HG_VAL_WIDTH + 2 * MEM_WIDTH
B_IN_WIDTH = 2 * SWA_WIDTH + 2 * MEM_WIDTH
OUT_WIDTH = HG_VAL_WIDTH + MEM_WIDTH

LANES = 128
SUBLANES = 8
MASK_VALUE = -1e30
LOG2E = 1.4426950408889634

IN_TILE = 512
IN_SUB_TILE = 256
OUT_TILE = 512
HG_TILE = 256
SWA_TILE = 512
VMEM_LIMIT = 56 * 1024 * 1024

F32 = jnp.float32
BF16 = jnp.bfloat16

_NT = (((1,), (1,)), ((), ()))
_TN = (((0,), (0,)), ((), ()))


def _dot(a, b):
    return jnp.dot(a, b, preferred_element_type=F32)


def _dot_nt(a, b):
    return lax.dot_general(a, b, _NT, preferred_element_type=F32)


def _dot_tn(a, b):
    return lax.dot_general(a, b, _TN, preferred_element_type=F32)


def _rms_norm(x, g):
    ms = jnp.mean(x * x, axis=-1, keepdims=True)
    return x * lax.rsqrt(ms + NORM_EPS) * g


def _sigmoid(x):
    return 1.0 / (1.0 + jnp.exp(-x))


def _silu(x):
    return x * _sigmoid(x)


def _params(*semantics):
    return pltpu.CompilerParams(dimension_semantics=semantics, vmem_limit_bytes=VMEM_LIMIT)


def _rope_table_kernel(pos_ref, invf_ref, sign_ref, cos_ref, sin_ref):
    ang = pos_ref[...].astype(F32) * invf_ref[...]
    cos_ref[...] = jnp.cos(ang)
    sin_ref[...] = jnp.sin(ang) * sign_ref[...]


def _rope_tables(positions):
    n = positions.size
    half = SWA_HEAD_DIM // 2
    inv_freq = ROPE_THETA ** (-jnp.arange(0, SWA_HEAD_DIM, 2, dtype=F32) / SWA_HEAD_DIM)
    invf = jnp.tile(inv_freq, LANES // half).reshape(1, LANES)
    sign = jnp.tile(jnp.concatenate([-jnp.ones((half,), F32), jnp.ones((half,), F32)]),
                    LANES // SWA_HEAD_DIM).reshape(1, LANES)
    pos = jnp.broadcast_to(positions.reshape(n, 1), (n, LANES))
    tile = 2048
    row = pl.BlockSpec((tile, LANES), lambda i: (i, 0))
    vec = pl.BlockSpec((1, LANES), lambda i: (0, 0))
    return pl.pallas_call(
        _rope_table_kernel,
        grid=(n // tile,),
        in_specs=[row, vec, vec],
        out_specs=[row, row],
        out_shape=[jax.ShapeDtypeStruct((n, LANES), F32)] * 2,
        compiler_params=_params("parallel"),
        name="rope_tables",
    )(pos, invf, sign)


def _rope_block(xc, cos, sin, first_half):
    swapped = jnp.where(first_half,
                        pltpu.roll(xc, LANES - SWA_HEAD_DIM // 2, 1),
                        pltpu.roll(xc, SWA_HEAD_DIM // 2, 1))
    return xc * cos + swapped * sin


def _first_half_mask(rows):
    lane = lax.broadcasted_iota(jnp.int32, (rows, LANES), 1)
    return (lane & (SWA_HEAD_DIM // 2)) == 0


def _mem_kv_kernel(mem_ref, g_ref, w_ref, k_ref, v_ref):
    mn = _rms_norm(mem_ref[...], g_ref[...]).astype(BF16)
    kv = _dot(mn, w_ref[...])
    k_ref[...] = kv[:, :MEM_WIDTH].astype(BF16)
    v_ref[...] = kv[:, MEM_WIDTH:].astype(BF16)


def _mem_kv(mem, mem_norm_g, w_mem_kv):
    b = mem.shape[0]
    out = pl.BlockSpec((None, None, MEM_TOKENS, MEM_WIDTH), lambda l, i: (l, i, 0, 0))
    return pl.pallas_call(
        _mem_kv_kernel,
        grid=(DEPTH, b),
        in_specs=[
            pl.BlockSpec((None, MEM_TOKENS, D_MODEL), lambda l, i: (i, 0, 0)),
            pl.BlockSpec((None, 1, D_MODEL), lambda l, i: (l, 0, 0)),
            pl.BlockSpec((None, D_MODEL, 2 * MEM_WIDTH), lambda l, i: (l, 0, 0)),
        ],
        out_specs=[out, out],
        out_shape=[jax.ShapeDtypeStruct((DEPTH, b, MEM_TOKENS, MEM_WIDTH), BF16)] * 2,
        compiler_params=_params("arbitrary", "arbitrary"),
        name="mem_kv",
    )(mem, mem_norm_g.reshape(DEPTH, 1, D_MODEL), w_mem_kv)


def _memory_probs(mq, mk_ref):
    mq = (mq * (MEM_HEAD_DIM ** -0.5)).astype(BF16)
    probs, inv_sums = [], []
    for hh in range(MEM_HEADS):
        hs = slice(hh * MEM_HEAD_DIM, (hh + 1) * MEM_HEAD_DIM)
        s = _dot_nt(mq[:, hs], mk_ref[:, hs])
        p = jnp.exp(s - jnp.max(s, axis=-1, keepdims=True))
        probs.append(p.astype(BF16))
        inv_sums.append(1.0 / jnp.sum(p, axis=-1, keepdims=True))
    return probs, inv_sums


def _memory_readout(probs, inv_sums, gate, mv_ref, out_ref, rows):
    for hh in range(MEM_HEADS):
        hs = slice(hh * MEM_HEAD_DIM, (hh + 1) * MEM_HEAD_DIM)
        o = _dot(probs[hh], mv_ref[:, hs]) * inv_sums[hh]
        out_ref[rows, hs] = (o * gate[:, hs]).astype(BF16)


def _a_in_kernel(layer, x_ref, g_ref, w_ref, lbl_ref, mk_ref, mv_ref,
                 qs_ref, lf_ref, kk_ref, v_ref, gm_ref, om_ref):
    logits = lbl_ref[...]
    e = jnp.exp(logits - jnp.max(logits, axis=0, keepdims=True))
    lb = jnp.sum(e[:layer + 1], axis=0, keepdims=True) / jnp.sum(e, axis=0, keepdims=True)

    col_q, col_f, col_v = 0, HG_KEY_WIDTH, 2 * HG_KEY_WIDTH
    col_g = col_v + HG_VAL_WIDTH
    col_mq = col_g + HG_VAL_WIDTH
    col_mg = col_mq + MEM_WIDTH

    for sub in range(IN_TILE // IN_SUB_TILE):
        rows = slice(sub * IN_SUB_TILE, (sub + 1) * IN_SUB_TILE)
        xn = _rms_norm(x_ref[rows, :], g_ref[...]).astype(BF16)

        def proj(lo, width):
            return _dot(xn, w_ref[:, lo:lo + width])

        mq = proj(col_mq, MEM_WIDTH)
        mem_gate = _silu(proj(col_mg, MEM_WIDTH))
        qs_ref[rows, :] = _silu(proj(col_q, HG_KEY_WIDTH)).astype(BF16)
        probs, inv_sums = _memory_probs(mq, mk_ref)

        fp = proj(col_f, HG_KEY_WIDTH)
        t = jnp.exp(-jnp.abs(fp))
        r = 1.0 / (1.0 + t)
        pos = fp >= 0
        sig = jnp.where(pos, r, t * r)
        sig_neg = jnp.where(pos, t * r, r)
        lf_ref[rows, :] = jnp.log2(lb + (1.0 - lb) * sig)
        kk_ref[rows, :] = ((1.0 - lb) * sig_neg).astype(BF16)

        _memory_readout(probs, inv_sums, mem_gate, mv_ref, om_ref, rows)
        v_ref[rows, :] = proj(col_v, HG_VAL_WIDTH).astype(BF16)
        gm_ref[rows, :] = _silu(proj(col_g, HG_VAL_WIDTH)).astype(BF16)


def _layer_weight(w, layer):
    return pl.BlockSpec((None,) + w.shape[1:], lambda i: (layer, 0, 0))


def _a_in_proj(layer, h, g, w, lb_logits, mem_k, mem_v, tiles_per_seq):
    n = h.shape[0]
    row = lambda width: pl.BlockSpec((IN_TILE, width), lambda i: (i, 0))
    full = lambda a: pl.BlockSpec(a.shape, lambda i: (0,) * a.ndim)
    mem = pl.BlockSpec((None, None, MEM_TOKENS, MEM_WIDTH), lambda i: (layer, i // tiles_per_seq, 0, 0))
    widths = (HG_KEY_WIDTH, HG_KEY_WIDTH, HG_KEY_WIDTH, HG_VAL_WIDTH, HG_VAL_WIDTH, MEM_WIDTH)
    dtypes = (BF16, F32, BF16, BF16, BF16, BF16)
    return pl.pallas_call(
        functools.partial(_a_in_kernel, layer),
        grid=(n // IN_TILE,),
        in_specs=[row(D_MODEL), full(g), _layer_weight(w, layer), full(lb_logits), mem, mem],
        out_specs=[row(wd) for wd in widths],
        out_shape=[jax.ShapeDtypeStruct((n, wd), dt) for wd, dt in zip(widths, dtypes)],
        compiler_params=_params("parallel"),
        name=f"a_in_proj_{layer}",
    )(h, g, w, lb_logits, mem_k, mem_v)


_LEVELS = (32, 16, 8, 4, 2, 1)


def _coarse_level(cum, qs, kk, m):
    c, w = cum.shape
    early = m % (2 * SUBLANES) == 0
    dt = BF16 if early else F32
    zeros = jnp.zeros((m, w), dt)
    qd, kd = [], []
    for blk in range(c // (2 * m)):
        lo, mid, hi = blk * 2 * m, blk * 2 * m + m, (blk + 1) * 2 * m
        ref = cum[mid - 1:mid, :]
        kd += [(kk[lo:mid] * jnp.exp2(ref - cum[lo:mid])).astype(dt), zeros]
        qd += [zeros, (qs[mid:hi] * jnp.exp2(cum[mid:hi] - ref)).astype(dt)]
    return jnp.concatenate(qd, axis=0).astype(BF16), jnp.concatenate(kd, axis=0).astype(BF16)


def _fine_level(cum3, qs3, kk3, m, upper_pen, lower_pen):
    groups, _, w = cum3.shape
    parts = [jnp.broadcast_to(cum3[:, blk * 2 * m + m - 1:blk * 2 * m + m, :], (groups, 2 * m, w))
             for blk in range(SUBLANES // (2 * m))]
    x = cum3 - (parts[0] if len(parts) == 1 else jnp.concatenate(parts, axis=1))
    qd = qs3 * jnp.exp2(x + upper_pen)
    kd = kk3 * jnp.exp2(lower_pen - x)
    c = groups * SUBLANES
    return qd.reshape(c, w).astype(BF16), kd.reshape(c, w).astype(BF16)


def _hgrn_kernel(qs_ref, lf_ref, kk_ref, v_ref, gm_ref, gn_ref, o_ref, st_ref):
    c = HG_CHUNK

    @pl.when(pl.program_id(1) == 0)
    def _():
        st_ref[...] = jnp.zeros_like(st_ref)

    row = lax.broadcasted_iota(jnp.int32, (c, c), 0)
    col = lax.broadcasted_iota(jnp.int32, (c, c), 1)
    tri = (col <= row).astype(BF16)
    same_block = {m: ((row // (2 * m)) == (col // (2 * m))).astype(F32) for m in _LEVELS if 2 * m < c}
    diag = (row == col).astype(F32)
    sub = lax.broadcasted_iota(jnp.int32, (1, SUBLANES, HG_DK), 1)
    fine = [m for m in _LEVELS if m < SUBLANES]
    upper_pen = {m: jnp.where((sub & m) != 0, 0.0, MASK_VALUE) for m in fine}
    lower_pen = {m: jnp.where((sub & m) != 0, MASK_VALUE, 0.0) for m in fine}
    even_row = jnp.where((sub & 1) != 0, 0.0, 1.0)
    gn = gn_ref[...]

    def chunk(ci, carry):
        rows = pl.ds(pl.multiple_of(ci * c, c), c)

        lf_all = lf_ref[rows, :]
        hi = lf_all.astype(BF16)
        lo = (lf_all - hi.astype(F32)).astype(BF16)
        cum_all = _dot(tri, hi) + _dot(tri, lo)

        heads = [slice(h * HG_DK, (h + 1) * HG_DK) for h in range(HG_HEADS)]
        scores_bf, q_ins, k_outs, lasts = [], [], [], []
        for hs in heads:
            cum = cum_all[:, hs]
            qs = qs_ref[rows, hs].astype(F32)
            kk = kk_ref[rows, hs].astype(F32)

            shape3 = (c // SUBLANES, SUBLANES, HG_DK)
            cum3, qs3, kk3 = cum.reshape(shape3), qs.reshape(shape3), kk.reshape(shape3)

            pairs = []
            for m in _LEVELS:
                if m >= SUBLANES:
                    qd, kd = _coarse_level(cum, qs, kk, m)
                elif m == 1:
                    qd3 = qs3 * jnp.exp2(lf_ref[rows, hs].reshape(shape3) + upper_pen[m])
                    qd = qd3.reshape(c, HG_DK).astype(BF16)
                    kd = (kk3 * even_row).reshape(c, HG_DK).astype(BF16)
                else:
                    qd, kd = _fine_level(cum3, qs3, kk3, m, upper_pen[m], lower_pen[m])
                pairs.append((qd, kd, same_block.get(m)))
            pairs.append((qs.astype(BF16), kk.astype(BF16), diag))

            scores = None
            for qd, kd, mask in pairs:
                s = _dot_nt(qd, kd)
                if mask is not None:
                    s = s * mask
                scores = s if scores is None else scores + s

            last = cum[c - 1:c, :]
            scores_bf.append(scores.astype(BF16))
            q_ins.append((qs * jnp.exp2(cum)).astype(BF16))
            k_outs.append((kk * jnp.exp2(last - cum)).astype(BF16))
            lasts.append(last)

        for h, hs in enumerate(heads):
            o = _dot_nt(q_ins[h], st_ref[h].astype(BF16)) + _dot(scores_bf[h], v_ref[rows, hs])
            o = o * lax.rsqrt(jnp.mean(o * o, axis=-1, keepdims=True) + NORM_EPS)
            o_ref[rows, hs] = (o * gn[:, hs] * gm_ref[rows, hs].astype(F32)).astype(BF16)

        for h, hs in enumerate(heads):
            st_ref[h] = st_ref[h] * jnp.exp2(lasts[h]) + _dot_tn(v_ref[rows, hs], k_outs[h])
        return carry

    lax.fori_loop(0, HG_TILE // c, chunk, 0, unroll=True)


def _hgrn(qs, lf, kk, v, gm, gn, batch):
    n = qs.shape[0]
    tiles = n // batch // HG_TILE
    row = pl.BlockSpec((HG_TILE, HG_KEY_WIDTH), lambda b, t: (b * tiles + t, 0))
    return pl.pallas_call(
        _hgrn_kernel,
        grid=(batch, tiles),
        in_specs=[row, row, row, row, row, pl.BlockSpec((1, HG_VAL_WIDTH), lambda b, t: (0, 0))],
        out_specs=row,
        out_shape=jax.ShapeDtypeStruct((n, HG_VAL_WIDTH), BF16),
        scratch_shapes=[pltpu.VMEM((HG_HEADS, HG_DV, HG_DK), F32)],
        compiler_params=_params("arbitrary", "arbitrary"),
        name="hgrn2",
    )(qs, lf, kk, v, gm, gn)


def _out_kernel(main_transposed, om_ref, ome_ref, w_ref, g_ref, h_ref, o_ref):
    main_dot = _dot_tn if main_transposed else _dot
    y = main_dot(om_ref[...], w_ref[:HG_VAL_WIDTH, :]) + _dot(ome_ref[...], w_ref[HG_VAL_WIDTH:, :])
    o_ref[...] = h_ref[...] + _rms_norm(y, g_ref[...])


def _out_proj(o_main, o_mem, w, layer, g, h, name, main_transposed=False):
    n = h.shape[0]
    row = lambda width: pl.BlockSpec((OUT_TILE, width), lambda i: (i, 0))
    col = lambda width: pl.BlockSpec((width, OUT_TILE), lambda i: (0, i))
    full = lambda a: pl.BlockSpec(a.shape, lambda i: (0,) * a.ndim)
    main = col(HG_VAL_WIDTH) if main_transposed else row(HG_VAL_WIDTH)
    return pl.pallas_call(
        functools.partial(_out_kernel, main_transposed),
        grid=(n // OUT_TILE,),
        in_specs=[main, row(MEM_WIDTH), _layer_weight(w, layer), full(g), row(D_MODEL)],
        out_specs=row(D_MODEL),
        out_shape=jax.ShapeDtypeStruct((n, D_MODEL), F32),
        compiler_params=_params("parallel"),
        name=name,
    )(o_main, o_mem, w, g, h)


def _shared_kv_kernel(x_ref, g_ref, w_ref, cos_ref, sin_ref, k_ref, v_ref):
    xn = _rms_norm(x_ref[...], g_ref[...]).astype(BF16)
    kv = _dot(xn, w_ref[...])
    cos = cos_ref[...]
    sin = sin_ref[...]
    first = _first_half_mask(IN_TILE)
    heads_per_block = LANES // SWA_HEAD_DIM
    for cb in range(SWA_KV_WIDTH // LANES):
        kr = _rope_block(kv[:, cb * LANES:(cb + 1) * LANES], cos, sin, first).astype(BF16)
        for j in range(heads_per_block):
            k_ref[cb * heads_per_block + j] = kr[:, j * SWA_HEAD_DIM:(j + 1) * SWA_HEAD_DIM]
    v_ref[...] = kv[:, SWA_KV_WIDTH:].T.astype(BF16)


def _shared_kv(h, g, w, cos, sin):
    n = h.shape[0]
    row = lambda width: pl.BlockSpec((IN_TILE, width), lambda i: (i, 0))
    full = lambda a: pl.BlockSpec(a.shape, lambda i: (0,) * a.ndim)
    return pl.pallas_call(
        _shared_kv_kernel,
        grid=(n // IN_TILE,),
        in_specs=[row(D_MODEL), full(g), full(w), row(LANES), row(LANES)],
        out_specs=[pl.BlockSpec((SWA_KV_HEADS, IN_TILE, SWA_HEAD_DIM), lambda i: (0, i, 0)),
                   pl.BlockSpec((SWA_KV_WIDTH, IN_TILE), lambda i: (0, i))],
        out_shape=[jax.ShapeDtypeStruct((SWA_KV_HEADS, n, SWA_HEAD_DIM), BF16),
                   jax.ShapeDtypeStruct((SWA_KV_WIDTH, n), BF16)],
        compiler_params=_params("parallel"),
        name="shared_kv",
    )(h, g, w, cos, sin)


def _b_in_kernel(x_ref, g_ref, w_ref, cos_ref, sin_ref, mk_ref, mv_ref, qt_ref, gmt_ref, om_ref):
    first = _first_half_mask(IN_SUB_TILE)
    scale = SWA_HEAD_DIM ** -0.5 * LOG2E
    col_mq = 2 * SWA_WIDTH
    col_mg = col_mq + MEM_WIDTH

    for sub in range(IN_TILE // IN_SUB_TILE):
        rows = slice(sub * IN_SUB_TILE, (sub + 1) * IN_SUB_TILE)
        xn = _rms_norm(x_ref[rows, :], g_ref[...]).astype(BF16)

        def proj(lo, width):
            return _dot(xn, w_ref[:, lo:lo + width])

        mq = proj(col_mq, MEM_WIDTH)
        mem_gate = _silu(proj(col_mg, MEM_WIDTH))

        q = proj(0, SWA_WIDTH)
        probs, inv_sums = _memory_probs(mq, mk_ref)
        cos = cos_ref[rows, :]
        sin = sin_ref[rows, :]
        for cb in range(SWA_WIDTH // LANES):
            cs = slice(cb * LANES, (cb + 1) * LANES)
            qt_ref[cs, rows] = (_rope_block(q[:, cs], cos, sin, first) * scale).T.astype(BF16)

        gate = _silu(proj(SWA_WIDTH, SWA_WIDTH))
        _memory_readout(probs, inv_sums, mem_gate, mv_ref, om_ref, rows)
        for cb in range(SWA_WIDTH // LANES):
            cs = slice(cb * LANES, (cb + 1) * LANES)
            gmt_ref[cs, rows] = gate[:, cs].T.astype(BF16)


def _b_in_proj(layer, h, g, w, cos, sin, mem_k, mem_v, tiles_per_seq):
    n = h.shape[0]
    row = lambda width: pl.BlockSpec((IN_TILE, width), lambda i: (i, 0))
    col = pl.BlockSpec((SWA_WIDTH, IN_TILE), lambda i: (0, i))
    full = lambda a: pl.BlockSpec(a.shape, lambda i: (0,) * a.ndim)
    mem = pl.BlockSpec((None, None, MEM_TOKENS, MEM_WIDTH), lambda i: (layer, i // tiles_per_seq, 0, 0))
    return pl.pallas_call(
        _b_in_kernel,
        grid=(n // IN_TILE,),
        in_specs=[row(D_MODEL), full(g), _layer_weight(w, layer - N_A_LAYERS), row(LANES), row(LANES), mem, mem],
        out_specs=[col, col, row(MEM_WIDTH)],
        out_shape=[jax.ShapeDtypeStruct((SWA_WIDTH, n), BF16), jax.ShapeDtypeStruct((SWA_WIDTH, n), BF16),
                   jax.ShapeDtypeStruct((n, MEM_WIDTH), BF16)],
        compiler_params=_params("parallel"),
        name=f"b_in_proj_{layer}",
    )(h, g, w, cos, sin, mem_k, mem_v)


def _swa_kernel(sink_ref, qt_ref, kc_ref, kp_ref, vtc_ref, vtp_ref, gmt_ref, ot_ref):
    blk = WINDOW
    dh = SWA_HEAD_DIM
    ki = lax.broadcasted_iota(jnp.int32, (2 * blk, blk), 0)
    qi = lax.broadcasted_iota(jnp.int32, (2 * blk, blk), 1)
    d = ki - qi
    band = (d > 0) & (d <= blk)
    first_key = jnp.where(pl.program_id(1) > 0, 0, blk)

    for sb in range(SWA_TILE // blk):
        cols = slice(sb * blk, (sb + 1) * blk)
        two = slice((sb - 1) * blk, (sb + 1) * blk)
        mask = band & (ki >= first_key) if sb == 0 else band

        probs, inv_den = [], []
        for g in range(SWA_KV_HEADS):
            k2 = jnp.concatenate([kp_ref[g], kc_ref[g, cols, :]], axis=0) if sb == 0 else kc_ref[g, two, :]
            heads = range(g * SWA_GROUP, (g + 1) * SWA_GROUP)
            qt = jnp.concatenate([qt_ref[h * dh:(h + 1) * dh, cols] for h in heads], axis=1)
            st = _dot(k2, qt)
            ps = []
            for j, h in enumerate(heads):
                s = jnp.where(mask, st[:, j * blk:(j + 1) * blk], MASK_VALUE)
                sink = sink_ref[h] * LOG2E
                m = jnp.maximum(jnp.max(s, axis=0, keepdims=True), sink)
                p = jnp.exp2(s - m)
                inv_den.append(1.0 / (jnp.sum(p, axis=0, keepdims=True) + jnp.exp2(sink - m)))
                ps.append(p.astype(BF16))
            probs.append(jnp.concatenate(ps, axis=1))

        for g in range(SWA_KV_HEADS):
            vrows = slice(g * dh, (g + 1) * dh)
            vt2 = (jnp.concatenate([vtp_ref[vrows, :], vtc_ref[vrows, cols]], axis=1) if sb == 0
                   else vtc_ref[vrows, two])
            ot = _dot(vt2, probs[g])
            for j, h in enumerate(range(g * SWA_GROUP, (g + 1) * SWA_GROUP)):
                rows = slice(h * dh, (h + 1) * dh)
                o = ot[:, j * blk:(j + 1) * blk] * inv_den[h]
                ot_ref[rows, cols] = (o * gmt_ref[rows, cols].astype(F32)).astype(BF16)


def _swa(sinks, qt, k, vt, gmt, batch):
    n = qt.shape[1]
    tiles = n // batch // SWA_TILE
    sub = SWA_TILE // WINDOW
    tile = lambda b, t: b * tiles + t
    prev = lambda b, t: jnp.maximum(tile(b, t) * sub - 1, 0)
    wide = pl.BlockSpec((SWA_WIDTH, SWA_TILE), lambda b, t: (0, tile(b, t)))
    return pl.pallas_call(
        _swa_kernel,
        grid=(batch, tiles),
        in_specs=[pl.BlockSpec(memory_space=pltpu.SMEM),
                  wide,
                  pl.BlockSpec((SWA_KV_HEADS, SWA_TILE, SWA_HEAD_DIM), lambda b, t: (0, tile(b, t), 0)),
                  pl.BlockSpec((SWA_KV_HEADS, WINDOW, SWA_HEAD_DIM), lambda b, t: (0, prev(b, t), 0)),
                  pl.BlockSpec((SWA_KV_WIDTH, SWA_TILE), lambda b, t: (0, tile(b, t))),
                  pl.BlockSpec((SWA_KV_WIDTH, WINDOW), lambda b, t: (0, prev(b, t))),
                  wide],
        out_specs=wide,
        out_shape=jax.ShapeDtypeStruct((SWA_WIDTH, n), BF16),
        compiler_params=_params("parallel", "arbitrary"),
        name="swa",
    )(sinks, qt, k, k, vt, vt, gmt)


def kernel(x, mem, positions, pre_norm_g, post_norm_g, mem_norm_g, w_mem_kv, a_w_in, a_lb_logits, a_out_norm_g, a_w_out, kv_norm_g, w_kv_shared, b_w_in, b_sinks, b_w_out):
    batch, seq, _ = x.shape
    n = batch * seq
    assert seq % max(IN_TILE, OUT_TILE, HG_TILE, SWA_TILE) == 0
    tiles_per_seq = seq // IN_TILE

    h = x.reshape(n, D_MODEL)
    mem_k, mem_v = _mem_kv(mem, mem_norm_g, w_mem_kv.astype(BF16))
    cos, sin = _rope_tables(positions)
    a_w_in = a_w_in.astype(BF16)
    a_w_out = a_w_out.astype(BF16)
    b_w_in = b_w_in.astype(BF16)
    b_w_out = b_w_out.astype(BF16)
    k_sh = v_sh = None

    for layer in range(DEPTH):
        pre_g = pre_norm_g[layer].reshape(1, D_MODEL)
        post_g = post_norm_g[layer].reshape(1, D_MODEL)
        if layer < N_A_LAYERS:
            qs, lf, kk, v, gm, o_mem = _a_in_proj(layer, h, pre_g, a_w_in, a_lb_logits,
                                                  mem_k, mem_v, tiles_per_seq)
            o_main = _hgrn(qs, lf, kk, v, gm, a_out_norm_g[layer].reshape(1, HG_VAL_WIDTH), batch)
            h = _out_proj(o_main, o_mem, a_w_out, layer, post_g, h, f"a_out_proj_{layer}")
        else:
            j = layer - N_A_LAYERS
            if k_sh is None:
                k_sh, v_sh = _shared_kv(h, kv_norm_g.reshape(1, D_MODEL), w_kv_shared.astype(BF16), cos, sin)
            qt, gmt, o_mem = _b_in_proj(layer, h, pre_g, b_w_in, cos, sin, mem_k, mem_v, tiles_per_seq)
            o_main_t = _swa(b_sinks[j], qt, k_sh, v_sh, gmt, batch)
            h = _out_proj(o_main_t, o_mem, b_w_out, j, post_g, h, f"b_out_proj_{layer}", main_transposed=True)
    return h.reshape(batch, seq, D_MODEL)
```

```python
import functools

import jax
import jax.numpy as jnp
from jax import lax
from jax.experimental import pallas as pl
from jax.experimental.pallas import tpu as pltpu

D_MODEL = 1024
DEPTH = 4
N_A_LAYERS = DEPTH // 2

HG_HEADS = 8
HG_DK = 128
HG_DV = D_MODEL // HG_HEADS
HG_KEY_WIDTH = HG_HEADS * HG_DK
HG_VAL_WIDTH = HG_HEADS * HG_DV
HG_CHUNK = 64

SWA_Q_HEADS = 16
SWA_KV_HEADS = 4
SWA_GROUP = SWA_Q_HEADS // SWA_KV_HEADS
SWA_HEAD_DIM = 64
SWA_WIDTH = SWA_Q_HEADS * SWA_HEAD_DIM
SWA_KV_WIDTH = SWA_KV_HEADS * SWA_HEAD_DIM
WINDOW = 128

MEM_TOKENS = 256
MEM_HEADS = 4
MEM_HEAD_DIM = 128
MEM_WIDTH = MEM_HEADS * MEM_HEAD_DIM

ROPE_THETA = 10000.0
NORM_EPS = 1e-6

A_IN_WIDTH = 2 * HG_KEY_WIDTH + 2 * HG_VAL_WIDTH + 2 * MEM_WIDTH
B_IN_WIDTH = 2 * SWA_WIDTH + 2 * MEM_WIDTH
OUT_WIDTH = HG_VAL_WIDTH + MEM_WIDTH

LANES = 128
SUBLANES = 8
MASK_VALUE = -1e30
LOG2E = 1.4426950408889634

IN_TILE = 512
IN_SUB_TILE = 256
OUT_TILE = 512
SWA_TILE = 512
VMEM_LIMIT = 56 * 1024 * 1024

F32 = jnp.float32
BF16 = jnp.bfloat16

_NT = (((1,), (1,)), ((), ()))
_TN = (((0,), (0,)), ((), ()))


def _dot(a, b):
    return jnp.dot(a, b, preferred_element_type=F32)


def _dot_nt(a, b):
    return lax.dot_general(a, b, _NT, preferred_element_type=F32)


def _dot_tn(a, b):
    return lax.dot_general(a, b, _TN, preferred_element_type=F32)


def _rms_norm(x, g):
    ms = jnp.mean(x * x, axis=-1, keepdims=True)
    return x * lax.rsqrt(ms + NORM_EPS) * g


def _sigmoid(x):
    return 1.0 / (1.0 + jnp.exp(-x))


def _silu(x):
    return x * _sigmoid(x)


def _params(*semantics):
    return pltpu.CompilerParams(dimension_semantics=semantics, vmem_limit_bytes=VMEM_LIMIT)


def _layer_weight(w, layer):
    return pl.BlockSpec((None,) + w.shape[1:], lambda *_: (layer, 0, 0))


def _interleave(streams):
    live = [[iter(gen), 0, n] for gen, n in streams]
    while live:
        entry = min(live, key=lambda e: e[1] / e[2])
        try:
            next(entry[0])
            entry[1] += 1
        except StopIteration:
            live.remove(entry)


def _rope_table_kernel(pos_ref, invf_ref, sign_ref, cos_ref, sin_ref):
    ang = pos_ref[...].astype(F32) * invf_ref[...]
    cos_ref[...] = jnp.cos(ang)
    sin_ref[...] = jnp.sin(ang) * sign_ref[...]


def _rope_tables(positions):
    n = positions.size
    half = SWA_HEAD_DIM // 2
    inv_freq = ROPE_THETA ** (-jnp.arange(0, SWA_HEAD_DIM, 2, dtype=F32) / SWA_HEAD_DIM)
    invf = jnp.tile(inv_freq, LANES // half).reshape(1, LANES)
    sign = jnp.tile(jnp.concatenate([-jnp.ones((half,), F32), jnp.ones((half,), F32)]),
                    LANES // SWA_HEAD_DIM).reshape(1, LANES)
    pos = jnp.broadcast_to(positions.reshape(n, 1), (n, LANES))
    tile = 2048
    row = pl.BlockSpec((tile, LANES), lambda i: (i, 0))
    vec = pl.BlockSpec((1, LANES), lambda i: (0, 0))
    return pl.pallas_call(
        _rope_table_kernel,
        grid=(n // tile,),
        in_specs=[row, vec, vec],
        out_specs=[row, row],
        out_shape=[jax.ShapeDtypeStruct((n, LANES), F32)] * 2,
        compiler_params=_params("parallel"),
        name="rope_tables",
    )(pos, invf, sign)


def _rope_block(xc, cos, sin, first_half):
    swapped = jnp.where(first_half,
                        pltpu.roll(xc, LANES - SWA_HEAD_DIM // 2, 1),
                        pltpu.roll(xc, SWA_HEAD_DIM // 2, 1))
    return xc * cos + swapped * sin


def _first_half_mask(rows):
    lane = lax.broadcasted_iota(jnp.int32, (rows, LANES), 1)
    return (lane & (SWA_HEAD_DIM // 2)) == 0


def _mem_kv_kernel(mem_ref, g_ref, w_ref, k_ref, v_ref):
    mn = _rms_norm(mem_ref[...], g_ref[...]).astype(BF16)
    kv = _dot(mn, w_ref[...])
    k_ref[...] = kv[:, :MEM_WIDTH].astype(BF16)
    v_ref[...] = kv[:, MEM_WIDTH:].astype(BF16)


def _mem_kv(mem, mem_norm_g, w_mem_kv):
    b = mem.shape[0]
    out = pl.BlockSpec((None, None, MEM_TOKENS, MEM_WIDTH), lambda l, i: (l, i, 0, 0))
    return pl.pallas_call(
        _mem_kv_kernel,
        grid=(DEPTH, b),
        in_specs=[
            pl.BlockSpec((None, MEM_TOKENS, D_MODEL), lambda l, i: (i, 0, 0)),
            pl.BlockSpec((None, 1, D_MODEL), lambda l, i: (l, 0, 0)),
            pl.BlockSpec((None, D_MODEL, 2 * MEM_WIDTH), lambda l, i: (l, 0, 0)),
        ],
        out_specs=[out, out],
        out_shape=[jax.ShapeDtypeStruct((DEPTH, b, MEM_TOKENS, MEM_WIDTH), BF16)] * 2,
        compiler_params=_params("arbitrary", "arbitrary"),
        name="mem_kv",
    )(mem, mem_norm_g.reshape(DEPTH, 1, D_MODEL), w_mem_kv)


def _memory_probs(mq, mk_ref):
    mq = (mq * (MEM_HEAD_DIM ** -0.5)).astype(BF16)
    probs, inv_sums = [], []
    for hh in range(MEM_HEADS):
        hs = slice(hh * MEM_HEAD_DIM, (hh + 1) * MEM_HEAD_DIM)
        s = _dot_nt(mq[:, hs], mk_ref[:, hs])
        p = jnp.exp(s - jnp.max(s, axis=-1, keepdims=True))
        probs.append(p.astype(BF16))
        inv_sums.append(1.0 / jnp.sum(p, axis=-1, keepdims=True))
    return probs, inv_sums


def _memory_readout(probs, inv_sums, gate, mv_ref, out_ref, rows):
    for hh in range(MEM_HEADS):
        hs = slice(hh * MEM_HEAD_DIM, (hh + 1) * MEM_HEAD_DIM)
        o = _dot(probs[hh], mv_ref[:, hs]) * inv_sums[hh]
        out_ref[rows, hs] = (o * gate[:, hs]).astype(BF16)


A_IN_PHASES = 12
PROJ_COLS = 512


def _a_in_stream(rows, lb, x_ref, g_ref, w_ref, mk_ref, mv_ref, qs_s, lf_s, kk_s, v_s, gm_s, om_ref):
    col_q, col_f, col_v = 0, HG_KEY_WIDTH, 2 * HG_KEY_WIDTH
    col_g = col_v + HG_VAL_WIDTH
    col_mq = col_g + HG_VAL_WIDTH
    col_mg = col_mq + MEM_WIDTH
    xn = _rms_norm(x_ref[rows, :], g_ref[...]).astype(BF16)

    def proj(lo):
        return _dot(xn, w_ref[:, lo:lo + PROJ_COLS])

    mq = proj(col_mq)
    yield
    mem_gate = _silu(proj(col_mg))
    yield
    for part in range(HG_KEY_WIDTH // PROJ_COLS):
        cs = slice(part * PROJ_COLS, (part + 1) * PROJ_COLS)
        qs_s[rows, cs] = _silu(proj(col_q + part * PROJ_COLS)).astype(BF16)
        yield
        if part == 0:
            probs, inv_sums = _memory_probs(mq, mk_ref)
            yield
    for part in range(HG_KEY_WIDTH // PROJ_COLS):
        cs = slice(part * PROJ_COLS, (part + 1) * PROJ_COLS)
        fp = proj(col_f + part * PROJ_COLS)
        t = jnp.exp(-jnp.abs(fp))
        r = 1.0 / (1.0 + t)
        pos = fp >= 0
        sig = jnp.where(pos, r, t * r)
        sig_neg = jnp.where(pos, t * r, r)
        lf_s[rows, cs] = jnp.log2(lb[:, cs] + (1.0 - lb[:, cs]) * sig)
        kk_s[rows, cs] = ((1.0 - lb[:, cs]) * sig_neg).astype(BF16)
        yield
        if part == 0:
            _memory_readout(probs, inv_sums, mem_gate, mv_ref, om_ref, rows)
            yield
    for part in range(HG_VAL_WIDTH // PROJ_COLS):
        cs = slice(part * PROJ_COLS, (part + 1) * PROJ_COLS)
        v_s[rows, cs] = proj(col_v + part * PROJ_COLS).astype(BF16)
        yield
    for part in range(HG_VAL_WIDTH // PROJ_COLS):
        cs = slice(part * PROJ_COLS, (part + 1) * PROJ_COLS)
        gm_s[rows, cs] = _silu(proj(col_g + part * PROJ_COLS)).astype(BF16)
        yield


_LEVELS = (32, 16, 8, 4, 2, 1)


def _coarse_level(cum, qs, kk, m):
    c, w = cum.shape
    early = m % (2 * SUBLANES) == 0
    dt = BF16 if early else F32
    zeros = jnp.zeros((m, w), dt)
    qd, kd = [], []
    for blk in range(c // (2 * m)):
        lo, mid, hi = blk * 2 * m, blk * 2 * m + m, (blk + 1) * 2 * m
        ref = cum[mid - 1:mid, :]
        kd += [(kk[lo:mid] * jnp.exp2(ref - cum[lo:mid])).astype(dt), zeros]
        qd += [zeros, (qs[mid:hi] * jnp.exp2(cum[mid:hi] - ref)).astype(dt)]
    return jnp.concatenate(qd, axis=0).astype(BF16), jnp.concatenate(kd, axis=0).astype(BF16)


def _fine_level(cum3, qs3, kk3, m, upper_pen, lower_pen):
    groups, _, w = cum3.shape
    parts = [jnp.broadcast_to(cum3[:, blk * 2 * m + m - 1:blk * 2 * m + m, :], (groups, 2 * m, w))
             for blk in range(SUBLANES // (2 * m))]
    x = cum3 - (parts[0] if len(parts) == 1 else jnp.concatenate(parts, axis=1))
    qd = qs3 * jnp.exp2(x + upper_pen)
    kd = kk3 * jnp.exp2(lower_pen - x)
    c = groups * SUBLANES
    return qd.reshape(c, w).astype(BF16), kd.reshape(c, w).astype(BF16)


def _hgrn_constants():
    c = HG_CHUNK
    row = lax.broadcasted_iota(jnp.int32, (c, c), 0)
    col = lax.broadcasted_iota(jnp.int32, (c, c), 1)
    sub = lax.broadcasted_iota(jnp.int32, (1, SUBLANES, HG_DK), 1)
    fine = [m for m in _LEVELS if m < SUBLANES]
    return dict(
        tri=(col <= row).astype(BF16),
        same_block={m: ((row // (2 * m)) == (col // (2 * m))).astype(F32) for m in _LEVELS if 2 * m < c},
        diag=(row == col).astype(F32),
        upper_pen={m: jnp.where((sub & m) != 0, 0.0, MASK_VALUE) for m in fine},
        lower_pen={m: jnp.where((sub & m) != 0, MASK_VALUE, 0.0) for m in fine},
        even_row=jnp.where((sub & 1) != 0, 0.0, 1.0),
    )


HG_PHASES_PER_CHUNK = 5


def _hgrn_stream(first_row, n_rows, k, qs_s, lf_s, kk_s, v_s, gm_s, gn, o_ref, st_ref):
    c = HG_CHUNK
    heads = [slice(h * HG_DK, (h + 1) * HG_DK) for h in range(HG_HEADS)]
    shape3 = (c // SUBLANES, SUBLANES, HG_DK)
    for ci in range(n_rows // c):
        rows = slice(first_row + ci * c, first_row + (ci + 1) * c)

        lf_all = lf_s[rows, :]
        hi = lf_all.astype(BF16)
        lo = (lf_all - hi.astype(F32)).astype(BF16)
        cum_all = _dot(k["tri"], hi) + _dot(k["tri"], lo)
        yield

        scores_bf, q_ins, k_outs, lasts = [], [], [], []
        for h, hs in enumerate(heads):
            cum = cum_all[:, hs]
            qs = qs_s[rows, hs].astype(F32)
            kk = kk_s[rows, hs].astype(F32)
            cum3, qs3, kk3 = cum.reshape(shape3), qs.reshape(shape3), kk.reshape(shape3)

            pairs = []
            for m in _LEVELS:
                if m >= SUBLANES:
                    qd, kd = _coarse_level(cum, qs, kk, m)
                elif m == 1:
                    qd3 = qs3 * jnp.exp2(lf_s[rows, hs].reshape(shape3) + k["upper_pen"][m])
                    qd = qd3.reshape(c, HG_DK).astype(BF16)
                    kd = (kk3 * k["even_row"]).reshape(c, HG_DK).astype(BF16)
                else:
                    qd, kd = _fine_level(cum3, qs3, kk3, m, k["upper_pen"][m], k["lower_pen"][m])
                pairs.append((qd, kd, k["same_block"].get(m)))
            pairs.append((qs.astype(BF16), kk.astype(BF16), k["diag"]))

            scores = None
            for qd, kd, mask in pairs:
                s = _dot_nt(qd, kd)
                if mask is not None:
                    s = s * mask
                scores = s if scores is None else scores + s

            last = cum[c - 1:c, :]
            scores_bf.append(scores.astype(BF16))
            q_ins.append((qs * jnp.exp2(cum)).astype(BF16))
            k_outs.append((kk * jnp.exp2(last - cum)).astype(BF16))
            lasts.append(last)
            if h == HG_HEADS // 2 - 1:
                yield
        yield

        for h, hs in enumerate(heads):
            o = _dot_nt(q_ins[h], st_ref[h].astype(BF16)) + _dot(scores_bf[h], v_s[rows, hs])
            o = o * lax.rsqrt(jnp.mean(o * o, axis=-1, keepdims=True) + NORM_EPS)
            o_ref[rows, hs] = (o * gn[:, hs] * gm_s[rows, hs].astype(F32)).astype(BF16)
        yield

        for h, hs in enumerate(heads):
            st_ref[h] = st_ref[h] * jnp.exp2(lasts[h]) + _dot_tn(v_s[rows, hs], k_outs[h])
        yield


def _a_layer_kernel(layer, x_ref, g_ref, w_ref, lbl_ref, mk_ref, mv_ref, gn_ref, o_ref, om_ref,
                    qs_s, lf_s, kk_s, v_s, gm_s, st_ref):
    @pl.when(pl.program_id(1) == 0)
    def _():
        st_ref[...] = jnp.zeros_like(st_ref)

    logits = lbl_ref[...]
    e = jnp.exp(logits - jnp.max(logits, axis=0, keepdims=True))
    lb = jnp.sum(e[:layer + 1], axis=0, keepdims=True) / jnp.sum(e, axis=0, keepdims=True)
    consts = _hgrn_constants()
    gn = gn_ref[...]

    def in_stream(sub):
        rows = slice(sub * IN_SUB_TILE, (sub + 1) * IN_SUB_TILE)
        return (_a_in_stream(rows, lb, x_ref, g_ref, w_ref, mk_ref, mv_ref, qs_s, lf_s, kk_s, v_s, gm_s, om_ref),
                A_IN_PHASES)

    def hgrn_stream(sub):
        return (_hgrn_stream(sub * IN_SUB_TILE, IN_SUB_TILE, consts, qs_s, lf_s, kk_s, v_s, gm_s, gn, o_ref, st_ref),
                HG_PHASES_PER_CHUNK * (IN_SUB_TILE // HG_CHUNK))

    subs = IN_TILE // IN_SUB_TILE
    _interleave([in_stream(0)])
    for sub in range(1, subs):
        _interleave([in_stream(sub), hgrn_stream(sub - 1)])
    _interleave([hgrn_stream(subs - 1)])


def _a_layer_mixer(layer, h, g, w, lb_logits, mem_k, mem_v, gn, batch):
    n = h.shape[0]
    tiles = n // batch // IN_TILE
    row = lambda width: pl.BlockSpec((IN_TILE, width), lambda b, t: (b * tiles + t, 0))
    full = lambda a: pl.BlockSpec(a.shape, lambda b, t: (0,) * a.ndim)
    mem = pl.BlockSpec((None, None, MEM_TOKENS, MEM_WIDTH), lambda b, t: (layer, b, 0, 0))
    wide = lambda dt: pltpu.VMEM((IN_TILE, HG_KEY_WIDTH), dt)
    return pl.pallas_call(
        functools.partial(_a_layer_kernel, layer),
        grid=(batch, tiles),
        in_specs=[row(D_MODEL), full(g), _layer_weight(w, layer), full(lb_logits), mem, mem, full(gn)],
        out_specs=[row(HG_VAL_WIDTH), row(MEM_WIDTH)],
        out_shape=[jax.ShapeDtypeStruct((n, HG_VAL_WIDTH), BF16), jax.ShapeDtypeStruct((n, MEM_WIDTH), BF16)],
        scratch_shapes=[wide(BF16), wide(F32), wide(BF16), wide(BF16), wide(BF16),
                        pltpu.VMEM((HG_HEADS, HG_DV, HG_DK), F32)],
        compiler_params=_params("arbitrary", "arbitrary"),
        name=f"a_layer_{layer}",
    )(h, g, w, lb_logits, mem_k, mem_v, gn)


def _out_kernel(main_transposed, om_ref, ome_ref, w_ref, g_ref, h_ref, o_ref):
    main_dot = _dot_tn if main_transposed else _dot
    y = main_dot(om_ref[...], w_ref[:HG_VAL_WIDTH, :]) + _dot(ome_ref[...], w_ref[HG_VAL_WIDTH:, :])
    o_ref[...] = h_ref[...] + _rms_norm(y, g_ref[...])


def _out_proj(o_main, o_mem, w, layer, g, h, name, main_transposed=False):
    n = h.shape[0]
    row = lambda width: pl.BlockSpec((OUT_TILE, width), lambda i: (i, 0))
    col = lambda width: pl.BlockSpec((width, OUT_TILE), lambda i: (0, i))
    full = lambda a: pl.BlockSpec(a.shape, lambda i: (0,) * a.ndim)
    main = col(HG_VAL_WIDTH) if main_transposed else row(HG_VAL_WIDTH)
    return pl.pallas_call(
        functools.partial(_out_kernel, main_transposed),
        grid=(n // OUT_TILE,),
        in_specs=[main, row(MEM_WIDTH), _layer_weight(w, layer), full(g), row(D_MODEL)],
        out_specs=row(D_MODEL),
        out_shape=jax.ShapeDtypeStruct((n, D_MODEL), F32),
        compiler_params=_params("parallel"),
        name=name,
    )(o_main, o_mem, w, g, h)


def _shared_kv_kernel(x_ref, g_ref, w_ref, cos_ref, sin_ref, k_ref, v_ref):
    xn = _rms_norm(x_ref[...], g_ref[...]).astype(BF16)
    kv = _dot(xn, w_ref[...])
    cos = cos_ref[...]
    sin = sin_ref[...]
    first = _first_half_mask(IN_TILE)
    heads_per_block = LANES // SWA_HEAD_DIM
    for cb in range(SWA_KV_WIDTH // LANES):
        kr = _rope_block(kv[:, cb * LANES:(cb + 1) * LANES], cos, sin, first).astype(BF16)
        for j in range(heads_per_block):
            k_ref[cb * heads_per_block + j] = kr[:, j * SWA_HEAD_DIM:(j + 1) * SWA_HEAD_DIM]
    v_ref[...] = kv[:, SWA_KV_WIDTH:].T.astype(BF16)


def _shared_kv(h, g, w, cos, sin):
    n = h.shape[0]
    row = lambda width: pl.BlockSpec((IN_TILE, width), lambda i: (i, 0))
    full = lambda a: pl.BlockSpec(a.shape, lambda i: (0,) * a.ndim)
    return pl.pallas_call(
        _shared_kv_kernel,
        grid=(n // IN_TILE,),
        in_specs=[row(D_MODEL), full(g), full(w), row(LANES), row(LANES)],
        out_specs=[pl.BlockSpec((SWA_KV_HEADS, IN_TILE, SWA_HEAD_DIM), lambda i: (0, i, 0)),
                   pl.BlockSpec((SWA_KV_WIDTH, IN_TILE), lambda i: (0, i))],
        out_shape=[jax.ShapeDtypeStruct((SWA_KV_HEADS, n, SWA_HEAD_DIM), BF16),
                   jax.ShapeDtypeStruct((SWA_KV_WIDTH, n), BF16)],
        compiler_params=_params("parallel"),
        name="shared_kv",
    )(h, g, w, cos, sin)


def _b_in_kernel(x_ref, g_ref, w_ref, cos_ref, sin_ref, mk_ref, mv_ref, qt_ref, gmt_ref, om_ref):
    first = _first_half_mask(IN_SUB_TILE)
    scale = SWA_HEAD_DIM ** -0.5 * LOG2E
    col_mq = 2 * SWA_WIDTH
    col_mg = col_mq + MEM_WIDTH

    for sub in range(IN_TILE // IN_SUB_TILE):
        rows = slice(sub * IN_SUB_TILE, (sub + 1) * IN_SUB_TILE)
        xn = _rms_norm(x_ref[rows, :], g_ref[...]).astype(BF16)

        def proj(lo, width):
            return _dot(xn, w_ref[:, lo:lo + width])

        mq = proj(col_mq, MEM_WIDTH)
        mem_gate = _silu(proj(col_mg, MEM_WIDTH))

        q = proj(0, SWA_WIDTH)
        probs, inv_sums = _memory_probs(mq, mk_ref)
        cos = cos_ref[rows, :]
        sin = sin_ref[rows, :]
        for cb in range(SWA_WIDTH // LANES):
            cs = slice(cb * LANES, (cb + 1) * LANES)
            qt_ref[cs, rows] = (_rope_block(q[:, cs], cos, sin, first) * scale).T.astype(BF16)

        gate = _silu(proj(SWA_WIDTH, SWA_WIDTH))
        _memory_readout(probs, inv_sums, mem_gate, mv_ref, om_ref, rows)
        for cb in range(SWA_WIDTH // LANES):
            cs = slice(cb * LANES, (cb + 1) * LANES)
            gmt_ref[cs, rows] = gate[:, cs].T.astype(BF16)


def _b_in_proj(layer, h, g, w, cos, sin, mem_k, mem_v, tiles_per_seq):
    n = h.shape[0]
    row = lambda width: pl.BlockSpec((IN_TILE, width), lambda i: (i, 0))
    col = pl.BlockSpec((SWA_WIDTH, IN_TILE), lambda i: (0, i))
    full = lambda a: pl.BlockSpec(a.shape, lambda i: (0,) * a.ndim)
    mem = pl.BlockSpec((None, None, MEM_TOKENS, MEM_WIDTH), lambda i: (layer, i // tiles_per_seq, 0, 0))
    return pl.pallas_call(
        _b_in_kernel,
        grid=(n // IN_TILE,),
        in_specs=[row(D_MODEL), full(g), _layer_weight(w, layer - N_A_LAYERS), row(LANES), row(LANES), mem, mem],
        out_specs=[col, col, row(MEM_WIDTH)],
        out_shape=[jax.ShapeDtypeStruct((SWA_WIDTH, n), BF16), jax.ShapeDtypeStruct((SWA_WIDTH, n), BF16),
                   jax.ShapeDtypeStruct((n, MEM_WIDTH), BF16)],
        compiler_params=_params("parallel"),
        name=f"b_in_proj_{layer}",
    )(h, g, w, cos, sin, mem_k, mem_v)


def _swa_kernel(sink_ref, qt_ref, kc_ref, kp_ref, vtc_ref, vtp_ref, gmt_ref, ot_ref):
    blk = WINDOW
    dh = SWA_HEAD_DIM
    ki = lax.broadcasted_iota(jnp.int32, (2 * blk, blk), 0)
    qi = lax.broadcasted_iota(jnp.int32, (2 * blk, blk), 1)
    d = ki - qi
    band = (d > 0) & (d <= blk)
    first_key = jnp.where(pl.program_id(1) > 0, 0, blk)

    for sb in range(SWA_TILE // blk):
        cols = slice(sb * blk, (sb + 1) * blk)
        two = slice((sb - 1) * blk, (sb + 1) * blk)
        mask = band & (ki >= first_key) if sb == 0 else band

        probs, inv_den = [], []
        for g in range(SWA_KV_HEADS):
            k2 = jnp.concatenate([kp_ref[g], kc_ref[g, cols, :]], axis=0) if sb == 0 else kc_ref[g, two, :]
            heads = range(g * SWA_GROUP, (g + 1) * SWA_GROUP)
            qt = jnp.concatenate([qt_ref[h * dh:(h + 1) * dh, cols] for h in heads], axis=1)
            st = _dot(k2, qt)
            ps = []
            for j, h in enumerate(heads):
                s = jnp.where(mask, st[:, j * blk:(j + 1) * blk], MASK_VALUE)
                sink = sink_ref[h] * LOG2E
                m = jnp.maximum(jnp.max(s, axis=0, keepdims=True), sink)
                p = jnp.exp2(s - m)
                inv_den.append(1.0 / (jnp.sum(p, axis=0, keepdims=True) + jnp.exp2(sink - m)))
                ps.append(p.astype(BF16))
            probs.append(jnp.concatenate(ps, axis=1))

        for g in range(SWA_KV_HEADS):
            vrows = slice(g * dh, (g + 1) * dh)
            vt2 = (jnp.concatenate([vtp_ref[vrows, :], vtc_ref[vrows, cols]], axis=1) if sb == 0
                   else vtc_ref[vrows, two])
            ot = _dot(vt2, probs[g])
            for j, h in enumerate(range(g * SWA_GROUP, (g + 1) * SWA_GROUP)):
                rows = slice(h * dh, (h + 1) * dh)
                o = ot[:, j * blk:(j + 1) * blk] * inv_den[h]
                ot_ref[rows, cols] = (o * gmt_ref[rows, cols].astype(F32)).astype(BF16)


def _swa(sinks, qt, k, vt, gmt, batch):
    n = qt.shape[1]
    tiles = n // batch // SWA_TILE
    sub = SWA_TILE // WINDOW
    tile = lambda b, t: b * tiles + t
    prev = lambda b, t: jnp.maximum(tile(b, t) * sub - 1, 0)
    wide = pl.BlockSpec((SWA_WIDTH, SWA_TILE), lambda b, t: (0, tile(b, t)))
    return pl.pallas_call(
        _swa_kernel,
        grid=(batch, tiles),
        in_specs=[pl.BlockSpec(memory_space=pltpu.SMEM),
                  wide,
                  pl.BlockSpec((SWA_KV_HEADS, SWA_TILE, SWA_HEAD_DIM), lambda b, t: (0, tile(b, t), 0)),
                  pl.BlockSpec((SWA_KV_HEADS, WINDOW, SWA_HEAD_DIM), lambda b, t: (0, prev(b, t), 0)),
                  pl.BlockSpec((SWA_KV_WIDTH, SWA_TILE), lambda b, t: (0, tile(b, t))),
                  pl.BlockSpec((SWA_KV_WIDTH, WINDOW), lambda b, t: (0, prev(b, t))),
                  wide],
        out_specs=wide,
        out_shape=jax.ShapeDtypeStruct((SWA_WIDTH, n), BF16),
        compiler_params=_params("parallel", "arbitrary"),
        name="swa",
    )(sinks, qt, k, k, vt, vt, gmt)


def kernel(x, mem, positions, pre_norm_g, post_norm_g, mem_norm_g, w_mem_kv, a_w_in, a_lb_logits, a_out_norm_g, a_w_out, kv_norm_g, w_kv_shared, b_w_in, b_sinks, b_w_out):
    batch, seq, _ = x.shape
    n = batch * seq
    assert seq % max(IN_TILE, OUT_TILE, SWA_TILE) == 0
    tiles_per_seq = seq // IN_TILE

    h = x.reshape(n, D_MODEL)
    mem_k, mem_v = _mem_kv(mem, mem_norm_g, w_mem_kv.astype(BF16))
    cos, sin = _rope_tables(positions)
    a_w_in = a_w_in.astype(BF16)
    a_w_out = a_w_out.astype(BF16)
    b_w_in = b_w_in.astype(BF16)
    b_w_out = b_w_out.astype(BF16)
    k_sh = v_sh = None

    for layer in range(DEPTH):
        pre_g = pre_norm_g[layer].reshape(1, D_MODEL)
        post_g = post_norm_g[layer].reshape(1, D_MODEL)
        if layer < N_A_LAYERS:
            o_main, o_mem = _a_layer_mixer(layer, h, pre_g, a_w_in, a_lb_logits, mem_k, mem_v,
                                           a_out_norm_g[layer].reshape(1, HG_VAL_WIDTH), batch)
            h = _out_proj(o_main, o_mem, a_w_out, layer, post_g, h, f"a_out_proj_{layer}")
        else:
            j = layer - N_A_LAYERS
            if k_sh is None:
                k_sh, v_sh = _shared_kv(h, kv_norm_g.reshape(1, D_MODEL), w_kv_shared.astype(BF16), cos, sin)
            qt, gmt, o_mem = _b_in_proj(layer, h, pre_g, b_w_in, cos, sin, mem_k, mem_v, tiles_per_seq)
            o_main_t = _swa(b_sinks[j], qt, k_sh, v_sh, gmt, batch)
            h = _out_proj(o_main_t, o_mem, b_w_out, j, post_g, h, f"b_out_proj_{layer}", main_transposed=True)
    return h.reshape(batch, seq, D_MODEL)
```

```python
import functools

import jax
import jax.numpy as jnp
from jax import lax
from jax.experimental import pallas as pl
from jax.experimental.pallas import tpu as pltpu

D_MODEL = 1024
DEPTH = 4
N_A_LAYERS = DEPTH // 2

HG_HEADS = 8
HG_DK = 128
HG_DV = D_MODEL // HG_HEADS
HG_KEY_WIDTH = HG_HEADS * HG_DK
HG_VAL_WIDTH = HG_HEADS * HG_DV
HG_CHUNK = 64

SWA_Q_HEADS = 16
SWA_KV_HEADS = 4
SWA_GROUP = SWA_Q_HEADS // SWA_KV_HEADS
SWA_HEAD_DIM = 64
SWA_WIDTH = SWA_Q_HEADS * SWA_HEAD_DIM
SWA_KV_WIDTH = SWA_KV_HEADS * SWA_HEAD_DIM
WINDOW = 128

MEM_TOKENS = 256
MEM_HEADS = 4
MEM_HEAD_DIM = 128
MEM_WIDTH = MEM_HEADS * MEM_HEAD_DIM

ROPE_THETA = 10000.0
NORM_EPS = 1e-6

A_IN_WIDTH = 2 * HG_KEY_WIDTH + 2 * HG_VAL_WIDTH + 2 * MEM_WIDTH
B_IN_WIDTH = 2 * SWA_WIDTH + 2 * MEM_WIDTH
OUT_WIDTH = HG_VAL_WIDTH + MEM_WIDTH

LANES = 128
SUBLANES = 8
MASK_VALUE = -1e30
LOG2E = 1.4426950408889634

IN_TILE = 512
IN_SUB_TILE = 256
OUT_TILE = 512
HG_TILE = 256
SWA_TILE = 512
VMEM_LIMIT = 56 * 1024 * 1024

F32 = jnp.float32
BF16 = jnp.bfloat16

_NT = (((1,), (1,)), ((), ()))
_TN = (((0,), (0,)), ((), ()))


def _dot(a, b):
    return jnp.dot(a, b, preferred_element_type=F32)


def _dot_nt(a, b):
    return lax.dot_general(a, b, _NT, preferred_element_type=F32)


def _dot_tn(a, b):
    return lax.dot_general(a, b, _TN, preferred_element_type=F32)


def _rms_norm(x, g):
    ms = jnp.mean(x * x, axis=-1, keepdims=True)
    return x * lax.rsqrt(ms + NORM_EPS) * g


def _sigmoid(x):
    return 1.0 / (1.0 + jnp.exp(-x))


def _silu(x):
    return x * _sigmoid(x)


def _params(*semantics):
    return pltpu.CompilerParams(dimension_semantics=semantics, vmem_limit_bytes=VMEM_LIMIT)


def _layer_weight(w, layer):
    return pl.BlockSpec((None,) + w.shape[1:], lambda *_: (layer, 0, 0))


def _rope_table_kernel(pos_ref, invf_ref, sign_ref, cos_ref, sin_ref):
    ang = pos_ref[...].astype(F32) * invf_ref[...]
    cos_ref[...] = jnp.cos(ang)
    sin_ref[...] = jnp.sin(ang) * sign_ref[...]


def _rope_tables(positions):
    n = positions.size
    half = SWA_HEAD_DIM // 2
    inv_freq = ROPE_THETA ** (-jnp.arange(0, SWA_HEAD_DIM, 2, dtype=F32) / SWA_HEAD_DIM)
    invf = jnp.tile(inv_freq, LANES // half).reshape(1, LANES)
    sign = jnp.tile(jnp.concatenate([-jnp.ones((half,), F32), jnp.ones((half,), F32)]),
                    LANES // SWA_HEAD_DIM).reshape(1, LANES)
    pos = jnp.broadcast_to(positions.reshape(n, 1), (n, LANES))
    tile = 2048
    row = pl.BlockSpec((tile, LANES), lambda i: (i, 0))
    vec = pl.BlockSpec((1, LANES), lambda i: (0, 0))
    return pl.pallas_call(
        _rope_table_kernel,
        grid=(n // tile,),
        in_specs=[row, vec, vec],
        out_specs=[row, row],
        out_shape=[jax.ShapeDtypeStruct((n, LANES), F32)] * 2,
        compiler_params=_params("parallel"),
        name="rope_tables",
    )(pos, invf, sign)


def _rope_block(xc, cos, sin, first_half):
    swapped = jnp.where(first_half,
                        pltpu.roll(xc, LANES - SWA_HEAD_DIM // 2, 1),
                        pltpu.roll(xc, SWA_HEAD_DIM // 2, 1))
    return xc * cos + swapped * sin


def _first_half_mask(rows):
    lane = lax.broadcasted_iota(jnp.int32, (rows, LANES), 1)
    return (lane & (SWA_HEAD_DIM // 2)) == 0


def _mem_kv_kernel(mem_ref, g_ref, w_ref, k_ref, v_ref):
    mn = _rms_norm(mem_ref[...], g_ref[...]).astype(BF16)
    kv = _dot(mn, w_ref[...])
    k_ref[...] = kv[:, :MEM_WIDTH].astype(BF16)
    v_ref[...] = kv[:, MEM_WIDTH:].astype(BF16)


def _mem_kv(mem, mem_norm_g, w_mem_kv):
    b = mem.shape[0]
    out = pl.BlockSpec((None, None, MEM_TOKENS, MEM_WIDTH), lambda l, i: (l, i, 0, 0))
    return pl.pallas_call(
        _mem_kv_kernel,
        grid=(DEPTH, b),
        in_specs=[
            pl.BlockSpec((None, MEM_TOKENS, D_MODEL), lambda l, i: (i, 0, 0)),
            pl.BlockSpec((None, 1, D_MODEL), lambda l, i: (l, 0, 0)),
            pl.BlockSpec((None, D_MODEL, 2 * MEM_WIDTH), lambda l, i: (l, 0, 0)),
        ],
        out_specs=[out, out],
        out_shape=[jax.ShapeDtypeStruct((DEPTH, b, MEM_TOKENS, MEM_WIDTH), BF16)] * 2,
        compiler_params=_params("arbitrary", "arbitrary"),
        name="mem_kv",
    )(mem, mem_norm_g.reshape(DEPTH, 1, D_MODEL), w_mem_kv)


def _memory_probs(mq, mk_ref):
    mq = (mq * (MEM_HEAD_DIM ** -0.5)).astype(BF16)
    probs, inv_sums = [], []
    for hh in range(MEM_HEADS):
        hs = slice(hh * MEM_HEAD_DIM, (hh + 1) * MEM_HEAD_DIM)
        s = _dot_nt(mq[:, hs], mk_ref[:, hs])
        p = jnp.exp(s - jnp.max(s, axis=-1, keepdims=True))
        probs.append(p.astype(BF16))
        inv_sums.append(1.0 / jnp.sum(p, axis=-1, keepdims=True))
    return probs, inv_sums


def _memory_readout(probs, inv_sums, gate, mv_ref, out_ref, rows):
    for hh in range(MEM_HEADS):
        hs = slice(hh * MEM_HEAD_DIM, (hh + 1) * MEM_HEAD_DIM)
        o = _dot(probs[hh], mv_ref[:, hs]) * inv_sums[hh]
        out_ref[rows, hs] = (o * gate[:, hs]).astype(BF16)


def _a_in_kernel(layer, x_ref, g_ref, w_ref, lbl_ref, mk_ref, mv_ref,
                 qs_ref, lf_ref, kk_ref, v_ref, gm_ref, om_ref):
    logits = lbl_ref[...]
    e = jnp.exp(logits - jnp.max(logits, axis=0, keepdims=True))
    lb = jnp.sum(e[:layer + 1], axis=0, keepdims=True) / jnp.sum(e, axis=0, keepdims=True)

    col_q, col_f, col_v = 0, HG_KEY_WIDTH, 2 * HG_KEY_WIDTH
    col_g = col_v + HG_VAL_WIDTH
    col_mq = col_g + HG_VAL_WIDTH
    col_mg = col_mq + MEM_WIDTH

    for sub in range(IN_TILE // IN_SUB_TILE):
        rows = slice(sub * IN_SUB_TILE, (sub + 1) * IN_SUB_TILE)
        xn = _rms_norm(x_ref[rows, :], g_ref[...]).astype(BF16)

        def proj(lo, width):
            return _dot(xn, w_ref[:, lo:lo + width])

        mq = proj(col_mq, MEM_WIDTH)
        mem_gate = _silu(proj(col_mg, MEM_WIDTH))
        qs_ref[rows, :] = _silu(proj(col_q, HG_KEY_WIDTH)).astype(BF16)
        probs, inv_sums = _memory_probs(mq, mk_ref)

        fp = proj(col_f, HG_KEY_WIDTH)
        t = jnp.exp(-jnp.abs(fp))
        r = 1.0 / (1.0 + t)
        pos = fp >= 0
        sig = jnp.where(pos, r, t * r)
        sig_neg = jnp.where(pos, t * r, r)
        lf_ref[rows, :] = jnp.log2(lb + (1.0 - lb) * sig)
        kk_ref[rows, :] = ((1.0 - lb) * sig_neg).astype(BF16)

        _memory_readout(probs, inv_sums, mem_gate, mv_ref, om_ref, rows)
        v_ref[rows, :] = proj(col_v, HG_VAL_WIDTH).astype(BF16)
        gm_ref[rows, :] = _silu(proj(col_g, HG_VAL_WIDTH)).astype(BF16)


def _a_in_proj(layer, h, g, w, lb_logits, mem_k, mem_v, tiles_per_seq):
    n = h.shape[0]
    row = lambda width: pl.BlockSpec((IN_TILE, width), lambda i: (i, 0))
    full = lambda a: pl.BlockSpec(a.shape, lambda i: (0,) * a.ndim)
    mem = pl.BlockSpec((None, None, MEM_TOKENS, MEM_WIDTH), lambda i: (layer, i // tiles_per_seq, 0, 0))
    widths = (HG_KEY_WIDTH, HG_KEY_WIDTH, HG_KEY_WIDTH, HG_VAL_WIDTH, HG_VAL_WIDTH, MEM_WIDTH)
    dtypes = (BF16, F32, BF16, BF16, BF16, BF16)
    return pl.pallas_call(
        functools.partial(_a_in_kernel, layer),
        grid=(n // IN_TILE,),
        in_specs=[row(D_MODEL), full(g), _layer_weight(w, layer), full(lb_logits), mem, mem],
        out_specs=[row(wd) for wd in widths],
        out_shape=[jax.ShapeDtypeStruct((n, wd), dt) for wd, dt in zip(widths, dtypes)],
        compiler_params=_params("parallel"),
        name=f"a_in_proj_{layer}",
    )(h, g, w, lb_logits, mem_k, mem_v)


_LEVELS = (32, 16, 8, 4, 2, 1)
_SAFE_BLOCK = 16
SAFE_BLOCK_DECAY = 100.0


def _coarse_level(cum, qs, kk, m):
    c, w = cum.shape
    early = m % (2 * SUBLANES) == 0
    dt = BF16 if early else F32
    zeros = jnp.zeros((m, w), dt)
    qd, kd = [], []
    for blk in range(c // (2 * m)):
        lo, mid, hi = blk * 2 * m, blk * 2 * m + m, (blk + 1) * 2 * m
        ref = cum[mid - 1:mid, :]
        kd += [(kk[lo:mid] * jnp.exp2(ref - cum[lo:mid])).astype(dt), zeros]
        qd += [zeros, (qs[mid:hi] * jnp.exp2(cum[mid:hi] - ref)).astype(dt)]
    return jnp.concatenate(qd, axis=0).astype(BF16), jnp.concatenate(kd, axis=0).astype(BF16)


def _fine_level(cum3, qs3, kk3, m, upper_pen, lower_pen):
    groups, _, w = cum3.shape
    parts = [jnp.broadcast_to(cum3[:, blk * 2 * m + m - 1:blk * 2 * m + m, :], (groups, 2 * m, w))
             for blk in range(SUBLANES // (2 * m))]
    x = cum3 - (parts[0] if len(parts) == 1 else jnp.concatenate(parts, axis=1))
    qd = qs3 * jnp.exp2(x + upper_pen)
    kd = kk3 * jnp.exp2(lower_pen - x)
    c = groups * SUBLANES
    return qd.reshape(c, w).astype(BF16), kd.reshape(c, w).astype(BF16)


def _block_level(cum, qs, kk):
    c, _ = cum.shape
    local = [cum[:_SAFE_BLOCK]] + [cum[b:b + _SAFE_BLOCK] - cum[b - 1:b] for b in range(_SAFE_BLOCK, c, _SAFE_BLOCK)]
    local = jnp.concatenate(local, axis=0)
    return (qs * jnp.exp2(local)).astype(BF16), (kk * jnp.exp2(-local)).astype(BF16)


def _hgrn_constants():
    c = HG_CHUNK
    row = lax.broadcasted_iota(jnp.int32, (c, c), 0)
    col = lax.broadcasted_iota(jnp.int32, (c, c), 1)
    sub = lax.broadcasted_iota(jnp.int32, (1, SUBLANES, HG_DK), 1)
    fine = [m for m in _LEVELS if m < SUBLANES]
    same_block = {m: (row // (2 * m)) == (col // (2 * m)) for m in _LEVELS if 2 * m < c}
    return dict(
        same_block={m: mask.astype(F32) for m, mask in same_block.items()},
        diag=(row == col).astype(F32),
        causal_block=(same_block[_SAFE_BLOCK // 2] & (col <= row)).astype(F32),
        upper_pen={m: jnp.where((sub & m) != 0, 0.0, MASK_VALUE) for m in fine},
        lower_pen={m: jnp.where((sub & m) != 0, MASK_VALUE, 0.0) for m in fine},
        even_row=jnp.where((sub & 1) != 0, 0.0, 1.0),
    )


def _hgrn_tile(single_reference_blocks, k, qs_ref, lf_ref, kk_ref, v_ref, gm_ref, gn, cum_s, o_ref, st_ref):
    c = HG_CHUNK
    heads = [slice(h * HG_DK, (h + 1) * HG_DK) for h in range(HG_HEADS)]
    shape3 = (c // SUBLANES, SUBLANES, HG_DK)
    for ci in range(HG_TILE // c):
        rows = slice(ci * c, (ci + 1) * c)
        scores_bf, q_ins, k_outs, lasts = [], [], [], []
        for hs in heads:
            cum = cum_s[rows, hs]
            qs = qs_ref[rows, hs].astype(F32)
            kk = kk_ref[rows, hs].astype(F32)

            pairs = []
            for m in _LEVELS:
                if single_reference_blocks and 2 * m <= _SAFE_BLOCK:
                    continue
                if m >= SUBLANES:
                    qd, kd = _coarse_level(cum, qs, kk, m)
                elif m == 1:
                    qs3, kk3 = qs.reshape(shape3), kk.reshape(shape3)
                    qd3 = qs3 * jnp.exp2(lf_ref[rows, hs].reshape(shape3) + k["upper_pen"][m])
                    qd = qd3.reshape(c, HG_DK).astype(BF16)
                    kd = (kk3 * k["even_row"]).reshape(c, HG_DK).astype(BF16)
                else:
                    qd, kd = _fine_level(cum.reshape(shape3), qs.reshape(shape3), kk.reshape(shape3), m,
                                         k["upper_pen"][m], k["lower_pen"][m])
                pairs.append((qd, kd, k["same_block"].get(m)))
            if single_reference_blocks:
                pairs.append(_block_level(cum, qs, kk) + (k["causal_block"],))
            else:
                pairs.append((qs.astype(BF16), kk.astype(BF16), k["diag"]))

            scores = None
            for qd, kd, mask in pairs:
                s = _dot_nt(qd, kd)
                if mask is not None:
                    s = s * mask
                scores = s if scores is None else scores + s

            last = cum[c - 1:c, :]
            scores_bf.append(scores.astype(BF16))
            q_ins.append((qs * jnp.exp2(cum)).astype(BF16))
            k_outs.append((kk * jnp.exp2(last - cum)).astype(BF16))
            lasts.append(last)

        for h, hs in enumerate(heads):
            o = _dot_nt(q_ins[h], st_ref[h].astype(BF16)) + _dot(scores_bf[h], v_ref[rows, hs])
            o = o * lax.rsqrt(jnp.mean(o * o, axis=-1, keepdims=True) + NORM_EPS)
            o_ref[rows, hs] = (o * gn[:, hs] * gm_ref[rows, hs].astype(F32)).astype(BF16)

        for h, hs in enumerate(heads):
            st_ref[h] = st_ref[h] * jnp.exp2(lasts[h]) + _dot_tn(v_ref[rows, hs], k_outs[h])


def _hgrn_kernel(qs_ref, lf_ref, kk_ref, v_ref, gm_ref, gn_ref, o_ref, cum_s, st_ref):
    c = HG_CHUNK

    @pl.when(pl.program_id(1) == 0)
    def _():
        st_ref[...] = jnp.zeros_like(st_ref)

    row = lax.broadcasted_iota(jnp.int32, (c, c), 0)
    col = lax.broadcasted_iota(jnp.int32, (c, c), 1)
    tri = (col <= row).astype(BF16)

    worst = None
    for ci in range(HG_TILE // c):
        rows = slice(ci * c, (ci + 1) * c)
        lf = lf_ref[rows, :]
        hi = lf.astype(BF16)
        lo = (lf - hi.astype(F32)).astype(BF16)
        cum = _dot(tri, hi) + _dot(tri, lo)
        cum_s[rows, :] = cum
        ends = [cum[b - 1:b, :] for b in range(_SAFE_BLOCK, c + 1, _SAFE_BLOCK)]
        for b, end in enumerate(ends):
            decay = end if b == 0 else end - ends[b - 1]
            worst = decay if worst is None else jnp.minimum(worst, decay)
    single_reference_ok = jnp.min(worst) >= -SAFE_BLOCK_DECAY

    consts = _hgrn_constants()
    args = (consts, qs_ref, lf_ref, kk_ref, v_ref, gm_ref, gn_ref[...], cum_s, o_ref, st_ref)

    @pl.when(single_reference_ok)
    def _():
        _hgrn_tile(True, *args)

    @pl.when(jnp.logical_not(single_reference_ok))
    def _():
        _hgrn_tile(False, *args)


def _hgrn(qs, lf, kk, v, gm, gn, batch):
    n = qs.shape[0]
    tiles = n // batch // HG_TILE
    row = pl.BlockSpec((HG_TILE, HG_KEY_WIDTH), lambda b, t: (b * tiles + t, 0))
    return pl.pallas_call(
        _hgrn_kernel,
        grid=(batch, tiles),
        in_specs=[row, row, row, row, row, pl.BlockSpec((1, HG_VAL_WIDTH), lambda b, t: (0, 0))],
        out_specs=row,
        out_shape=jax.ShapeDtypeStruct((n, HG_VAL_WIDTH), BF16),
        scratch_shapes=[pltpu.VMEM((HG_TILE, HG_KEY_WIDTH), F32),
                        pltpu.VMEM((HG_HEADS, HG_DV, HG_DK), F32)],
        compiler_params=_params("arbitrary", "arbitrary"),
        name="hgrn2",
    )(qs, lf, kk, v, gm, gn)


def _out_kernel(main_transposed, om_ref, ome_ref, w_ref, g_ref, h_ref, o_ref):
    main_dot = _dot_tn if main_transposed else _dot
    y = main_dot(om_ref[...], w_ref[:HG_VAL_WIDTH, :]) + _dot(ome_ref[...], w_ref[HG_VAL_WIDTH:, :])
    o_ref[...] = h_ref[...] + _rms_norm(y, g_ref[...])


def _out_proj(o_main, o_mem, w, layer, g, h, name, main_transposed=False):
    n = h.shape[0]
    row = lambda width: pl.BlockSpec((OUT_TILE, width), lambda i: (i, 0))
    col = lambda width: pl.BlockSpec((width, OUT_TILE), lambda i: (0, i))
    full = lambda a: pl.BlockSpec(a.shape, lambda i: (0,) * a.ndim)
    main = col(HG_VAL_WIDTH) if main_transposed else row(HG_VAL_WIDTH)
    return pl.pallas_call(
        functools.partial(_out_kernel, main_transposed),
        grid=(n // OUT_TILE,),
        in_specs=[main, row(MEM_WIDTH), _layer_weight(w, layer), full(g), row(D_MODEL)],
        out_specs=row(D_MODEL),
        out_shape=jax.ShapeDtypeStruct((n, D_MODEL), F32),
        compiler_params=_params("parallel"),
        name=name,
    )(o_main, o_mem, w, g, h)


def _shared_kv_kernel(x_ref, g_ref, w_ref, cos_ref, sin_ref, k_ref, v_ref):
    xn = _rms_norm(x_ref[...], g_ref[...]).astype(BF16)
    kv = _dot(xn, w_ref[...])
    cos = cos_ref[...]
    sin = sin_ref[...]
    first = _first_half_mask(IN_TILE)
    heads_per_block = LANES // SWA_HEAD_DIM
    for cb in range(SWA_KV_WIDTH // LANES):
        kr = _rope_block(kv[:, cb * LANES:(cb + 1) * LANES], cos, sin, first).astype(BF16)
        for j in range(heads_per_block):
            k_ref[cb * heads_per_block + j] = kr[:, j * SWA_HEAD_DIM:(j + 1) * SWA_HEAD_DIM]
    v_ref[...] = kv[:, SWA_KV_WIDTH:].T.astype(BF16)


def _shared_kv(h, g, w, cos, sin):
    n = h.shape[0]
    row = lambda width: pl.BlockSpec((IN_TILE, width), lambda i: (i, 0))
    full = lambda a: pl.BlockSpec(a.shape, lambda i: (0,) * a.ndim)
    return pl.pallas_call(
        _shared_kv_kernel,
        grid=(n // IN_TILE,),
        in_specs=[row(D_MODEL), full(g), full(w), row(LANES), row(LANES)],
        out_specs=[pl.BlockSpec((SWA_KV_HEADS, IN_TILE, SWA_HEAD_DIM), lambda i: (0, i, 0)),
                   pl.BlockSpec((SWA_KV_WIDTH, IN_TILE), lambda i: (0, i))],
        out_shape=[jax.ShapeDtypeStruct((SWA_KV_HEADS, n, SWA_HEAD_DIM), BF16),
                   jax.ShapeDtypeStruct((SWA_KV_WIDTH, n), BF16)],
        compiler_params=_params("parallel"),
        name="shared_kv",
    )(h, g, w, cos, sin)


def _b_in_kernel(x_ref, g_ref, w_ref, cos_ref, sin_ref, mk_ref, mv_ref, qt_ref, gmt_ref, om_ref):
    first = _first_half_mask(IN_SUB_TILE)
    scale = SWA_HEAD_DIM ** -0.5 * LOG2E
    col_mq = 2 * SWA_WIDTH
    col_mg = col_mq + MEM_WIDTH

    for sub in range(IN_TILE // IN_SUB_TILE):
        rows = slice(sub * IN_SUB_TILE, (sub + 1) * IN_SUB_TILE)
        xn = _rms_norm(x_ref[rows, :], g_ref[...]).astype(BF16)

        def proj(lo, width):
            return _dot(xn, w_ref[:, lo:lo + width])

        mq = proj(col_mq, MEM_WIDTH)
        mem_gate = _silu(proj(col_mg, MEM_WIDTH))

        q = proj(0, SWA_WIDTH)
        probs, inv_sums = _memory_probs(mq, mk_ref)
        cos = cos_ref[rows, :]
        sin = sin_ref[rows, :]
        for cb in range(SWA_WIDTH // LANES):
            cs = slice(cb * LANES, (cb + 1) * LANES)
            qt_ref[cs, rows] = (_rope_block(q[:, cs], cos, sin, first) * scale).T.astype(BF16)

        gate = _silu(proj(SWA_WIDTH, SWA_WIDTH))
        _memory_readout(probs, inv_sums, mem_gate, mv_ref, om_ref, rows)
        for cb in range(SWA_WIDTH // LANES):
            cs = slice(cb * LANES, (cb + 1) * LANES)
            gmt_ref[cs, rows] = gate[:, cs].T.astype(BF16)


def _b_in_proj(layer, h, g, w, cos, sin, mem_k, mem_v, tiles_per_seq):
    n = h.shape[0]
    row = lambda width: pl.BlockSpec((IN_TILE, width), lambda i: (i, 0))
    col = pl.BlockSpec((SWA_WIDTH, IN_TILE), lambda i: (0, i))
    full = lambda a: pl.BlockSpec(a.shape, lambda i: (0,) * a.ndim)
    mem = pl.BlockSpec((None, None, MEM_TOKENS, MEM_WIDTH), lambda i: (layer, i // tiles_per_seq, 0, 0))
    return pl.pallas_call(
        _b_in_kernel,
        grid=(n // IN_TILE,),
        in_specs=[row(D_MODEL), full(g), _layer_weight(w, layer - N_A_LAYERS), row(LANES), row(LANES), mem, mem],
        out_specs=[col, col, row(MEM_WIDTH)],
        out_shape=[jax.ShapeDtypeStruct((SWA_WIDTH, n), BF16), jax.ShapeDtypeStruct((SWA_WIDTH, n), BF16),
                   jax.ShapeDtypeStruct((n, MEM_WIDTH), BF16)],
        compiler_params=_params("parallel"),
        name=f"b_in_proj_{layer}",
    )(h, g, w, cos, sin, mem_k, mem_v)


def _swa_kernel(sink_ref, qt_ref, kc_ref, kp_ref, vtc_ref, vtp_ref, gmt_ref, ot_ref):
    blk = WINDOW
    dh = SWA_HEAD_DIM
    ki = lax.broadcasted_iota(jnp.int32, (2 * blk, blk), 0)
    qi = lax.broadcasted_iota(jnp.int32, (2 * blk, blk), 1)
    d = ki - qi
    band = (d > 0) & (d <= blk)
    first_key = jnp.where(pl.program_id(1) > 0, 0, blk)

    for sb in range(SWA_TILE // blk):
        cols = slice(sb * blk, (sb + 1) * blk)
        two = slice((sb - 1) * blk, (sb + 1) * blk)
        mask = band & (ki >= first_key) if sb == 0 else band

        probs, inv_den = [], []
        for g in range(SWA_KV_HEADS):
            k2 = jnp.concatenate([kp_ref[g], kc_ref[g, cols, :]], axis=0) if sb == 0 else kc_ref[g, two, :]
            heads = range(g * SWA_GROUP, (g + 1) * SWA_GROUP)
            qt = jnp.concatenate([qt_ref[h * dh:(h + 1) * dh, cols] for h in heads], axis=1)
            st = _dot(k2, qt)
            ps = []
            for j, h in enumerate(heads):
                s = jnp.where(mask, st[:, j * blk:(j + 1) * blk], MASK_VALUE)
                sink = sink_ref[h] * LOG2E
                m = jnp.maximum(jnp.max(s, axis=0, keepdims=True), sink)
                p = jnp.exp2(s - m)
                inv_den.append(1.0 / (jnp.sum(p, axis=0, keepdims=True) + jnp.exp2(sink - m)))
                ps.append(p.astype(BF16))
            probs.append(jnp.concatenate(ps, axis=1))

        for g in range(SWA_KV_HEADS):
            vrows = slice(g * dh, (g + 1) * dh)
            vt2 = (jnp.concatenate([vtp_ref[vrows, :], vtc_ref[vrows, cols]], axis=1) if sb == 0
                   else vtc_ref[vrows, two])
            ot = _dot(vt2, probs[g])
            for j, h in enumerate(range(g * SWA_GROUP, (g + 1) * SWA_GROUP)):
                rows = slice(h * dh, (h + 1) * dh)
                o = ot[:, j * blk:(j + 1) * blk] * inv_den[h]
                ot_ref[rows, cols] = (o * gmt_ref[rows, cols].astype(F32)).astype(BF16)


def _swa(sinks, qt, k, vt, gmt, batch):
    n = qt.shape[1]
    tiles = n // batch // SWA_TILE
    sub = SWA_TILE // WINDOW
    tile = lambda b, t: b * tiles + t
    prev = lambda b, t: jnp.maximum(tile(b, t) * sub - 1, 0)
    wide = pl.BlockSpec((SWA_WIDTH, SWA_TILE), lambda b, t: (0, tile(b, t)))
    return pl.pallas_call(
        _swa_kernel,
        grid=(batch, tiles),
        in_specs=[pl.BlockSpec(memory_space=pltpu.SMEM),
                  wide,
                  pl.BlockSpec((SWA_KV_HEADS, SWA_TILE, SWA_HEAD_DIM), lambda b, t: (0, tile(b, t), 0)),
                  pl.BlockSpec((SWA_KV_HEADS, WINDOW, SWA_HEAD_DIM), lambda b, t: (0, prev(b, t), 0)),
                  pl.BlockSpec((SWA_KV_WIDTH, SWA_TILE), lambda b, t: (0, tile(b, t))),
                  pl.BlockSpec((SWA_KV_WIDTH, WINDOW), lambda b, t: (0, prev(b, t))),
                  wide],
        out_specs=wide,
        out_shape=jax.ShapeDtypeStruct((SWA_WIDTH, n), BF16),
        compiler_params=_params("parallel", "arbitrary"),
        name="swa",
    )(sinks, qt, k, k, vt, vt, gmt)


def kernel(x, mem, positions, pre_norm_g, post_norm_g, mem_norm_g, w_mem_kv, a_w_in, a_lb_logits, a_out_norm_g, a_w_out, kv_norm_g, w_kv_shared, b_w_in, b_sinks, b_w_out):
    batch, seq, _ = x.shape
    n = batch * seq
    assert seq % max(IN_TILE, OUT_TILE, HG_TILE, SWA_TILE) == 0
    tiles_per_seq = seq // IN_TILE

    h = x.reshape(n, D_MODEL)
    mem_k, mem_v = _mem_kv(mem, mem_norm_g, w_mem_kv.astype(BF16))
    cos, sin = _rope_tables(positions)
    a_w_in = a_w_in.astype(BF16)
    a_w_out = a_w_out.astype(BF16)
    b_w_in = b_w_in.astype(BF16)
    b_w_out = b_w_out.astype(BF16)
    k_sh = v_sh = None

    for layer in range(DEPTH):
        pre_g = pre_norm_g[layer].reshape(1, D_MODEL)
        post_g = post_norm_g[layer].reshape(1, D_MODEL)
        if layer < N_A_LAYERS:
            qs, lf, kk, v, gm, o_mem = _a_in_proj(layer, h, pre_g, a_w_in, a_lb_logits,
                                                  mem_k, mem_v, tiles_per_seq)
            o_main = _hgrn(qs, lf, kk, v, gm, a_out_norm_g[layer].reshape(1, HG_VAL_WIDTH), batch)
            h = _out_proj(o_main, o_mem, a_w_out, layer, post_g, h, f"a_out_proj_{layer}")
        else:
            j = layer - N_A_LAYERS
            if k_sh is None:
                k_sh, v_sh = _shared_kv(h, kv_norm_g.reshape(1, D_MODEL), w_kv_shared.astype(BF16), cos, sin)
            qt, gmt, o_mem = _b_in_proj(layer, h, pre_g, b_w_in, cos, sin, mem_k, mem_v, tiles_per_seq)
            o_main_t = _swa(b_sinks[j], qt, k_sh, v_sh, gmt, batch)
            h = _out_proj(o_main_t, o_mem, b_w_out, j, post_g, h, f"b_out_proj_{layer}", main_transposed=True)
    return h.reshape(batch, seq, D_MODEL)
```

```python
import functools

import jax
import jax.numpy as jnp
from jax import lax
from jax.experimental import pallas as pl
from jax.experimental.pallas import tpu as pltpu

D_MODEL = 1024
DEPTH = 4
N_A_LAYERS = DEPTH // 2

HG_HEADS = 8
HG_DK = 128
HG_DV = D_MODEL // HG_HEADS
HG_KEY_WIDTH = HG_HEADS * HG_DK
HG_VAL_WIDTH = HG_HEADS * HG_DV
HG_CHUNK = 64

SWA_Q_HEADS = 16
SWA_KV_HEADS = 4
SWA_GROUP = SWA_Q_HEADS // SWA_KV_HEADS
SWA_HEAD_DIM = 64
SWA_WIDTH = SWA_Q_HEADS * SWA_HEAD_DIM
SWA_KV_WIDTH = SWA_KV_HEADS * SWA_HEAD_DIM
WINDOW = 128

MEM_TOKENS = 256
MEM_HEADS = 4
MEM_HEAD_DIM = 128
MEM_WIDTH = MEM_HEADS * MEM_HEAD_DIM

ROPE_THETA = 10000.0
NORM_EPS = 1e-6

A_IN_WIDTH = 2 * HG_KEY_WIDTH + 2 * HG_VAL_WIDTH + 2 * MEM_WIDTH
B_IN_WIDTH = 2 * SWA_WIDTH + 2 * MEM_WIDTH
OUT_WIDTH = HG_VAL_WIDTH + MEM_WIDTH

LANES = 128
SUBLANES = 8
MASK_VALUE = -1e30
LOG2E = 1.4426950408889634

IN_TILE = 512
IN_SUB_TILE = 256
OUT_TILE = 512
HG_TILE = 256
SWA_TILE = 512
VMEM_LIMIT = 56 * 1024 * 1024

F32 = jnp.float32
BF16 = jnp.bfloat16

_NT = (((1,), (1,)), ((), ()))
_TN = (((0,), (0,)), ((), ()))


def _dot(a, b):
    return jnp.dot(a, b, preferred_element_type=F32)


def _dot_nt(a, b):
    return lax.dot_general(a, b, _NT, preferred_element_type=F32)


def _dot_tn(a, b):
    return lax.dot_general(a, b, _TN, preferred_element_type=F32)


def _rms_norm(x, g):
    ms = jnp.mean(x * x, axis=-1, keepdims=True)
    return x * lax.rsqrt(ms + NORM_EPS) * g


def _silu(x):
    h = 0.5 * x
    return h + h * jnp.tanh(h)


def _params(*semantics):
    return pltpu.CompilerParams(dimension_semantics=semantics, vmem_limit_bytes=VMEM_LIMIT)


def _layer_weight(w, layer):
    return pl.BlockSpec((None,) + w.shape[1:], lambda *_: (layer, 0, 0))


def _rope_table_kernel(pos_ref, invf_ref, sign_ref, cos_ref, sin_ref):
    ang = pos_ref[...].astype(F32) * invf_ref[...]
    cos_ref[...] = jnp.cos(ang)
    sin_ref[...] = jnp.sin(ang) * sign_ref[...]


def _rope_tables(positions):
    n = positions.size
    half = SWA_HEAD_DIM // 2
    inv_freq = ROPE_THETA ** (-jnp.arange(0, SWA_HEAD_DIM, 2, dtype=F32) / SWA_HEAD_DIM)
    invf = jnp.tile(inv_freq, LANES // half).reshape(1, LANES)
    sign = jnp.tile(jnp.concatenate([-jnp.ones((half,), F32), jnp.ones((half,), F32)]),
                    LANES // SWA_HEAD_DIM).reshape(1, LANES)
    pos = jnp.broadcast_to(positions.reshape(n, 1), (n, LANES))
    tile = 2048
    row = pl.BlockSpec((tile, LANES), lambda i: (i, 0))
    vec = pl.BlockSpec((1, LANES), lambda i: (0, 0))
    return pl.pallas_call(
        _rope_table_kernel,
        grid=(n // tile,),
        in_specs=[row, vec, vec],
        out_specs=[row, row],
        out_shape=[jax.ShapeDtypeStruct((n, LANES), F32)] * 2,
        compiler_params=_params("parallel"),
        name="rope_tables",
    )(pos, invf, sign)


def _rope_block(xc, cos, sin, first_half):
    swapped = jnp.where(first_half,
                        pltpu.roll(xc, LANES - SWA_HEAD_DIM // 2, 1),
                        pltpu.roll(xc, SWA_HEAD_DIM // 2, 1))
    return xc * cos + swapped * sin


def _first_half_mask(rows):
    lane = lax.broadcasted_iota(jnp.int32, (rows, LANES), 1)
    return (lane & (SWA_HEAD_DIM // 2)) == 0


def _mem_kv_kernel(mem_ref, g_ref, w_ref, k_ref, v_ref):
    mn = _rms_norm(mem_ref[...], g_ref[...]).astype(BF16)
    kv = _dot(mn, w_ref[...])
    k_ref[...] = kv[:, :MEM_WIDTH].astype(BF16)
    v_ref[...] = kv[:, MEM_WIDTH:].astype(BF16)


def _mem_kv(mem, mem_norm_g, w_mem_kv):
    b = mem.shape[0]
    out = pl.BlockSpec((None, None, MEM_TOKENS, MEM_WIDTH), lambda l, i: (l, i, 0, 0))
    return pl.pallas_call(
        _mem_kv_kernel,
        grid=(DEPTH, b),
        in_specs=[
            pl.BlockSpec((None, MEM_TOKENS, D_MODEL), lambda l, i: (i, 0, 0)),
            pl.BlockSpec((None, 1, D_MODEL), lambda l, i: (l, 0, 0)),
            pl.BlockSpec((None, D_MODEL, 2 * MEM_WIDTH), lambda l, i: (l, 0, 0)),
        ],
        out_specs=[out, out],
        out_shape=[jax.ShapeDtypeStruct((DEPTH, b, MEM_TOKENS, MEM_WIDTH), BF16)] * 2,
        compiler_params=_params("arbitrary", "arbitrary"),
        name="mem_kv",
    )(mem, mem_norm_g.reshape(DEPTH, 1, D_MODEL), w_mem_kv)


def _memory_probs(mq, mk_ref):
    mq = (mq * (MEM_HEAD_DIM ** -0.5 * LOG2E)).astype(BF16)
    probs, inv_sums = [], []
    for hh in range(MEM_HEADS):
        hs = slice(hh * MEM_HEAD_DIM, (hh + 1) * MEM_HEAD_DIM)
        s = _dot_nt(mq[:, hs], mk_ref[:, hs])
        p = jnp.exp2(s - jnp.max(s, axis=-1, keepdims=True))
        probs.append(p.astype(BF16))
        inv_sums.append(1.0 / jnp.sum(p, axis=-1, keepdims=True))
    return probs, inv_sums


def _memory_readout(probs, inv_sums, gate, mv_ref, out_ref, rows):
    for hh in range(MEM_HEADS):
        hs = slice(hh * MEM_HEAD_DIM, (hh + 1) * MEM_HEAD_DIM)
        o = _dot(probs[hh], mv_ref[:, hs]) * inv_sums[hh]
        out_ref[rows, hs] = (o * gate[:, hs]).astype(BF16)


def _a_in_kernel(layer, x_ref, g_ref, w_ref, lbl_ref, mk_ref, mv_ref,
                 qs_ref, lf_ref, kk_ref, v_ref, gm_ref, om_ref):
    logits = lbl_ref[...]
    e = jnp.exp(logits - jnp.max(logits, axis=0, keepdims=True))
    lb = jnp.sum(e[:layer + 1], axis=0, keepdims=True) / jnp.sum(e, axis=0, keepdims=True)
    mid = 0.5 * (1.0 + lb)
    half = 0.5 * (1.0 - lb)

    col_q, col_f, col_v = 0, HG_KEY_WIDTH, 2 * HG_KEY_WIDTH
    col_g = col_v + HG_VAL_WIDTH
    col_mq = col_g + HG_VAL_WIDTH
    col_mg = col_mq + MEM_WIDTH

    for sub in range(IN_TILE // IN_SUB_TILE):
        rows = slice(sub * IN_SUB_TILE, (sub + 1) * IN_SUB_TILE)
        xn = _rms_norm(x_ref[rows, :], g_ref[...]).astype(BF16)

        def proj(lo, width):
            return _dot(xn, w_ref[:, lo:lo + width])

        mq = proj(col_mq, MEM_WIDTH)
        mem_gate = _silu(proj(col_mg, MEM_WIDTH))
        qs_ref[rows, :] = _silu(proj(col_q, HG_KEY_WIDTH)).astype(BF16)
        probs, inv_sums = _memory_probs(mq, mk_ref)

        th = half * jnp.tanh(0.5 * proj(col_f, HG_KEY_WIDTH))
        lf_ref[rows, :] = jnp.log2(mid + th)
        kk_ref[rows, :] = (half - th).astype(BF16)

        _memory_readout(probs, inv_sums, mem_gate, mv_ref, om_ref, rows)
        v_ref[rows, :] = proj(col_v, HG_VAL_WIDTH).astype(BF16)
        gm_ref[rows, :] = _silu(proj(col_g, HG_VAL_WIDTH)).astype(BF16)


def _a_in_proj(layer, h, g, w, lb_logits, mem_k, mem_v, tiles_per_seq):
    n = h.shape[0]
    row = lambda width: pl.BlockSpec((IN_TILE, width), lambda i: (i, 0))
    full = lambda a: pl.BlockSpec(a.shape, lambda i: (0,) * a.ndim)
    mem = pl.BlockSpec((None, None, MEM_TOKENS, MEM_WIDTH), lambda i: (layer, i // tiles_per_seq, 0, 0))
    widths = (HG_KEY_WIDTH, HG_KEY_WIDTH, HG_KEY_WIDTH, HG_VAL_WIDTH, HG_VAL_WIDTH, MEM_WIDTH)
    dtypes = (BF16, F32, BF16, BF16, BF16, BF16)
    return pl.pallas_call(
        functools.partial(_a_in_kernel, layer),
        grid=(n // IN_TILE,),
        in_specs=[row(D_MODEL), full(g), _layer_weight(w, layer), full(lb_logits), mem, mem],
        out_specs=[row(wd) for wd in widths],
        out_shape=[jax.ShapeDtypeStruct((n, wd), dt) for wd, dt in zip(widths, dtypes)],
        compiler_params=_params("parallel"),
        name=f"a_in_proj_{layer}",
    )(h, g, w, lb_logits, mem_k, mem_v)


_LEVELS = (32, 16, 8, 4, 2, 1)
_SAFE_BLOCK = 16
SAFE_BLOCK_DECAY = 100.0


def _coarse_level(cum, qs, kk, m):
    c, w = cum.shape
    early = m % (2 * SUBLANES) == 0
    dt = BF16 if early else F32
    zeros = jnp.zeros((m, w), dt)
    qd, kd = [], []
    for blk in range(c // (2 * m)):
        lo, mid, hi = blk * 2 * m, blk * 2 * m + m, (blk + 1) * 2 * m
        ref = cum[mid - 1:mid, :]
        kd += [(kk[lo:mid] * jnp.exp2(ref - cum[lo:mid])).astype(dt), zeros]
        qd += [zeros, (qs[mid:hi] * jnp.exp2(cum[mid:hi] - ref)).astype(dt)]
    return jnp.concatenate(qd, axis=0).astype(BF16), jnp.concatenate(kd, axis=0).astype(BF16)


def _fine_level(cum3, qs3, kk3, m, upper_pen, lower_pen):
    groups, _, w = cum3.shape
    parts = [jnp.broadcast_to(cum3[:, blk * 2 * m + m - 1:blk * 2 * m + m, :], (groups, 2 * m, w))
             for blk in range(SUBLANES // (2 * m))]
    x = cum3 - (parts[0] if len(parts) == 1 else jnp.concatenate(parts, axis=1))
    qd = qs3 * jnp.exp2(x + upper_pen)
    kd = kk3 * jnp.exp2(lower_pen - x)
    c = groups * SUBLANES
    return qd.reshape(c, w).astype(BF16), kd.reshape(c, w).astype(BF16)


def _block_level(cum, qs, kk):
    c, _ = cum.shape
    local = [cum[:_SAFE_BLOCK]] + [cum[b:b + _SAFE_BLOCK] - cum[b - 1:b] for b in range(_SAFE_BLOCK, c, _SAFE_BLOCK)]
    local = jnp.concatenate(local, axis=0)
    return (qs * jnp.exp2(local)).astype(BF16), (kk * jnp.exp2(-local)).astype(BF16)


def _hgrn_constants():
    c = HG_CHUNK
    row = lax.broadcasted_iota(jnp.int32, (c, c), 0)
    col = lax.broadcasted_iota(jnp.int32, (c, c), 1)
    sub = lax.broadcasted_iota(jnp.int32, (1, SUBLANES, HG_DK), 1)
    fine = [m for m in _LEVELS if m < SUBLANES]
    same_block = {m: (row // (2 * m)) == (col // (2 * m)) for m in _LEVELS if 2 * m < c}
    return dict(
        same_block={m: mask.astype(F32) for m, mask in same_block.items()},
        diag=(row == col).astype(F32),
        causal_block=(same_block[_SAFE_BLOCK // 2] & (col <= row)).astype(F32),
        upper_pen={m: jnp.where((sub & m) != 0, 0.0, MASK_VALUE) for m in fine},
        lower_pen={m: jnp.where((sub & m) != 0, MASK_VALUE, 0.0) for m in fine},
        even_row=jnp.where((sub & 1) != 0, 0.0, 1.0),
    )


def _hgrn_tile(single_reference_blocks, k, qs_ref, lf_ref, kk_ref, v_ref, gm_ref, gn, cum_s, o_ref, st_ref):
    c = HG_CHUNK
    heads = [slice(h * HG_DK, (h + 1) * HG_DK) for h in range(HG_HEADS)]
    shape3 = (c // SUBLANES, SUBLANES, HG_DK)
    for ci in range(HG_TILE // c):
        rows = slice(ci * c, (ci + 1) * c)
        scores_bf, q_ins, k_outs, lasts = [], [], [], []
        for hs in heads:
            cum = cum_s[rows, hs]
            qs = qs_ref[rows, hs].astype(F32)
            kk = kk_ref[rows, hs].astype(F32)

            pairs = []
            for m in _LEVELS:
                if single_reference_blocks and 2 * m <= _SAFE_BLOCK:
                    continue
                if m >= SUBLANES:
                    qd, kd = _coarse_level(cum, qs, kk, m)
                elif m == 1:
                    qs3, kk3 = qs.reshape(shape3), kk.reshape(shape3)
                    qd3 = qs3 * jnp.exp2(lf_ref[rows, hs].reshape(shape3) + k["upper_pen"][m])
                    qd = qd3.reshape(c, HG_DK).astype(BF16)
                    kd = (kk3 * k["even_row"]).reshape(c, HG_DK).astype(BF16)
                else:
                    qd, kd = _fine_level(cum.reshape(shape3), qs.reshape(shape3), kk.reshape(shape3), m,
                                         k["upper_pen"][m], k["lower_pen"][m])
                pairs.append((qd, kd, k["same_block"].get(m)))
            if single_reference_blocks:
                pairs.append(_block_level(cum, qs, kk) + (k["causal_block"],))
            else:
                pairs.append((qs.astype(BF16), kk.astype(BF16), k["diag"]))

            scores = None
            for qd, kd, mask in pairs:
                s = _dot_nt(qd, kd)
                if mask is not None:
                    s = s * mask
                scores = s if scores is None else scores + s

            last = cum[c - 1:c, :]
            scores_bf.append(scores.astype(BF16))
            q_ins.append((qs * jnp.exp2(cum)).astype(BF16))
            k_outs.append((kk * jnp.exp2(last - cum)).astype(BF16))
            lasts.append(last)

        for h, hs in enumerate(heads):
            o = _dot_nt(q_ins[h], st_ref[h].astype(BF16)) + _dot(scores_bf[h], v_ref[rows, hs])
            o = o * lax.rsqrt(jnp.mean(o * o, axis=-1, keepdims=True) + NORM_EPS)
            o_ref[rows, hs] = (o * gn[:, hs] * gm_ref[rows, hs].astype(F32)).astype(BF16)

        for h, hs in enumerate(heads):
            st_ref[h] = st_ref[h] * jnp.exp2(lasts[h]) + _dot_tn(v_ref[rows, hs], k_outs[h])


def _hgrn_kernel(qs_ref, lf_ref, kk_ref, v_ref, gm_ref, gn_ref, o_ref, cum_s, st_ref):
    c = HG_CHUNK

    @pl.when(pl.program_id(1) == 0)
    def _():
        st_ref[...] = jnp.zeros_like(st_ref)

    row = lax.broadcasted_iota(jnp.int32, (c, c), 0)
    col = lax.broadcasted_iota(jnp.int32, (c, c), 1)
    tri = (col <= row).astype(BF16)

    worst = None
    for ci in range(HG_TILE // c):
        rows = slice(ci * c, (ci + 1) * c)
        lf = lf_ref[rows, :]
        hi = lf.astype(BF16)
        lo = (lf - hi.astype(F32)).astype(BF16)
        cum = _dot(tri, hi) + _dot(tri, lo)
        cum_s[rows, :] = cum
        ends = [cum[b - 1:b, :] for b in range(_SAFE_BLOCK, c + 1, _SAFE_BLOCK)]
        for b, end in enumerate(ends):
            decay = end if b == 0 else end - ends[b - 1]
            worst = decay if worst is None else jnp.minimum(worst, decay)
    single_reference_ok = jnp.min(worst) >= -SAFE_BLOCK_DECAY

    consts = _hgrn_constants()
    args = (consts, qs_ref, lf_ref, kk_ref, v_ref, gm_ref, gn_ref[...], cum_s, o_ref, st_ref)

    @pl.when(single_reference_ok)
    def _():
        _hgrn_tile(True, *args)

    @pl.when(jnp.logical_not(single_reference_ok))
    def _():
        _hgrn_tile(False, *args)


def _hgrn(qs, lf, kk, v, gm, gn, batch):
    n = qs.shape[0]
    tiles = n // batch // HG_TILE
    row = pl.BlockSpec((HG_TILE, HG_KEY_WIDTH), lambda b, t: (b * tiles + t, 0))
    return pl.pallas_call(
        _hgrn_kernel,
        grid=(batch, tiles),
        in_specs=[row, row, row, row, row, pl.BlockSpec((1, HG_VAL_WIDTH), lambda b, t: (0, 0))],
        out_specs=row,
        out_shape=jax.ShapeDtypeStruct((n, HG_VAL_WIDTH), BF16),
        scratch_shapes=[pltpu.VMEM((HG_TILE, HG_KEY_WIDTH), F32),
                        pltpu.VMEM((HG_HEADS, HG_DV, HG_DK), F32)],
        compiler_params=_params("arbitrary", "arbitrary"),
        name="hgrn2",
    )(qs, lf, kk, v, gm, gn)


def _out_kernel(main_transposed, om_ref, ome_ref, w_ref, g_ref, h_ref, o_ref):
    main_dot = _dot_tn if main_transposed else _dot
    y = main_dot(om_ref[...], w_ref[:HG_VAL_WIDTH, :]) + _dot(ome_ref[...], w_ref[HG_VAL_WIDTH:, :])
    o_ref[...] = h_ref[...] + _rms_norm(y, g_ref[...])


def _out_proj(o_main, o_mem, w, layer, g, h, name, main_transposed=False):
    n = h.shape[0]
    row = lambda width: pl.BlockSpec((OUT_TILE, width), lambda i: (i, 0))
    col = lambda width: pl.BlockSpec((width, OUT_TILE), lambda i: (0, i))
    full = lambda a: pl.BlockSpec(a.shape, lambda i: (0,) * a.ndim)
    main = col(HG_VAL_WIDTH) if main_transposed else row(HG_VAL_WIDTH)
    return pl.pallas_call(
        functools.partial(_out_kernel, main_transposed),
        grid=(n // OUT_TILE,),
        in_specs=[main, row(MEM_WIDTH), _layer_weight(w, layer), full(g), row(D_MODEL)],
        out_specs=row(D_MODEL),
        out_shape=jax.ShapeDtypeStruct((n, D_MODEL), F32),
        compiler_params=_params("parallel"),
        name=name,
    )(o_main, o_mem, w, g, h)


def _shared_kv_kernel(x_ref, g_ref, w_ref, cos_ref, sin_ref, k_ref, v_ref):
    xn = _rms_norm(x_ref[...], g_ref[...]).astype(BF16)
    kv = _dot(xn, w_ref[...])
    cos = cos_ref[...]
    sin = sin_ref[...]
    first = _first_half_mask(IN_TILE)
    heads_per_block = LANES // SWA_HEAD_DIM
    for cb in range(SWA_KV_WIDTH // LANES):
        kr = _rope_block(kv[:, cb * LANES:(cb + 1) * LANES], cos, sin, first).astype(BF16)
        for j in range(heads_per_block):
            k_ref[cb * heads_per_block + j] = kr[:, j * SWA_HEAD_DIM:(j + 1) * SWA_HEAD_DIM]
    v_ref[...] = kv[:, SWA_KV_WIDTH:].T.astype(BF16)


def _shared_kv(h, g, w, cos, sin):
    n = h.shape[0]
    row = lambda width: pl.BlockSpec((IN_TILE, width), lambda i: (i, 0))
    full = lambda a: pl.BlockSpec(a.shape, lambda i: (0,) * a.ndim)
    return pl.pallas_call(
        _shared_kv_kernel,
        grid=(n // IN_TILE,),
        in_specs=[row(D_MODEL), full(g), full(w), row(LANES), row(LANES)],
        out_specs=[pl.BlockSpec((SWA_KV_HEADS, IN_TILE, SWA_HEAD_DIM), lambda i: (0, i, 0)),
                   pl.BlockSpec((SWA_KV_WIDTH, IN_TILE), lambda i: (0, i))],
        out_shape=[jax.ShapeDtypeStruct((SWA_KV_HEADS, n, SWA_HEAD_DIM), BF16),
                   jax.ShapeDtypeStruct((SWA_KV_WIDTH, n), BF16)],
        compiler_params=_params("parallel"),
        name="shared_kv",
    )(h, g, w, cos, sin)


def _b_in_kernel(x_ref, g_ref, w_ref, cos_ref, sin_ref, mk_ref, mv_ref, qt_ref, gmt_ref, om_ref):
    first = _first_half_mask(IN_SUB_TILE)
    scale = SWA_HEAD_DIM ** -0.5 * LOG2E
    col_mq = 2 * SWA_WIDTH
    col_mg = col_mq + MEM_WIDTH

    for sub in range(IN_TILE // IN_SUB_TILE):
        rows = slice(sub * IN_SUB_TILE, (sub + 1) * IN_SUB_TILE)
        xn = _rms_norm(x_ref[rows, :], g_ref[...]).astype(BF16)

        def proj(lo, width):
            return _dot(xn, w_ref[:, lo:lo + width])

        mq = proj(col_mq, MEM_WIDTH)
        mem_gate = _silu(proj(col_mg, MEM_WIDTH))

        q = proj(0, SWA_WIDTH)
        probs, inv_sums = _memory_probs(mq, mk_ref)
        cos = cos_ref[rows, :]
        sin = sin_ref[rows, :]
        for cb in range(SWA_WIDTH // LANES):
            cs = slice(cb * LANES, (cb + 1) * LANES)
            qt_ref[cs, rows] = (_rope_block(q[:, cs], cos, sin, first) * scale).T.astype(BF16)

        gate = _silu(proj(SWA_WIDTH, SWA_WIDTH))
        _memory_readout(probs, inv_sums, mem_gate, mv_ref, om_ref, rows)
        for cb in range(SWA_WIDTH // LANES):
            cs = slice(cb * LANES, (cb + 1) * LANES)
            gmt_ref[cs, rows] = gate[:, cs].T.astype(BF16)


def _b_in_proj(layer, h, g, w, cos, sin, mem_k, mem_v, tiles_per_seq):
    n = h.shape[0]
    row = lambda width: pl.BlockSpec((IN_TILE, width), lambda i: (i, 0))
    col = pl.BlockSpec((SWA_WIDTH, IN_TILE), lambda i: (0, i))
    full = lambda a: pl.BlockSpec(a.shape, lambda i: (0,) * a.ndim)
    mem = pl.BlockSpec((None, None, MEM_TOKENS, MEM_WIDTH), lambda i: (layer, i // tiles_per_seq, 0, 0))
    return pl.pallas_call(
        _b_in_kernel,
        grid=(n // IN_TILE,),
        in_specs=[row(D_MODEL), full(g), _layer_weight(w, layer - N_A_LAYERS), row(LANES), row(LANES), mem, mem],
        out_specs=[col, col, row(MEM_WIDTH)],
        out_shape=[jax.ShapeDtypeStruct((SWA_WIDTH, n), BF16), jax.ShapeDtypeStruct((SWA_WIDTH, n), BF16),
                   jax.ShapeDtypeStruct((n, MEM_WIDTH), BF16)],
        compiler_params=_params("parallel"),
        name=f"b_in_proj_{layer}",
    )(h, g, w, cos, sin, mem_k, mem_v)


def _swa_kernel(sink_ref, qt_ref, kc_ref, kp_ref, vtc_ref, vtp_ref, gmt_ref, ot_ref):
    blk = WINDOW
    dh = SWA_HEAD_DIM
    ki = lax.broadcasted_iota(jnp.int32, (2 * blk, blk), 0)
    qi = lax.broadcasted_iota(jnp.int32, (2 * blk, blk), 1)
    d = ki - qi
    band = (d > 0) & (d <= blk)
    first_key = jnp.where(pl.program_id(1) > 0, 0, blk)

    for sb in range(SWA_TILE // blk):
        cols = slice(sb * blk, (sb + 1) * blk)
        two = slice((sb - 1) * blk, (sb + 1) * blk)
        mask = band & (ki >= first_key) if sb == 0 else band

        probs, inv_den = [], []
        for g in range(SWA_KV_HEADS):
            k2 = jnp.concatenate([kp_ref[g], kc_ref[g, cols, :]], axis=0) if sb == 0 else kc_ref[g, two, :]
            heads = range(g * SWA_GROUP, (g + 1) * SWA_GROUP)
            qt = jnp.concatenate([qt_ref[h * dh:(h + 1) * dh, cols] for h in heads], axis=1)
            st = _dot(k2, qt)
            ps = []
            for j, h in enumerate(heads):
                s = jnp.where(mask, st[:, j * blk:(j + 1) * blk], MASK_VALUE)
                sink = sink_ref[h] * LOG2E
                m = jnp.maximum(jnp.max(s, axis=0, keepdims=True), sink)
                p = jnp.exp2(s - m)
                inv_den.append(1.0 / (jnp.sum(p, axis=0, keepdims=True) + jnp.exp2(sink - m)))
                ps.append(p.astype(BF16))
            probs.append(jnp.concatenate(ps, axis=1))

        for g in range(SWA_KV_HEADS):
            vrows = slice(g * dh, (g + 1) * dh)
            vt2 = (jnp.concatenate([vtp_ref[vrows, :], vtc_ref[vrows, cols]], axis=1) if sb == 0
                   else vtc_ref[vrows, two])
            ot = _dot(vt2, probs[g])
            for j, h in enumerate(range(g * SWA_GROUP, (g + 1) * SWA_GROUP)):
                rows = slice(h * dh, (h + 1) * dh)
                o = ot[:, j * blk:(j + 1) * blk] * inv_den[h]
                ot_ref[rows, cols] = (o * gmt_ref[rows, cols].astype(F32)).astype(BF16)


def _swa(sinks, qt, k, vt, gmt, batch):
    n = qt.shape[1]
    tiles = n // batch // SWA_TILE
    sub = SWA_TILE // WINDOW
    tile = lambda b, t: b * tiles + t
    prev = lambda b, t: jnp.maximum(tile(b, t) * sub - 1, 0)
    wide = pl.BlockSpec((SWA_WIDTH, SWA_TILE), lambda b, t: (0, tile(b, t)))
    return pl.pallas_call(
        _swa_kernel,
        grid=(batch, tiles),
        in_specs=[pl.BlockSpec(memory_space=pltpu.SMEM),
                  wide,
                  pl.BlockSpec((SWA_KV_HEADS, SWA_TILE, SWA_HEAD_DIM), lambda b, t: (0, tile(b, t), 0)),
                  pl.BlockSpec((SWA_KV_HEADS, WINDOW, SWA_HEAD_DIM), lambda b, t: (0, prev(b, t), 0)),
                  pl.BlockSpec((SWA_KV_WIDTH, SWA_TILE), lambda b, t: (0, tile(b, t))),
                  pl.BlockSpec((SWA_KV_WIDTH, WINDOW), lambda b, t: (0, prev(b, t))),
                  wide],
        out_specs=wide,
        out_shape=jax.ShapeDtypeStruct((SWA_WIDTH, n), BF16),
        compiler_params=_params("parallel", "arbitrary"),
        name="swa",
    )(sinks, qt, k, k, vt, vt, gmt)


def kernel(x, mem, positions, pre_norm_g, post_norm_g, mem_norm_g, w_mem_kv, a_w_in, a_lb_logits, a_out_norm_g, a_w_out, kv_norm_g, w_kv_shared, b_w_in, b_sinks, b_w_out):
    batch, seq, _ = x.shape
    n = batch * seq
    assert seq % max(IN_TILE, OUT_TILE, HG_TILE, SWA_TILE) == 0
    tiles_per_seq = seq // IN_TILE

    h = x.reshape(n, D_MODEL)
    mem_k, mem_v = _mem_kv(mem, mem_norm_g, w_mem_kv.astype(BF16))
    cos, sin = _rope_tables(positions)
    a_w_in = a_w_in.astype(BF16)
    a_w_out = a_w_out.astype(BF16)
    b_w_in = b_w_in.astype(BF16)
    b_w_out = b_w_out.astype(BF16)
    k_sh = v_sh = None

    for layer in range(DEPTH):
        pre_g = pre_norm_g[layer].reshape(1, D_MODEL)
        post_g = post_norm_g[layer].reshape(1, D_MODEL)
        if layer < N_A_LAYERS:
            qs, lf, kk, v, gm, o_mem = _a_in_proj(layer, h, pre_g, a_w_in, a_lb_logits,
                                                  mem_k, mem_v, tiles_per_seq)
            o_main = _hgrn(qs, lf, kk, v, gm, a_out_norm_g[layer].reshape(1, HG_VAL_WIDTH), batch)
            h = _out_proj(o_main, o_mem, a_w_out, layer, post_g, h, f"a_out_proj_{layer}")
        else:
            j = layer - N_A_LAYERS
            if k_sh is None:
                k_sh, v_sh = _shared_kv(h, kv_norm_g.reshape(1, D_MODEL), w_kv_shared.astype(BF16), cos, sin)
            qt, gmt, o_mem = _b_in_proj(layer, h, pre_g, b_w_in, cos, sin, mem_k, mem_v, tiles_per_seq)
            o_main_t = _swa(b_sinks[j], qt, k_sh, v_sh, gmt, batch)
            h = _out_proj(o_main_t, o_mem, b_w_out, j, post_g, h, f"b_out_proj_{layer}", main_transposed=True)
    return h.reshape(batch, seq, D_MODEL)
```

```python
import functools

import jax
import jax.numpy as jnp
from jax import lax
from jax.experimental import pallas as pl
from jax.experimental.pallas import tpu as pltpu

D_MODEL = 1024
DEPTH = 4
N_A_LAYERS = DEPTH // 2

HG_HEADS = 8
HG_DK = 128
HG_DV = D_MODEL // HG_HEADS
HG_KEY_WIDTH = HG_HEADS * HG_DK
HG_VAL_WIDTH = HG_HEADS * HG_DV
HG_CHUNK = 64

SWA_Q_HEADS = 16
SWA_KV_HEADS = 4
SWA_GROUP = SWA_Q_HEADS // SWA_KV_HEADS
SWA_HEAD_DIM = 64
SWA_WIDTH = SWA_Q_HEADS * SWA_HEAD_DIM
SWA_KV_WIDTH = SWA_KV_HEADS * SWA_HEAD_DIM
WINDOW = 128

MEM_TOKENS = 256
MEM_HEADS = 4
MEM_HEAD_DIM = 128
MEM_WIDTH = MEM_HEADS * MEM_HEAD_DIM

ROPE_THETA = 10000.0
NORM_EPS = 1e-6

A_IN_WIDTH = 2 * HG_KEY_WIDTH + 2 * HG_VAL_WIDTH + 2 * MEM_WIDTH
B_IN_WIDTH = 2 * SWA_WIDTH + 2 * MEM_WIDTH
OUT_WIDTH = HG_VAL_WIDTH + MEM_WIDTH

LANES = 128
SUBLANES = 8
MASK_VALUE = -1e30
LOG2E = 1.4426950408889634

IN_TILE = 512
IN_SUB_TILE = 256
OUT_TILE = 512
HG_TILE = 256
SWA_TILE = 512
VMEM_LIMIT = 56 * 1024 * 1024

F32 = jnp.float32
BF16 = jnp.bfloat16

_NT = (((1,), (1,)), ((), ()))
_TN = (((0,), (0,)), ((), ()))


def _dot(a, b):
    return jnp.dot(a, b, preferred_element_type=F32)


def _dot_nt(a, b):
    return lax.dot_general(a, b, _NT, preferred_element_type=F32)


def _dot_tn(a, b):
    return lax.dot_general(a, b, _TN, preferred_element_type=F32)


def _rms_norm(x, g):
    ms = jnp.mean(x * x, axis=-1, keepdims=True)
    return x * lax.rsqrt(ms + NORM_EPS) * g


def _silu(x):
    h = 0.5 * x
    return h + h * jnp.tanh(h)


def _params(*semantics):
    return pltpu.CompilerParams(dimension_semantics=semantics, vmem_limit_bytes=VMEM_LIMIT)


def _layer_weight(w, layer):
    return pl.BlockSpec((None,) + w.shape[1:], lambda *_: (layer, 0, 0), pipeline_mode=pl.Buffered(1))


def _layer_input(prev, h_ref, rows):
    if prev is None:
        return h_ref[rows, :]
    om_ref, ome_ref, wout_ref, pg_ref, hnew_ref, main_transposed = prev
    if main_transposed:
        y = _dot_tn(om_ref[:, rows], wout_ref[:HG_VAL_WIDTH, :])
    else:
        y = _dot(om_ref[rows, :], wout_ref[:HG_VAL_WIDTH, :])
    y = y + _dot(ome_ref[rows, :], wout_ref[HG_VAL_WIDTH:, :])
    h_new = h_ref[rows, :] + _rms_norm(y, pg_ref[...])
    hnew_ref[rows, :] = h_new
    return h_new


def _split_refs(prev_transposed, refs, n_in):
    if prev_transposed is None:
        return None, refs[:n_in], refs[n_in:]
    prev_in, ins, (hnew_ref, *outs) = refs[:4], refs[4:4 + n_in], refs[4 + n_in:]
    return (*prev_in, hnew_ref, prev_transposed), ins, outs


def _pending_specs(pending, row, col, full):
    if pending is None:
        return [], []
    o_main, o_mem, w_out, w_layer, post_g, transposed = pending
    main = col(HG_VAL_WIDTH) if transposed else row(HG_VAL_WIDTH)
    return [o_main, o_mem, w_out, post_g], [main, row(MEM_WIDTH), _layer_weight(w_out, w_layer), full(post_g)]


def _rope_table_kernel(pos_ref, invf_ref, sign_ref, cos_ref, sin_ref):
    ang = pos_ref[...].astype(F32) * invf_ref[...]
    cos_ref[...] = jnp.cos(ang)
    sin_ref[...] = jnp.sin(ang) * sign_ref[...]


def _rope_tables(positions):
    n = positions.size
    half = SWA_HEAD_DIM // 2
    inv_freq = ROPE_THETA ** (-jnp.arange(0, SWA_HEAD_DIM, 2, dtype=F32) / SWA_HEAD_DIM)
    invf = jnp.tile(inv_freq, LANES // half).reshape(1, LANES)
    sign = jnp.tile(jnp.concatenate([-jnp.ones((half,), F32), jnp.ones((half,), F32)]),
                    LANES // SWA_HEAD_DIM).reshape(1, LANES)
    pos = jnp.broadcast_to(positions.reshape(n, 1), (n, LANES))
    tile = 2048
    row = pl.BlockSpec((tile, LANES), lambda i: (i, 0))
    vec = pl.BlockSpec((1, LANES), lambda i: (0, 0))
    return pl.pallas_call(
        _rope_table_kernel,
        grid=(n // tile,),
        in_specs=[row, vec, vec],
        out_specs=[row, row],
        out_shape=[jax.ShapeDtypeStruct((n, LANES), F32)] * 2,
        compiler_params=_params("parallel"),
        name="rope_tables",
    )(pos, invf, sign)


def _rope_block(xc, cos, sin, first_half):
    swapped = jnp.where(first_half,
                        pltpu.roll(xc, LANES - SWA_HEAD_DIM // 2, 1),
                        pltpu.roll(xc, SWA_HEAD_DIM // 2, 1))
    return xc * cos + swapped * sin


def _first_half_mask(rows):
    lane = lax.broadcasted_iota(jnp.int32, (rows, LANES), 1)
    return (lane & (SWA_HEAD_DIM // 2)) == 0


def _mem_kv_kernel(mem_ref, g_ref, w_ref, k_ref, v_ref):
    mn = _rms_norm(mem_ref[...], g_ref[...]).astype(BF16)
    kv = _dot(mn, w_ref[...])
    k_ref[...] = kv[:, :MEM_WIDTH].astype(BF16)
    v_ref[...] = kv[:, MEM_WIDTH:].astype(BF16)


def _mem_kv(mem, mem_norm_g, w_mem_kv):
    b = mem.shape[0]
    out = pl.BlockSpec((None, None, MEM_TOKENS, MEM_WIDTH), lambda l, i: (l, i, 0, 0))
    return pl.pallas_call(
        _mem_kv_kernel,
        grid=(DEPTH, b),
        in_specs=[
            pl.BlockSpec((None, MEM_TOKENS, D_MODEL), lambda l, i: (i, 0, 0)),
            pl.BlockSpec((None, 1, D_MODEL), lambda l, i: (l, 0, 0)),
            pl.BlockSpec((None, D_MODEL, 2 * MEM_WIDTH), lambda l, i: (l, 0, 0)),
        ],
        out_specs=[out, out],
        out_shape=[jax.ShapeDtypeStruct((DEPTH, b, MEM_TOKENS, MEM_WIDTH), BF16)] * 2,
        compiler_params=_params("arbitrary", "arbitrary"),
        name="mem_kv",
    )(mem, mem_norm_g.reshape(DEPTH, 1, D_MODEL), w_mem_kv)


def _memory_probs(mq, mk_ref):
    mq = (mq * (MEM_HEAD_DIM ** -0.5 * LOG2E)).astype(BF16)
    probs, inv_sums = [], []
    for hh in range(MEM_HEADS):
        hs = slice(hh * MEM_HEAD_DIM, (hh + 1) * MEM_HEAD_DIM)
        s = _dot_nt(mq[:, hs], mk_ref[:, hs])
        p = jnp.exp2(s - jnp.max(s, axis=-1, keepdims=True))
        probs.append(p.astype(BF16))
        inv_sums.append(1.0 / jnp.sum(p, axis=-1, keepdims=True))
    return probs, inv_sums


def _memory_readout(probs, inv_sums, gate, mv_ref, out_ref, rows):
    for hh in range(MEM_HEADS):
        hs = slice(hh * MEM_HEAD_DIM, (hh + 1) * MEM_HEAD_DIM)
        o = _dot(probs[hh], mv_ref[:, hs]) * inv_sums[hh]
        out_ref[rows, hs] = (o * gate[:, hs]).astype(BF16)


def _a_in_kernel(layer, prev_transposed, h_ref, *refs):
    prev, (g_ref, w_ref, lbl_ref, mk_ref, mv_ref), outs = _split_refs(prev_transposed, refs, 5)
    qs_ref, lf_ref, kk_ref, v_ref, gm_ref, om_ref = outs

    logits = lbl_ref[...]
    e = jnp.exp(logits - jnp.max(logits, axis=0, keepdims=True))
    lb = jnp.sum(e[:layer + 1], axis=0, keepdims=True) / jnp.sum(e, axis=0, keepdims=True)
    mid = 0.5 * (1.0 + lb)
    half = 0.5 * (1.0 - lb)

    col_q, col_f, col_v = 0, HG_KEY_WIDTH, 2 * HG_KEY_WIDTH
    col_g = col_v + HG_VAL_WIDTH
    col_mq = col_g + HG_VAL_WIDTH
    col_mg = col_mq + MEM_WIDTH

    for sub in range(IN_TILE // IN_SUB_TILE):
        rows = slice(sub * IN_SUB_TILE, (sub + 1) * IN_SUB_TILE)
        xn = _rms_norm(_layer_input(prev, h_ref, rows), g_ref[...]).astype(BF16)

        def proj(lo, width):
            return _dot(xn, w_ref[:, lo:lo + width])

        mq = proj(col_mq, MEM_WIDTH)
        mem_gate = _silu(proj(col_mg, MEM_WIDTH))
        qs_ref[rows, :] = _silu(proj(col_q, HG_KEY_WIDTH)).astype(BF16)
        probs, inv_sums = _memory_probs(mq, mk_ref)

        th = half * jnp.tanh(0.5 * proj(col_f, HG_KEY_WIDTH))
        lf_ref[rows, :] = jnp.log2(mid + th)
        kk_ref[rows, :] = (half - th).astype(BF16)

        _memory_readout(probs, inv_sums, mem_gate, mv_ref, om_ref, rows)
        v_ref[rows, :] = proj(col_v, HG_VAL_WIDTH).astype(BF16)
        gm_ref[rows, :] = _silu(proj(col_g, HG_VAL_WIDTH)).astype(BF16)


def _a_in_proj(layer, h, pending, g, w, lb_logits, mem_k, mem_v, tiles_per_seq):
    n = h.shape[0]
    row = lambda width: pl.BlockSpec((IN_TILE, width), lambda i: (i, 0))
    col = lambda width: pl.BlockSpec((width, IN_TILE), lambda i: (0, i))
    full = lambda a: pl.BlockSpec(a.shape, lambda i: (0,) * a.ndim)
    mem = pl.BlockSpec((None, None, MEM_TOKENS, MEM_WIDTH), lambda i: (layer, i // tiles_per_seq, 0, 0))
    widths = (HG_KEY_WIDTH, HG_KEY_WIDTH, HG_KEY_WIDTH, HG_VAL_WIDTH, HG_VAL_WIDTH, MEM_WIDTH)
    dtypes = (BF16, F32, BF16, BF16, BF16, BF16)
    if pending is not None:
        widths, dtypes = (D_MODEL,) + widths, (F32,) + dtypes
    pend_args, pend_specs = _pending_specs(pending, row, col, full)
    return pl.pallas_call(
        functools.partial(_a_in_kernel, layer, None if pending is None else pending[-1]),
        grid=(n // IN_TILE,),
        in_specs=[row(D_MODEL)] + pend_specs + [full(g), _layer_weight(w, layer), full(lb_logits), mem, mem],
        out_specs=[row(wd) for wd in widths],
        out_shape=[jax.ShapeDtypeStruct((n, wd), dt) for wd, dt in zip(widths, dtypes)],
        compiler_params=_params("parallel"),
        name=f"a_in_proj_{layer}",
    )(h, *pend_args, g, w, lb_logits, mem_k, mem_v)


_LEVELS = (32, 16, 8, 4, 2, 1)
_SAFE_BLOCK = 16
SAFE_BLOCK_DECAY = 100.0


def _coarse_level(cum, qs, kk, m):
    c, w = cum.shape
    early = m % (2 * SUBLANES) == 0
    dt = BF16 if early else F32
    zeros = jnp.zeros((m, w), dt)
    qd, kd = [], []
    for blk in range(c // (2 * m)):
        lo, mid, hi = blk * 2 * m, blk * 2 * m + m, (blk + 1) * 2 * m
        ref = cum[mid - 1:mid, :]
        kd += [(kk[lo:mid] * jnp.exp2(ref - cum[lo:mid])).astype(dt), zeros]
        qd += [zeros, (qs[mid:hi] * jnp.exp2(cum[mid:hi] - ref)).astype(dt)]
    return jnp.concatenate(qd, axis=0).astype(BF16), jnp.concatenate(kd, axis=0).astype(BF16)


def _fine_level(cum3, qs3, kk3, m, upper_pen, lower_pen):
    groups, _, w = cum3.shape
    parts = [jnp.broadcast_to(cum3[:, blk * 2 * m + m - 1:blk * 2 * m + m, :], (groups, 2 * m, w))
             for blk in range(SUBLANES // (2 * m))]
    x = cum3 - (parts[0] if len(parts) == 1 else jnp.concatenate(parts, axis=1))
    qd = qs3 * jnp.exp2(x + upper_pen)
    kd = kk3 * jnp.exp2(lower_pen - x)
    c = groups * SUBLANES
    return qd.reshape(c, w).astype(BF16), kd.reshape(c, w).astype(BF16)


def _block_level(cum, qs, kk):
    c, _ = cum.shape
    local = [cum[:_SAFE_BLOCK]] + [cum[b:b + _SAFE_BLOCK] - cum[b - 1:b] for b in range(_SAFE_BLOCK, c, _SAFE_BLOCK)]
    local = jnp.concatenate(local, axis=0)
    return (qs * jnp.exp2(local)).astype(BF16), (kk * jnp.exp2(-local)).astype(BF16)


def _hgrn_constants():
    c = HG_CHUNK
    row = lax.broadcasted_iota(jnp.int32, (c, c), 0)
    col = lax.broadcasted_iota(jnp.int32, (c, c), 1)
    sub = lax.broadcasted_iota(jnp.int32, (1, SUBLANES, HG_DK), 1)
    fine = [m for m in _LEVELS if m < SUBLANES]
    same_block = {m: (row // (2 * m)) == (col // (2 * m)) for m in _LEVELS if 2 * m < c}
    return dict(
        same_block={m: mask.astype(F32) for m, mask in same_block.items()},
        diag=(row == col).astype(F32),
        causal_block=(same_block[_SAFE_BLOCK // 2] & (col <= row)).astype(F32),
        upper_pen={m: jnp.where((sub & m) != 0, 0.0, MASK_VALUE) for m in fine},
        lower_pen={m: jnp.where((sub & m) != 0, MASK_VALUE, 0.0) for m in fine},
        even_row=jnp.where((sub & 1) != 0, 0.0, 1.0),
    )


def _hgrn_tile(single_reference_blocks, k, qs_ref, lf_ref, kk_ref, v_ref, gm_ref, gn, cum_s, o_ref, st_ref):
    c = HG_CHUNK
    heads = [slice(h * HG_DK, (h + 1) * HG_DK) for h in range(HG_HEADS)]
    shape3 = (c // SUBLANES, SUBLANES, HG_DK)
    for ci in range(HG_TILE // c):
        rows = slice(ci * c, (ci + 1) * c)
        scores_bf, q_ins, k_outs, lasts = [], [], [], []
        for hs in heads:
            cum = cum_s[rows, hs]
            qs = qs_ref[rows, hs].astype(F32)
            kk = kk_ref[rows, hs].astype(F32)

            pairs = []
            for m in _LEVELS:
                if single_reference_blocks and 2 * m <= _SAFE_BLOCK:
                    continue
                if m >= SUBLANES:
                    qd, kd = _coarse_level(cum, qs, kk, m)
                elif m == 1:
                    qs3, kk3 = qs.reshape(shape3), kk.reshape(shape3)
                    qd3 = qs3 * jnp.exp2(lf_ref[rows, hs].reshape(shape3) + k["upper_pen"][m])
                    qd = qd3.reshape(c, HG_DK).astype(BF16)
                    kd = (kk3 * k["even_row"]).reshape(c, HG_DK).astype(BF16)
                else:
                    qd, kd = _fine_level(cum.reshape(shape3), qs.reshape(shape3), kk.reshape(shape3), m,
                                         k["upper_pen"][m], k["lower_pen"][m])
                pairs.append((qd, kd, k["same_block"].get(m)))
            if single_reference_blocks:
                pairs.append(_block_level(cum, qs, kk) + (k["causal_block"],))
            else:
                pairs.append((qs.astype(BF16), kk.astype(BF16), k["diag"]))

            scores = None
            for qd, kd, mask in pairs:
                s = _dot_nt(qd, kd)
                if mask is not None:
                    s = s * mask
                scores = s if scores is None else scores + s

            last = cum[c - 1:c, :]
            scores_bf.append(scores.astype(BF16))
            q_ins.append((qs * jnp.exp2(cum)).astype(BF16))
            k_outs.append((kk * jnp.exp2(last - cum)).astype(BF16))
            lasts.append(last)

        for h, hs in enumerate(heads):
            o = _dot_nt(q_ins[h], st_ref[h].astype(BF16)) + _dot(scores_bf[h], v_ref[rows, hs])
            o = o * lax.rsqrt(jnp.mean(o * o, axis=-1, keepdims=True) + NORM_EPS)
            o_ref[rows, hs] = (o * gn[:, hs] * gm_ref[rows, hs].astype(F32)).astype(BF16)

        for h, hs in enumerate(heads):
            st_ref[h] = st_ref[h] * jnp.exp2(lasts[h]) + _dot_tn(v_ref[rows, hs], k_outs[h])


def _hgrn_kernel(qs_ref, lf_ref, kk_ref, v_ref, gm_ref, gn_ref, o_ref, cum_s, st_ref):
    c = HG_CHUNK

    @pl.when(pl.program_id(1) == 0)
    def _():
        st_ref[...] = jnp.zeros_like(st_ref)

    row = lax.broadcasted_iota(jnp.int32, (c, c), 0)
    col = lax.broadcasted_iota(jnp.int32, (c, c), 1)
    tri = (col <= row).astype(BF16)

    worst = None
    for ci in range(HG_TILE // c):
        rows = slice(ci * c, (ci + 1) * c)
        lf = lf_ref[rows, :]
        hi = lf.astype(BF16)
        lo = (lf - hi.astype(F32)).astype(BF16)
        cum = _dot(tri, hi) + _dot(tri, lo)
        cum_s[rows, :] = cum
        ends = [cum[b - 1:b, :] for b in range(_SAFE_BLOCK, c + 1, _SAFE_BLOCK)]
        for b, end in enumerate(ends):
            decay = end if b == 0 else end - ends[b - 1]
            worst = decay if worst is None else jnp.minimum(worst, decay)
    single_reference_ok = jnp.min(worst) >= -SAFE_BLOCK_DECAY

    consts = _hgrn_constants()
    args = (consts, qs_ref, lf_ref, kk_ref, v_ref, gm_ref, gn_ref[...], cum_s, o_ref, st_ref)

    @pl.when(single_reference_ok)
    def _():
        _hgrn_tile(True, *args)

    @pl.when(jnp.logical_not(single_reference_ok))
    def _():
        _hgrn_tile(False, *args)


def _hgrn(qs, lf, kk, v, gm, gn, batch):
    n = qs.shape[0]
    tiles = n // batch // HG_TILE
    row = pl.BlockSpec((HG_TILE, HG_KEY_WIDTH), lambda b, t: (b * tiles + t, 0))
    return pl.pallas_call(
        _hgrn_kernel,
        grid=(batch, tiles),
        in_specs=[row, row, row, row, row, pl.BlockSpec((1, HG_VAL_WIDTH), lambda b, t: (0, 0))],
        out_specs=row,
        out_shape=jax.ShapeDtypeStruct((n, HG_VAL_WIDTH), BF16),
        scratch_shapes=[pltpu.VMEM((HG_TILE, HG_KEY_WIDTH), F32),
                        pltpu.VMEM((HG_HEADS, HG_DV, HG_DK), F32)],
        compiler_params=_params("arbitrary", "arbitrary"),
        name="hgrn2",
    )(qs, lf, kk, v, gm, gn)


def _out_kernel(main_transposed, om_ref, ome_ref, w_ref, g_ref, h_ref, o_ref):
    main_dot = _dot_tn if main_transposed else _dot
    y = main_dot(om_ref[...], w_ref[:HG_VAL_WIDTH, :]) + _dot(ome_ref[...], w_ref[HG_VAL_WIDTH:, :])
    o_ref[...] = h_ref[...] + _rms_norm(y, g_ref[...])


def _out_proj(o_main, o_mem, w, layer, g, h, name, main_transposed=False):
    n = h.shape[0]
    row = lambda width: pl.BlockSpec((OUT_TILE, width), lambda i: (i, 0))
    col = lambda width: pl.BlockSpec((width, OUT_TILE), lambda i: (0, i))
    full = lambda a: pl.BlockSpec(a.shape, lambda i: (0,) * a.ndim)
    main = col(HG_VAL_WIDTH) if main_transposed else row(HG_VAL_WIDTH)
    return pl.pallas_call(
        functools.partial(_out_kernel, main_transposed),
        grid=(n // OUT_TILE,),
        in_specs=[main, row(MEM_WIDTH), _layer_weight(w, layer), full(g), row(D_MODEL)],
        out_specs=row(D_MODEL),
        out_shape=jax.ShapeDtypeStruct((n, D_MODEL), F32),
        compiler_params=_params("parallel"),
        name=name,
    )(o_main, o_mem, w, g, h)


def _shared_kv_kernel(x_ref, g_ref, w_ref, cos_ref, sin_ref, k_ref, v_ref):
    xn = _rms_norm(x_ref[...], g_ref[...]).astype(BF16)
    kv = _dot(xn, w_ref[...])
    cos = cos_ref[...]
    sin = sin_ref[...]
    first = _first_half_mask(IN_TILE)
    heads_per_block = LANES // SWA_HEAD_DIM
    for cb in range(SWA_KV_WIDTH // LANES):
        kr = _rope_block(kv[:, cb * LANES:(cb + 1) * LANES], cos, sin, first).astype(BF16)
        for j in range(heads_per_block):
            k_ref[cb * heads_per_block + j] = kr[:, j * SWA_HEAD_DIM:(j + 1) * SWA_HEAD_DIM]
    v_ref[...] = kv[:, SWA_KV_WIDTH:].T.astype(BF16)


def _shared_kv(h, g, w, cos, sin):
    n = h.shape[0]
    row = lambda width: pl.BlockSpec((IN_TILE, width), lambda i: (i, 0))
    full = lambda a: pl.BlockSpec(a.shape, lambda i: (0,) * a.ndim)
    return pl.pallas_call(
        _shared_kv_kernel,
        grid=(n // IN_TILE,),
        in_specs=[row(D_MODEL), full(g), full(w), row(LANES), row(LANES)],
        out_specs=[pl.BlockSpec((SWA_KV_HEADS, IN_TILE, SWA_HEAD_DIM), lambda i: (0, i, 0)),
                   pl.BlockSpec((SWA_KV_WIDTH, IN_TILE), lambda i: (0, i))],
        out_shape=[jax.ShapeDtypeStruct((SWA_KV_HEADS, n, SWA_HEAD_DIM), BF16),
                   jax.ShapeDtypeStruct((SWA_KV_WIDTH, n), BF16)],
        compiler_params=_params("parallel"),
        name="shared_kv",
    )(h, g, w, cos, sin)


def _b_in_kernel(prev_transposed, h_ref, *refs):
    prev, (g_ref, w_ref, cos_ref, sin_ref, mk_ref, mv_ref), outs = _split_refs(prev_transposed, refs, 6)
    qt_ref, gmt_ref, om_ref = outs
    first = _first_half_mask(IN_SUB_TILE)
    scale = SWA_HEAD_DIM ** -0.5 * LOG2E
    col_mq = 2 * SWA_WIDTH
    col_mg = col_mq + MEM_WIDTH

    for sub in range(IN_TILE // IN_SUB_TILE):
        rows = slice(sub * IN_SUB_TILE, (sub + 1) * IN_SUB_TILE)
        xn = _rms_norm(_layer_input(prev, h_ref, rows), g_ref[...]).astype(BF16)

        def proj(lo, width):
            return _dot(xn, w_ref[:, lo:lo + width])

        mq = proj(col_mq, MEM_WIDTH)
        mem_gate = _silu(proj(col_mg, MEM_WIDTH))

        q = proj(0, SWA_WIDTH)
        probs, inv_sums = _memory_probs(mq, mk_ref)
        cos = cos_ref[rows, :]
        sin = sin_ref[rows, :]
        for cb in range(SWA_WIDTH // LANES):
            cs = slice(cb * LANES, (cb + 1) * LANES)
            qt_ref[cs, rows] = (_rope_block(q[:, cs], cos, sin, first) * scale).T.astype(BF16)

        gate = _silu(proj(SWA_WIDTH, SWA_WIDTH))
        _memory_readout(probs, inv_sums, mem_gate, mv_ref, om_ref, rows)
        for cb in range(SWA_WIDTH // LANES):
            cs = slice(cb * LANES, (cb + 1) * LANES)
            gmt_ref[cs, rows] = gate[:, cs].T.astype(BF16)


def _b_in_proj(layer, h, pending, g, w, cos, sin, mem_k, mem_v, tiles_per_seq):
    n = h.shape[0]
    row = lambda width: pl.BlockSpec((IN_TILE, width), lambda i: (i, 0))
    col = lambda width: pl.BlockSpec((width, IN_TILE), lambda i: (0, i))
    full = lambda a: pl.BlockSpec(a.shape, lambda i: (0,) * a.ndim)
    mem = pl.BlockSpec((None, None, MEM_TOKENS, MEM_WIDTH), lambda i: (layer, i // tiles_per_seq, 0, 0))
    out_specs = [col(SWA_WIDTH), col(SWA_WIDTH), row(MEM_WIDTH)]
    out_shape = [jax.ShapeDtypeStruct((SWA_WIDTH, n), BF16), jax.ShapeDtypeStruct((SWA_WIDTH, n), BF16),
                 jax.ShapeDtypeStruct((n, MEM_WIDTH), BF16)]
    if pending is not None:
        out_specs, out_shape = [row(D_MODEL)] + out_specs, [jax.ShapeDtypeStruct((n, D_MODEL), F32)] + out_shape
    pend_args, pend_specs = _pending_specs(pending, row, col, full)
    return pl.pallas_call(
        functools.partial(_b_in_kernel, None if pending is None else pending[-1]),
        grid=(n // IN_TILE,),
        in_specs=[row(D_MODEL)] + pend_specs
        + [full(g), _layer_weight(w, layer - N_A_LAYERS), row(LANES), row(LANES), mem, mem],
        out_specs=out_specs,
        out_shape=out_shape,
        compiler_params=_params("parallel"),
        name=f"b_in_proj_{layer}",
    )(h, *pend_args, g, w, cos, sin, mem_k, mem_v)


def _swa_kernel(sink_ref, qt_ref, kc_ref, kp_ref, vtc_ref, vtp_ref, gmt_ref, ot_ref):
    blk = WINDOW
    dh = SWA_HEAD_DIM
    ki = lax.broadcasted_iota(jnp.int32, (2 * blk, blk), 0)
    qi = lax.broadcasted_iota(jnp.int32, (2 * blk, blk), 1)
    d = ki - qi
    band = (d > 0) & (d <= blk)
    first_key = jnp.where(pl.program_id(1) > 0, 0, blk)

    for sb in range(SWA_TILE // blk):
        cols = slice(sb * blk, (sb + 1) * blk)
        two = slice((sb - 1) * blk, (sb + 1) * blk)
        mask = band & (ki >= first_key) if sb == 0 else band

        probs, inv_den = [], []
        for g in range(SWA_KV_HEADS):
            k2 = jnp.concatenate([kp_ref[g], kc_ref[g, cols, :]], axis=0) if sb == 0 else kc_ref[g, two, :]
            heads = range(g * SWA_GROUP, (g + 1) * SWA_GROUP)
            qt = jnp.concatenate([qt_ref[h * dh:(h + 1) * dh, cols] for h in heads], axis=1)
            st = _dot(k2, qt)
            ps = []
            for j, h in enumerate(heads):
                s = jnp.where(mask, st[:, j * blk:(j + 1) * blk], MASK_VALUE)
                sink = sink_ref[h] * LOG2E
                m = jnp.maximum(jnp.max(s, axis=0, keepdims=True), sink)
                p = jnp.exp2(s - m)
                inv_den.append(1.0 / (jnp.sum(p, axis=0, keepdims=True) + jnp.exp2(sink - m)))
                ps.append(p.astype(BF16))
            probs.append(jnp.concatenate(ps, axis=1))

        for g in range(SWA_KV_HEADS):
            vrows = slice(g * dh, (g + 1) * dh)
            vt2 = (jnp.concatenate([vtp_ref[vrows, :], vtc_ref[vrows, cols]], axis=1) if sb == 0
                   else vtc_ref[vrows, two])
            ot = _dot(vt2, probs[g])
            for j, h in enumerate(range(g * SWA_GROUP, (g + 1) * SWA_GROUP)):
                rows = slice(h * dh, (h + 1) * dh)
                o = ot[:, j * blk:(j + 1) * blk] * inv_den[h]
                ot_ref[rows, cols] = (o * gmt_ref[rows, cols].astype(F32)).astype(BF16)


def _swa(sinks, qt, k, vt, gmt, batch):
    n = qt.shape[1]
    tiles = n // batch // SWA_TILE
    sub = SWA_TILE // WINDOW
    tile = lambda b, t: b * tiles + t
    prev = lambda b, t: jnp.maximum(tile(b, t) * sub - 1, 0)
    wide = pl.BlockSpec((SWA_WIDTH, SWA_TILE), lambda b, t: (0, tile(b, t)))
    return pl.pallas_call(
        _swa_kernel,
        grid=(batch, tiles),
        in_specs=[pl.BlockSpec(memory_space=pltpu.SMEM),
                  wide,
                  pl.BlockSpec((SWA_KV_HEADS, SWA_TILE, SWA_HEAD_DIM), lambda b, t: (0, tile(b, t), 0)),
                  pl.BlockSpec((SWA_KV_HEADS, WINDOW, SWA_HEAD_DIM), lambda b, t: (0, prev(b, t), 0)),
                  pl.BlockSpec((SWA_KV_WIDTH, SWA_TILE), lambda b, t: (0, tile(b, t))),
                  pl.BlockSpec((SWA_KV_WIDTH, WINDOW), lambda b, t: (0, prev(b, t))),
                  wide],
        out_specs=wide,
        out_shape=jax.ShapeDtypeStruct((SWA_WIDTH, n), BF16),
        compiler_params=_params("parallel", "arbitrary"),
        name="swa",
    )(sinks, qt, k, k, vt, vt, gmt)


def kernel(x, mem, positions, pre_norm_g, post_norm_g, mem_norm_g, w_mem_kv, a_w_in, a_lb_logits, a_out_norm_g, a_w_out, kv_norm_g, w_kv_shared, b_w_in, b_sinks, b_w_out):
    batch, seq, _ = x.shape
    n = batch * seq
    assert seq % max(IN_TILE, OUT_TILE, HG_TILE, SWA_TILE) == 0
    tiles_per_seq = seq // IN_TILE

    h = x.reshape(n, D_MODEL)
    mem_k, mem_v = _mem_kv(mem, mem_norm_g, w_mem_kv.astype(BF16))
    cos, sin = _rope_tables(positions)
    a_w_in = a_w_in.astype(BF16)
    a_w_out = a_w_out.astype(BF16)
    b_w_in = b_w_in.astype(BF16)
    b_w_out = b_w_out.astype(BF16)
    k_sh = v_sh = None
    pending = None

    for layer in range(DEPTH):
        pre_g = pre_norm_g[layer].reshape(1, D_MODEL)
        post_g = post_norm_g[layer].reshape(1, D_MODEL)
        if layer < N_A_LAYERS:
            outs = _a_in_proj(layer, h, pending, pre_g, a_w_in, a_lb_logits, mem_k, mem_v, tiles_per_seq)
            if pending is not None:
                h, *outs = outs
            qs, lf, kk, v, gm, o_mem = outs
            o_main = _hgrn(qs, lf, kk, v, gm, a_out_norm_g[layer].reshape(1, HG_VAL_WIDTH), batch)
            pending = (o_main, o_mem, a_w_out, layer, post_g, False)
        else:
            j = layer - N_A_LAYERS
            outs = _b_in_proj(layer, h, pending, pre_g, b_w_in, cos, sin, mem_k, mem_v, tiles_per_seq)
            if pending is not None:
                h, *outs = outs
            qt, gmt, o_mem = outs
            if k_sh is None:
                k_sh, v_sh = _shared_kv(h, kv_norm_g.reshape(1, D_MODEL), w_kv_shared.astype(BF16), cos, sin)
            o_main_t = _swa(b_sinks[j], qt, k_sh, v_sh, gmt, batch)
            pending = (o_main_t, o_mem, b_w_out, j, post_g, True)
    o_main, o_mem, w_out, w_layer, post_g, transposed = pending
    h = _out_proj(o_main, o_mem, w_out, w_layer, post_g, h, "out_proj_last", main_transposed=transposed)
    return h.reshape(batch, seq, D_MODEL)
```

```python
import functools

import jax
import jax.numpy as jnp
from jax import lax
from jax.experimental import pallas as pl
from jax.experimental.pallas import tpu as pltpu

D_MODEL = 1024
DEPTH = 4
N_A_LAYERS = DEPTH // 2

HG_HEADS = 8
HG_DK = 128
HG_DV = D_MODEL // HG_HEADS
HG_KEY_WIDTH = HG_HEADS * HG_DK
HG_VAL_WIDTH = HG_HEADS * HG_DV
HG_CHUNK = 64

SWA_Q_HEADS = 16
SWA_KV_HEADS = 4
SWA_GROUP = SWA_Q_HEADS // SWA_KV_HEADS
SWA_HEAD_DIM = 64
SWA_WIDTH = SWA_Q_HEADS * SWA_HEAD_DIM
SWA_KV_WIDTH = SWA_KV_HEADS * SWA_HEAD_DIM
WINDOW = 128

MEM_TOKENS = 256
MEM_HEADS = 4
MEM_HEAD_DIM = 128
MEM_WIDTH = MEM_HEADS * MEM_HEAD_DIM

ROPE_THETA = 10000.0
NORM_EPS = 1e-6

A_IN_WIDTH = 2 * HG_KEY_WIDTH + 2 * HG_VAL_WIDTH + 2 * MEM_WIDTH
B_IN_WIDTH = 2 * SWA_WIDTH + 2 * MEM_WIDTH
OUT_WIDTH = HG_VAL_WIDTH + MEM_WIDTH

LANES = 128
SUBLANES = 8
MASK_VALUE = -1e30
LOG2E = 1.4426950408889634

IN_TILE = 512
IN_SUB_TILE = 256
OUT_TILE = 512
HG_TILE = 256
SWA_TILE = 512
VMEM_LIMIT = 56 * 1024 * 1024

F32 = jnp.float32
BF16 = jnp.bfloat16

_NT = (((1,), (1,)), ((), ()))
_TN = (((0,), (0,)), ((), ()))


def _dot(a, b):
    return jnp.dot(a, b, preferred_element_type=F32)


def _dot_nt(a, b):
    return lax.dot_general(a, b, _NT, preferred_element_type=F32)


def _dot_tn(a, b):
    return lax.dot_general(a, b, _TN, preferred_element_type=F32)


def _rms_norm(x, g):
    ms = jnp.mean(x * x, axis=-1, keepdims=True)
    return x * lax.rsqrt(ms + NORM_EPS) * g


def _silu(x):
    h = 0.5 * x
    return h + h * jnp.tanh(h)


def _params(*semantics):
    return pltpu.CompilerParams(dimension_semantics=semantics, vmem_limit_bytes=VMEM_LIMIT)


def _layer_weight(w, layer):
    return pl.BlockSpec((None,) + w.shape[1:], lambda *_: (layer, 0, 0), pipeline_mode=pl.Buffered(1))


def _layer_input(prev, h_ref, rows):
    if prev is None:
        return h_ref[rows, :]
    om_ref, ome_ref, wout_ref, pg_ref, hnew_ref, main_transposed = prev
    if main_transposed:
        y = _dot_tn(om_ref[:, rows], wout_ref[:HG_VAL_WIDTH, :])
    else:
        y = _dot(om_ref[rows, :], wout_ref[:HG_VAL_WIDTH, :])
    y = y + _dot(ome_ref[rows, :], wout_ref[HG_VAL_WIDTH:, :])
    h_new = h_ref[rows, :] + _rms_norm(y, pg_ref[...])
    hnew_ref[rows, :] = h_new
    return h_new


def _split_refs(prev_transposed, refs, n_in):
    if prev_transposed is None:
        return None, refs[:n_in], refs[n_in:]
    prev_in, ins, (hnew_ref, *outs) = refs[:4], refs[4:4 + n_in], refs[4 + n_in:]
    return (*prev_in, hnew_ref, prev_transposed), ins, outs


def _pending_specs(pending, row, col, full):
    if pending is None:
        return [], []
    o_main, o_mem, w_out, w_layer, post_g, transposed = pending
    main = col(HG_VAL_WIDTH) if transposed else row(HG_VAL_WIDTH)
    return [o_main, o_mem, w_out, post_g], [main, row(MEM_WIDTH), _layer_weight(w_out, w_layer), full(post_g)]


def _rope_table_kernel(pos_ref, invf_ref, sign_ref, cos_ref, sin_ref):
    ang = pos_ref[...].astype(F32) * invf_ref[...]
    cos_ref[...] = jnp.cos(ang)
    sin_ref[...] = jnp.sin(ang) * sign_ref[...]


def _rope_tables(positions):
    n = positions.size
    half = SWA_HEAD_DIM // 2
    inv_freq = ROPE_THETA ** (-jnp.arange(0, SWA_HEAD_DIM, 2, dtype=F32) / SWA_HEAD_DIM)
    invf = jnp.tile(inv_freq, LANES // half).reshape(1, LANES)
    sign = jnp.tile(jnp.concatenate([-jnp.ones((half,), F32), jnp.ones((half,), F32)]),
                    LANES // SWA_HEAD_DIM).reshape(1, LANES)
    pos = jnp.broadcast_to(positions.reshape(n, 1), (n, LANES))
    tile = 2048
    row = pl.BlockSpec((tile, LANES), lambda i: (i, 0))
    vec = pl.BlockSpec((1, LANES), lambda i: (0, 0))
    return pl.pallas_call(
        _rope_table_kernel,
        grid=(n // tile,),
        in_specs=[row, vec, vec],
        out_specs=[row, row],
        out_shape=[jax.ShapeDtypeStruct((n, LANES), F32)] * 2,
        compiler_params=_params("parallel"),
        name="rope_tables",
    )(pos, invf, sign)


def _rope_block(xc, cos, sin, first_half):
    swapped = jnp.where(first_half,
                        pltpu.roll(xc, LANES - SWA_HEAD_DIM // 2, 1),
                        pltpu.roll(xc, SWA_HEAD_DIM // 2, 1))
    return xc * cos + swapped * sin


def _first_half_mask(rows):
    lane = lax.broadcasted_iota(jnp.int32, (rows, LANES), 1)
    return (lane & (SWA_HEAD_DIM // 2)) == 0


def _mem_kv_kernel(mem_ref, g_ref, w_ref, k_ref, v_ref):
    mn = _rms_norm(mem_ref[...], g_ref[...]).astype(BF16)
    kv = _dot(mn, w_ref[...])
    k_ref[...] = kv[:, :MEM_WIDTH].astype(BF16)
    v_ref[...] = kv[:, MEM_WIDTH:].astype(BF16)


def _mem_kv(mem, mem_norm_g, w_mem_kv):
    b = mem.shape[0]
    out = pl.BlockSpec((None, None, MEM_TOKENS, MEM_WIDTH), lambda l, i: (l, i, 0, 0))
    return pl.pallas_call(
        _mem_kv_kernel,
        grid=(DEPTH, b),
        in_specs=[
            pl.BlockSpec((None, MEM_TOKENS, D_MODEL), lambda l, i: (i, 0, 0)),
            pl.BlockSpec((None, 1, D_MODEL), lambda l, i: (l, 0, 0)),
            pl.BlockSpec((None, D_MODEL, 2 * MEM_WIDTH), lambda l, i: (l, 0, 0)),
        ],
        out_specs=[out, out],
        out_shape=[jax.ShapeDtypeStruct((DEPTH, b, MEM_TOKENS, MEM_WIDTH), BF16)] * 2,
        compiler_params=_params("arbitrary", "arbitrary"),
        name="mem_kv",
    )(mem, mem_norm_g.reshape(DEPTH, 1, D_MODEL), w_mem_kv)


def _memory_probs(mq, mk_ref):
    mq = (mq * (MEM_HEAD_DIM ** -0.5 * LOG2E)).astype(BF16)
    probs, inv_sums = [], []
    for hh in range(MEM_HEADS):
        hs = slice(hh * MEM_HEAD_DIM, (hh + 1) * MEM_HEAD_DIM)
        s = _dot_nt(mq[:, hs], mk_ref[:, hs])
        p = jnp.exp2(s - jnp.max(s, axis=-1, keepdims=True))
        probs.append(p.astype(BF16))
        inv_sums.append(1.0 / jnp.sum(p, axis=-1, keepdims=True))
    return probs, inv_sums


def _memory_readout(probs, inv_sums, gate, mv_ref, out_ref, rows):
    for hh in range(MEM_HEADS):
        hs = slice(hh * MEM_HEAD_DIM, (hh + 1) * MEM_HEAD_DIM)
        o = _dot(probs[hh], mv_ref[:, hs]) * inv_sums[hh]
        out_ref[rows, hs] = (o * gate[:, hs]).astype(BF16)


def _a_in_kernel(layer, prev_transposed, h_ref, *refs):
    prev, (g_ref, w_ref, lbl_ref, mk_ref, mv_ref), outs = _split_refs(prev_transposed, refs, 5)
    qs_ref, lf_ref, kk_ref, v_ref, gm_ref, om_ref = outs

    logits = lbl_ref[...]
    e = jnp.exp(logits - jnp.max(logits, axis=0, keepdims=True))
    lb = jnp.sum(e[:layer + 1], axis=0, keepdims=True) / jnp.sum(e, axis=0, keepdims=True)
    mid = 0.5 * (1.0 + lb)
    half = 0.5 * (1.0 - lb)

    col_q, col_f, col_v = 0, HG_KEY_WIDTH, 2 * HG_KEY_WIDTH
    col_g = col_v + HG_VAL_WIDTH
    col_mq = col_g + HG_VAL_WIDTH
    col_mg = col_mq + MEM_WIDTH

    for sub in range(IN_TILE // IN_SUB_TILE):
        rows = slice(sub * IN_SUB_TILE, (sub + 1) * IN_SUB_TILE)
        xn = _rms_norm(_layer_input(prev, h_ref, rows), g_ref[...]).astype(BF16)

        def proj(lo, width):
            return _dot(xn, w_ref[:, lo:lo + width])

        mq = proj(col_mq, MEM_WIDTH)
        mem_gate = _silu(proj(col_mg, MEM_WIDTH))
        qs_ref[rows, :] = _silu(proj(col_q, HG_KEY_WIDTH)).astype(BF16)
        probs, inv_sums = _memory_probs(mq, mk_ref)

        th = half * jnp.tanh(0.5 * proj(col_f, HG_KEY_WIDTH))
        lf_ref[rows, :] = jnp.log2(mid + th)
        kk_ref[rows, :] = (half - th).astype(BF16)

        _memory_readout(probs, inv_sums, mem_gate, mv_ref, om_ref, rows)
        v_ref[rows, :] = proj(col_v, HG_VAL_WIDTH).astype(BF16)
        gm_ref[rows, :] = _silu(proj(col_g, HG_VAL_WIDTH)).astype(BF16)


def _a_in_proj(layer, h, pending, g, w, lb_logits, mem_k, mem_v, tiles_per_seq):
    n = h.shape[0]
    row = lambda width: pl.BlockSpec((IN_TILE, width), lambda i: (i, 0))
    col = lambda width: pl.BlockSpec((width, IN_TILE), lambda i: (0, i))
    full = lambda a: pl.BlockSpec(a.shape, lambda i: (0,) * a.ndim)
    mem = pl.BlockSpec((None, None, MEM_TOKENS, MEM_WIDTH), lambda i: (layer, i // tiles_per_seq, 0, 0))
    widths = (HG_KEY_WIDTH, HG_KEY_WIDTH, HG_KEY_WIDTH, HG_VAL_WIDTH, HG_VAL_WIDTH, MEM_WIDTH)
    dtypes = (BF16, F32, BF16, BF16, BF16, BF16)
    if pending is not None:
        widths, dtypes = (D_MODEL,) + widths, (F32,) + dtypes
    pend_args, pend_specs = _pending_specs(pending, row, col, full)
    return pl.pallas_call(
        functools.partial(_a_in_kernel, layer, None if pending is None else pending[-1]),
        grid=(n // IN_TILE,),
        in_specs=[row(D_MODEL)] + pend_specs + [full(g), _layer_weight(w, layer), full(lb_logits), mem, mem],
        out_specs=[row(wd) for wd in widths],
        out_shape=[jax.ShapeDtypeStruct((n, wd), dt) for wd, dt in zip(widths, dtypes)],
        compiler_params=_params("parallel"),
        name=f"a_in_proj_{layer}",
    )(h, *pend_args, g, w, lb_logits, mem_k, mem_v)


_LEVELS = (32, 16, 8, 4, 2, 1)
_SAFE_BLOCK = 16
SAFE_BLOCK_DECAY = 100.0


def _coarse_level(cum, qs, kk, m):
    c, w = cum.shape
    early = m % (2 * SUBLANES) == 0
    dt = BF16 if early else F32
    zeros = jnp.zeros((m, w), dt)
    qd, kd = [], []
    for blk in range(c // (2 * m)):
        lo, mid, hi = blk * 2 * m, blk * 2 * m + m, (blk + 1) * 2 * m
        ref = cum[mid - 1:mid, :]
        kd += [(kk[lo:mid] * jnp.exp2(ref - cum[lo:mid])).astype(dt), zeros]
        qd += [zeros, (qs[mid:hi] * jnp.exp2(cum[mid:hi] - ref)).astype(dt)]
    return jnp.concatenate(qd, axis=0).astype(BF16), jnp.concatenate(kd, axis=0).astype(BF16)


def _fine_level(cum3, qs3, kk3, m, upper_pen, lower_pen):
    groups, _, w = cum3.shape
    parts = [jnp.broadcast_to(cum3[:, blk * 2 * m + m - 1:blk * 2 * m + m, :], (groups, 2 * m, w))
             for blk in range(SUBLANES // (2 * m))]
    x = cum3 - (parts[0] if len(parts) == 1 else jnp.concatenate(parts, axis=1))
    qd = qs3 * jnp.exp2(x + upper_pen)
    kd = kk3 * jnp.exp2(lower_pen - x)
    c = groups * SUBLANES
    return qd.reshape(c, w).astype(BF16), kd.reshape(c, w).astype(BF16)


def _block_level(cum, qs, kk):
    c, _ = cum.shape
    local = [cum[:_SAFE_BLOCK]] + [cum[b:b + _SAFE_BLOCK] - cum[b - 1:b] for b in range(_SAFE_BLOCK, c, _SAFE_BLOCK)]
    local = jnp.concatenate(local, axis=0)
    return (qs * jnp.exp2(local)).astype(BF16), (kk * jnp.exp2(-local)).astype(BF16)


def _hgrn_constants():
    c = HG_CHUNK
    row = lax.broadcasted_iota(jnp.int32, (c, c), 0)
    col = lax.broadcasted_iota(jnp.int32, (c, c), 1)
    sub = lax.broadcasted_iota(jnp.int32, (1, SUBLANES, HG_DK), 1)
    fine = [m for m in _LEVELS if m < SUBLANES]
    same_block = {m: (row // (2 * m)) == (col // (2 * m)) for m in _LEVELS if 2 * m < c}
    return dict(
        same_block={m: mask.astype(F32) for m, mask in same_block.items()},
        diag=(row == col).astype(F32),
        causal_block=(same_block[_SAFE_BLOCK // 2] & (col <= row)).astype(F32),
        upper_pen={m: jnp.where((sub & m) != 0, 0.0, MASK_VALUE) for m in fine},
        lower_pen={m: jnp.where((sub & m) != 0, MASK_VALUE, 0.0) for m in fine},
        even_row=jnp.where((sub & 1) != 0, 0.0, 1.0),
    )


def _hgrn_tile(single_reference_blocks, k, qs_ref, lf_ref, kk_ref, v_ref, gm_ref, gn, cum_s, o_ref, st_ref):
    c = HG_CHUNK
    heads = [slice(h * HG_DK, (h + 1) * HG_DK) for h in range(HG_HEADS)]
    shape3 = (c // SUBLANES, SUBLANES, HG_DK)
    for ci in range(HG_TILE // c):
        rows = slice(ci * c, (ci + 1) * c)
        scores_bf, q_ins, k_outs, lasts = [], [], [], []
        for hs in heads:
            cum = cum_s[rows, hs]
            qs = qs_ref[rows, hs].astype(F32)
            kk = kk_ref[rows, hs].astype(F32)

            pairs = []
            for m in _LEVELS:
                if single_reference_blocks and 2 * m <= _SAFE_BLOCK:
                    continue
                if m >= SUBLANES:
                    qd, kd = _coarse_level(cum, qs, kk, m)
                elif m == 1:
                    qs3, kk3 = qs.reshape(shape3), kk.reshape(shape3)
                    qd3 = qs3 * jnp.exp2(lf_ref[rows, hs].reshape(shape3) + k["upper_pen"][m])
                    qd = qd3.reshape(c, HG_DK).astype(BF16)
                    kd = (kk3 * k["even_row"]).reshape(c, HG_DK).astype(BF16)
                else:
                    qd, kd = _fine_level(cum.reshape(shape3), qs.reshape(shape3), kk.reshape(shape3), m,
                                         k["upper_pen"][m], k["lower_pen"][m])
                pairs.append((qd, kd, k["same_block"].get(m)))
            if single_reference_blocks:
                pairs.append(_block_level(cum, qs, kk) + (k["causal_block"],))
            else:
                pairs.append((qs.astype(BF16), kk.astype(BF16), k["diag"]))

            scores = None
            for qd, kd, mask in pairs:
                s = _dot_nt(qd, kd)
                if mask is not None:
                    s = s * mask
                scores = s if scores is None else scores + s

            last = cum[c - 1:c, :]
            scores_bf.append(scores.astype(BF16))
            q_ins.append((qs * jnp.exp2(cum)).astype(BF16))
            k_outs.append((kk * jnp.exp2(last - cum)).astype(BF16))
            lasts.append(last)

        for h, hs in enumerate(heads):
            o = _dot_nt(q_ins[h], st_ref[h].astype(BF16)) + _dot(scores_bf[h], v_ref[rows, hs])
            o = o * lax.rsqrt(jnp.mean(o * o, axis=-1, keepdims=True) + NORM_EPS)
            o_ref[rows, hs] = (o * gn[:, hs] * gm_ref[rows, hs].astype(F32)).astype(BF16)

        for h, hs in enumerate(heads):
            st_ref[h] = st_ref[h] * jnp.exp2(lasts[h]) + _dot_tn(v_ref[rows, hs], k_outs[h])


def _hgrn_kernel(qs_ref, lf_ref, kk_ref, v_ref, gm_ref, gn_ref, o_ref, cum_s, st_ref):
    c = HG_CHUNK

    @pl.when(pl.program_id(1) == 0)
    def _():
        st_ref[...] = jnp.zeros_like(st_ref)

    row = lax.broadcasted_iota(jnp.int32, (c, c), 0)
    col = lax.broadcasted_iota(jnp.int32, (c, c), 1)
    tri = (col <= row).astype(BF16)

    worst = None
    for ci in range(HG_TILE // c):
        rows = slice(ci * c, (ci + 1) * c)
        lf = lf_ref[rows, :]
        hi = lf.astype(BF16)
        lo = (lf - hi.astype(F32)).astype(BF16)
        cum = _dot(tri, hi) + _dot(tri, lo)
        cum_s[rows, :] = cum
        ends = [cum[b - 1:b, :] for b in range(_SAFE_BLOCK, c + 1, _SAFE_BLOCK)]
        for b, end in enumerate(ends):
            decay = end if b == 0 else end - ends[b - 1]
            worst = decay if worst is None else jnp.minimum(worst, decay)
    single_reference_ok = jnp.min(worst) >= -SAFE_BLOCK_DECAY

    consts = _hgrn_constants()
    args = (consts, qs_ref, lf_ref, kk_ref, v_ref, gm_ref, gn_ref[...], cum_s, o_ref, st_ref)

    @pl.when(single_reference_ok)
    def _():
        _hgrn_tile(True, *args)

    @pl.when(jnp.logical_not(single_reference_ok))
    def _():
        _hgrn_tile(False, *args)


def _hgrn(qs, lf, kk, v, gm, gn, batch):
    n = qs.shape[0]
    tiles = n // batch // HG_TILE
    row = pl.BlockSpec((HG_TILE, HG_KEY_WIDTH), lambda b, t: (b * tiles + t, 0))
    return pl.pallas_call(
        _hgrn_kernel,
        grid=(batch, tiles),
        in_specs=[row, row, row, row, row, pl.BlockSpec((1, HG_VAL_WIDTH), lambda b, t: (0, 0))],
        out_specs=row,
        out_shape=jax.ShapeDtypeStruct((n, HG_VAL_WIDTH), BF16),
        scratch_shapes=[pltpu.VMEM((HG_TILE, HG_KEY_WIDTH), F32),
                        pltpu.VMEM((HG_HEADS, HG_DV, HG_DK), F32)],
        compiler_params=_params("arbitrary", "arbitrary"),
        name="hgrn2",
    )(qs, lf, kk, v, gm, gn)


def _out_kernel(main_transposed, om_ref, ome_ref, w_ref, g_ref, h_ref, o_ref):
    main_dot = _dot_tn if main_transposed else _dot
    y = main_dot(om_ref[...], w_ref[:HG_VAL_WIDTH, :]) + _dot(ome_ref[...], w_ref[HG_VAL_WIDTH:, :])
    o_ref[...] = h_ref[...] + _rms_norm(y, g_ref[...])


def _out_proj(o_main, o_mem, w, layer, g, h, name, main_transposed=False):
    n = h.shape[0]
    row = lambda width: pl.BlockSpec((OUT_TILE, width), lambda i: (i, 0))
    col = lambda width: pl.BlockSpec((width, OUT_TILE), lambda i: (0, i))
    full = lambda a: pl.BlockSpec(a.shape, lambda i: (0,) * a.ndim)
    main = col(HG_VAL_WIDTH) if main_transposed else row(HG_VAL_WIDTH)
    return pl.pallas_call(
        functools.partial(_out_kernel, main_transposed),
        grid=(n // OUT_TILE,),
        in_specs=[main, row(MEM_WIDTH), _layer_weight(w, layer), full(g), row(D_MODEL)],
        out_specs=row(D_MODEL),
        out_shape=jax.ShapeDtypeStruct((n, D_MODEL), F32),
        compiler_params=_params("parallel"),
        name=name,
    )(o_main, o_mem, w, g, h)


def _shared_kv_kernel(x_ref, g_ref, w_ref, cos_ref, sin_ref, k_ref, v_ref):
    xn = _rms_norm(x_ref[...], g_ref[...]).astype(BF16)
    kv = _dot(xn, w_ref[...])
    cos = cos_ref[...]
    sin = sin_ref[...]
    first = _first_half_mask(IN_TILE)
    heads_per_block = LANES // SWA_HEAD_DIM
    for cb in range(SWA_KV_WIDTH // LANES):
        kr = _rope_block(kv[:, cb * LANES:(cb + 1) * LANES], cos, sin, first).astype(BF16)
        for j in range(heads_per_block):
            k_ref[cb * heads_per_block + j] = kr[:, j * SWA_HEAD_DIM:(j + 1) * SWA_HEAD_DIM]
    v_ref[...] = kv[:, SWA_KV_WIDTH:].T.astype(BF16)


def _shared_kv(h, g, w, cos, sin):
    n = h.shape[0]
    row = lambda width: pl.BlockSpec((IN_TILE, width), lambda i: (i, 0))
    full = lambda a: pl.BlockSpec(a.shape, lambda i: (0,) * a.ndim)
    return pl.pallas_call(
        _shared_kv_kernel,
        grid=(n // IN_TILE,),
        in_specs=[row(D_MODEL), full(g), full(w), row(LANES), row(LANES)],
        out_specs=[pl.BlockSpec((SWA_KV_HEADS, IN_TILE, SWA_HEAD_DIM), lambda i: (0, i, 0)),
                   pl.BlockSpec((SWA_KV_WIDTH, IN_TILE), lambda i: (0, i))],
        out_shape=[jax.ShapeDtypeStruct((SWA_KV_HEADS, n, SWA_HEAD_DIM), BF16),
                   jax.ShapeDtypeStruct((SWA_KV_WIDTH, n), BF16)],
        compiler_params=_params("parallel"),
        name="shared_kv",
    )(h, g, w, cos, sin)


def _b_in_kernel(prev_transposed, h_ref, *refs):
    prev, (g_ref, w_ref, cos_ref, sin_ref, mk_ref, mv_ref), outs = _split_refs(prev_transposed, refs, 6)
    qt_ref, gmt_ref, om_ref = outs
    first = _first_half_mask(IN_SUB_TILE)
    scale = SWA_HEAD_DIM ** -0.5 * LOG2E
    col_mq = 2 * SWA_WIDTH
    col_mg = col_mq + MEM_WIDTH

    for sub in range(IN_TILE // IN_SUB_TILE):
        rows = slice(sub * IN_SUB_TILE, (sub + 1) * IN_SUB_TILE)
        xn = _rms_norm(_layer_input(prev, h_ref, rows), g_ref[...]).astype(BF16)

        def proj(lo, width):
            return _dot(xn, w_ref[:, lo:lo + width])

        mq = proj(col_mq, MEM_WIDTH)
        mem_gate = _silu(proj(col_mg, MEM_WIDTH))

        q = proj(0, SWA_WIDTH)
        probs, inv_sums = _memory_probs(mq, mk_ref)
        cos = cos_ref[rows, :]
        sin = sin_ref[rows, :]
        for cb in range(SWA_WIDTH // LANES):
            cs = slice(cb * LANES, (cb + 1) * LANES)
            qt_ref[cs, rows] = (_rope_block(q[:, cs], cos, sin, first) * scale).T.astype(BF16)

        gate = _silu(proj(SWA_WIDTH, SWA_WIDTH))
        _memory_readout(probs, inv_sums, mem_gate, mv_ref, om_ref, rows)
        for cb in range(SWA_WIDTH // LANES):
            cs = slice(cb * LANES, (cb + 1) * LANES)
            gmt_ref[cs, rows] = gate[:, cs].T.astype(BF16)


def _b_in_proj(layer, h, pending, g, w, cos, sin, mem_k, mem_v, tiles_per_seq):
    n = h.shape[0]
    row = lambda width: pl.BlockSpec((IN_TILE, width), lambda i: (i, 0))
    col = lambda width: pl.BlockSpec((width, IN_TILE), lambda i: (0, i))
    full = lambda a: pl.BlockSpec(a.shape, lambda i: (0,) * a.ndim)
    mem = pl.BlockSpec((None, None, MEM_TOKENS, MEM_WIDTH), lambda i: (layer, i // tiles_per_seq, 0, 0))
    out_specs = [col(SWA_WIDTH), col(SWA_WIDTH), row(MEM_WIDTH)]
    out_shape = [jax.ShapeDtypeStruct((SWA_WIDTH, n), BF16), jax.ShapeDtypeStruct((SWA_WIDTH, n), BF16),
                 jax.ShapeDtypeStruct((n, MEM_WIDTH), BF16)]
    if pending is not None:
        out_specs, out_shape = [row(D_MODEL)] + out_specs, [jax.ShapeDtypeStruct((n, D_MODEL), F32)] + out_shape
    pend_args, pend_specs = _pending_specs(pending, row, col, full)
    return pl.pallas_call(
        functools.partial(_b_in_kernel, None if pending is None else pending[-1]),
        grid=(n // IN_TILE,),
        in_specs=[row(D_MODEL)] + pend_specs
        + [full(g), _layer_weight(w, layer - N_A_LAYERS), row(LANES), row(LANES), mem, mem],
        out_specs=out_specs,
        out_shape=out_shape,
        compiler_params=_params("parallel"),
        name=f"b_in_proj_{layer}",
    )(h, *pend_args, g, w, cos, sin, mem_k, mem_v)


def _swa_kernel(sink_ref, qt_ref, kc_ref, kp_ref, vtc_ref, vtp_ref, gmt_ref, ot_ref):
    blk = WINDOW
    dh = SWA_HEAD_DIM
    ki = lax.broadcasted_iota(jnp.int32, (blk, blk), 0)
    qi = lax.broadcasted_iota(jnp.int32, (blk, blk), 1)
    from_prev = ki > qi
    from_prev_bf = from_prev.astype(BF16)
    prev_bias = jnp.where(pl.program_id(1) > 0, 0.0, MASK_VALUE)

    for sb in range(SWA_TILE // blk):
        cols = slice(sb * blk, (sb + 1) * blk)
        two = slice((sb - 1) * blk, (sb + 1) * blk)

        probs, inv_den = [], []
        for g in range(SWA_KV_HEADS):
            k2 = jnp.concatenate([kp_ref[g], kc_ref[g, cols, :]], axis=0) if sb == 0 else kc_ref[g, two, :]
            heads = range(g * SWA_GROUP, (g + 1) * SWA_GROUP)
            qt = jnp.concatenate([qt_ref[h * dh:(h + 1) * dh, cols] for h in heads], axis=1)
            st = _dot(k2, qt)
            ps = []
            for j, h in enumerate(heads):
                s_prev = st[:blk, j * blk:(j + 1) * blk]
                if sb == 0:
                    s_prev = s_prev + prev_bias
                s = jnp.where(from_prev, s_prev, st[blk:, j * blk:(j + 1) * blk])
                sink = sink_ref[h] * LOG2E
                m = jnp.maximum(jnp.max(s, axis=0, keepdims=True), sink)
                p = jnp.exp2(s - m)
                inv_den.append(1.0 / (jnp.sum(p, axis=0, keepdims=True) + jnp.exp2(sink - m)))
                p = p.astype(BF16)
                p_prev = p * from_prev_bf
                ps.append(jnp.concatenate([p_prev, p - p_prev], axis=0))
            probs.append(jnp.concatenate(ps, axis=1))

        for g in range(SWA_KV_HEADS):
            vrows = slice(g * dh, (g + 1) * dh)
            vt2 = (jnp.concatenate([vtp_ref[vrows, :], vtc_ref[vrows, cols]], axis=1) if sb == 0
                   else vtc_ref[vrows, two])
            ot = _dot(vt2, probs[g])
            for j, h in enumerate(range(g * SWA_GROUP, (g + 1) * SWA_GROUP)):
                rows = slice(h * dh, (h + 1) * dh)
                o = ot[:, j * blk:(j + 1) * blk] * inv_den[h]
                ot_ref[rows, cols] = (o * gmt_ref[rows, cols].astype(F32)).astype(BF16)


def _swa(sinks, qt, k, vt, gmt, batch):
    n = qt.shape[1]
    tiles = n // batch // SWA_TILE
    sub = SWA_TILE // WINDOW
    tile = lambda b, t: b * tiles + t
    prev = lambda b, t: jnp.maximum(tile(b, t) * sub - 1, 0)
    wide = pl.BlockSpec((SWA_WIDTH, SWA_TILE), lambda b, t: (0, tile(b, t)))
    return pl.pallas_call(
        _swa_kernel,
        grid=(batch, tiles),
        in_specs=[pl.BlockSpec(memory_space=pltpu.SMEM),
                  wide,
                  pl.BlockSpec((SWA_KV_HEADS, SWA_TILE, SWA_HEAD_DIM), lambda b, t: (0, tile(b, t), 0)),
                  pl.BlockSpec((SWA_KV_HEADS, WINDOW, SWA_HEAD_DIM), lambda b, t: (0, prev(b, t), 0)),
                  pl.BlockSpec((SWA_KV_WIDTH, SWA_TILE), lambda b, t: (0, tile(b, t))),
                  pl.BlockSpec((SWA_KV_WIDTH, WINDOW), lambda b, t: (0, prev(b, t))),
                  wide],
        out_specs=wide,
        out_shape=jax.ShapeDtypeStruct((SWA_WIDTH, n), BF16),
        compiler_params=_params("parallel", "arbitrary"),
        name="swa",
    )(sinks, qt, k, k, vt, vt, gmt)


def kernel(x, mem, positions, pre_norm_g, post_norm_g, mem_norm_g, w_mem_kv, a_w_in, a_lb_logits, a_out_norm_g, a_w_out, kv_norm_g, w_kv_shared, b_w_in, b_sinks, b_w_out):
    batch, seq, _ = x.shape
    n = batch * seq
    assert seq % max(IN_TILE, OUT_TILE, HG_TILE, SWA_TILE) == 0
    tiles_per_seq = seq // IN_TILE

    h = x.reshape(n, D_MODEL)
    mem_k, mem_v = _mem_kv(mem, mem_norm_g, w_mem_kv.astype(BF16))
    cos, sin = _rope_tables(positions)
    a_w_in = a_w_in.astype(BF16)
    a_w_out = a_w_out.astype(BF16)
    b_w_in = b_w_in.astype(BF16)
    b_w_out = b_w_out.astype(BF16)
    k_sh = v_sh = None
    pending = None

    for layer in range(DEPTH):
        pre_g = pre_norm_g[layer].reshape(1, D_MODEL)
        post_g = post_norm_g[layer].reshape(1, D_MODEL)
        if layer < N_A_LAYERS:
            outs = _a_in_proj(layer, h, pending, pre_g, a_w_in, a_lb_logits, mem_k, mem_v, tiles_per_seq)
            if pending is not None:
                h, *outs = outs
            qs, lf, kk, v, gm, o_mem = outs
            o_main = _hgrn(qs, lf, kk, v, gm, a_out_norm_g[layer].reshape(1, HG_VAL_WIDTH), batch)
            pending = (o_main, o_mem, a_w_out, layer, post_g, False)
        else:
            j = layer - N_A_LAYERS
            outs = _b_in_proj(layer, h, pending, pre_g, b_w_in, cos, sin, mem_k, mem_v, tiles_per_seq)
            if pending is not None:
                h, *outs = outs
            qt, gmt, o_mem = outs
            if k_sh is None:
                k_sh, v_sh = _shared_kv(h, kv_norm_g.reshape(1, D_MODEL), w_kv_shared.astype(BF16), cos, sin)
            o_main_t = _swa(b_sinks[j], qt, k_sh, v_sh, gmt, batch)
            pending = (o_main_t, o_mem, b_w_out, j, post_g, True)
    o_main, o_mem, w_out, w_layer, post_g, transposed = pending
    h = _out_proj(o_main, o_mem, w_out, w_layer, post_g, h, "out_proj_last", main_transposed=transposed)
    return h.reshape(batch, seq, D_MODEL)
```

```python
import functools

import jax
import jax.numpy as jnp
from jax import lax
from jax.experimental import pallas as pl
from jax.experimental.pallas import tpu as pltpu

D_MODEL = 1024
DEPTH = 4
N_A_LAYERS = DEPTH // 2

HG_HEADS = 8
HG_DK = 128
HG_DV = D_MODEL // HG_HEADS
HG_KEY_WIDTH = HG_HEADS * HG_DK
HG_VAL_WIDTH = HG_HEADS * HG_DV
HG_CHUNK = 64

SWA_Q_HEADS = 16
SWA_KV_HEADS = 4
SWA_GROUP = SWA_Q_HEADS // SWA_KV_HEADS
SWA_HEAD_DIM = 64
SWA_WIDTH = SWA_Q_HEADS * SWA_HEAD_DIM
SWA_KV_WIDTH = SWA_KV_HEADS * SWA_HEAD_DIM
WINDOW = 128

MEM_TOKENS = 256
MEM_HEADS = 4
MEM_HEAD_DIM = 128
MEM_WIDTH = MEM_HEADS * MEM_HEAD_DIM

ROPE_THETA = 10000.0
NORM_EPS = 1e-6

A_IN_WIDTH = 2 * HG_KEY_WIDTH + 2 * HG_VAL_WIDTH + 2 * MEM_WIDTH
B_IN_WIDTH = 2 * SWA_WIDTH + 2 * MEM_WIDTH
OUT_WIDTH = HG_VAL_WIDTH + MEM_WIDTH

LANES = 128
SUBLANES = 8
MASK_VALUE = -1e30
LOG2E = 1.4426950408889634

IN_TILE = 512
IN_SUB_TILE = 256
OUT_TILE = 512
HG_TILE = 256
SWA_TILE = 512
VMEM_LIMIT = 56 * 1024 * 1024

F32 = jnp.float32
BF16 = jnp.bfloat16

_NT = (((1,), (1,)), ((), ()))
_TN = (((0,), (0,)), ((), ()))


def _dot(a, b):
    return jnp.dot(a, b, preferred_element_type=F32)


def _dot_nt(a, b):
    return lax.dot_general(a, b, _NT, preferred_element_type=F32)


def _dot_tn(a, b):
    return lax.dot_general(a, b, _TN, preferred_element_type=F32)


def _rms_norm(x, g):
    ms = jnp.mean(x * x, axis=-1, keepdims=True)
    return x * lax.rsqrt(ms + NORM_EPS) * g


def _silu(x):
    h = 0.5 * x
    return h + h * jnp.tanh(h)


def _params(*semantics):
    return pltpu.CompilerParams(dimension_semantics=semantics, vmem_limit_bytes=VMEM_LIMIT)


def _layer_weight(w, layer):
    return pl.BlockSpec((None,) + w.shape[1:], lambda *_: (layer, 0, 0), pipeline_mode=pl.Buffered(1))


def _layer_input(prev, h_ref, rows):
    if prev is None:
        return h_ref[rows, :]
    om_ref, ome_ref, wout_ref, pg_ref, hnew_ref, main_transposed = prev
    if main_transposed:
        y = _dot_tn(om_ref[:, rows], wout_ref[:HG_VAL_WIDTH, :])
    else:
        y = _dot(om_ref[rows, :], wout_ref[:HG_VAL_WIDTH, :])
    y = y + _dot(ome_ref[rows, :], wout_ref[HG_VAL_WIDTH:, :])
    h_new = h_ref[rows, :] + _rms_norm(y, pg_ref[...])
    hnew_ref[rows, :] = h_new
    return h_new


def _normed_sub_tiles(prev, h_ref, g_ref):
    return [_rms_norm(_layer_input(prev, h_ref, slice(s * IN_SUB_TILE, (s + 1) * IN_SUB_TILE)),
                      g_ref[...]).astype(BF16) for s in range(IN_TILE // IN_SUB_TILE)]


def _split_refs(prev_transposed, refs, n_in):
    if prev_transposed is None:
        return None, refs[:n_in], refs[n_in:]
    prev_in, ins, (hnew_ref, *outs) = refs[:4], refs[4:4 + n_in], refs[4 + n_in:]
    return (*prev_in, hnew_ref, prev_transposed), ins, outs


def _pending_specs(pending, row, col, full):
    if pending is None:
        return [], []
    o_main, o_mem, w_out, w_layer, post_g, transposed = pending
    main = col(HG_VAL_WIDTH) if transposed else row(HG_VAL_WIDTH)
    return [o_main, o_mem, w_out, post_g], [main, row(MEM_WIDTH), _layer_weight(w_out, w_layer), full(post_g)]


def _rope_table_kernel(pos_ref, invf_ref, sign_ref, cos_ref, sin_ref):
    ang = pos_ref[...].astype(F32) * invf_ref[...]
    cos_ref[...] = jnp.cos(ang)
    sin_ref[...] = jnp.sin(ang) * sign_ref[...]


def _rope_tables(positions):
    n = positions.size
    half = SWA_HEAD_DIM // 2
    inv_freq = ROPE_THETA ** (-jnp.arange(0, SWA_HEAD_DIM, 2, dtype=F32) / SWA_HEAD_DIM)
    invf = jnp.tile(inv_freq, LANES // half).reshape(1, LANES)
    sign = jnp.tile(jnp.concatenate([-jnp.ones((half,), F32), jnp.ones((half,), F32)]),
                    LANES // SWA_HEAD_DIM).reshape(1, LANES)
    pos = jnp.broadcast_to(positions.reshape(n, 1), (n, LANES))
    tile = 2048
    row = pl.BlockSpec((tile, LANES), lambda i: (i, 0))
    vec = pl.BlockSpec((1, LANES), lambda i: (0, 0))
    return pl.pallas_call(
        _rope_table_kernel,
        grid=(n // tile,),
        in_specs=[row, vec, vec],
        out_specs=[row, row],
        out_shape=[jax.ShapeDtypeStruct((n, LANES), F32)] * 2,
        compiler_params=_params("parallel"),
        name="rope_tables",
    )(pos, invf, sign)


def _rope_block(xc, cos, sin, first_half):
    swapped = jnp.where(first_half,
                        pltpu.roll(xc, LANES - SWA_HEAD_DIM // 2, 1),
                        pltpu.roll(xc, SWA_HEAD_DIM // 2, 1))
    return xc * cos + swapped * sin


def _first_half_mask(rows):
    lane = lax.broadcasted_iota(jnp.int32, (rows, LANES), 1)
    return (lane & (SWA_HEAD_DIM // 2)) == 0


def _mem_kv_kernel(mem_ref, g_ref, w_ref, k_ref, v_ref):
    mn = _rms_norm(mem_ref[...], g_ref[...]).astype(BF16)
    kv = _dot(mn, w_ref[...])
    k_ref[...] = kv[:, :MEM_WIDTH].astype(BF16)
    v_ref[...] = kv[:, MEM_WIDTH:].astype(BF16)


def _mem_kv(mem, mem_norm_g, w_mem_kv):
    b = mem.shape[0]
    out = pl.BlockSpec((None, None, MEM_TOKENS, MEM_WIDTH), lambda l, i: (l, i, 0, 0))
    return pl.pallas_call(
        _mem_kv_kernel,
        grid=(DEPTH, b),
        in_specs=[
            pl.BlockSpec((None, MEM_TOKENS, D_MODEL), lambda l, i: (i, 0, 0)),
            pl.BlockSpec((None, 1, D_MODEL), lambda l, i: (l, 0, 0)),
            pl.BlockSpec((None, D_MODEL, 2 * MEM_WIDTH), lambda l, i: (l, 0, 0)),
        ],
        out_specs=[out, out],
        out_shape=[jax.ShapeDtypeStruct((DEPTH, b, MEM_TOKENS, MEM_WIDTH), BF16)] * 2,
        compiler_params=_params("arbitrary", "arbitrary"),
        name="mem_kv",
    )(mem, mem_norm_g.reshape(DEPTH, 1, D_MODEL), w_mem_kv)


def _memory_probs(mq, mk_ref):
    mq = (mq * (MEM_HEAD_DIM ** -0.5 * LOG2E)).astype(BF16)
    probs, inv_sums = [], []
    for hh in range(MEM_HEADS):
        hs = slice(hh * MEM_HEAD_DIM, (hh + 1) * MEM_HEAD_DIM)
        s = _dot_nt(mq[:, hs], mk_ref[:, hs])
        p = jnp.exp2(s - jnp.max(s, axis=-1, keepdims=True))
        probs.append(p.astype(BF16))
        inv_sums.append(1.0 / jnp.sum(p, axis=-1, keepdims=True))
    return probs, inv_sums


def _memory_readout(probs, inv_sums, gate, mv_ref, out_ref, rows):
    for hh in range(MEM_HEADS):
        hs = slice(hh * MEM_HEAD_DIM, (hh + 1) * MEM_HEAD_DIM)
        o = _dot(probs[hh], mv_ref[:, hs]) * inv_sums[hh]
        out_ref[rows, hs] = (o * gate[:, hs]).astype(BF16)


def _a_in_kernel(layer, prev_transposed, h_ref, *refs):
    prev, (g_ref, w_ref, lbl_ref, mk_ref, mv_ref), outs = _split_refs(prev_transposed, refs, 5)
    qs_ref, lf_ref, kk_ref, v_ref, gm_ref, om_ref = outs

    logits = lbl_ref[...]
    e = jnp.exp(logits - jnp.max(logits, axis=0, keepdims=True))
    lb = jnp.sum(e[:layer + 1], axis=0, keepdims=True) / jnp.sum(e, axis=0, keepdims=True)
    mid = 0.5 * (1.0 + lb)
    half = 0.5 * (1.0 - lb)

    col_q, col_f, col_v = 0, HG_KEY_WIDTH, 2 * HG_KEY_WIDTH
    col_g = col_v + HG_VAL_WIDTH
    col_mq = col_g + HG_VAL_WIDTH
    col_mg = col_mq + MEM_WIDTH

    for sub, xn in enumerate(_normed_sub_tiles(prev, h_ref, g_ref)):
        rows = slice(sub * IN_SUB_TILE, (sub + 1) * IN_SUB_TILE)

        def proj(lo, width):
            return _dot(xn, w_ref[:, lo:lo + width])

        mq = proj(col_mq, MEM_WIDTH)
        mem_gate = _silu(proj(col_mg, MEM_WIDTH))
        qs_ref[rows, :] = _silu(proj(col_q, HG_KEY_WIDTH)).astype(BF16)
        probs, inv_sums = _memory_probs(mq, mk_ref)

        th = half * jnp.tanh(0.5 * proj(col_f, HG_KEY_WIDTH))
        lf_ref[rows, :] = jnp.log2(mid + th)
        kk_ref[rows, :] = (half - th).astype(BF16)

        _memory_readout(probs, inv_sums, mem_gate, mv_ref, om_ref, rows)
        v_ref[rows, :] = proj(col_v, HG_VAL_WIDTH).astype(BF16)
        gm_ref[rows, :] = _silu(proj(col_g, HG_VAL_WIDTH)).astype(BF16)


def _a_in_proj(layer, h, pending, g, w, lb_logits, mem_k, mem_v, tiles_per_seq):
    n = h.shape[0]
    row = lambda width: pl.BlockSpec((IN_TILE, width), lambda i: (i, 0))
    col = lambda width: pl.BlockSpec((width, IN_TILE), lambda i: (0, i))
    full = lambda a: pl.BlockSpec(a.shape, lambda i: (0,) * a.ndim)
    mem = pl.BlockSpec((None, None, MEM_TOKENS, MEM_WIDTH), lambda i: (layer, i // tiles_per_seq, 0, 0))
    widths = (HG_KEY_WIDTH, HG_KEY_WIDTH, HG_KEY_WIDTH, HG_VAL_WIDTH, HG_VAL_WIDTH, MEM_WIDTH)
    dtypes = (BF16, F32, BF16, BF16, BF16, BF16)
    if pending is not None:
        widths, dtypes = (D_MODEL,) + widths, (F32,) + dtypes
    pend_args, pend_specs = _pending_specs(pending, row, col, full)
    return pl.pallas_call(
        functools.partial(_a_in_kernel, layer, None if pending is None else pending[-1]),
        grid=(n // IN_TILE,),
        in_specs=[row(D_MODEL)] + pend_specs + [full(g), _layer_weight(w, layer), full(lb_logits), mem, mem],
        out_specs=[row(wd) for wd in widths],
        out_shape=[jax.ShapeDtypeStruct((n, wd), dt) for wd, dt in zip(widths, dtypes)],
        compiler_params=_params("parallel"),
        name=f"a_in_proj_{layer}",
    )(h, *pend_args, g, w, lb_logits, mem_k, mem_v)


_LEVELS = (32, 16, 8, 4, 2, 1)
_SAFE_BLOCK = 16
SAFE_BLOCK_DECAY = 100.0


def _coarse_level(cum, qs, kk, m):
    c, w = cum.shape
    early = m % (2 * SUBLANES) == 0
    dt = BF16 if early else F32
    zeros = jnp.zeros((m, w), dt)
    qd, kd = [], []
    for blk in range(c // (2 * m)):
        lo, mid, hi = blk * 2 * m, blk * 2 * m + m, (blk + 1) * 2 * m
        ref = cum[mid - 1:mid, :]
        kd += [(kk[lo:mid] * jnp.exp2(ref - cum[lo:mid])).astype(dt), zeros]
        qd += [zeros, (qs[mid:hi] * jnp.exp2(cum[mid:hi] - ref)).astype(dt)]
    return jnp.concatenate(qd, axis=0).astype(BF16), jnp.concatenate(kd, axis=0).astype(BF16)


def _fine_level(cum3, qs3, kk3, m, upper_pen, lower_pen):
    groups, _, w = cum3.shape
    parts = [jnp.broadcast_to(cum3[:, blk * 2 * m + m - 1:blk * 2 * m + m, :], (groups, 2 * m, w))
             for blk in range(SUBLANES // (2 * m))]
    x = cum3 - (parts[0] if len(parts) == 1 else jnp.concatenate(parts, axis=1))
    qd = qs3 * jnp.exp2(x + upper_pen)
    kd = kk3 * jnp.exp2(lower_pen - x)
    c = groups * SUBLANES
    return qd.reshape(c, w).astype(BF16), kd.reshape(c, w).astype(BF16)


def _block_level(cum, qs, kk):
    c, _ = cum.shape
    local = [cum[:_SAFE_BLOCK]] + [cum[b:b + _SAFE_BLOCK] - cum[b - 1:b] for b in range(_SAFE_BLOCK, c, _SAFE_BLOCK)]
    local = jnp.concatenate(local, axis=0)
    return (qs * jnp.exp2(local)).astype(BF16), (kk * jnp.exp2(-local)).astype(BF16)


def _hgrn_constants():
    c = HG_CHUNK
    row = lax.broadcasted_iota(jnp.int32, (c, c), 0)
    col = lax.broadcasted_iota(jnp.int32, (c, c), 1)
    sub = lax.broadcasted_iota(jnp.int32, (1, SUBLANES, HG_DK), 1)
    fine = [m for m in _LEVELS if m < SUBLANES]
    same_block = {m: (row // (2 * m)) == (col // (2 * m)) for m in _LEVELS if 2 * m < c}
    return dict(
        same_block={m: mask.astype(F32) for m, mask in same_block.items()},
        diag=(row == col).astype(F32),
        causal_block=(same_block[_SAFE_BLOCK // 2] & (col <= row)).astype(F32),
        upper_pen={m: jnp.where((sub & m) != 0, 0.0, MASK_VALUE) for m in fine},
        lower_pen={m: jnp.where((sub & m) != 0, MASK_VALUE, 0.0) for m in fine},
        even_row=jnp.where((sub & 1) != 0, 0.0, 1.0),
    )


def _hgrn_tile(single_reference_blocks, k, qs_ref, lf_ref, kk_ref, v_ref, gm_ref, gn, cum_s, o_ref, st_ref):
    c = HG_CHUNK
    heads = [slice(h * HG_DK, (h + 1) * HG_DK) for h in range(HG_HEADS)]
    shape3 = (c // SUBLANES, SUBLANES, HG_DK)
    for ci in range(HG_TILE // c):
        rows = slice(ci * c, (ci + 1) * c)
        scores_bf, q_ins, k_outs, lasts = [], [], [], []
        for hs in heads:
            cum = cum_s[rows, hs]
            qs = qs_ref[rows, hs].astype(F32)
            kk = kk_ref[rows, hs].astype(F32)

            pairs = []
            for m in _LEVELS:
                if single_reference_blocks and 2 * m <= _SAFE_BLOCK:
                    continue
                if m >= SUBLANES:
                    qd, kd = _coarse_level(cum, qs, kk, m)
                elif m == 1:
                    qs3, kk3 = qs.reshape(shape3), kk.reshape(shape3)
                    qd3 = qs3 * jnp.exp2(lf_ref[rows, hs].reshape(shape3) + k["upper_pen"][m])
                    qd = qd3.reshape(c, HG_DK).astype(BF16)
                    kd = (kk3 * k["even_row"]).reshape(c, HG_DK).astype(BF16)
                else:
                    qd, kd = _fine_level(cum.reshape(shape3), qs.reshape(shape3), kk.reshape(shape3), m,
                                         k["upper_pen"][m], k["lower_pen"][m])
                pairs.append((qd, kd, k["same_block"].get(m)))
            if single_reference_blocks:
                pairs.append(_block_level(cum, qs, kk) + (k["causal_block"],))
            else:
                pairs.append((qs.astype(BF16), kk.astype(BF16), k["diag"]))

            scores = None
            for qd, kd, mask in pairs:
                s = _dot_nt(qd, kd)
                if mask is not None:
                    s = s * mask
                scores = s if scores is None else scores + s

            last = cum[c - 1:c, :]
            scores_bf.append(scores.astype(BF16))
            q_ins.append((qs * jnp.exp2(cum)).astype(BF16))
            k_outs.append((kk * jnp.exp2(last - cum)).astype(BF16))
            lasts.append(last)

        for h, hs in enumerate(heads):
            o = _dot_nt(q_ins[h], st_ref[h].astype(BF16)) + _dot(scores_bf[h], v_ref[rows, hs])
            o = o * lax.rsqrt(jnp.mean(o * o, axis=-1, keepdims=True) + NORM_EPS)
            o_ref[rows, hs] = (o * gn[:, hs] * gm_ref[rows, hs].astype(F32)).astype(BF16)

        for h, hs in enumerate(heads):
            st_ref[h] = st_ref[h] * jnp.exp2(lasts[h]) + _dot_tn(v_ref[rows, hs], k_outs[h])


def _hgrn_kernel(qs_ref, lf_ref, kk_ref, v_ref, gm_ref, gn_ref, o_ref, cum_s, st_ref):
    c = HG_CHUNK

    @pl.when(pl.program_id(1) == 0)
    def _():
        st_ref[...] = jnp.zeros_like(st_ref)

    row = lax.broadcasted_iota(jnp.int32, (c, c), 0)
    col = lax.broadcasted_iota(jnp.int32, (c, c), 1)
    tri = (col <= row).astype(BF16)

    worst = None
    for ci in range(HG_TILE // c):
        rows = slice(ci * c, (ci + 1) * c)
        lf = lf_ref[rows, :]
        hi = lf.astype(BF16)
        lo = (lf - hi.astype(F32)).astype(BF16)
        cum = _dot(tri, hi) + _dot(tri, lo)
        cum_s[rows, :] = cum
        ends = [cum[b - 1:b, :] for b in range(_SAFE_BLOCK, c + 1, _SAFE_BLOCK)]
        for b, end in enumerate(ends):
            decay = end if b == 0 else end - ends[b - 1]
            worst = decay if worst is None else jnp.minimum(worst, decay)
    single_reference_ok = jnp.min(worst) >= -SAFE_BLOCK_DECAY

    consts = _hgrn_constants()
    args = (consts, qs_ref, lf_ref, kk_ref, v_ref, gm_ref, gn_ref[...], cum_s, o_ref, st_ref)

    @pl.when(single_reference_ok)
    def _():
        _hgrn_tile(True, *args)

    @pl.when(jnp.logical_not(single_reference_ok))
    def _():
        _hgrn_tile(False, *args)


def _hgrn(qs, lf, kk, v, gm, gn, batch):
    n = qs.shape[0]
    tiles = n // batch // HG_TILE
    row = pl.BlockSpec((HG_TILE, HG_KEY_WIDTH), lambda b, t: (b * tiles + t, 0))
    return pl.pallas_call(
        _hgrn_kernel,
        grid=(batch, tiles),
        in_specs=[row, row, row, row, row, pl.BlockSpec((1, HG_VAL_WIDTH), lambda b, t: (0, 0))],
        out_specs=row,
        out_shape=jax.ShapeDtypeStruct((n, HG_VAL_WIDTH), BF16),
        scratch_shapes=[pltpu.VMEM((HG_TILE, HG_KEY_WIDTH), F32),
                        pltpu.VMEM((HG_HEADS, HG_DV, HG_DK), F32)],
        compiler_params=_params("arbitrary", "arbitrary"),
        name="hgrn2",
    )(qs, lf, kk, v, gm, gn)


def _out_kernel(main_transposed, om_ref, ome_ref, w_ref, g_ref, h_ref, o_ref):
    main_dot = _dot_tn if main_transposed else _dot
    y = main_dot(om_ref[...], w_ref[:HG_VAL_WIDTH, :]) + _dot(ome_ref[...], w_ref[HG_VAL_WIDTH:, :])
    o_ref[...] = h_ref[...] + _rms_norm(y, g_ref[...])


def _out_proj(o_main, o_mem, w, layer, g, h, name, main_transposed=False):
    n = h.shape[0]
    row = lambda width: pl.BlockSpec((OUT_TILE, width), lambda i: (i, 0))
    col = lambda width: pl.BlockSpec((width, OUT_TILE), lambda i: (0, i))
    full = lambda a: pl.BlockSpec(a.shape, lambda i: (0,) * a.ndim)
    main = col(HG_VAL_WIDTH) if main_transposed else row(HG_VAL_WIDTH)
    return pl.pallas_call(
        functools.partial(_out_kernel, main_transposed),
        grid=(n // OUT_TILE,),
        in_specs=[main, row(MEM_WIDTH), _layer_weight(w, layer), full(g), row(D_MODEL)],
        out_specs=row(D_MODEL),
        out_shape=jax.ShapeDtypeStruct((n, D_MODEL), F32),
        compiler_params=_params("parallel"),
        name=name,
    )(o_main, o_mem, w, g, h)


def _shared_kv_kernel(x_ref, g_ref, w_ref, cos_ref, sin_ref, k_ref, v_ref):
    xn = _rms_norm(x_ref[...], g_ref[...]).astype(BF16)
    kv = _dot(xn, w_ref[...])
    cos = cos_ref[...]
    sin = sin_ref[...]
    first = _first_half_mask(IN_TILE)
    heads_per_block = LANES // SWA_HEAD_DIM
    for cb in range(SWA_KV_WIDTH // LANES):
        kr = _rope_block(kv[:, cb * LANES:(cb + 1) * LANES], cos, sin, first).astype(BF16)
        for j in range(heads_per_block):
            k_ref[cb * heads_per_block + j] = kr[:, j * SWA_HEAD_DIM:(j + 1) * SWA_HEAD_DIM]
    v_ref[...] = kv[:, SWA_KV_WIDTH:].T.astype(BF16)


def _shared_kv(h, g, w, cos, sin):
    n = h.shape[0]
    row = lambda width: pl.BlockSpec((IN_TILE, width), lambda i: (i, 0))
    full = lambda a: pl.BlockSpec(a.shape, lambda i: (0,) * a.ndim)
    return pl.pallas_call(
        _shared_kv_kernel,
        grid=(n // IN_TILE,),
        in_specs=[row(D_MODEL), full(g), full(w), row(LANES), row(LANES)],
        out_specs=[pl.BlockSpec((SWA_KV_HEADS, IN_TILE, SWA_HEAD_DIM), lambda i: (0, i, 0)),
                   pl.BlockSpec((SWA_KV_WIDTH, IN_TILE), lambda i: (0, i))],
        out_shape=[jax.ShapeDtypeStruct((SWA_KV_HEADS, n, SWA_HEAD_DIM), BF16),
                   jax.ShapeDtypeStruct((SWA_KV_WIDTH, n), BF16)],
        compiler_params=_params("parallel"),
        name="shared_kv",
    )(h, g, w, cos, sin)


def _b_in_kernel(prev_transposed, h_ref, *refs):
    prev, (g_ref, w_ref, cos_ref, sin_ref, mk_ref, mv_ref), outs = _split_refs(prev_transposed, refs, 6)
    qt_ref, gmt_ref, om_ref = outs
    first = _first_half_mask(IN_SUB_TILE)
    scale = SWA_HEAD_DIM ** -0.5 * LOG2E
    col_mq = 2 * SWA_WIDTH
    col_mg = col_mq + MEM_WIDTH

    for sub, xn in enumerate(_normed_sub_tiles(prev, h_ref, g_ref)):
        rows = slice(sub * IN_SUB_TILE, (sub + 1) * IN_SUB_TILE)

        def proj(lo, width):
            return _dot(xn, w_ref[:, lo:lo + width])

        mq = proj(col_mq, MEM_WIDTH)
        mem_gate = _silu(proj(col_mg, MEM_WIDTH))

        q = proj(0, SWA_WIDTH)
        probs, inv_sums = _memory_probs(mq, mk_ref)
        cos = cos_ref[rows, :]
        sin = sin_ref[rows, :]
        for cb in range(SWA_WIDTH // LANES):
            cs = slice(cb * LANES, (cb + 1) * LANES)
            qt_ref[cs, rows] = (_rope_block(q[:, cs], cos, sin, first) * scale).T.astype(BF16)

        gate = _silu(proj(SWA_WIDTH, SWA_WIDTH))
        _memory_readout(probs, inv_sums, mem_gate, mv_ref, om_ref, rows)
        for cb in range(SWA_WIDTH // LANES):
            cs = slice(cb * LANES, (cb + 1) * LANES)
            gmt_ref[cs, rows] = gate[:, cs].T.astype(BF16)


def _b_in_proj(layer, h, pending, g, w, cos, sin, mem_k, mem_v, tiles_per_seq):
    n = h.shape[0]
    row = lambda width: pl.BlockSpec((IN_TILE, width), lambda i: (i, 0))
    col = lambda width: pl.BlockSpec((width, IN_TILE), lambda i: (0, i))
    full = lambda a: pl.BlockSpec(a.shape, lambda i: (0,) * a.ndim)
    mem = pl.BlockSpec((None, None, MEM_TOKENS, MEM_WIDTH), lambda i: (layer, i // tiles_per_seq, 0, 0))
    out_specs = [col(SWA_WIDTH), col(SWA_WIDTH), row(MEM_WIDTH)]
    out_shape = [jax.ShapeDtypeStruct((SWA_WIDTH, n), BF16), jax.ShapeDtypeStruct((SWA_WIDTH, n), BF16),
                 jax.ShapeDtypeStruct((n, MEM_WIDTH), BF16)]
    if pending is not None:
        out_specs, out_shape = [row(D_MODEL)] + out_specs, [jax.ShapeDtypeStruct((n, D_MODEL), F32)] + out_shape
    pend_args, pend_specs = _pending_specs(pending, row, col, full)
    return pl.pallas_call(
        functools.partial(_b_in_kernel, None if pending is None else pending[-1]),
        grid=(n // IN_TILE,),
        in_specs=[row(D_MODEL)] + pend_specs
        + [full(g), _layer_weight(w, layer - N_A_LAYERS), row(LANES), row(LANES), mem, mem],
        out_specs=out_specs,
        out_shape=out_shape,
        compiler_params=_params("parallel"),
        name=f"b_in_proj_{layer}",
    )(h, *pend_args, g, w, cos, sin, mem_k, mem_v)


def _swa_kernel(sink_ref, qt_ref, kc_ref, kp_ref, vtc_ref, vtp_ref, gmt_ref, ot_ref):
    blk = WINDOW
    dh = SWA_HEAD_DIM
    ki = lax.broadcasted_iota(jnp.int32, (blk, blk), 0)
    qi = lax.broadcasted_iota(jnp.int32, (blk, blk), 1)
    from_prev = ki > qi
    from_prev_bf = from_prev.astype(BF16)
    prev_bias = jnp.where(pl.program_id(1) > 0, 0.0, MASK_VALUE)

    for sb in range(SWA_TILE // blk):
        cols = slice(sb * blk, (sb + 1) * blk)
        two = slice((sb - 1) * blk, (sb + 1) * blk)

        probs, inv_den = [], []
        for g in range(SWA_KV_HEADS):
            k2 = jnp.concatenate([kp_ref[g], kc_ref[g, cols, :]], axis=0) if sb == 0 else kc_ref[g, two, :]
            heads = range(g * SWA_GROUP, (g + 1) * SWA_GROUP)
            qt = jnp.concatenate([qt_ref[h * dh:(h + 1) * dh, cols] for h in heads], axis=1)
            st = _dot(k2, qt)
            ps = []
            for j, h in enumerate(heads):
                s_prev = st[:blk, j * blk:(j + 1) * blk]
                if sb == 0:
                    s_prev = s_prev + prev_bias
                s = jnp.where(from_prev, s_prev, st[blk:, j * blk:(j + 1) * blk])
                sink = sink_ref[h] * LOG2E
                m = jnp.maximum(jnp.max(s, axis=0, keepdims=True), sink)
                p = jnp.exp2(s - m)
                inv_den.append(1.0 / (jnp.sum(p, axis=0, keepdims=True) + jnp.exp2(sink - m)))
                p = p.astype(BF16)
                p_prev = p * from_prev_bf
                ps.append(jnp.concatenate([p_prev, p - p_prev], axis=0))
            probs.append(jnp.concatenate(ps, axis=1))

        for g in range(SWA_KV_HEADS):
            vrows = slice(g * dh, (g + 1) * dh)
            vt2 = (jnp.concatenate([vtp_ref[vrows, :], vtc_ref[vrows, cols]], axis=1) if sb == 0
                   else vtc_ref[vrows, two])
            ot = _dot(vt2, probs[g])
            for j, h in enumerate(range(g * SWA_GROUP, (g + 1) * SWA_GROUP)):
                rows = slice(h * dh, (h + 1) * dh)
                o = ot[:, j * blk:(j + 1) * blk] * inv_den[h]
                ot_ref[rows, cols] = (o * gmt_ref[rows, cols].astype(F32)).astype(BF16)


def _swa(sinks, qt, k, vt, gmt, batch):
    n = qt.shape[1]
    tiles = n // batch // SWA_TILE
    sub = SWA_TILE // WINDOW
    tile = lambda b, t: b * tiles + t
    prev = lambda b, t: jnp.maximum(tile(b, t) * sub - 1, 0)
    wide = pl.BlockSpec((SWA_WIDTH, SWA_TILE), lambda b, t: (0, tile(b, t)))
    return pl.pallas_call(
        _swa_kernel,
        grid=(batch, tiles),
        in_specs=[pl.BlockSpec(memory_space=pltpu.SMEM),
                  wide,
                  pl.BlockSpec((SWA_KV_HEADS, SWA_TILE, SWA_HEAD_DIM), lambda b, t: (0, tile(b, t), 0)),
                  pl.BlockSpec((SWA_KV_HEADS, WINDOW, SWA_HEAD_DIM), lambda b, t: (0, prev(b, t), 0)),
                  pl.BlockSpec((SWA_KV_WIDTH, SWA_TILE), lambda b, t: (0, tile(b, t))),
                  pl.BlockSpec((SWA_KV_WIDTH, WINDOW), lambda b, t: (0, prev(b, t))),
                  wide],
        out_specs=wide,
        out_shape=jax.ShapeDtypeStruct((SWA_WIDTH, n), BF16),
        compiler_params=_params("parallel", "arbitrary"),
        name="swa",
    )(sinks, qt, k, k, vt, vt, gmt)


def kernel(x, mem, positions, pre_norm_g, post_norm_g, mem_norm_g, w_mem_kv, a_w_in, a_lb_logits, a_out_norm_g, a_w_out, kv_norm_g, w_kv_shared, b_w_in, b_sinks, b_w_out):
    batch, seq, _ = x.shape
    n = batch * seq
    assert seq % max(IN_TILE, OUT_TILE, HG_TILE, SWA_TILE) == 0
    tiles_per_seq = seq // IN_TILE

    h = x.reshape(n, D_MODEL)
    mem_k, mem_v = _mem_kv(mem, mem_norm_g, w_mem_kv.astype(BF16))
    cos, sin = _rope_tables(positions)
    a_w_in = a_w_in.astype(BF16)
    a_w_out = a_w_out.astype(BF16)
    b_w_in = b_w_in.astype(BF16)
    b_w_out = b_w_out.astype(BF16)
    k_sh = v_sh = None
    pending = None

    for layer in range(DEPTH):
        pre_g = pre_norm_g[layer].reshape(1, D_MODEL)
        post_g = post_norm_g[layer].reshape(1, D_MODEL)
        if layer < N_A_LAYERS:
            outs = _a_in_proj(layer, h, pending, pre_g, a_w_in, a_lb_logits, mem_k, mem_v, tiles_per_seq)
            if pending is not None:
                h, *outs = outs
            qs, lf, kk, v, gm, o_mem = outs
            o_main = _hgrn(qs, lf, kk, v, gm, a_out_norm_g[layer].reshape(1, HG_VAL_WIDTH), batch)
            pending = (o_main, o_mem, a_w_out, layer, post_g, False)
        else:
            j = layer - N_A_LAYERS
            outs = _b_in_proj(layer, h, pending, pre_g, b_w_in, cos, sin, mem_k, mem_v, tiles_per_seq)
            if pending is not None:
                h, *outs = outs
            qt, gmt, o_mem = outs
            if k_sh is None:
                k_sh, v_sh = _shared_kv(h, kv_norm_g.reshape(1, D_MODEL), w_kv_shared.astype(BF16), cos, sin)
            o_main_t = _swa(b_sinks[j], qt, k_sh, v_sh, gmt, batch)
            pending = (o_main_t, o_mem, b_w_out, j, post_g, True)
    o_main, o_mem, w_out, w_layer, post_g, transposed = pending
    h = _out_proj(o_main, o_mem, w_out, w_layer, post_g, h, "out_proj_last", main_transposed=transposed)
    return h.reshape(batch, seq, D_MODEL)
```

```python
import functools

import jax
import jax.numpy as jnp
from jax import lax
from jax.experimental import pallas as pl
from jax.experimental.pallas import tpu as pltpu

D_MODEL = 1024
DEPTH = 4
N_A_LAYERS = DEPTH // 2

HG_HEADS = 8
HG_DK = 128
HG_DV = D_MODEL // HG_HEADS
HG_KEY_WIDTH = HG_HEADS * HG_DK
HG_VAL_WIDTH = HG_HEADS * HG_DV
HG_CHUNK = 64

SWA_Q_HEADS = 16
SWA_KV_HEADS = 4
SWA_GROUP = SWA_Q_HEADS // SWA_KV_HEADS
SWA_HEAD_DIM = 64
SWA_WIDTH = SWA_Q_HEADS * SWA_HEAD_DIM
SWA_KV_WIDTH = SWA_KV_HEADS * SWA_HEAD_DIM
WINDOW = 128

MEM_TOKENS = 256
MEM_HEADS = 4
MEM_HEAD_DIM = 128
MEM_WIDTH = MEM_HEADS * MEM_HEAD_DIM

ROPE_THETA = 10000.0
NORM_EPS = 1e-6

A_IN_WIDTH = 2 * HG_KEY_WIDTH + 2 * HG_VAL_WIDTH + 2 * MEM_WIDTH
B_IN_WIDTH = 2 * SWA_WIDTH + 2 * MEM_WIDTH
OUT_WIDTH = HG_VAL_WIDTH + MEM_WIDTH

LANES = 128
SUBLANES = 8
MASK_VALUE = -1e30
LOG2E = 1.4426950408889634

IN_TILE = 512
IN_SUB_TILE = 256
OUT_TILE = 512
HG_TILE = 256
SWA_TILE = 512
VMEM_LIMIT = 56 * 1024 * 1024

F32 = jnp.float32
BF16 = jnp.bfloat16

_NT = (((1,), (1,)), ((), ()))
_TN = (((0,), (0,)), ((), ()))


def _dot(a, b):
    return jnp.dot(a, b, preferred_element_type=F32)


def _dot_nt(a, b):
    return lax.dot_general(a, b, _NT, preferred_element_type=F32)


def _dot_tn(a, b):
    return lax.dot_general(a, b, _TN, preferred_element_type=F32)


def _rms_norm(x, g):
    ms = jnp.mean(x * x, axis=-1, keepdims=True)
    return x * lax.rsqrt(ms + NORM_EPS) * g


def _silu(x):
    h = 0.5 * x
    return h + h * jnp.tanh(h)


def _params(*semantics):
    return pltpu.CompilerParams(dimension_semantics=semantics, vmem_limit_bytes=VMEM_LIMIT)


def _layer_weight(w, layer):
    return pl.BlockSpec((None,) + w.shape[1:], lambda *_: (layer, 0, 0), pipeline_mode=pl.Buffered(1))


def _layer_input(prev, h_ref, rows):
    if prev is None:
        return h_ref[rows, :]
    om_ref, ome_ref, wout_ref, pg_ref, hnew_ref, main_transposed = prev
    if main_transposed:
        y = _dot_tn(om_ref[:, rows], wout_ref[:HG_VAL_WIDTH, :])
    else:
        y = _dot(om_ref[rows, :], wout_ref[:HG_VAL_WIDTH, :])
    y = y + _dot(ome_ref[rows, :], wout_ref[HG_VAL_WIDTH:, :])
    h_new = h_ref[rows, :] + _rms_norm(y, pg_ref[...])
    hnew_ref[rows, :] = h_new
    return h_new


def _normed_sub_tiles(prev, h_ref, g_ref):
    return [_rms_norm(_layer_input(prev, h_ref, slice(s * IN_SUB_TILE, (s + 1) * IN_SUB_TILE)),
                      g_ref[...]).astype(BF16) for s in range(IN_TILE // IN_SUB_TILE)]


def _split_refs(prev_transposed, refs, n_in):
    if prev_transposed is None:
        return None, refs[:n_in], refs[n_in:]
    prev_in, ins, (hnew_ref, *outs) = refs[:4], refs[4:4 + n_in], refs[4 + n_in:]
    return (*prev_in, hnew_ref, prev_transposed), ins, outs


def _pending_specs(pending, row, col, full):
    if pending is None:
        return [], []
    o_main, o_mem, w_out, w_layer, post_g, transposed = pending
    main = col(HG_VAL_WIDTH) if transposed else row(HG_VAL_WIDTH)
    return [o_main, o_mem, w_out, post_g], [main, row(MEM_WIDTH), _layer_weight(w_out, w_layer), full(post_g)]


def _rope_table_kernel(pos_ref, invf_ref, sign_ref, cos_ref, sin_ref):
    ang = pos_ref[...].astype(F32) * invf_ref[...]
    cos_ref[...] = jnp.cos(ang)
    sin_ref[...] = jnp.sin(ang) * sign_ref[...]


def _rope_tables(positions):
    n = positions.size
    half = SWA_HEAD_DIM // 2
    inv_freq = ROPE_THETA ** (-jnp.arange(0, SWA_HEAD_DIM, 2, dtype=F32) / SWA_HEAD_DIM)
    invf = jnp.tile(inv_freq, LANES // half).reshape(1, LANES)
    sign = jnp.tile(jnp.concatenate([-jnp.ones((half,), F32), jnp.ones((half,), F32)]),
                    LANES // SWA_HEAD_DIM).reshape(1, LANES)
    pos = jnp.broadcast_to(positions.reshape(n, 1), (n, LANES))
    tile = 2048
    row = pl.BlockSpec((tile, LANES), lambda i: (i, 0))
    vec = pl.BlockSpec((1, LANES), lambda i: (0, 0))
    return pl.pallas_call(
        _rope_table_kernel,
        grid=(n // tile,),
        in_specs=[row, vec, vec],
        out_specs=[row, row],
        out_shape=[jax.ShapeDtypeStruct((n, LANES), F32)] * 2,
        compiler_params=_params("parallel"),
        name="rope_tables",
    )(pos, invf, sign)


def _rope_block(xc, cos, sin, first_half):
    swapped = jnp.where(first_half,
                        pltpu.roll(xc, LANES - SWA_HEAD_DIM // 2, 1),
                        pltpu.roll(xc, SWA_HEAD_DIM // 2, 1))
    return xc * cos + swapped * sin


def _first_half_mask(rows):
    lane = lax.broadcasted_iota(jnp.int32, (rows, LANES), 1)
    return (lane & (SWA_HEAD_DIM // 2)) == 0


def _mem_kv_kernel(mem_ref, g_ref, w_ref, k_ref, v_ref):
    mn = _rms_norm(mem_ref[...], g_ref[...]).astype(BF16)
    kv = _dot(mn, w_ref[...])
    k_ref[...] = kv[:, :MEM_WIDTH].astype(BF16)
    v_ref[...] = kv[:, MEM_WIDTH:].astype(BF16)


def _mem_kv(mem, mem_norm_g, w_mem_kv):
    b = mem.shape[0]
    out = pl.BlockSpec((None, None, MEM_TOKENS, MEM_WIDTH), lambda l, i: (l, i, 0, 0))
    return pl.pallas_call(
        _mem_kv_kernel,
        grid=(DEPTH, b),
        in_specs=[
            pl.BlockSpec((None, MEM_TOKENS, D_MODEL), lambda l, i: (i, 0, 0)),
            pl.BlockSpec((None, 1, D_MODEL), lambda l, i: (l, 0, 0)),
            pl.BlockSpec((None, D_MODEL, 2 * MEM_WIDTH), lambda l, i: (l, 0, 0)),
        ],
        out_specs=[out, out],
        out_shape=[jax.ShapeDtypeStruct((DEPTH, b, MEM_TOKENS, MEM_WIDTH), BF16)] * 2,
        compiler_params=_params("arbitrary", "arbitrary"),
        name="mem_kv",
    )(mem, mem_norm_g.reshape(DEPTH, 1, D_MODEL), w_mem_kv)


def _memory_probs(mq, mk_ref):
    mq = (mq * (MEM_HEAD_DIM ** -0.5 * LOG2E)).astype(BF16)
    probs, inv_sums = [], []
    for hh in range(MEM_HEADS):
        hs = slice(hh * MEM_HEAD_DIM, (hh + 1) * MEM_HEAD_DIM)
        s = _dot_nt(mq[:, hs], mk_ref[:, hs])
        p = jnp.exp2(s - jnp.max(s, axis=-1, keepdims=True))
        probs.append(p.astype(BF16))
        inv_sums.append(1.0 / jnp.sum(p, axis=-1, keepdims=True))
    return probs, inv_sums


def _memory_readout(probs, inv_sums, gate, mv_ref, out_ref, rows):
    for hh in range(MEM_HEADS):
        hs = slice(hh * MEM_HEAD_DIM, (hh + 1) * MEM_HEAD_DIM)
        o = _dot(probs[hh], mv_ref[:, hs]) * inv_sums[hh]
        out_ref[rows, hs] = (o * gate[:, hs]).astype(BF16)


def _a_in_kernel(layer, prev_transposed, h_ref, *refs):
    prev, (g_ref, w_ref, lbl_ref, mk_ref, mv_ref), outs = _split_refs(prev_transposed, refs, 5)
    qs_ref, lf_ref, kk_ref, v_ref, gm_ref, om_ref = outs

    logits = lbl_ref[...]
    e = jnp.exp(logits - jnp.max(logits, axis=0, keepdims=True))
    lb = jnp.sum(e[:layer + 1], axis=0, keepdims=True) / jnp.sum(e, axis=0, keepdims=True)
    mid = 0.5 * (1.0 + lb)
    half = 0.5 * (1.0 - lb)

    col_q, col_f, col_v = 0, HG_KEY_WIDTH, 2 * HG_KEY_WIDTH
    col_g = col_v + HG_VAL_WIDTH
    col_mq = col_g + HG_VAL_WIDTH
    col_mg = col_mq + MEM_WIDTH

    for sub, xn in enumerate(_normed_sub_tiles(prev, h_ref, g_ref)):
        rows = slice(sub * IN_SUB_TILE, (sub + 1) * IN_SUB_TILE)

        def proj(lo, width):
            return _dot(xn, w_ref[:, lo:lo + width])

        mq = proj(col_mq, MEM_WIDTH)
        mem_gate = _silu(proj(col_mg, MEM_WIDTH))
        qs_ref[rows, :] = _silu(proj(col_q, HG_KEY_WIDTH)).astype(BF16)
        probs, inv_sums = _memory_probs(mq, mk_ref)

        th = half * jnp.tanh(0.5 * proj(col_f, HG_KEY_WIDTH))
        lf_ref[rows, :] = jnp.log2(mid + th)
        kk_ref[rows, :] = (half - th).astype(BF16)

        _memory_readout(probs, inv_sums, mem_gate, mv_ref, om_ref, rows)
        v_ref[rows, :] = proj(col_v, HG_VAL_WIDTH).astype(BF16)
        gm_ref[rows, :] = _silu(proj(col_g, HG_VAL_WIDTH)).astype(BF16)


def _a_in_proj(layer, h, pending, g, w, lb_logits, mem_k, mem_v, tiles_per_seq):
    n = h.shape[0]
    row = lambda width: pl.BlockSpec((IN_TILE, width), lambda i: (i, 0))
    col = lambda width: pl.BlockSpec((width, IN_TILE), lambda i: (0, i))
    full = lambda a: pl.BlockSpec(a.shape, lambda i: (0,) * a.ndim)
    mem = pl.BlockSpec((None, None, MEM_TOKENS, MEM_WIDTH), lambda i: (layer, i // tiles_per_seq, 0, 0))
    widths = (HG_KEY_WIDTH, HG_KEY_WIDTH, HG_KEY_WIDTH, HG_VAL_WIDTH, HG_VAL_WIDTH, MEM_WIDTH)
    dtypes = (BF16, F32, BF16, BF16, BF16, BF16)
    if pending is not None:
        widths, dtypes = (D_MODEL,) + widths, (F32,) + dtypes
    pend_args, pend_specs = _pending_specs(pending, row, col, full)
    return pl.pallas_call(
        functools.partial(_a_in_kernel, layer, None if pending is None else pending[-1]),
        grid=(n // IN_TILE,),
        in_specs=[row(D_MODEL)] + pend_specs + [full(g), _layer_weight(w, layer), full(lb_logits), mem, mem],
        out_specs=[row(wd) for wd in widths],
        out_shape=[jax.ShapeDtypeStruct((n, wd), dt) for wd, dt in zip(widths, dtypes)],
        compiler_params=_params("parallel"),
        name=f"a_in_proj_{layer}",
    )(h, *pend_args, g, w, lb_logits, mem_k, mem_v)


_LEVELS = (32, 16, 8, 4, 2, 1)
_SAFE_BLOCKS = (HG_CHUNK, 16)
SAFE_BLOCK_DECAY = 100.0


def _coarse_level(cum, qs, kk, m):
    c, w = cum.shape
    early = m % (2 * SUBLANES) == 0
    dt = BF16 if early else F32
    zeros = jnp.zeros((m, w), dt)
    qd, kd = [], []
    for blk in range(c // (2 * m)):
        lo, mid, hi = blk * 2 * m, blk * 2 * m + m, (blk + 1) * 2 * m
        ref = cum[mid - 1:mid, :]
        kd += [(kk[lo:mid] * jnp.exp2(ref - cum[lo:mid])).astype(dt), zeros]
        qd += [zeros, (qs[mid:hi] * jnp.exp2(cum[mid:hi] - ref)).astype(dt)]
    return jnp.concatenate(qd, axis=0).astype(BF16), jnp.concatenate(kd, axis=0).astype(BF16)


def _fine_level(cum3, qs3, kk3, m, upper_pen, lower_pen):
    groups, _, w = cum3.shape
    parts = [jnp.broadcast_to(cum3[:, blk * 2 * m + m - 1:blk * 2 * m + m, :], (groups, 2 * m, w))
             for blk in range(SUBLANES // (2 * m))]
    x = cum3 - (parts[0] if len(parts) == 1 else jnp.concatenate(parts, axis=1))
    qd = qs3 * jnp.exp2(x + upper_pen)
    kd = kk3 * jnp.exp2(lower_pen - x)
    c = groups * SUBLANES
    return qd.reshape(c, w).astype(BF16), kd.reshape(c, w).astype(BF16)


def _block_level(cum, qs, kk, block):
    c, _ = cum.shape
    local = [cum[:block]] + [cum[b:b + block] - cum[b - 1:b] for b in range(block, c, block)]
    local = local[0] if len(local) == 1 else jnp.concatenate(local, axis=0)
    return (qs * jnp.exp2(local)).astype(BF16), (kk * jnp.exp2(-local)).astype(BF16)


def _hgrn_constants():
    c = HG_CHUNK
    row = lax.broadcasted_iota(jnp.int32, (c, c), 0)
    col = lax.broadcasted_iota(jnp.int32, (c, c), 1)
    sub = lax.broadcasted_iota(jnp.int32, (1, SUBLANES, HG_DK), 1)
    fine = [m for m in _LEVELS if m < SUBLANES]
    same_block = {m: (row // (2 * m)) == (col // (2 * m)) for m in _LEVELS if 2 * m < c}
    return dict(
        same_block={m: mask.astype(F32) for m, mask in same_block.items()},
        diag=(row == col).astype(F32),
        causal_block={b: ((col <= row) if b == c else same_block[b // 2] & (col <= row)).astype(F32)
                      for b in _SAFE_BLOCKS},
        upper_pen={m: jnp.where((sub & m) != 0, 0.0, MASK_VALUE) for m in fine},
        lower_pen={m: jnp.where((sub & m) != 0, MASK_VALUE, 0.0) for m in fine},
        even_row=jnp.where((sub & 1) != 0, 0.0, 1.0),
    )


def _hgrn_tile(block, k, qs_ref, lf_ref, kk_ref, v_ref, gm_ref, gn, cum_s, o_ref, st_ref):
    c = HG_CHUNK
    heads = [slice(h * HG_DK, (h + 1) * HG_DK) for h in range(HG_HEADS)]
    shape3 = (c // SUBLANES, SUBLANES, HG_DK)
    n_chunks = HG_TILE // c

    def scores_pass(ci):
        rows = slice(ci * c, (ci + 1) * c)
        scores_bf, q_ins, k_outs, lasts = [], [], [], []
        for hs in heads:
            cum = cum_s[rows, hs]
            qs = qs_ref[rows, hs].astype(F32)
            kk = kk_ref[rows, hs].astype(F32)

            pairs = []
            for m in _LEVELS:
                if block is not None and 2 * m <= block:
                    continue
                if m >= SUBLANES:
                    qd, kd = _coarse_level(cum, qs, kk, m)
                elif m == 1:
                    qs3, kk3 = qs.reshape(shape3), kk.reshape(shape3)
                    qd3 = qs3 * jnp.exp2(lf_ref[rows, hs].reshape(shape3) + k["upper_pen"][m])
                    qd = qd3.reshape(c, HG_DK).astype(BF16)
                    kd = (kk3 * k["even_row"]).reshape(c, HG_DK).astype(BF16)
                else:
                    qd, kd = _fine_level(cum.reshape(shape3), qs.reshape(shape3), kk.reshape(shape3), m,
                                         k["upper_pen"][m], k["lower_pen"][m])
                pairs.append((qd, kd, k["same_block"].get(m)))
            if block is not None:
                pairs.append(_block_level(cum, qs, kk, block) + (k["causal_block"][block],))
            else:
                pairs.append((qs.astype(BF16), kk.astype(BF16), k["diag"]))

            scores = None
            for qd, kd, mask in pairs:
                s = _dot_nt(qd, kd)
                if mask is not None:
                    s = s * mask
                scores = s if scores is None else scores + s

            last = cum[c - 1:c, :]
            scores_bf.append(scores.astype(BF16))
            q_ins.append((qs * jnp.exp2(cum)).astype(BF16))
            k_outs.append((kk * jnp.exp2(last - cum)).astype(BF16))
            lasts.append(last)
        return scores_bf, q_ins, k_outs, lasts

    def recurrence_pass(ci, scores_bf, q_ins, k_outs, lasts):
        rows = slice(ci * c, (ci + 1) * c)
        for h, hs in enumerate(heads):
            o = _dot_nt(q_ins[h], st_ref[h].astype(BF16)) + _dot(scores_bf[h], v_ref[rows, hs])
            o = o * lax.rsqrt(jnp.mean(o * o, axis=-1, keepdims=True) + NORM_EPS)
            o_ref[rows, hs] = (o * gn[:, hs] * gm_ref[rows, hs].astype(F32)).astype(BF16)

        for h, hs in enumerate(heads):
            st_ref[h] = st_ref[h] * jnp.exp2(lasts[h]) + _dot_tn(v_ref[rows, hs], k_outs[h])

    ahead = scores_pass(0)
    for ci in range(n_chunks):
        current, ahead = ahead, (scores_pass(ci + 1) if ci + 1 < n_chunks else None)
        recurrence_pass(ci, *current)


def _hgrn_kernel(qs_ref, lf_ref, kk_ref, v_ref, gm_ref, gn_ref, o_ref, cum_s, st_ref):
    c = HG_CHUNK

    @pl.when(pl.program_id(1) == 0)
    def _():
        st_ref[...] = jnp.zeros_like(st_ref)

    row = lax.broadcasted_iota(jnp.int32, (c, c), 0)
    col = lax.broadcasted_iota(jnp.int32, (c, c), 1)
    tri = (col <= row).astype(BF16)

    worst = {b: None for b in _SAFE_BLOCKS}
    for ci in range(HG_TILE // c):
        rows = slice(ci * c, (ci + 1) * c)
        lf = lf_ref[rows, :]
        hi = lf.astype(BF16)
        lo = (lf - hi.astype(F32)).astype(BF16)
        cum = _dot(tri, hi) + _dot(tri, lo)
        cum_s[rows, :] = cum
        for block in _SAFE_BLOCKS:
            ends = [cum[b - 1:b, :] for b in range(block, c + 1, block)]
            for b, end in enumerate(ends):
                decay = end if b == 0 else end - ends[b - 1]
                worst[block] = decay if worst[block] is None else jnp.minimum(worst[block], decay)
    safe = [jnp.min(worst[b]) >= -SAFE_BLOCK_DECAY for b in _SAFE_BLOCKS]

    consts = _hgrn_constants()
    args = (consts, qs_ref, lf_ref, kk_ref, v_ref, gm_ref, gn_ref[...], cum_s, o_ref, st_ref)

    not_yet = None
    for block, ok in zip(_SAFE_BLOCKS + (None,), safe + [True]):
        take = ok if not_yet is None else jnp.logical_and(not_yet, ok)

        @pl.when(take)
        def _(block=block):
            _hgrn_tile(block, *args)

        if block is not None:
            not_yet = jnp.logical_not(ok) if not_yet is None else jnp.logical_and(not_yet, jnp.logical_not(ok))


def _hgrn(qs, lf, kk, v, gm, gn, batch):
    n = qs.shape[0]
    tiles = n // batch // HG_TILE
    row = pl.BlockSpec((HG_TILE, HG_KEY_WIDTH), lambda b, t: (b * tiles + t, 0))
    return pl.pallas_call(
        _hgrn_kernel,
        grid=(batch, tiles),
        in_specs=[row, row, row, row, row, pl.BlockSpec((1, HG_VAL_WIDTH), lambda b, t: (0, 0))],
        out_specs=row,
        out_shape=jax.ShapeDtypeStruct((n, HG_VAL_WIDTH), BF16),
        scratch_shapes=[pltpu.VMEM((HG_TILE, HG_KEY_WIDTH), F32),
                        pltpu.VMEM((HG_HEADS, HG_DV, HG_DK), F32)],
        compiler_params=_params("arbitrary", "arbitrary"),
        name="hgrn2",
    )(qs, lf, kk, v, gm, gn)


def _out_kernel(main_transposed, om_ref, ome_ref, w_ref, g_ref, h_ref, o_ref):
    main_dot = _dot_tn if main_transposed else _dot
    y = main_dot(om_ref[...], w_ref[:HG_VAL_WIDTH, :]) + _dot(ome_ref[...], w_ref[HG_VAL_WIDTH:, :])
    o_ref[...] = h_ref[...] + _rms_norm(y, g_ref[...])


def _out_proj(o_main, o_mem, w, layer, g, h, name, main_transposed=False):
    n = h.shape[0]
    row = lambda width: pl.BlockSpec((OUT_TILE, width), lambda i: (i, 0))
    col = lambda width: pl.BlockSpec((width, OUT_TILE), lambda i: (0, i))
    full = lambda a: pl.BlockSpec(a.shape, lambda i: (0,) * a.ndim)
    main = col(HG_VAL_WIDTH) if main_transposed else row(HG_VAL_WIDTH)
    return pl.pallas_call(
        functools.partial(_out_kernel, main_transposed),
        grid=(n // OUT_TILE,),
        in_specs=[main, row(MEM_WIDTH), _layer_weight(w, layer), full(g), row(D_MODEL)],
        out_specs=row(D_MODEL),
        out_shape=jax.ShapeDtypeStruct((n, D_MODEL), F32),
        compiler_params=_params("parallel"),
        name=name,
    )(o_main, o_mem, w, g, h)


def _shared_kv_kernel(x_ref, g_ref, w_ref, cos_ref, sin_ref, k_ref, v_ref):
    xn = _rms_norm(x_ref[...], g_ref[...]).astype(BF16)
    kv = _dot(xn, w_ref[...])
    cos = cos_ref[...]
    sin = sin_ref[...]
    first = _first_half_mask(IN_TILE)
    heads_per_block = LANES // SWA_HEAD_DIM
    for cb in range(SWA_KV_WIDTH // LANES):
        kr = _rope_block(kv[:, cb * LANES:(cb + 1) * LANES], cos, sin, first).astype(BF16)
        for j in range(heads_per_block):
            k_ref[cb * heads_per_block + j] = kr[:, j * SWA_HEAD_DIM:(j + 1) * SWA_HEAD_DIM]
    v_ref[...] = kv[:, SWA_KV_WIDTH:].T.astype(BF16)


def _shared_kv(h, g, w, cos, sin):
    n = h.shape[0]
    row = lambda width: pl.BlockSpec((IN_TILE, width), lambda i: (i, 0))
    full = lambda a: pl.BlockSpec(a.shape, lambda i: (0,) * a.ndim)
    return pl.pallas_call(
        _shared_kv_kernel,
        grid=(n // IN_TILE,),
        in_specs=[row(D_MODEL), full(g), full(w), row(LANES), row(LANES)],
        out_specs=[pl.BlockSpec((SWA_KV_HEADS, IN_TILE, SWA_HEAD_DIM), lambda i: (0, i, 0)),
                   pl.BlockSpec((SWA_KV_WIDTH, IN_TILE), lambda i: (0, i))],
        out_shape=[jax.ShapeDtypeStruct((SWA_KV_HEADS, n, SWA_HEAD_DIM), BF16),
                   jax.ShapeDtypeStruct((SWA_KV_WIDTH, n), BF16)],
        compiler_params=_params("parallel"),
        name="shared_kv",
    )(h, g, w, cos, sin)


def _b_in_kernel(prev_transposed, h_ref, *refs):
    prev, (g_ref, w_ref, cos_ref, sin_ref, mk_ref, mv_ref), outs = _split_refs(prev_transposed, refs, 6)
    qt_ref, gmt_ref, om_ref = outs
    first = _first_half_mask(IN_SUB_TILE)
    scale = SWA_HEAD_DIM ** -0.5 * LOG2E
    col_mq = 2 * SWA_WIDTH
    col_mg = col_mq + MEM_WIDTH

    for sub, xn in enumerate(_normed_sub_tiles(prev, h_ref, g_ref)):
        rows = slice(sub * IN_SUB_TILE, (sub + 1) * IN_SUB_TILE)

        def proj(lo, width):
            return _dot(xn, w_ref[:, lo:lo + width])

        mq = proj(col_mq, MEM_WIDTH)
        mem_gate = _silu(proj(col_mg, MEM_WIDTH))

        q = proj(0, SWA_WIDTH)
        probs, inv_sums = _memory_probs(mq, mk_ref)
        cos = cos_ref[rows, :]
        sin = sin_ref[rows, :]
        for cb in range(SWA_WIDTH // LANES):
            cs = slice(cb * LANES, (cb + 1) * LANES)
            qt_ref[cs, rows] = (_rope_block(q[:, cs], cos, sin, first) * scale).T.astype(BF16)

        gate = _silu(proj(SWA_WIDTH, SWA_WIDTH))
        _memory_readout(probs, inv_sums, mem_gate, mv_ref, om_ref, rows)
        for cb in range(SWA_WIDTH // LANES):
            cs = slice(cb * LANES, (cb + 1) * LANES)
            gmt_ref[cs, rows] = gate[:, cs].T.astype(BF16)


def _b_in_proj(layer, h, pending, g, w, cos, sin, mem_k, mem_v, tiles_per_seq):
    n = h.shape[0]
    row = lambda width: pl.BlockSpec((IN_TILE, width), lambda i: (i, 0))
    col = lambda width: pl.BlockSpec((width, IN_TILE), lambda i: (0, i))
    full = lambda a: pl.BlockSpec(a.shape, lambda i: (0,) * a.ndim)
    mem = pl.BlockSpec((None, None, MEM_TOKENS, MEM_WIDTH), lambda i: (layer, i // tiles_per_seq, 0, 0))
    out_specs = [col(SWA_WIDTH), col(SWA_WIDTH), row(MEM_WIDTH)]
    out_shape = [jax.ShapeDtypeStruct((SWA_WIDTH, n), BF16), jax.ShapeDtypeStruct((SWA_WIDTH, n), BF16),
                 jax.ShapeDtypeStruct((n, MEM_WIDTH), BF16)]
    if pending is not None:
        out_specs, out_shape = [row(D_MODEL)] + out_specs, [jax.ShapeDtypeStruct((n, D_MODEL), F32)] + out_shape
    pend_args, pend_specs = _pending_specs(pending, row, col, full)
    return pl.pallas_call(
        functools.partial(_b_in_kernel, None if pending is None else pending[-1]),
        grid=(n // IN_TILE,),
        in_specs=[row(D_MODEL)] + pend_specs
        + [full(g), _layer_weight(w, layer - N_A_LAYERS), row(LANES), row(LANES), mem, mem],
        out_specs=out_specs,
        out_shape=out_shape,
        compiler_params=_params("parallel"),
        name=f"b_in_proj_{layer}",
    )(h, *pend_args, g, w, cos, sin, mem_k, mem_v)


def _swa_kernel(sink_ref, qt_ref, kc_ref, kp_ref, vtc_ref, vtp_ref, gmt_ref, ot_ref):
    blk = WINDOW
    dh = SWA_HEAD_DIM
    ki = lax.broadcasted_iota(jnp.int32, (blk, blk), 0)
    qi = lax.broadcasted_iota(jnp.int32, (blk, blk), 1)
    from_prev = ki > qi
    from_prev_bf = from_prev.astype(BF16)
    prev_bias = jnp.where(pl.program_id(1) > 0, 0.0, MASK_VALUE)

    for sb in range(SWA_TILE // blk):
        cols = slice(sb * blk, (sb + 1) * blk)
        two = slice((sb - 1) * blk, (sb + 1) * blk)

        probs, inv_den = [], []
        for g in range(SWA_KV_HEADS):
            k2 = jnp.concatenate([kp_ref[g], kc_ref[g, cols, :]], axis=0) if sb == 0 else kc_ref[g, two, :]
            heads = range(g * SWA_GROUP, (g + 1) * SWA_GROUP)
            qt = jnp.concatenate([qt_ref[h * dh:(h + 1) * dh, cols] for h in heads], axis=1)
            st = _dot(k2, qt)
            ps = []
            for j, h in enumerate(heads):
                s_prev = st[:blk, j * blk:(j + 1) * blk]
                if sb == 0:
                    s_prev = s_prev + prev_bias
                s = jnp.where(from_prev, s_prev, st[blk:, j * blk:(j + 1) * blk])
                sink = sink_ref[h] * LOG2E
                m = jnp.maximum(jnp.max(s, axis=0, keepdims=True), sink)
                p = jnp.exp2(s - m)
                inv_den.append(1.0 / (jnp.sum(p, axis=0, keepdims=True) + jnp.exp2(sink - m)))
                p = p.astype(BF16)
                p_prev = p * from_prev_bf
                ps.append(jnp.concatenate([p_prev, p - p_prev], axis=0))
            probs.append(jnp.concatenate(ps, axis=1))

        for g in range(SWA_KV_HEADS):
            vrows = slice(g * dh, (g + 1) * dh)
            vt2 = (jnp.concatenate([vtp_ref[vrows, :], vtc_ref[vrows, cols]], axis=1) if sb == 0
                   else vtc_ref[vrows, two])
            ot = _dot(vt2, probs[g])
            for j, h in enumerate(range(g * SWA_GROUP, (g + 1) * SWA_GROUP)):
                rows = slice(h * dh, (h + 1) * dh)
                o = ot[:, j * blk:(j + 1) * blk] * inv_den[h]
                ot_ref[rows, cols] = (o * gmt_ref[rows, cols].astype(F32)).astype(BF16)


def _swa(sinks, qt, k, vt, gmt, batch):
    n = qt.shape[1]
    tiles = n // batch // SWA_TILE
    sub = SWA_TILE // WINDOW
    tile = lambda b, t: b * tiles + t
    prev = lambda b, t: jnp.maximum(tile(b, t) * sub - 1, 0)
    wide = pl.BlockSpec((SWA_WIDTH, SWA_TILE), lambda b, t: (0, tile(b, t)))
    return pl.pallas_call(
        _swa_kernel,
        grid=(batch, tiles),
        in_specs=[pl.BlockSpec(memory_space=pltpu.SMEM),
                  wide,
                  pl.BlockSpec((SWA_KV_HEADS, SWA_TILE, SWA_HEAD_DIM), lambda b, t: (0, tile(b, t), 0)),
                  pl.BlockSpec((SWA_KV_HEADS, WINDOW, SWA_HEAD_DIM), lambda b, t: (0, prev(b, t), 0)),
                  pl.BlockSpec((SWA_KV_WIDTH, SWA_TILE), lambda b, t: (0, tile(b, t))),
                  pl.BlockSpec((SWA_KV_WIDTH, WINDOW), lambda b, t: (0, prev(b, t))),
                  wide],
        out_specs=wide,
        out_shape=jax.ShapeDtypeStruct((SWA_WIDTH, n), BF16),
        compiler_params=_params("parallel", "arbitrary"),
        name="swa",
    )(sinks, qt, k, k, vt, vt, gmt)


def kernel(x, mem, positions, pre_norm_g, post_norm_g, mem_norm_g, w_mem_kv, a_w_in, a_lb_logits, a_out_norm_g, a_w_out, kv_norm_g, w_kv_shared, b_w_in, b_sinks, b_w_out):
    batch, seq, _ = x.shape
    n = batch * seq
    assert seq % max(IN_TILE, OUT_TILE, HG_TILE, SWA_TILE) == 0
    tiles_per_seq = seq // IN_TILE

    h = x.reshape(n, D_MODEL)
    mem_k, mem_v = _mem_kv(mem, mem_norm_g, w_mem_kv.astype(BF16))
    cos, sin = _rope_tables(positions)
    a_w_in = a_w_in.astype(BF16)
    a_w_out = a_w_out.astype(BF16)
    b_w_in = b_w_in.astype(BF16)
    b_w_out = b_w_out.astype(BF16)
    k_sh = v_sh = None
    pending = None

    for layer in range(DEPTH):
        pre_g = pre_norm_g[layer].reshape(1, D_MODEL)
        post_g = post_norm_g[layer].reshape(1, D_MODEL)
        if layer < N_A_LAYERS:
            outs = _a_in_proj(layer, h, pending, pre_g, a_w_in, a_lb_logits, mem_k, mem_v, tiles_per_seq)
            if pending is not None:
                h, *outs = outs
            qs, lf, kk, v, gm, o_mem = outs
            o_main = _hgrn(qs, lf, kk, v, gm, a_out_norm_g[layer].reshape(1, HG_VAL_WIDTH), batch)
            pending = (o_main, o_mem, a_w_out, layer, post_g, False)
        else:
            j = layer - N_A_LAYERS
            outs = _b_in_proj(layer, h, pending, pre_g, b_w_in, cos, sin, mem_k, mem_v, tiles_per_seq)
            if pending is not None:
                h, *outs = outs
            qt, gmt, o_mem = outs
            if k_sh is None:
                k_sh, v_sh = _shared_kv(h, kv_norm_g.reshape(1, D_MODEL), w_kv_shared.astype(BF16), cos, sin)
            o_main_t = _swa(b_sinks[j], qt, k_sh, v_sh, gmt, batch)
            pending = (o_main_t, o_mem, b_w_out, j, post_g, True)
    o_main, o_mem, w_out, w_layer, post_g, transposed = pending
    h = _out_proj(o_main, o_mem, w_out, w_layer, post_g, h, "out_proj_last", main_transposed=transposed)
    return h.reshape(batch, seq, D_MODEL)
```

```python
import functools

import jax
import jax.numpy as jnp
from jax import lax
from jax.experimental import pallas as pl
from jax.experimental.pallas import tpu as pltpu

D_MODEL = 1024
DEPTH = 4
N_A_LAYERS = DEPTH // 2

HG_HEADS = 8
HG_DK = 128
HG_DV = D_MODEL // HG_HEADS
HG_KEY_WIDTH = HG_HEADS * HG_DK
HG_VAL_WIDTH = HG_HEADS * HG_DV
HG_CHUNK = 64

SWA_Q_HEADS = 16
SWA_KV_HEADS = 4
SWA_GROUP = SWA_Q_HEADS // SWA_KV_HEADS
SWA_HEAD_DIM = 64
SWA_WIDTH = SWA_Q_HEADS * SWA_HEAD_DIM
SWA_KV_WIDTH = SWA_KV_HEADS * SWA_HEAD_DIM
WINDOW = 128

MEM_TOKENS = 256
MEM_HEADS = 4
MEM_HEAD_DIM = 128
MEM_WIDTH = MEM_HEADS * MEM_HEAD_DIM

ROPE_THETA = 10000.0
NORM_EPS = 1e-6

A_IN_WIDTH = 2 * HG_KEY_WIDTH + 2 * HG_VAL_WIDTH + 2 * MEM_WIDTH
B_IN_WIDTH = 2 * SWA_WIDTH + 2 * MEM_WIDTH
OUT_WIDTH = HG_VAL_WIDTH + MEM_WIDTH

LANES = 128
SUBLANES = 8
MASK_VALUE = -1e30
LOG2E = 1.4426950408889634

IN_TILE = 512
IN_SUB_TILE = 256
OUT_TILE = 512
HG_TILE = 512
SWA_TILE = 512
VMEM_LIMIT = 56 * 1024 * 1024

F32 = jnp.float32
BF16 = jnp.bfloat16

_NT = (((1,), (1,)), ((), ()))
_TN = (((0,), (0,)), ((), ()))


def _dot(a, b):
    return jnp.dot(a, b, preferred_element_type=F32)


def _dot_nt(a, b):
    return lax.dot_general(a, b, _NT, preferred_element_type=F32)


def _dot_tn(a, b):
    return lax.dot_general(a, b, _TN, preferred_element_type=F32)


def _rms_norm(x, g):
    ms = jnp.mean(x * x, axis=-1, keepdims=True)
    return x * lax.rsqrt(ms + NORM_EPS) * g


def _silu(x):
    h = 0.5 * x
    return h + h * jnp.tanh(h)


def _params(*semantics):
    return pltpu.CompilerParams(dimension_semantics=semantics, vmem_limit_bytes=VMEM_LIMIT)


def _layer_weight(w, layer):
    return pl.BlockSpec((None,) + w.shape[1:], lambda *_: (layer, 0, 0), pipeline_mode=pl.Buffered(1))


def _layer_input(prev, h_ref, rows):
    if prev is None:
        return h_ref[rows, :]
    om_ref, ome_ref, wout_ref, pg_ref, hnew_ref, main_transposed = prev
    if main_transposed:
        y = _dot_tn(om_ref[:, rows], wout_ref[:HG_VAL_WIDTH, :])
    else:
        y = _dot(om_ref[rows, :], wout_ref[:HG_VAL_WIDTH, :])
    y = y + _dot(ome_ref[rows, :], wout_ref[HG_VAL_WIDTH:, :])
    h_new = h_ref[rows, :] + _rms_norm(y, pg_ref[...])
    hnew_ref[rows, :] = h_new
    return h_new


def _normed_sub_tiles(prev, h_ref, g_ref):
    return [_rms_norm(_layer_input(prev, h_ref, slice(s * IN_SUB_TILE, (s + 1) * IN_SUB_TILE)),
                      g_ref[...]).astype(BF16) for s in range(IN_TILE // IN_SUB_TILE)]


def _split_refs(prev_transposed, refs, n_in):
    if prev_transposed is None:
        return None, refs[:n_in], refs[n_in:]
    prev_in, ins, (hnew_ref, *outs) = refs[:4], refs[4:4 + n_in], refs[4 + n_in:]
    return (*prev_in, hnew_ref, prev_transposed), ins, outs


def _pending_specs(pending, row, col, full):
    if pending is None:
        return [], []
    o_main, o_mem, w_out, w_layer, post_g, transposed = pending
    main = col(HG_VAL_WIDTH) if transposed else row(HG_VAL_WIDTH)
    return [o_main, o_mem, w_out, post_g], [main, row(MEM_WIDTH), _layer_weight(w_out, w_layer), full(post_g)]


def _rope_table_kernel(pos_ref, invf_ref, sign_ref, cos_ref, sin_ref):
    ang = pos_ref[...].astype(F32) * invf_ref[...]
    cos_ref[...] = jnp.cos(ang)
    sin_ref[...] = jnp.sin(ang) * sign_ref[...]


def _rope_tables(positions):
    n = positions.size
    half = SWA_HEAD_DIM // 2
    inv_freq = ROPE_THETA ** (-jnp.arange(0, SWA_HEAD_DIM, 2, dtype=F32) / SWA_HEAD_DIM)
    invf = jnp.tile(inv_freq, LANES // half).reshape(1, LANES)
    sign = jnp.tile(jnp.concatenate([-jnp.ones((half,), F32), jnp.ones((half,), F32)]),
                    LANES // SWA_HEAD_DIM).reshape(1, LANES)
    pos = jnp.broadcast_to(positions.reshape(n, 1), (n, LANES))
    tile = 2048
    row = pl.BlockSpec((tile, LANES), lambda i: (i, 0))
    vec = pl.BlockSpec((1, LANES), lambda i: (0, 0))
    return pl.pallas_call(
        _rope_table_kernel,
        grid=(n // tile,),
        in_specs=[row, vec, vec],
        out_specs=[row, row],
        out_shape=[jax.ShapeDtypeStruct((n, LANES), F32)] * 2,
        compiler_params=_params("parallel"),
        name="rope_tables",
    )(pos, invf, sign)


def _rope_block(xc, cos, sin, first_half):
    swapped = jnp.where(first_half,
                        pltpu.roll(xc, LANES - SWA_HEAD_DIM // 2, 1),
                        pltpu.roll(xc, SWA_HEAD_DIM // 2, 1))
    return xc * cos + swapped * sin


def _first_half_mask(rows):
    lane = lax.broadcasted_iota(jnp.int32, (rows, LANES), 1)
    return (lane & (SWA_HEAD_DIM // 2)) == 0


def _mem_kv_kernel(mem_ref, g_ref, w_ref, k_ref, v_ref):
    mn = _rms_norm(mem_ref[...], g_ref[...]).astype(BF16)
    kv = _dot(mn, w_ref[...])
    k_ref[...] = kv[:, :MEM_WIDTH].astype(BF16)
    v_ref[...] = kv[:, MEM_WIDTH:].astype(BF16)


def _mem_kv(mem, mem_norm_g, w_mem_kv):
    b = mem.shape[0]
    out = pl.BlockSpec((None, None, MEM_TOKENS, MEM_WIDTH), lambda l, i: (l, i, 0, 0))
    return pl.pallas_call(
        _mem_kv_kernel,
        grid=(DEPTH, b),
        in_specs=[
            pl.BlockSpec((None, MEM_TOKENS, D_MODEL), lambda l, i: (i, 0, 0)),
            pl.BlockSpec((None, 1, D_MODEL), lambda l, i: (l, 0, 0)),
            pl.BlockSpec((None, D_MODEL, 2 * MEM_WIDTH), lambda l, i: (l, 0, 0)),
        ],
        out_specs=[out, out],
        out_shape=[jax.ShapeDtypeStruct((DEPTH, b, MEM_TOKENS, MEM_WIDTH), BF16)] * 2,
        compiler_params=_params("arbitrary", "arbitrary"),
        name="mem_kv",
    )(mem, mem_norm_g.reshape(DEPTH, 1, D_MODEL), w_mem_kv)


def _memory_probs(mq, mk_ref):
    mq = (mq * (MEM_HEAD_DIM ** -0.5 * LOG2E)).astype(BF16)
    probs, inv_sums = [], []
    for hh in range(MEM_HEADS):
        hs = slice(hh * MEM_HEAD_DIM, (hh + 1) * MEM_HEAD_DIM)
        s = _dot_nt(mq[:, hs], mk_ref[:, hs])
        p = jnp.exp2(s - jnp.max(s, axis=-1, keepdims=True))
        probs.append(p.astype(BF16))
        inv_sums.append(1.0 / jnp.sum(p, axis=-1, keepdims=True))
    return probs, inv_sums


def _memory_readout(probs, inv_sums, gate, mv_ref, out_ref, rows):
    for hh in range(MEM_HEADS):
        hs = slice(hh * MEM_HEAD_DIM, (hh + 1) * MEM_HEAD_DIM)
        o = _dot(probs[hh], mv_ref[:, hs]) * inv_sums[hh]
        out_ref[rows, hs] = (o * gate[:, hs]).astype(BF16)


def _a_in_kernel(layer, prev_transposed, h_ref, *refs):
    prev, (g_ref, w_ref, lbl_ref, mk_ref, mv_ref), outs = _split_refs(prev_transposed, refs, 5)
    qs_ref, lf_ref, kk_ref, v_ref, gm_ref, om_ref = outs

    logits = lbl_ref[...]
    e = jnp.exp(logits - jnp.max(logits, axis=0, keepdims=True))
    lb = jnp.sum(e[:layer + 1], axis=0, keepdims=True) / jnp.sum(e, axis=0, keepdims=True)
    mid = 0.5 * (1.0 + lb)
    half = 0.5 * (1.0 - lb)

    col_q, col_f, col_v = 0, HG_KEY_WIDTH, 2 * HG_KEY_WIDTH
    col_g = col_v + HG_VAL_WIDTH
    col_mq = col_g + HG_VAL_WIDTH
    col_mg = col_mq + MEM_WIDTH

    for sub, xn in enumerate(_normed_sub_tiles(prev, h_ref, g_ref)):
        rows = slice(sub * IN_SUB_TILE, (sub + 1) * IN_SUB_TILE)

        def proj(lo, width):
            return _dot(xn, w_ref[:, lo:lo + width])

        mq = proj(col_mq, MEM_WIDTH)
        mem_gate = _silu(proj(col_mg, MEM_WIDTH))
        qs_ref[rows, :] = _silu(proj(col_q, HG_KEY_WIDTH)).astype(BF16)
        probs, inv_sums = _memory_probs(mq, mk_ref)

        th = half * jnp.tanh(0.5 * proj(col_f, HG_KEY_WIDTH))
        lf_ref[rows, :] = jnp.log2(mid + th)
        kk_ref[rows, :] = (half - th).astype(BF16)

        _memory_readout(probs, inv_sums, mem_gate, mv_ref, om_ref, rows)
        v_ref[rows, :] = proj(col_v, HG_VAL_WIDTH).astype(BF16)
        gm_ref[rows, :] = _silu(proj(col_g, HG_VAL_WIDTH)).astype(BF16)


def _a_in_proj(layer, h, pending, g, w, lb_logits, mem_k, mem_v, tiles_per_seq):
    n = h.shape[0]
    row = lambda width: pl.BlockSpec((IN_TILE, width), lambda i: (i, 0))
    col = lambda width: pl.BlockSpec((width, IN_TILE), lambda i: (0, i))
    full = lambda a: pl.BlockSpec(a.shape, lambda i: (0,) * a.ndim)
    mem = pl.BlockSpec((None, None, MEM_TOKENS, MEM_WIDTH), lambda i: (layer, i // tiles_per_seq, 0, 0))
    widths = (HG_KEY_WIDTH, HG_KEY_WIDTH, HG_KEY_WIDTH, HG_VAL_WIDTH, HG_VAL_WIDTH, MEM_WIDTH)
    dtypes = (BF16, F32, BF16, BF16, BF16, BF16)
    if pending is not None:
        widths, dtypes = (D_MODEL,) + widths, (F32,) + dtypes
    pend_args, pend_specs = _pending_specs(pending, row, col, full)
    return pl.pallas_call(
        functools.partial(_a_in_kernel, layer, None if pending is None else pending[-1]),
        grid=(n // IN_TILE,),
        in_specs=[row(D_MODEL)] + pend_specs + [full(g), _layer_weight(w, layer), full(lb_logits), mem, mem],
        out_specs=[row(wd) for wd in widths],
        out_shape=[jax.ShapeDtypeStruct((n, wd), dt) for wd, dt in zip(widths, dtypes)],
        compiler_params=_params("parallel"),
        name=f"a_in_proj_{layer}",
    )(h, *pend_args, g, w, lb_logits, mem_k, mem_v)


_LEVELS = (32, 16, 8, 4, 2, 1)
_SAFE_BLOCKS = (HG_CHUNK, 16)
SAFE_BLOCK_DECAY = 100.0


def _coarse_level(cum, qs, kk, m):
    c, w = cum.shape
    early = m % (2 * SUBLANES) == 0
    dt = BF16 if early else F32
    zeros = jnp.zeros((m, w), dt)
    qd, kd = [], []
    for blk in range(c // (2 * m)):
        lo, mid, hi = blk * 2 * m, blk * 2 * m + m, (blk + 1) * 2 * m
        ref = cum[mid - 1:mid, :]
        kd += [(kk[lo:mid] * jnp.exp2(ref - cum[lo:mid])).astype(dt), zeros]
        qd += [zeros, (qs[mid:hi] * jnp.exp2(cum[mid:hi] - ref)).astype(dt)]
    return jnp.concatenate(qd, axis=0).astype(BF16), jnp.concatenate(kd, axis=0).astype(BF16)


def _fine_level(cum3, qs3, kk3, m, upper_pen, lower_pen):
    groups, _, w = cum3.shape
    parts = [jnp.broadcast_to(cum3[:, blk * 2 * m + m - 1:blk * 2 * m + m, :], (groups, 2 * m, w))
             for blk in range(SUBLANES // (2 * m))]
    x = cum3 - (parts[0] if len(parts) == 1 else jnp.concatenate(parts, axis=1))
    qd = qs3 * jnp.exp2(x + upper_pen)
    kd = kk3 * jnp.exp2(lower_pen - x)
    c = groups * SUBLANES
    return qd.reshape(c, w).astype(BF16), kd.reshape(c, w).astype(BF16)


def _block_level(cum, qs, kk, block):
    c, _ = cum.shape
    local = [cum[:block]] + [cum[b:b + block] - cum[b - 1:b] for b in range(block, c, block)]
    local = local[0] if len(local) == 1 else jnp.concatenate(local, axis=0)
    return (qs * jnp.exp2(local)).astype(BF16), (kk * jnp.exp2(-local)).astype(BF16)


def _hgrn_constants():
    c = HG_CHUNK
    row = lax.broadcasted_iota(jnp.int32, (c, c), 0)
    col = lax.broadcasted_iota(jnp.int32, (c, c), 1)
    sub = lax.broadcasted_iota(jnp.int32, (1, SUBLANES, HG_DK), 1)
    fine = [m for m in _LEVELS if m < SUBLANES]
    same_block = {m: (row // (2 * m)) == (col // (2 * m)) for m in _LEVELS if 2 * m < c}
    return dict(
        same_block={m: mask.astype(F32) for m, mask in same_block.items()},
        diag=(row == col).astype(F32),
        causal_block={b: ((col <= row) if b == c else same_block[b // 2] & (col <= row)).astype(F32)
                      for b in _SAFE_BLOCKS},
        upper_pen={m: jnp.where((sub & m) != 0, 0.0, MASK_VALUE) for m in fine},
        lower_pen={m: jnp.where((sub & m) != 0, MASK_VALUE, 0.0) for m in fine},
        even_row=jnp.where((sub & 1) != 0, 0.0, 1.0),
    )


def _hgrn_tile(block, k, qs_ref, lf_ref, kk_ref, v_ref, gm_ref, gn, cum_s, o_ref, st_ref):
    c = HG_CHUNK
    heads = [slice(h * HG_DK, (h + 1) * HG_DK) for h in range(HG_HEADS)]
    shape3 = (c // SUBLANES, SUBLANES, HG_DK)
    n_chunks = HG_TILE // c

    def scores_pass(ci):
        rows = slice(ci * c, (ci + 1) * c)
        scores_bf, q_ins, k_outs, lasts = [], [], [], []
        for hs in heads:
            cum = cum_s[rows, hs]
            qs = qs_ref[rows, hs].astype(F32)
            kk = kk_ref[rows, hs].astype(F32)

            pairs = []
            for m in _LEVELS:
                if block is not None and 2 * m <= block:
                    continue
                if m >= SUBLANES:
                    qd, kd = _coarse_level(cum, qs, kk, m)
                elif m == 1:
                    qs3, kk3 = qs.reshape(shape3), kk.reshape(shape3)
                    qd3 = qs3 * jnp.exp2(lf_ref[rows, hs].reshape(shape3) + k["upper_pen"][m])
                    qd = qd3.reshape(c, HG_DK).astype(BF16)
                    kd = (kk3 * k["even_row"]).reshape(c, HG_DK).astype(BF16)
                else:
                    qd, kd = _fine_level(cum.reshape(shape3), qs.reshape(shape3), kk.reshape(shape3), m,
                                         k["upper_pen"][m], k["lower_pen"][m])
                pairs.append((qd, kd, k["same_block"].get(m)))
            if block is not None:
                pairs.append(_block_level(cum, qs, kk, block) + (k["causal_block"][block],))
            else:
                pairs.append((qs.astype(BF16), kk.astype(BF16), k["diag"]))

            scores = None
            for qd, kd, mask in pairs:
                s = _dot_nt(qd, kd)
                if mask is not None:
                    s = s * mask
                scores = s if scores is None else scores + s

            last = cum[c - 1:c, :]
            scores_bf.append(scores.astype(BF16))
            q_ins.append((qs * jnp.exp2(cum)).astype(BF16))
            k_outs.append((kk * jnp.exp2(last - cum)).astype(BF16))
            lasts.append(last)
        return scores_bf, q_ins, k_outs, lasts

    def recurrence_pass(ci, scores_bf, q_ins, k_outs, lasts):
        rows = slice(ci * c, (ci + 1) * c)
        for h, hs in enumerate(heads):
            o = _dot_nt(q_ins[h], st_ref[h].astype(BF16)) + _dot(scores_bf[h], v_ref[rows, hs])
            o = o * lax.rsqrt(jnp.mean(o * o, axis=-1, keepdims=True) + NORM_EPS)
            o_ref[rows, hs] = (o * gn[:, hs] * gm_ref[rows, hs].astype(F32)).astype(BF16)

        for h, hs in enumerate(heads):
            st_ref[h] = st_ref[h] * jnp.exp2(lasts[h]) + _dot_tn(v_ref[rows, hs], k_outs[h])

    ahead = scores_pass(0)
    for ci in range(n_chunks):
        current, ahead = ahead, (scores_pass(ci + 1) if ci + 1 < n_chunks else None)
        recurrence_pass(ci, *current)


def _hgrn_kernel(qs_ref, lf_ref, kk_ref, v_ref, gm_ref, gn_ref, o_ref, cum_s, st_ref):
    c = HG_CHUNK

    @pl.when(pl.program_id(1) == 0)
    def _():
        st_ref[...] = jnp.zeros_like(st_ref)

    row = lax.broadcasted_iota(jnp.int32, (c, c), 0)
    col = lax.broadcasted_iota(jnp.int32, (c, c), 1)
    tri = (col <= row).astype(BF16)

    worst = {b: None for b in _SAFE_BLOCKS}
    for ci in range(HG_TILE // c):
        rows = slice(ci * c, (ci + 1) * c)
        lf = lf_ref[rows, :]
        hi = lf.astype(BF16)
        lo = (lf - hi.astype(F32)).astype(BF16)
        cum = _dot(tri, hi) + _dot(tri, lo)
        cum_s[rows, :] = cum
        for block in _SAFE_BLOCKS:
            ends = [cum[b - 1:b, :] for b in range(block, c + 1, block)]
            for b, end in enumerate(ends):
                decay = end if b == 0 else end - ends[b - 1]
                worst[block] = decay if worst[block] is None else jnp.minimum(worst[block], decay)
    safe = [jnp.min(worst[b]) >= -SAFE_BLOCK_DECAY for b in _SAFE_BLOCKS]

    consts = _hgrn_constants()
    args = (consts, qs_ref, lf_ref, kk_ref, v_ref, gm_ref, gn_ref[...], cum_s, o_ref, st_ref)

    not_yet = None
    for block, ok in zip(_SAFE_BLOCKS + (None,), safe + [True]):
        take = ok if not_yet is None else jnp.logical_and(not_yet, ok)

        @pl.when(take)
        def _(block=block):
            _hgrn_tile(block, *args)

        if block is not None:
            not_yet = jnp.logical_not(ok) if not_yet is None else jnp.logical_and(not_yet, jnp.logical_not(ok))


def _hgrn(qs, lf, kk, v, gm, gn, batch):
    n = qs.shape[0]
    tiles = n // batch // HG_TILE
    row = pl.BlockSpec((HG_TILE, HG_KEY_WIDTH), lambda b, t: (b * tiles + t, 0))
    return pl.pallas_call(
        _hgrn_kernel,
        grid=(batch, tiles),
        in_specs=[row, row, row, row, row, pl.BlockSpec((1, HG_VAL_WIDTH), lambda b, t: (0, 0))],
        out_specs=row,
        out_shape=jax.ShapeDtypeStruct((n, HG_VAL_WIDTH), BF16),
        scratch_shapes=[pltpu.VMEM((HG_TILE, HG_KEY_WIDTH), F32),
                        pltpu.VMEM((HG_HEADS, HG_DV, HG_DK), F32)],
        compiler_params=_params("arbitrary", "arbitrary"),
        name="hgrn2",
    )(qs, lf, kk, v, gm, gn)


def _out_kernel(main_transposed, om_ref, ome_ref, w_ref, g_ref, h_ref, o_ref):
    main_dot = _dot_tn if main_transposed else _dot
    y = main_dot(om_ref[...], w_ref[:HG_VAL_WIDTH, :]) + _dot(ome_ref[...], w_ref[HG_VAL_WIDTH:, :])
    o_ref[...] = h_ref[...] + _rms_norm(y, g_ref[...])


def _out_proj(o_main, o_mem, w, layer, g, h, name, main_transposed=False):
    n = h.shape[0]
    row = lambda width: pl.BlockSpec((OUT_TILE, width), lambda i: (i, 0))
    col = lambda width: pl.BlockSpec((width, OUT_TILE), lambda i: (0, i))
    full = lambda a: pl.BlockSpec(a.shape, lambda i: (0,) * a.ndim)
    main = col(HG_VAL_WIDTH) if main_transposed else row(HG_VAL_WIDTH)
    return pl.pallas_call(
        functools.partial(_out_kernel, main_transposed),
        grid=(n // OUT_TILE,),
        in_specs=[main, row(MEM_WIDTH), _layer_weight(w, layer), full(g), row(D_MODEL)],
        out_specs=row(D_MODEL),
        out_shape=jax.ShapeDtypeStruct((n, D_MODEL), F32),
        compiler_params=_params("parallel"),
        name=name,
    )(o_main, o_mem, w, g, h)


def _shared_kv_kernel(x_ref, g_ref, w_ref, cos_ref, sin_ref, k_ref, v_ref):
    xn = _rms_norm(x_ref[...], g_ref[...]).astype(BF16)
    kv = _dot(xn, w_ref[...])
    cos = cos_ref[...]
    sin = sin_ref[...]
    first = _first_half_mask(IN_TILE)
    heads_per_block = LANES // SWA_HEAD_DIM
    for cb in range(SWA_KV_WIDTH // LANES):
        kr = _rope_block(kv[:, cb * LANES:(cb + 1) * LANES], cos, sin, first).astype(BF16)
        for j in range(heads_per_block):
            k_ref[cb * heads_per_block + j] = kr[:, j * SWA_HEAD_DIM:(j + 1) * SWA_HEAD_DIM]
    v_ref[...] = kv[:, SWA_KV_WIDTH:].T.astype(BF16)


def _shared_kv(h, g, w, cos, sin):
    n = h.shape[0]
    row = lambda width: pl.BlockSpec((IN_TILE, width), lambda i: (i, 0))
    full = lambda a: pl.BlockSpec(a.shape, lambda i: (0,) * a.ndim)
    return pl.pallas_call(
        _shared_kv_kernel,
        grid=(n // IN_TILE,),
        in_specs=[row(D_MODEL), full(g), full(w), row(LANES), row(LANES)],
        out_specs=[pl.BlockSpec((SWA_KV_HEADS, IN_TILE, SWA_HEAD_DIM), lambda i: (0, i, 0)),
                   pl.BlockSpec((SWA_KV_WIDTH, IN_TILE), lambda i: (0, i))],
        out_shape=[jax.ShapeDtypeStruct((SWA_KV_HEADS, n, SWA_HEAD_DIM), BF16),
                   jax.ShapeDtypeStruct((SWA_KV_WIDTH, n), BF16)],
        compiler_params=_params("parallel"),
        name="shared_kv",
    )(h, g, w, cos, sin)


def _b_in_kernel(prev_transposed, h_ref, *refs):
    prev, (g_ref, w_ref, cos_ref, sin_ref, mk_ref, mv_ref), outs = _split_refs(prev_transposed, refs, 6)
    qt_ref, gmt_ref, om_ref = outs
    first = _first_half_mask(IN_SUB_TILE)
    scale = SWA_HEAD_DIM ** -0.5 * LOG2E
    col_mq = 2 * SWA_WIDTH
    col_mg = col_mq + MEM_WIDTH

    for sub, xn in enumerate(_normed_sub_tiles(prev, h_ref, g_ref)):
        rows = slice(sub * IN_SUB_TILE, (sub + 1) * IN_SUB_TILE)

        def proj(lo, width):
            return _dot(xn, w_ref[:, lo:lo + width])

        mq = proj(col_mq, MEM_WIDTH)
        mem_gate = _silu(proj(col_mg, MEM_WIDTH))

        q = proj(0, SWA_WIDTH)
        probs, inv_sums = _memory_probs(mq, mk_ref)
        cos = cos_ref[rows, :]
        sin = sin_ref[rows, :]
        for cb in range(SWA_WIDTH // LANES):
            cs = slice(cb * LANES, (cb + 1) * LANES)
            qt_ref[cs, rows] = (_rope_block(q[:, cs], cos, sin, first) * scale).T.astype(BF16)

        gate = _silu(proj(SWA_WIDTH, SWA_WIDTH))
        _memory_readout(probs, inv_sums, mem_gate, mv_ref, om_ref, rows)
        for cb in range(SWA_WIDTH // LANES):
            cs = slice(cb * LANES, (cb + 1) * LANES)
            gmt_ref[cs, rows] = gate[:, cs].T.astype(BF16)


def _b_in_proj(layer, h, pending, g, w, cos, sin, mem_k, mem_v, tiles_per_seq):
    n = h.shape[0]
    row = lambda width: pl.BlockSpec((IN_TILE, width), lambda i: (i, 0))
    col = lambda width: pl.BlockSpec((width, IN_TILE), lambda i: (0, i))
    full = lambda a: pl.BlockSpec(a.shape, lambda i: (0,) * a.ndim)
    mem = pl.BlockSpec((None, None, MEM_TOKENS, MEM_WIDTH), lambda i: (layer, i // tiles_per_seq, 0, 0))
    out_specs = [col(SWA_WIDTH), col(SWA_WIDTH), row(MEM_WIDTH)]
    out_shape = [jax.ShapeDtypeStruct((SWA_WIDTH, n), BF16), jax.ShapeDtypeStruct((SWA_WIDTH, n), BF16),
                 jax.ShapeDtypeStruct((n, MEM_WIDTH), BF16)]
    if pending is not None:
        out_specs, out_shape = [row(D_MODEL)] + out_specs, [jax.ShapeDtypeStruct((n, D_MODEL), F32)] + out_shape
    pend_args, pend_specs = _pending_specs(pending, row, col, full)
    return pl.pallas_call(
        functools.partial(_b_in_kernel, None if pending is None else pending[-1]),
        grid=(n // IN_TILE,),
        in_specs=[row(D_MODEL)] + pend_specs
        + [full(g), _layer_weight(w, layer - N_A_LAYERS), row(LANES), row(LANES), mem, mem],
        out_specs=out_specs,
        out_shape=out_shape,
        compiler_params=_params("parallel"),
        name=f"b_in_proj_{layer}",
    )(h, *pend_args, g, w, cos, sin, mem_k, mem_v)


def _swa_kernel(sink_ref, qt_ref, kc_ref, kp_ref, vtc_ref, vtp_ref, gmt_ref, ot_ref):
    blk = WINDOW
    dh = SWA_HEAD_DIM
    ki = lax.broadcasted_iota(jnp.int32, (blk, blk), 0)
    qi = lax.broadcasted_iota(jnp.int32, (blk, blk), 1)
    from_prev = ki > qi
    from_prev_bf = from_prev.astype(BF16)
    prev_bias = jnp.where(pl.program_id(1) > 0, 0.0, MASK_VALUE)

    for sb in range(SWA_TILE // blk):
        cols = slice(sb * blk, (sb + 1) * blk)
        two = slice((sb - 1) * blk, (sb + 1) * blk)

        probs, inv_den = [], []
        for g in range(SWA_KV_HEADS):
            k2 = jnp.concatenate([kp_ref[g], kc_ref[g, cols, :]], axis=0) if sb == 0 else kc_ref[g, two, :]
            heads = range(g * SWA_GROUP, (g + 1) * SWA_GROUP)
            qt = jnp.concatenate([qt_ref[h * dh:(h + 1) * dh, cols] for h in heads], axis=1)
            st = _dot(k2, qt)
            ps = []
            for j, h in enumerate(heads):
                s_prev = st[:blk, j * blk:(j + 1) * blk]
                if sb == 0:
                    s_prev = s_prev + prev_bias
                s = jnp.where(from_prev, s_prev, st[blk:, j * blk:(j + 1) * blk])
                sink = sink_ref[h] * LOG2E
                m = jnp.maximum(jnp.max(s, axis=0, keepdims=True), sink)
                p = jnp.exp2(s - m)
                inv_den.append(1.0 / (jnp.sum(p, axis=0, keepdims=True) + jnp.exp2(sink - m)))
                p = p.astype(BF16)
                p_prev = p * from_prev_bf
                ps.append(jnp.concatenate([p_prev, p - p_prev], axis=0))
            probs.append(jnp.concatenate(ps, axis=1))

        for g in range(SWA_KV_HEADS):
            vrows = slice(g * dh, (g + 1) * dh)
            vt2 = (jnp.concatenate([vtp_ref[vrows, :], vtc_ref[vrows, cols]], axis=1) if sb == 0
                   else vtc_ref[vrows, two])
            ot = _dot(vt2, probs[g])
            for j, h in enumerate(range(g * SWA_GROUP, (g + 1) * SWA_GROUP)):
                rows = slice(h * dh, (h + 1) * dh)
                o = ot[:, j * blk:(j + 1) * blk] * inv_den[h]
                ot_ref[rows, cols] = (o * gmt_ref[rows, cols].astype(F32)).astype(BF16)


def _swa(sinks, qt, k, vt, gmt, batch):
    n = qt.shape[1]
    tiles = n // batch // SWA_TILE
    sub = SWA_TILE // WINDOW
    tile = lambda b, t: b * tiles + t
    prev = lambda b, t: jnp.maximum(tile(b, t) * sub - 1, 0)
    wide = pl.BlockSpec((SWA_WIDTH, SWA_TILE), lambda b, t: (0, tile(b, t)))
    return pl.pallas_call(
        _swa_kernel,
        grid=(batch, tiles),
        in_specs=[pl.BlockSpec(memory_space=pltpu.SMEM),
                  wide,
                  pl.BlockSpec((SWA_KV_HEADS, SWA_TILE, SWA_HEAD_DIM), lambda b, t: (0, tile(b, t), 0)),
                  pl.BlockSpec((SWA_KV_HEADS, WINDOW, SWA_HEAD_DIM), lambda b, t: (0, prev(b, t), 0)),
                  pl.BlockSpec((SWA_KV_WIDTH, SWA_TILE), lambda b, t: (0, tile(b, t))),
                  pl.BlockSpec((SWA_KV_WIDTH, WINDOW), lambda b, t: (0, prev(b, t))),
                  wide],
        out_specs=wide,
        out_shape=jax.ShapeDtypeStruct((SWA_WIDTH, n), BF16),
        compiler_params=_params("parallel", "arbitrary"),
        name="swa",
    )(sinks, qt, k, k, vt, vt, gmt)


def kernel(x, mem, positions, pre_norm_g, post_norm_g, mem_norm_g, w_mem_kv, a_w_in, a_lb_logits, a_out_norm_g, a_w_out, kv_norm_g, w_kv_shared, b_w_in, b_sinks, b_w_out):
    batch, seq, _ = x.shape
    n = batch * seq
    assert seq % max(IN_TILE, OUT_TILE, HG_TILE, SWA_TILE) == 0
    tiles_per_seq = seq // IN_TILE

    h = x.reshape(n, D_MODEL)
    mem_k, mem_v = _mem_kv(mem, mem_norm_g, w_mem_kv.astype(BF16))
    cos, sin = _rope_tables(positions)
    a_w_in = a_w_in.astype(BF16)
    a_w_out = a_w_out.astype(BF16)
    b_w_in = b_w_in.astype(BF16)
    b_w_out = b_w_out.astype(BF16)
    k_sh = v_sh = None
    pending = None

    for layer in range(DEPTH):
        pre_g = pre_norm_g[layer].reshape(1, D_MODEL)
        post_g = post_norm_g[layer].reshape(1, D_MODEL)
        if layer < N_A_LAYERS:
            outs = _a_in_proj(layer, h, pending, pre_g, a_w_in, a_lb_logits, mem_k, mem_v, tiles_per_seq)
            if pending is not None:
                h, *outs = outs
            qs, lf, kk, v, gm, o_mem = outs
            o_main = _hgrn(qs, lf, kk, v, gm, a_out_norm_g[layer].reshape(1, HG_VAL_WIDTH), batch)
            pending = (o_main, o_mem, a_w_out, layer, post_g, False)
        else:
            j = layer - N_A_LAYERS
            outs = _b_in_proj(layer, h, pending, pre_g, b_w_in, cos, sin, mem_k, mem_v, tiles_per_seq)
            if pending is not None:
                h, *outs = outs
            qt, gmt, o_mem = outs
            if k_sh is None:
                k_sh, v_sh = _shared_kv(h, kv_norm_g.reshape(1, D_MODEL), w_kv_shared.astype(BF16), cos, sin)
            o_main_t = _swa(b_sinks[j], qt, k_sh, v_sh, gmt, batch)
            pending = (o_main_t, o_mem, b_w_out, j, post_g, True)
    o_main, o_mem, w_out, w_layer, post_g, transposed = pending
    h = _out_proj(o_main, o_mem, w_out, w_layer, post_g, h, "out_proj_last", main_transposed=transposed)
    return h.reshape(batch, seq, D_MODEL)
```

```python
import functools

import jax
import jax.numpy as jnp
from jax import lax
from jax.experimental import pallas as pl
from jax.experimental.pallas import tpu as pltpu

D_MODEL = 1024
DEPTH = 4
N_A_LAYERS = DEPTH // 2

HG_HEADS = 8
HG_DK = 128
HG_DV = D_MODEL // HG_HEADS
HG_KEY_WIDTH = HG_HEADS * HG_DK
HG_VAL_WIDTH = HG_HEADS * HG_DV
HG_CHUNK = 64

SWA_Q_HEADS = 16
SWA_KV_HEADS = 4
SWA_GROUP = SWA_Q_HEADS // SWA_KV_HEADS
SWA_HEAD_DIM = 64
SWA_WIDTH = SWA_Q_HEADS * SWA_HEAD_DIM
SWA_KV_WIDTH = SWA_KV_HEADS * SWA_HEAD_DIM
WINDOW = 128

MEM_TOKENS = 256
MEM_HEADS = 4
MEM_HEAD_DIM = 128
MEM_WIDTH = MEM_HEADS * MEM_HEAD_DIM

ROPE_THETA = 10000.0
NORM_EPS = 1e-6

A_IN_WIDTH = 2 * HG_KEY_WIDTH + 2 * HG_VAL_WIDTH + 2 * MEM_WIDTH
B_IN_WIDTH = 2 * SWA_WIDTH + 2 * MEM_WIDTH
OUT_WIDTH = HG_VAL_WIDTH + MEM_WIDTH

LANES = 128
SUBLANES = 8
MASK_VALUE = -1e30
LOG2E = 1.4426950408889634

IN_TILE = 512
IN_SUB_TILE = 256
OUT_TILE = 512
HG_TILE = 512
SWA_TILE = 512
VMEM_LIMIT = 56 * 1024 * 1024

F32 = jnp.float32
BF16 = jnp.bfloat16

_NT = (((1,), (1,)), ((), ()))
_TN = (((0,), (0,)), ((), ()))


def _dot(a, b):
    return jnp.dot(a, b, preferred_element_type=F32)


def _dot_nt(a, b):
    return lax.dot_general(a, b, _NT, preferred_element_type=F32)


def _dot_tn(a, b):
    return lax.dot_general(a, b, _TN, preferred_element_type=F32)


def _rms_norm(x, g):
    ms = jnp.mean(x * x, axis=-1, keepdims=True)
    return x * lax.rsqrt(ms + NORM_EPS) * g


def _silu(x):
    h = 0.5 * x
    return h + h * jnp.tanh(h)


def _params(*semantics):
    return pltpu.CompilerParams(dimension_semantics=semantics, vmem_limit_bytes=VMEM_LIMIT)


def _layer_weight(w, layer):
    return pl.BlockSpec((None,) + w.shape[1:], lambda *_: (layer, 0, 0), pipeline_mode=pl.Buffered(1))


def _layer_input(prev, h_ref, rows):
    if prev is None:
        return h_ref[rows, :]
    om_ref, ome_ref, wout_ref, pg_ref, hnew_ref, main_transposed = prev
    if main_transposed:
        y = _dot_tn(om_ref[:, rows], wout_ref[:HG_VAL_WIDTH, :])
    else:
        y = _dot(om_ref[rows, :], wout_ref[:HG_VAL_WIDTH, :])
    y = y + _dot(ome_ref[rows, :], wout_ref[HG_VAL_WIDTH:, :])
    h_new = h_ref[rows, :] + _rms_norm(y, pg_ref[...])
    hnew_ref[rows, :] = h_new
    return h_new


def _normed_sub_tiles(prev, h_ref, g_ref):
    return [_rms_norm(_layer_input(prev, h_ref, slice(s * IN_SUB_TILE, (s + 1) * IN_SUB_TILE)),
                      g_ref[...]).astype(BF16) for s in range(IN_TILE // IN_SUB_TILE)]


def _split_refs(prev_transposed, refs, n_in):
    if prev_transposed is None:
        return None, refs[:n_in], refs[n_in:]
    prev_in, ins, (hnew_ref, *outs) = refs[:4], refs[4:4 + n_in], refs[4 + n_in:]
    return (*prev_in, hnew_ref, prev_transposed), ins, outs


def _pending_specs(pending, row, col, full):
    if pending is None:
        return [], []
    o_main, o_mem, w_out, w_layer, post_g, transposed = pending
    main = col(HG_VAL_WIDTH) if transposed else row(HG_VAL_WIDTH)
    return [o_main, o_mem, w_out, post_g], [main, row(MEM_WIDTH), _layer_weight(w_out, w_layer), full(post_g)]


ROPE_FREQS = SWA_HEAD_DIM // 2
ROPE_PACK = LANES // ROPE_FREQS


def _rope_table_kernel(pos_ref, invf_ref, cos_ref, sin_ref):
    ang = pos_ref[...].astype(F32) * invf_ref[...]
    rows = ang.shape[0]
    src = lax.broadcasted_iota(jnp.int32, (LANES, LANES), 0)
    dst = lax.broadcasted_iota(jnp.int32, (LANES, LANES), 1)
    sin_sign = jnp.where((dst & ROPE_FREQS) == 0, -1.0, 1.0)
    for table, out_ref, sign in ((jnp.cos(ang), cos_ref, 1.0), (jnp.sin(ang), sin_ref, sin_sign)):
        hi = table.astype(BF16)
        rest = table - hi.astype(F32)
        mid = rest.astype(BF16)
        lo = (rest - mid.astype(F32)).astype(BF16)
        for i in range(ROPE_PACK):
            pick = src == i * ROPE_FREQS + (dst & (ROPE_FREQS - 1))
            spread = jnp.where(pick, sign, 0.0).astype(BF16)
            out_ref[pl.ds(i, rows, stride=ROPE_PACK), :] = _dot(hi, spread) + _dot(mid, spread) + _dot(lo, spread)


def _rope_tables(positions):
    n = positions.size
    inv_freq = ROPE_THETA ** (-jnp.arange(0, SWA_HEAD_DIM, 2, dtype=F32) / SWA_HEAD_DIM)
    invf = jnp.tile(inv_freq, ROPE_PACK).reshape(1, LANES)
    pos = jnp.repeat(positions.reshape(n // ROPE_PACK, ROPE_PACK), ROPE_FREQS, axis=1)
    tile = 2048
    row = pl.BlockSpec((tile, LANES), lambda i: (i, 0))
    return pl.pallas_call(
        _rope_table_kernel,
        grid=(n // tile,),
        in_specs=[pl.BlockSpec((tile // ROPE_PACK, LANES), lambda i: (i, 0)),
                  pl.BlockSpec((1, LANES), lambda i: (0, 0))],
        out_specs=[row, row],
        out_shape=[jax.ShapeDtypeStruct((n, LANES), F32)] * 2,
        compiler_params=_params("parallel"),
        name="rope_tables",
    )(pos, invf)


def _rope_block(xc, cos, sin, first_half):
    swapped = jnp.where(first_half,
                        pltpu.roll(xc, LANES - SWA_HEAD_DIM // 2, 1),
                        pltpu.roll(xc, SWA_HEAD_DIM // 2, 1))
    return xc * cos + swapped * sin


def _first_half_mask(rows):
    lane = lax.broadcasted_iota(jnp.int32, (rows, LANES), 1)
    return (lane & (SWA_HEAD_DIM // 2)) == 0


def _mem_kv_kernel(mem_ref, g_ref, w_ref, k_ref, v_ref):
    mn = _rms_norm(mem_ref[...], g_ref[...]).astype(BF16)
    kv = _dot(mn, w_ref[...].astype(BF16))
    k_ref[...] = kv[:, :MEM_WIDTH].astype(BF16)
    v_ref[...] = kv[:, MEM_WIDTH:].astype(BF16)


def _mem_kv(mem, mem_norm_g, w_mem_kv):
    b = mem.shape[0]
    out = pl.BlockSpec((None, None, MEM_TOKENS, MEM_WIDTH), lambda l, i: (l, i, 0, 0))
    return pl.pallas_call(
        _mem_kv_kernel,
        grid=(DEPTH, b),
        in_specs=[
            pl.BlockSpec((None, MEM_TOKENS, D_MODEL), lambda l, i: (i, 0, 0)),
            pl.BlockSpec((None, 1, D_MODEL), lambda l, i: (l, 0, 0)),
            pl.BlockSpec((None, D_MODEL, 2 * MEM_WIDTH), lambda l, i: (l, 0, 0)),
        ],
        out_specs=[out, out],
        out_shape=[jax.ShapeDtypeStruct((DEPTH, b, MEM_TOKENS, MEM_WIDTH), BF16)] * 2,
        compiler_params=_params("arbitrary", "arbitrary"),
        name="mem_kv",
    )(mem, mem_norm_g.reshape(DEPTH, 1, D_MODEL), w_mem_kv)


def _memory_probs(mq, mk_ref):
    mq = (mq * (MEM_HEAD_DIM ** -0.5 * LOG2E)).astype(BF16)
    probs, inv_sums = [], []
    for hh in range(MEM_HEADS):
        hs = slice(hh * MEM_HEAD_DIM, (hh + 1) * MEM_HEAD_DIM)
        s = _dot_nt(mq[:, hs], mk_ref[:, hs])
        p = jnp.exp2(s - jnp.max(s, axis=-1, keepdims=True))
        probs.append(p.astype(BF16))
        inv_sums.append(1.0 / jnp.sum(p, axis=-1, keepdims=True))
    return probs, inv_sums


def _memory_readout(probs, inv_sums, gate, mv_ref, out_ref, rows):
    for hh in range(MEM_HEADS):
        hs = slice(hh * MEM_HEAD_DIM, (hh + 1) * MEM_HEAD_DIM)
        o = _dot(probs[hh], mv_ref[:, hs]) * inv_sums[hh]
        out_ref[rows, hs] = (o * gate[:, hs]).astype(BF16)


def _a_in_kernel(layer, prev_transposed, h_ref, *refs):
    prev, (g_ref, w_ref, lbl_ref, mk_ref, mv_ref), outs = _split_refs(prev_transposed, refs, 5)
    qs_ref, lf_ref, kk_ref, v_ref, gm_ref, om_ref = outs

    logits = lbl_ref[...]
    e = jnp.exp(logits - jnp.max(logits, axis=0, keepdims=True))
    lb = jnp.sum(e[:layer + 1], axis=0, keepdims=True) / jnp.sum(e, axis=0, keepdims=True)
    mid = 0.5 * (1.0 + lb)
    half = 0.5 * (1.0 - lb)

    col_q, col_f, col_v = 0, HG_KEY_WIDTH, 2 * HG_KEY_WIDTH
    col_g = col_v + HG_VAL_WIDTH
    col_mq = col_g + HG_VAL_WIDTH
    col_mg = col_mq + MEM_WIDTH

    for sub, xn in enumerate(_normed_sub_tiles(prev, h_ref, g_ref)):
        rows = slice(sub * IN_SUB_TILE, (sub + 1) * IN_SUB_TILE)

        def proj(lo, width):
            return _dot(xn, w_ref[:, lo:lo + width])

        mq = proj(col_mq, MEM_WIDTH)
        mem_gate = _silu(proj(col_mg, MEM_WIDTH))
        qs_ref[rows, :] = _silu(proj(col_q, HG_KEY_WIDTH)).astype(BF16)
        probs, inv_sums = _memory_probs(mq, mk_ref)

        th = half * jnp.tanh(0.5 * proj(col_f, HG_KEY_WIDTH))
        lf_ref[rows, :] = jnp.log2(mid + th)
        kk_ref[rows, :] = (half - th).astype(BF16)

        _memory_readout(probs, inv_sums, mem_gate, mv_ref, om_ref, rows)
        v_ref[rows, :] = proj(col_v, HG_VAL_WIDTH).astype(BF16)
        gm_ref[rows, :] = _silu(proj(col_g, HG_VAL_WIDTH)).astype(BF16)


def _a_in_proj(layer, h, pending, g, w, lb_logits, mem_k, mem_v, tiles_per_seq):
    n = h.shape[0]
    row = lambda width: pl.BlockSpec((IN_TILE, width), lambda i: (i, 0))
    col = lambda width: pl.BlockSpec((width, IN_TILE), lambda i: (0, i))
    full = lambda a: pl.BlockSpec(a.shape, lambda i: (0,) * a.ndim)
    mem = pl.BlockSpec((None, None, MEM_TOKENS, MEM_WIDTH), lambda i: (layer, i // tiles_per_seq, 0, 0))
    widths = (HG_KEY_WIDTH, HG_KEY_WIDTH, HG_KEY_WIDTH, HG_VAL_WIDTH, HG_VAL_WIDTH, MEM_WIDTH)
    dtypes = (BF16, F32, BF16, BF16, BF16, BF16)
    if pending is not None:
        widths, dtypes = (D_MODEL,) + widths, (F32,) + dtypes
    pend_args, pend_specs = _pending_specs(pending, row, col, full)
    return pl.pallas_call(
        functools.partial(_a_in_kernel, layer, None if pending is None else pending[-1]),
        grid=(n // IN_TILE,),
        in_specs=[row(D_MODEL)] + pend_specs + [full(g), _layer_weight(w, layer), full(lb_logits), mem, mem],
        out_specs=[row(wd) for wd in widths],
        out_shape=[jax.ShapeDtypeStruct((n, wd), dt) for wd, dt in zip(widths, dtypes)],
        compiler_params=_params("parallel"),
        name=f"a_in_proj_{layer}",
    )(h, *pend_args, g, w, lb_logits, mem_k, mem_v)


_LEVELS = (32, 16, 8, 4, 2, 1)
_SAFE_BLOCKS = (HG_CHUNK, 16)
SAFE_BLOCK_DECAY = 100.0


def _coarse_level(cum, qs, kk, m):
    c, w = cum.shape
    early = m % (2 * SUBLANES) == 0
    dt = BF16 if early else F32
    zeros = jnp.zeros((m, w), dt)
    qd, kd = [], []
    for blk in range(c // (2 * m)):
        lo, mid, hi = blk * 2 * m, blk * 2 * m + m, (blk + 1) * 2 * m
        ref = cum[mid - 1:mid, :]
        kd += [(kk[lo:mid] * jnp.exp2(ref - cum[lo:mid])).astype(dt), zeros]
        qd += [zeros, (qs[mid:hi] * jnp.exp2(cum[mid:hi] - ref)).astype(dt)]
    return jnp.concatenate(qd, axis=0).astype(BF16), jnp.concatenate(kd, axis=0).astype(BF16)


def _fine_level(cum3, qs3, kk3, m, upper_pen, lower_pen):
    groups, _, w = cum3.shape
    parts = [jnp.broadcast_to(cum3[:, blk * 2 * m + m - 1:blk * 2 * m + m, :], (groups, 2 * m, w))
             for blk in range(SUBLANES // (2 * m))]
    x = cum3 - (parts[0] if len(parts) == 1 else jnp.concatenate(parts, axis=1))
    qd = qs3 * jnp.exp2(x + upper_pen)
    kd = kk3 * jnp.exp2(lower_pen - x)
    c = groups * SUBLANES
    return qd.reshape(c, w).astype(BF16), kd.reshape(c, w).astype(BF16)


def _block_level(cum, qs, kk, block):
    c, _ = cum.shape
    local = [cum[:block]] + [cum[b:b + block] - cum[b - 1:b] for b in range(block, c, block)]
    local = local[0] if len(local) == 1 else jnp.concatenate(local, axis=0)
    return (qs * jnp.exp2(local)).astype(BF16), (kk * jnp.exp2(-local)).astype(BF16)


def _hgrn_constants():
    c = HG_CHUNK
    row = lax.broadcasted_iota(jnp.int32, (c, c), 0)
    col = lax.broadcasted_iota(jnp.int32, (c, c), 1)
    sub = lax.broadcasted_iota(jnp.int32, (1, SUBLANES, HG_DK), 1)
    fine = [m for m in _LEVELS if m < SUBLANES]
    same_block = {m: (row // (2 * m)) == (col // (2 * m)) for m in _LEVELS if 2 * m < c}
    return dict(
        same_block={m: mask.astype(F32) for m, mask in same_block.items()},
        diag=(row == col).astype(F32),
        causal_block={b: ((col <= row) if b == c else same_block[b // 2] & (col <= row)).astype(F32)
                      for b in _SAFE_BLOCKS},
        upper_pen={m: jnp.where((sub & m) != 0, 0.0, MASK_VALUE) for m in fine},
        lower_pen={m: jnp.where((sub & m) != 0, MASK_VALUE, 0.0) for m in fine},
        even_row=jnp.where((sub & 1) != 0, 0.0, 1.0),
    )


def _hgrn_tile(block, k, qs_ref, lf_ref, kk_ref, v_ref, gm_ref, gn, cum_s, o_ref, st_ref):
    c = HG_CHUNK
    heads = [slice(h * HG_DK, (h + 1) * HG_DK) for h in range(HG_HEADS)]
    shape3 = (c // SUBLANES, SUBLANES, HG_DK)
    n_chunks = HG_TILE // c

    def scores_pass(ci):
        rows = slice(ci * c, (ci + 1) * c)
        scores_bf, q_ins, k_outs, lasts = [], [], [], []
        for hs in heads:
            cum = cum_s[rows, hs]
            qs = qs_ref[rows, hs].astype(F32)
            kk = kk_ref[rows, hs].astype(F32)

            pairs = []
            for m in _LEVELS:
                if block is not None and 2 * m <= block:
                    continue
                if m >= SUBLANES:
                    qd, kd = _coarse_level(cum, qs, kk, m)
                elif m == 1:
                    qs3, kk3 = qs.reshape(shape3), kk.reshape(shape3)
                    qd3 = qs3 * jnp.exp2(lf_ref[rows, hs].reshape(shape3) + k["upper_pen"][m])
                    qd = qd3.reshape(c, HG_DK).astype(BF16)
                    kd = (kk3 * k["even_row"]).reshape(c, HG_DK).astype(BF16)
                else:
                    qd, kd = _fine_level(cum.reshape(shape3), qs.reshape(shape3), kk.reshape(shape3), m,
                                         k["upper_pen"][m], k["lower_pen"][m])
                pairs.append((qd, kd, k["same_block"].get(m)))
            last = cum[c - 1:c, :]
            q_in = (qs * jnp.exp2(cum)).astype(BF16)
            if block == c:
                k_side = kk * jnp.exp2(-cum)
                pairs.append((q_in, k_side.astype(BF16), k["causal_block"][block]))
                k_out = (k_side * jnp.exp2(last)).astype(BF16)
            else:
                if block is not None:
                    pairs.append(_block_level(cum, qs, kk, block) + (k["causal_block"][block],))
                else:
                    pairs.append((qs.astype(BF16), kk.astype(BF16), k["diag"]))
                k_out = (kk * jnp.exp2(last - cum)).astype(BF16)

            scores = None
            for qd, kd, mask in pairs:
                s = _dot_nt(qd, kd)
                if mask is not None:
                    s = s * mask
                scores = s if scores is None else scores + s

            scores_bf.append(scores.astype(BF16))
            q_ins.append(q_in)
            k_outs.append(k_out)
            lasts.append(last)
        return scores_bf, q_ins, k_outs, lasts

    def recurrence_pass(ci, scores_bf, q_ins, k_outs, lasts):
        rows = slice(ci * c, (ci + 1) * c)
        for h, hs in enumerate(heads):
            o = _dot_nt(q_ins[h], st_ref[h].astype(BF16)) + _dot(scores_bf[h], v_ref[rows, hs])
            o = o * lax.rsqrt(jnp.mean(o * o, axis=-1, keepdims=True) + NORM_EPS)
            o_ref[rows, hs] = (o * gn[:, hs] * gm_ref[rows, hs].astype(F32)).astype(BF16)

        for h, hs in enumerate(heads):
            st_ref[h] = st_ref[h] * jnp.exp2(lasts[h]) + _dot_tn(v_ref[rows, hs], k_outs[h])

    ahead = scores_pass(0)
    for ci in range(n_chunks):
        current, ahead = ahead, (scores_pass(ci + 1) if ci + 1 < n_chunks else None)
        recurrence_pass(ci, *current)


def _hgrn_kernel(qs_ref, lf_ref, kk_ref, v_ref, gm_ref, gn_ref, o_ref, cum_s, st_ref):
    c = HG_CHUNK

    @pl.when(pl.program_id(1) == 0)
    def _():
        st_ref[...] = jnp.zeros_like(st_ref)

    row = lax.broadcasted_iota(jnp.int32, (c, c), 0)
    col = lax.broadcasted_iota(jnp.int32, (c, c), 1)
    tri = (col <= row).astype(BF16)

    worst = {b: None for b in _SAFE_BLOCKS}
    for ci in range(HG_TILE // c):
        rows = slice(ci * c, (ci + 1) * c)
        lf = lf_ref[rows, :]
        hi = lf.astype(BF16)
        lo = (lf - hi.astype(F32)).astype(BF16)
        cum = _dot(tri, hi) + _dot(tri, lo)
        cum_s[rows, :] = cum
        for block in _SAFE_BLOCKS:
            ends = [cum[b - 1:b, :] for b in range(block, c + 1, block)]
            for b, end in enumerate(ends):
                decay = end if b == 0 else end - ends[b - 1]
                worst[block] = decay if worst[block] is None else jnp.minimum(worst[block], decay)
    safe = [jnp.min(worst[b]) >= -SAFE_BLOCK_DECAY for b in _SAFE_BLOCKS]

    consts = _hgrn_constants()
    args = (consts, qs_ref, lf_ref, kk_ref, v_ref, gm_ref, gn_ref[...], cum_s, o_ref, st_ref)

    not_yet = None
    for block, ok in zip(_SAFE_BLOCKS + (None,), safe + [True]):
        take = ok if not_yet is None else jnp.logical_and(not_yet, ok)

        @pl.when(take)
        def _(block=block):
            _hgrn_tile(block, *args)

        if block is not None:
            not_yet = jnp.logical_not(ok) if not_yet is None else jnp.logical_and(not_yet, jnp.logical_not(ok))


def _hgrn(qs, lf, kk, v, gm, gn, batch):
    n = qs.shape[0]
    tiles = n // batch // HG_TILE
    row = pl.BlockSpec((HG_TILE, HG_KEY_WIDTH), lambda b, t: (b * tiles + t, 0))
    return pl.pallas_call(
        _hgrn_kernel,
        grid=(batch, tiles),
        in_specs=[row, row, row, row, row, pl.BlockSpec((1, HG_VAL_WIDTH), lambda b, t: (0, 0))],
        out_specs=row,
        out_shape=jax.ShapeDtypeStruct((n, HG_VAL_WIDTH), BF16),
        scratch_shapes=[pltpu.VMEM((HG_TILE, HG_KEY_WIDTH), F32),
                        pltpu.VMEM((HG_HEADS, HG_DV, HG_DK), F32)],
        compiler_params=_params("arbitrary", "arbitrary"),
        name="hgrn2",
    )(qs, lf, kk, v, gm, gn)


def _out_kernel(main_transposed, om_ref, ome_ref, w_ref, g_ref, h_ref, o_ref):
    main_dot = _dot_tn if main_transposed else _dot
    y = main_dot(om_ref[...], w_ref[:HG_VAL_WIDTH, :]) + _dot(ome_ref[...], w_ref[HG_VAL_WIDTH:, :])
    o_ref[...] = h_ref[...] + _rms_norm(y, g_ref[...])


def _out_proj(o_main, o_mem, w, layer, g, h, name, main_transposed=False):
    n = h.shape[0]
    row = lambda width: pl.BlockSpec((OUT_TILE, width), lambda i: (i, 0))
    col = lambda width: pl.BlockSpec((width, OUT_TILE), lambda i: (0, i))
    full = lambda a: pl.BlockSpec(a.shape, lambda i: (0,) * a.ndim)
    main = col(HG_VAL_WIDTH) if main_transposed else row(HG_VAL_WIDTH)
    return pl.pallas_call(
        functools.partial(_out_kernel, main_transposed),
        grid=(n // OUT_TILE,),
        in_specs=[main, row(MEM_WIDTH), _layer_weight(w, layer), full(g), row(D_MODEL)],
        out_specs=row(D_MODEL),
        out_shape=jax.ShapeDtypeStruct((n, D_MODEL), F32),
        compiler_params=_params("parallel"),
        name=name,
    )(o_main, o_mem, w, g, h)


def _shared_kv_kernel(x_ref, g_ref, w_ref, cos_ref, sin_ref, k_ref, v_ref):
    xn = _rms_norm(x_ref[...], g_ref[...]).astype(BF16)
    kv = _dot(xn, w_ref[...])
    cos = cos_ref[...]
    sin = sin_ref[...]
    first = _first_half_mask(IN_TILE)
    heads_per_block = LANES // SWA_HEAD_DIM
    for cb in range(SWA_KV_WIDTH // LANES):
        kr = _rope_block(kv[:, cb * LANES:(cb + 1) * LANES], cos, sin, first).astype(BF16)
        for j in range(heads_per_block):
            k_ref[cb * heads_per_block + j] = kr[:, j * SWA_HEAD_DIM:(j + 1) * SWA_HEAD_DIM]
    v_ref[...] = kv[:, SWA_KV_WIDTH:].T.astype(BF16)


def _shared_kv(h, g, w, cos, sin):
    n = h.shape[0]
    row = lambda width: pl.BlockSpec((IN_TILE, width), lambda i: (i, 0))
    full = lambda a: pl.BlockSpec(a.shape, lambda i: (0,) * a.ndim)
    return pl.pallas_call(
        _shared_kv_kernel,
        grid=(n // IN_TILE,),
        in_specs=[row(D_MODEL), full(g), full(w), row(LANES), row(LANES)],
        out_specs=[pl.BlockSpec((SWA_KV_HEADS, IN_TILE, SWA_HEAD_DIM), lambda i: (0, i, 0)),
                   pl.BlockSpec((SWA_KV_WIDTH, IN_TILE), lambda i: (0, i))],
        out_shape=[jax.ShapeDtypeStruct((SWA_KV_HEADS, n, SWA_HEAD_DIM), BF16),
                   jax.ShapeDtypeStruct((SWA_KV_WIDTH, n), BF16)],
        compiler_params=_params("parallel"),
        name="shared_kv",
    )(h, g, w, cos, sin)


def _b_in_kernel(prev_transposed, h_ref, *refs):
    prev, (g_ref, w_ref, cos_ref, sin_ref, mk_ref, mv_ref), outs = _split_refs(prev_transposed, refs, 6)
    qt_ref, gmt_ref, om_ref = outs
    first = _first_half_mask(IN_SUB_TILE)
    scale = SWA_HEAD_DIM ** -0.5 * LOG2E
    col_mq = 2 * SWA_WIDTH
    col_mg = col_mq + MEM_WIDTH

    for sub, xn in enumerate(_normed_sub_tiles(prev, h_ref, g_ref)):
        rows = slice(sub * IN_SUB_TILE, (sub + 1) * IN_SUB_TILE)

        def proj(lo, width):
            return _dot(xn, w_ref[:, lo:lo + width])

        mq = proj(col_mq, MEM_WIDTH)
        mem_gate = _silu(proj(col_mg, MEM_WIDTH))

        q = proj(0, SWA_WIDTH)
        probs, inv_sums = _memory_probs(mq, mk_ref)
        cos = cos_ref[rows, :]
        sin = sin_ref[rows, :]
        for cb in range(SWA_WIDTH // LANES):
            cs = slice(cb * LANES, (cb + 1) * LANES)
            qt_ref[cs, rows] = (_rope_block(q[:, cs], cos, sin, first) * scale).T.astype(BF16)

        gate = _silu(proj(SWA_WIDTH, SWA_WIDTH))
        _memory_readout(probs, inv_sums, mem_gate, mv_ref, om_ref, rows)
        for cb in range(SWA_WIDTH // LANES):
            cs = slice(cb * LANES, (cb + 1) * LANES)
            gmt_ref[cs, rows] = gate[:, cs].T.astype(BF16)


def _b_in_proj(layer, h, pending, g, w, cos, sin, mem_k, mem_v, tiles_per_seq):
    n = h.shape[0]
    row = lambda width: pl.BlockSpec((IN_TILE, width), lambda i: (i, 0))
    col = lambda width: pl.BlockSpec((width, IN_TILE), lambda i: (0, i))
    full = lambda a: pl.BlockSpec(a.shape, lambda i: (0,) * a.ndim)
    mem = pl.BlockSpec((None, None, MEM_TOKENS, MEM_WIDTH), lambda i: (layer, i // tiles_per_seq, 0, 0))
    out_specs = [col(SWA_WIDTH), col(SWA_WIDTH), row(MEM_WIDTH)]
    out_shape = [jax.ShapeDtypeStruct((SWA_WIDTH, n), BF16), jax.ShapeDtypeStruct((SWA_WIDTH, n), BF16),
                 jax.ShapeDtypeStruct((n, MEM_WIDTH), BF16)]
    if pending is not None:
        out_specs, out_shape = [row(D_MODEL)] + out_specs, [jax.ShapeDtypeStruct((n, D_MODEL), F32)] + out_shape
    pend_args, pend_specs = _pending_specs(pending, row, col, full)
    return pl.pallas_call(
        functools.partial(_b_in_kernel, None if pending is None else pending[-1]),
        grid=(n // IN_TILE,),
        in_specs=[row(D_MODEL)] + pend_specs
        + [full(g), _layer_weight(w, layer - N_A_LAYERS), row(LANES), row(LANES), mem, mem],
        out_specs=out_specs,
        out_shape=out_shape,
        compiler_params=_params("parallel"),
        name=f"b_in_proj_{layer}",
    )(h, *pend_args, g, w, cos, sin, mem_k, mem_v)


def _swa_kernel(sink_ref, qt_ref, kc_ref, kp_ref, vtc_ref, vtp_ref, gmt_ref, ot_ref):
    blk = WINDOW
    dh = SWA_HEAD_DIM
    ki = lax.broadcasted_iota(jnp.int32, (blk, blk), 0)
    qi = lax.broadcasted_iota(jnp.int32, (blk, blk), 1)
    from_prev = ki > qi
    from_prev_bf = from_prev.astype(BF16)
    prev_bias = jnp.where(pl.program_id(1) > 0, 0.0, MASK_VALUE)

    for sb in range(SWA_TILE // blk):
        cols = slice(sb * blk, (sb + 1) * blk)
        two = slice((sb - 1) * blk, (sb + 1) * blk)

        probs, inv_den = [], []
        for g in range(SWA_KV_HEADS):
            k2 = jnp.concatenate([kp_ref[g], kc_ref[g, cols, :]], axis=0) if sb == 0 else kc_ref[g, two, :]
            heads = range(g * SWA_GROUP, (g + 1) * SWA_GROUP)
            qt = jnp.concatenate([qt_ref[h * dh:(h + 1) * dh, cols] for h in heads], axis=1)
            st = _dot(k2, qt)
            ps = []
            for j, h in enumerate(heads):
                s_prev = st[:blk, j * blk:(j + 1) * blk]
                if sb == 0:
                    s_prev = s_prev + prev_bias
                s = jnp.where(from_prev, s_prev, st[blk:, j * blk:(j + 1) * blk])
                sink = sink_ref[h] * LOG2E
                m = jnp.maximum(jnp.max(s, axis=0, keepdims=True), sink)
                p = jnp.exp2(s - m)
                inv_den.append(1.0 / (jnp.sum(p, axis=0, keepdims=True) + jnp.exp2(sink - m)))
                p = p.astype(BF16)
                p_prev = p * from_prev_bf
                ps.append(jnp.concatenate([p_prev, p - p_prev], axis=0))
            probs.append(jnp.concatenate(ps, axis=1))

        for g in range(SWA_KV_HEADS):
            vrows = slice(g * dh, (g + 1) * dh)
            vt2 = (jnp.concatenate([vtp_ref[vrows, :], vtc_ref[vrows, cols]], axis=1) if sb == 0
                   else vtc_ref[vrows, two])
            ot = _dot(vt2, probs[g])
            for j, h in enumerate(range(g * SWA_GROUP, (g + 1) * SWA_GROUP)):
                rows = slice(h * dh, (h + 1) * dh)
                o = ot[:, j * blk:(j + 1) * blk] * inv_den[h]
                ot_ref[rows, cols] = (o * gmt_ref[rows, cols].astype(F32)).astype(BF16)


def _swa(sinks, qt, k, vt, gmt, batch):
    n = qt.shape[1]
    tiles = n // batch // SWA_TILE
    sub = SWA_TILE // WINDOW
    tile = lambda b, t: b * tiles + t
    prev = lambda b, t: jnp.maximum(tile(b, t) * sub - 1, 0)
    wide = pl.BlockSpec((SWA_WIDTH, SWA_TILE), lambda b, t: (0, tile(b, t)))
    return pl.pallas_call(
        _swa_kernel,
        grid=(batch, tiles),
        in_specs=[pl.BlockSpec(memory_space=pltpu.SMEM),
                  wide,
                  pl.BlockSpec((SWA_KV_HEADS, SWA_TILE, SWA_HEAD_DIM), lambda b, t: (0, tile(b, t), 0)),
                  pl.BlockSpec((SWA_KV_HEADS, WINDOW, SWA_HEAD_DIM), lambda b, t: (0, prev(b, t), 0)),
                  pl.BlockSpec((SWA_KV_WIDTH, SWA_TILE), lambda b, t: (0, tile(b, t))),
                  pl.BlockSpec((SWA_KV_WIDTH, WINDOW), lambda b, t: (0, prev(b, t))),
                  wide],
        out_specs=wide,
        out_shape=jax.ShapeDtypeStruct((SWA_WIDTH, n), BF16),
        compiler_params=_params("parallel", "arbitrary"),
        name="swa",
    )(sinks, qt, k, k, vt, vt, gmt)


def kernel(x, mem, positions, pre_norm_g, post_norm_g, mem_norm_g, w_mem_kv, a_w_in, a_lb_logits, a_out_norm_g, a_w_out, kv_norm_g, w_kv_shared, b_w_in, b_sinks, b_w_out):
    batch, seq, _ = x.shape
    n = batch * seq
    assert seq % max(IN_TILE, OUT_TILE, HG_TILE, SWA_TILE) == 0
    tiles_per_seq = seq // IN_TILE

    h = x.reshape(n, D_MODEL)
    mem_k, mem_v = _mem_kv(mem, mem_norm_g, w_mem_kv)
    cos, sin = _rope_tables(positions)
    a_w_in = a_w_in.astype(BF16)
    a_w_out = a_w_out.astype(BF16)
    b_w_in = b_w_in.astype(BF16)
    b_w_out = b_w_out.astype(BF16)
    k_sh = v_sh = None
    pending = None

    for layer in range(DEPTH):
        pre_g = pre_norm_g[layer].reshape(1, D_MODEL)
        post_g = post_norm_g[layer].reshape(1, D_MODEL)
        if layer < N_A_LAYERS:
            outs = _a_in_proj(layer, h, pending, pre_g, a_w_in, a_lb_logits, mem_k, mem_v, tiles_per_seq)
            if pending is not None:
                h, *outs = outs
            qs, lf, kk, v, gm, o_mem = outs
            o_main = _hgrn(qs, lf, kk, v, gm, a_out_norm_g[layer].reshape(1, HG_VAL_WIDTH), batch)
            pending = (o_main, o_mem, a_w_out, layer, post_g, False)
        else:
            j = layer - N_A_LAYERS
            outs = _b_in_proj(layer, h, pending, pre_g, b_w_in, cos, sin, mem_k, mem_v, tiles_per_seq)
            if pending is not None:
                h, *outs = outs
            qt, gmt, o_mem = outs
            if k_sh is None:
                k_sh, v_sh = _shared_kv(h, kv_norm_g.reshape(1, D_MODEL), w_kv_shared.astype(BF16), cos, sin)
            o_main_t = _swa(b_sinks[j], qt, k_sh, v_sh, gmt, batch)
            pending = (o_main_t, o_mem, b_w_out, j, post_g, True)
    o_main, o_mem, w_out, w_layer, post_g, transposed = pending
    h = _out_proj(o_main, o_mem, w_out, w_layer, post_g, h, "out_proj_last", main_transposed=transposed)
    return h.reshape(batch, seq, D_MODEL)
```

```python
import functools

import jax
import jax.numpy as jnp
from jax import lax
from jax.experimental import pallas as pl
from jax.experimental.pallas import tpu as pltpu

D_MODEL = 1024
DEPTH = 4
N_A_LAYERS = DEPTH // 2

HG_HEADS = 8
HG_DK = 128
HG_DV = D_MODEL // HG_HEADS
HG_KEY_WIDTH = HG_HEADS * HG_DK
HG_VAL_WIDTH = HG_HEADS * HG_DV
HG_CHUNK = 64

SWA_Q_HEADS = 16
SWA_KV_HEADS = 4
SWA_GROUP = SWA_Q_HEADS // SWA_KV_HEADS
SWA_HEAD_DIM = 64
SWA_WIDTH = SWA_Q_HEADS * SWA_HEAD_DIM
SWA_KV_WIDTH = SWA_KV_HEADS * SWA_HEAD_DIM
WINDOW = 128

MEM_TOKENS = 256
MEM_HEADS = 4
MEM_HEAD_DIM = 128
MEM_WIDTH = MEM_HEADS * MEM_HEAD_DIM

ROPE_THETA = 10000.0
NORM_EPS = 1e-6

A_IN_WIDTH = 2 * HG_KEY_WIDTH + 2 * HG_VAL_WIDTH + 2 * MEM_WIDTH
B_IN_WIDTH = 2 * SWA_WIDTH + 2 * MEM_WIDTH
OUT_WIDTH = HG_VAL_WIDTH + MEM_WIDTH

LANES = 128
SUBLANES = 8
MASK_VALUE = -1e30
LOG2E = 1.4426950408889634

IN_TILE = 512
IN_SUB_TILE = 256
OUT_TILE = 1024
HG_TILE = 512
SWA_TILE = 1024
VMEM_LIMIT = 56 * 1024 * 1024

F32 = jnp.float32
BF16 = jnp.bfloat16

_NT = (((1,), (1,)), ((), ()))
_TN = (((0,), (0,)), ((), ()))


def _dot(a, b):
    return jnp.dot(a, b, preferred_element_type=F32)


def _dot_nt(a, b):
    return lax.dot_general(a, b, _NT, preferred_element_type=F32)


def _dot_tn(a, b):
    return lax.dot_general(a, b, _TN, preferred_element_type=F32)


def _rms_norm(x, g):
    ms = jnp.mean(x * x, axis=-1, keepdims=True)
    return x * lax.rsqrt(ms + NORM_EPS) * g


def _silu(x):
    h = 0.5 * x
    return h + h * jnp.tanh(h)


def _params(*semantics):
    return pltpu.CompilerParams(dimension_semantics=semantics, vmem_limit_bytes=VMEM_LIMIT)


def _layer_weight(w, layer):
    return pl.BlockSpec((None,) + w.shape[1:], lambda *_: (layer, 0, 0), pipeline_mode=pl.Buffered(1))


def _layer_input(prev, h_ref, rows):
    if prev is None:
        return h_ref[rows, :]
    om_ref, ome_ref, wout_ref, pg_ref, hnew_ref, main_transposed = prev
    if main_transposed:
        y = _dot_tn(om_ref[:, rows], wout_ref[:HG_VAL_WIDTH, :])
    else:
        y = _dot(om_ref[rows, :], wout_ref[:HG_VAL_WIDTH, :])
    y = y + _dot(ome_ref[rows, :], wout_ref[HG_VAL_WIDTH:, :])
    h_new = h_ref[rows, :] + _rms_norm(y, pg_ref[...])
    hnew_ref[rows, :] = h_new
    return h_new


def _normed_sub_tiles(prev, h_ref, g_ref):
    return [_rms_norm(_layer_input(prev, h_ref, slice(s * IN_SUB_TILE, (s + 1) * IN_SUB_TILE)),
                      g_ref[...]).astype(BF16) for s in range(IN_TILE // IN_SUB_TILE)]


def _split_refs(prev_transposed, refs, n_in):
    if prev_transposed is None:
        return None, refs[:n_in], refs[n_in:]
    prev_in, ins, (hnew_ref, *outs) = refs[:4], refs[4:4 + n_in], refs[4 + n_in:]
    return (*prev_in, hnew_ref, prev_transposed), ins, outs


def _pending_specs(pending, row, col, full):
    if pending is None:
        return [], []
    o_main, o_mem, w_out, w_layer, post_g, transposed = pending
    main = col(HG_VAL_WIDTH) if transposed else row(HG_VAL_WIDTH)
    return [o_main, o_mem, w_out, post_g], [main, row(MEM_WIDTH), _layer_weight(w_out, w_layer), full(post_g)]


ROPE_FREQS = SWA_HEAD_DIM // 2
ROPE_PACK = LANES // ROPE_FREQS


def _rope_table_kernel(pos_ref, invf_ref, cos_ref, sin_ref):
    ang = pos_ref[...].astype(F32) * invf_ref[...]
    rows = ang.shape[0]
    src = lax.broadcasted_iota(jnp.int32, (LANES, LANES), 0)
    dst = lax.broadcasted_iota(jnp.int32, (LANES, LANES), 1)
    sin_sign = jnp.where((dst & ROPE_FREQS) == 0, -1.0, 1.0)
    for table, out_ref, sign in ((jnp.cos(ang), cos_ref, 1.0), (jnp.sin(ang), sin_ref, sin_sign)):
        hi = table.astype(BF16)
        rest = table - hi.astype(F32)
        mid = rest.astype(BF16)
        lo = (rest - mid.astype(F32)).astype(BF16)
        for i in range(ROPE_PACK):
            pick = src == i * ROPE_FREQS + (dst & (ROPE_FREQS - 1))
            spread = jnp.where(pick, sign, 0.0).astype(BF16)
            out_ref[pl.ds(i, rows, stride=ROPE_PACK), :] = _dot(hi, spread) + _dot(mid, spread) + _dot(lo, spread)


def _rope_tables(positions):
    n = positions.size
    inv_freq = ROPE_THETA ** (-jnp.arange(0, SWA_HEAD_DIM, 2, dtype=F32) / SWA_HEAD_DIM)
    invf = jnp.tile(inv_freq, ROPE_PACK).reshape(1, LANES)
    pos = jnp.repeat(positions.reshape(n // ROPE_PACK, ROPE_PACK), ROPE_FREQS, axis=1)
    tile = 2048
    row = pl.BlockSpec((tile, LANES), lambda i: (i, 0))
    return pl.pallas_call(
        _rope_table_kernel,
        grid=(n // tile,),
        in_specs=[pl.BlockSpec((tile // ROPE_PACK, LANES), lambda i: (i, 0)),
                  pl.BlockSpec((1, LANES), lambda i: (0, 0))],
        out_specs=[row, row],
        out_shape=[jax.ShapeDtypeStruct((n, LANES), F32)] * 2,
        compiler_params=_params("parallel"),
        name="rope_tables",
    )(pos, invf)


def _rope_block(xc, cos, sin, first_half):
    swapped = jnp.where(first_half,
                        pltpu.roll(xc, LANES - SWA_HEAD_DIM // 2, 1),
                        pltpu.roll(xc, SWA_HEAD_DIM // 2, 1))
    return xc * cos + swapped * sin


def _first_half_mask(rows):
    lane = lax.broadcasted_iota(jnp.int32, (rows, LANES), 1)
    return (lane & (SWA_HEAD_DIM // 2)) == 0


def _mem_kv_kernel(mem_ref, g_ref, w_ref, k_ref, v_ref):
    mn = _rms_norm(mem_ref[...], g_ref[...]).astype(BF16)
    kv = _dot(mn, w_ref[...].astype(BF16))
    k_ref[...] = kv[:, :MEM_WIDTH].astype(BF16)
    v_ref[...] = kv[:, MEM_WIDTH:].astype(BF16)


def _mem_kv(mem, mem_norm_g, w_mem_kv):
    b = mem.shape[0]
    out = pl.BlockSpec((None, None, MEM_TOKENS, MEM_WIDTH), lambda l, i: (l, i, 0, 0))
    return pl.pallas_call(
        _mem_kv_kernel,
        grid=(DEPTH, b),
        in_specs=[
            pl.BlockSpec((None, MEM_TOKENS, D_MODEL), lambda l, i: (i, 0, 0)),
            pl.BlockSpec((None, 1, D_MODEL), lambda l, i: (l, 0, 0)),
            pl.BlockSpec((None, D_MODEL, 2 * MEM_WIDTH), lambda l, i: (l, 0, 0)),
        ],
        out_specs=[out, out],
        out_shape=[jax.ShapeDtypeStruct((DEPTH, b, MEM_TOKENS, MEM_WIDTH), BF16)] * 2,
        compiler_params=_params("arbitrary", "arbitrary"),
        name="mem_kv",
    )(mem, mem_norm_g.reshape(DEPTH, 1, D_MODEL), w_mem_kv)


def _memory_probs(mq, mk_ref):
    mq = (mq * (MEM_HEAD_DIM ** -0.5 * LOG2E)).astype(BF16)
    probs, inv_sums = [], []
    for hh in range(MEM_HEADS):
        hs = slice(hh * MEM_HEAD_DIM, (hh + 1) * MEM_HEAD_DIM)
        s = _dot_nt(mq[:, hs], mk_ref[:, hs])
        p = jnp.exp2(s - jnp.max(s, axis=-1, keepdims=True))
        probs.append(p.astype(BF16))
        inv_sums.append(1.0 / jnp.sum(p, axis=-1, keepdims=True))
    return probs, inv_sums


def _memory_readout(probs, inv_sums, gate, mv_ref, out_ref, rows):
    for hh in range(MEM_HEADS):
        hs = slice(hh * MEM_HEAD_DIM, (hh + 1) * MEM_HEAD_DIM)
        o = _dot(probs[hh], mv_ref[:, hs]) * inv_sums[hh]
        out_ref[rows, hs] = (o * gate[:, hs]).astype(BF16)


def _a_in_kernel(layer, prev_transposed, h_ref, *refs):
    prev, (g_ref, w_ref, lbl_ref, mk_ref, mv_ref), outs = _split_refs(prev_transposed, refs, 5)
    qs_ref, lf_ref, kk_ref, v_ref, gm_ref, om_ref = outs

    logits = lbl_ref[...]
    e = jnp.exp(logits - jnp.max(logits, axis=0, keepdims=True))
    lb = jnp.sum(e[:layer + 1], axis=0, keepdims=True) / jnp.sum(e, axis=0, keepdims=True)
    mid = 0.5 * (1.0 + lb)
    half = 0.5 * (1.0 - lb)

    col_q, col_f, col_v = 0, HG_KEY_WIDTH, 2 * HG_KEY_WIDTH
    col_g = col_v + HG_VAL_WIDTH
    col_mq = col_g + HG_VAL_WIDTH
    col_mg = col_mq + MEM_WIDTH

    for sub, xn in enumerate(_normed_sub_tiles(prev, h_ref, g_ref)):
        rows = slice(sub * IN_SUB_TILE, (sub + 1) * IN_SUB_TILE)

        def proj(lo, width):
            return _dot(xn, w_ref[:, lo:lo + width])

        mq = proj(col_mq, MEM_WIDTH)
        mem_gate = _silu(proj(col_mg, MEM_WIDTH))
        qs_ref[rows, :] = _silu(proj(col_q, HG_KEY_WIDTH)).astype(BF16)
        probs, inv_sums = _memory_probs(mq, mk_ref)

        th = half * jnp.tanh(0.5 * proj(col_f, HG_KEY_WIDTH))
        lf_ref[rows, :] = jnp.log2(mid + th)
        kk_ref[rows, :] = (half - th).astype(BF16)

        _memory_readout(probs, inv_sums, mem_gate, mv_ref, om_ref, rows)
        v_ref[rows, :] = proj(col_v, HG_VAL_WIDTH).astype(BF16)
        gm_ref[rows, :] = _silu(proj(col_g, HG_VAL_WIDTH)).astype(BF16)


def _a_in_proj(layer, h, pending, g, w, lb_logits, mem_k, mem_v, tiles_per_seq):
    n = h.shape[0]
    row = lambda width: pl.BlockSpec((IN_TILE, width), lambda i: (i, 0))
    col = lambda width: pl.BlockSpec((width, IN_TILE), lambda i: (0, i))
    full = lambda a: pl.BlockSpec(a.shape, lambda i: (0,) * a.ndim)
    mem = pl.BlockSpec((None, None, MEM_TOKENS, MEM_WIDTH), lambda i: (layer, i // tiles_per_seq, 0, 0))
    widths = (HG_KEY_WIDTH, HG_KEY_WIDTH, HG_KEY_WIDTH, HG_VAL_WIDTH, HG_VAL_WIDTH, MEM_WIDTH)
    dtypes = (BF16, F32, BF16, BF16, BF16, BF16)
    if pending is not None:
        widths, dtypes = (D_MODEL,) + widths, (F32,) + dtypes
    pend_args, pend_specs = _pending_specs(pending, row, col, full)
    return pl.pallas_call(
        functools.partial(_a_in_kernel, layer, None if pending is None else pending[-1]),
        grid=(n // IN_TILE,),
        in_specs=[row(D_MODEL)] + pend_specs + [full(g), _layer_weight(w, layer), full(lb_logits), mem, mem],
        out_specs=[row(wd) for wd in widths],
        out_shape=[jax.ShapeDtypeStruct((n, wd), dt) for wd, dt in zip(widths, dtypes)],
        compiler_params=_params("parallel"),
        name=f"a_in_proj_{layer}",
    )(h, *pend_args, g, w, lb_logits, mem_k, mem_v)


_LEVELS = (32, 16, 8, 4, 2, 1)
_SAFE_BLOCKS = (HG_CHUNK, 16)
SAFE_BLOCK_DECAY = 100.0


def _coarse_level(cum, qs, kk, m):
    c, w = cum.shape
    early = m % (2 * SUBLANES) == 0
    dt = BF16 if early else F32
    zeros = jnp.zeros((m, w), dt)
    qd, kd = [], []
    for blk in range(c // (2 * m)):
        lo, mid, hi = blk * 2 * m, blk * 2 * m + m, (blk + 1) * 2 * m
        ref = cum[mid - 1:mid, :]
        kd += [(kk[lo:mid] * jnp.exp2(ref - cum[lo:mid])).astype(dt), zeros]
        qd += [zeros, (qs[mid:hi] * jnp.exp2(cum[mid:hi] - ref)).astype(dt)]
    return jnp.concatenate(qd, axis=0).astype(BF16), jnp.concatenate(kd, axis=0).astype(BF16)


def _fine_level(cum3, qs3, kk3, m, upper_pen, lower_pen):
    groups, _, w = cum3.shape
    parts = [jnp.broadcast_to(cum3[:, blk * 2 * m + m - 1:blk * 2 * m + m, :], (groups, 2 * m, w))
             for blk in range(SUBLANES // (2 * m))]
    x = cum3 - (parts[0] if len(parts) == 1 else jnp.concatenate(parts, axis=1))
    qd = qs3 * jnp.exp2(x + upper_pen)
    kd = kk3 * jnp.exp2(lower_pen - x)
    c = groups * SUBLANES
    return qd.reshape(c, w).astype(BF16), kd.reshape(c, w).astype(BF16)


def _block_level(cum, qs, kk, block):
    c, _ = cum.shape
    local = [cum[:block]] + [cum[b:b + block] - cum[b - 1:b] for b in range(block, c, block)]
    local = local[0] if len(local) == 1 else jnp.concatenate(local, axis=0)
    return (qs * jnp.exp2(local)).astype(BF16), (kk * jnp.exp2(-local)).astype(BF16)


def _hgrn_constants():
    c = HG_CHUNK
    row = lax.broadcasted_iota(jnp.int32, (c, c), 0)
    col = lax.broadcasted_iota(jnp.int32, (c, c), 1)
    sub = lax.broadcasted_iota(jnp.int32, (1, SUBLANES, HG_DK), 1)
    fine = [m for m in _LEVELS if m < SUBLANES]
    same_block = {m: (row // (2 * m)) == (col // (2 * m)) for m in _LEVELS if 2 * m < c}
    return dict(
        same_block={m: mask.astype(F32) for m, mask in same_block.items()},
        diag=(row == col).astype(F32),
        causal_block={b: ((col <= row) if b == c else same_block[b // 2] & (col <= row)).astype(F32)
                      for b in _SAFE_BLOCKS},
        upper_pen={m: jnp.where((sub & m) != 0, 0.0, MASK_VALUE) for m in fine},
        lower_pen={m: jnp.where((sub & m) != 0, MASK_VALUE, 0.0) for m in fine},
        even_row=jnp.where((sub & 1) != 0, 0.0, 1.0),
    )


def _hgrn_tile(block, k, qs_ref, lf_ref, kk_ref, v_ref, gm_ref, gn, cum_s, o_ref, st_ref):
    c = HG_CHUNK
    heads = [slice(h * HG_DK, (h + 1) * HG_DK) for h in range(HG_HEADS)]
    shape3 = (c // SUBLANES, SUBLANES, HG_DK)
    n_chunks = HG_TILE // c

    def scores_pass(ci):
        rows = slice(ci * c, (ci + 1) * c)
        scores_bf, q_ins, k_outs, lasts = [], [], [], []
        for hs in heads:
            cum = cum_s[rows, hs]
            qs = qs_ref[rows, hs].astype(F32)
            kk = kk_ref[rows, hs].astype(F32)

            pairs = []
            for m in _LEVELS:
                if block is not None and 2 * m <= block:
                    continue
                if m >= SUBLANES:
                    qd, kd = _coarse_level(cum, qs, kk, m)
                elif m == 1:
                    qs3, kk3 = qs.reshape(shape3), kk.reshape(shape3)
                    qd3 = qs3 * jnp.exp2(lf_ref[rows, hs].reshape(shape3) + k["upper_pen"][m])
                    qd = qd3.reshape(c, HG_DK).astype(BF16)
                    kd = (kk3 * k["even_row"]).reshape(c, HG_DK).astype(BF16)
                else:
                    qd, kd = _fine_level(cum.reshape(shape3), qs.reshape(shape3), kk.reshape(shape3), m,
                                         k["upper_pen"][m], k["lower_pen"][m])
                pairs.append((qd, kd, k["same_block"].get(m)))
            last = cum[c - 1:c, :]
            q_in = (qs * jnp.exp2(cum)).astype(BF16)
            if block == c:
                k_side = kk * jnp.exp2(-cum)
                pairs.append((q_in, k_side.astype(BF16), k["causal_block"][block]))
                k_out = (k_side * jnp.exp2(last)).astype(BF16)
            else:
                if block is not None:
                    pairs.append(_block_level(cum, qs, kk, block) + (k["causal_block"][block],))
                else:
                    pairs.append((qs.astype(BF16), kk.astype(BF16), k["diag"]))
                k_out = (kk * jnp.exp2(last - cum)).astype(BF16)

            scores = None
            for qd, kd, mask in pairs:
                s = _dot_nt(qd, kd)
                if mask is not None:
                    s = s * mask
                scores = s if scores is None else scores + s

            scores_bf.append(scores.astype(BF16))
            q_ins.append(q_in)
            k_outs.append(k_out)
            lasts.append(last)
        return scores_bf, q_ins, k_outs, lasts

    def recurrence_pass(ci, scores_bf, q_ins, k_outs, lasts):
        rows = slice(ci * c, (ci + 1) * c)
        for h, hs in enumerate(heads):
            o = _dot_nt(q_ins[h], st_ref[h].astype(BF16)) + _dot(scores_bf[h], v_ref[rows, hs])
            o = o * lax.rsqrt(jnp.mean(o * o, axis=-1, keepdims=True) + NORM_EPS)
            o_ref[rows, hs] = (o * gn[:, hs] * gm_ref[rows, hs].astype(F32)).astype(BF16)

        for h, hs in enumerate(heads):
            st_ref[h] = st_ref[h] * jnp.exp2(lasts[h]) + _dot_tn(v_ref[rows, hs], k_outs[h])

    ahead = scores_pass(0)
    for ci in range(n_chunks):
        current, ahead = ahead, (scores_pass(ci + 1) if ci + 1 < n_chunks else None)
        recurrence_pass(ci, *current)


def _hgrn_kernel(qs_ref, lf_ref, kk_ref, v_ref, gm_ref, gn_ref, o_ref, cum_s, st_ref):
    c = HG_CHUNK

    @pl.when(pl.program_id(1) == 0)
    def _():
        st_ref[...] = jnp.zeros_like(st_ref)

    row = lax.broadcasted_iota(jnp.int32, (c, c), 0)
    col = lax.broadcasted_iota(jnp.int32, (c, c), 1)
    tri = (col <= row).astype(BF16)
    tri2 = jnp.concatenate([tri, tri], axis=1)

    worst = {b: None for b in _SAFE_BLOCKS}
    for ci in range(HG_TILE // c):
        rows = slice(ci * c, (ci + 1) * c)
        lf = lf_ref[rows, :]
        hi = lf.astype(BF16)
        lo = (lf - hi.astype(F32)).astype(BF16)
        cum = _dot(tri2, jnp.concatenate([hi, lo], axis=0))
        cum_s[rows, :] = cum
        for block in _SAFE_BLOCKS:
            ends = [cum[b - 1:b, :] for b in range(block, c + 1, block)]
            for b, end in enumerate(ends):
                decay = end if b == 0 else end - ends[b - 1]
                worst[block] = decay if worst[block] is None else jnp.minimum(worst[block], decay)
    safe = [jnp.min(worst[b]) >= -SAFE_BLOCK_DECAY for b in _SAFE_BLOCKS]

    consts = _hgrn_constants()
    args = (consts, qs_ref, lf_ref, kk_ref, v_ref, gm_ref, gn_ref[...], cum_s, o_ref, st_ref)

    not_yet = None
    for block, ok in zip(_SAFE_BLOCKS + (None,), safe + [True]):
        take = ok if not_yet is None else jnp.logical_and(not_yet, ok)

        @pl.when(take)
        def _(block=block):
            _hgrn_tile(block, *args)

        if block is not None:
            not_yet = jnp.logical_not(ok) if not_yet is None else jnp.logical_and(not_yet, jnp.logical_not(ok))


def _hgrn(qs, lf, kk, v, gm, gn, batch):
    n = qs.shape[0]
    tiles = n // batch // HG_TILE
    row = pl.BlockSpec((HG_TILE, HG_KEY_WIDTH), lambda b, t: (b * tiles + t, 0))
    return pl.pallas_call(
        _hgrn_kernel,
        grid=(batch, tiles),
        in_specs=[row, row, row, row, row, pl.BlockSpec((1, HG_VAL_WIDTH), lambda b, t: (0, 0))],
        out_specs=row,
        out_shape=jax.ShapeDtypeStruct((n, HG_VAL_WIDTH), BF16),
        scratch_shapes=[pltpu.VMEM((HG_TILE, HG_KEY_WIDTH), F32),
                        pltpu.VMEM((HG_HEADS, HG_DV, HG_DK), F32)],
        compiler_params=_params("arbitrary", "arbitrary"),
        name="hgrn2",
    )(qs, lf, kk, v, gm, gn)


def _out_kernel(main_transposed, om_ref, ome_ref, w_ref, g_ref, h_ref, o_ref):
    main_dot = _dot_tn if main_transposed else _dot
    y = main_dot(om_ref[...], w_ref[:HG_VAL_WIDTH, :]) + _dot(ome_ref[...], w_ref[HG_VAL_WIDTH:, :])
    o_ref[...] = h_ref[...] + _rms_norm(y, g_ref[...])


def _out_proj(o_main, o_mem, w, layer, g, h, name, main_transposed=False):
    n = h.shape[0]
    row = lambda width: pl.BlockSpec((OUT_TILE, width), lambda i: (i, 0))
    col = lambda width: pl.BlockSpec((width, OUT_TILE), lambda i: (0, i))
    full = lambda a: pl.BlockSpec(a.shape, lambda i: (0,) * a.ndim)
    main = col(HG_VAL_WIDTH) if main_transposed else row(HG_VAL_WIDTH)
    return pl.pallas_call(
        functools.partial(_out_kernel, main_transposed),
        grid=(n // OUT_TILE,),
        in_specs=[main, row(MEM_WIDTH), _layer_weight(w, layer), full(g), row(D_MODEL)],
        out_specs=row(D_MODEL),
        out_shape=jax.ShapeDtypeStruct((n, D_MODEL), F32),
        compiler_params=_params("parallel"),
        name=name,
    )(o_main, o_mem, w, g, h)


def _shared_kv_kernel(x_ref, g_ref, w_ref, cos_ref, sin_ref, k_ref, v_ref):
    xn = _rms_norm(x_ref[...], g_ref[...]).astype(BF16)
    kv = _dot(xn, w_ref[...])
    cos = cos_ref[...]
    sin = sin_ref[...]
    first = _first_half_mask(IN_TILE)
    heads_per_block = LANES // SWA_HEAD_DIM
    for cb in range(SWA_KV_WIDTH // LANES):
        kr = _rope_block(kv[:, cb * LANES:(cb + 1) * LANES], cos, sin, first).astype(BF16)
        for j in range(heads_per_block):
            k_ref[cb * heads_per_block + j] = kr[:, j * SWA_HEAD_DIM:(j + 1) * SWA_HEAD_DIM]
    v_ref[...] = kv[:, SWA_KV_WIDTH:].T.astype(BF16)


def _shared_kv(h, g, w, cos, sin):
    n = h.shape[0]
    row = lambda width: pl.BlockSpec((IN_TILE, width), lambda i: (i, 0))
    full = lambda a: pl.BlockSpec(a.shape, lambda i: (0,) * a.ndim)
    return pl.pallas_call(
        _shared_kv_kernel,
        grid=(n // IN_TILE,),
        in_specs=[row(D_MODEL), full(g), full(w), row(LANES), row(LANES)],
        out_specs=[pl.BlockSpec((SWA_KV_HEADS, IN_TILE, SWA_HEAD_DIM), lambda i: (0, i, 0)),
                   pl.BlockSpec((SWA_KV_WIDTH, IN_TILE), lambda i: (0, i))],
        out_shape=[jax.ShapeDtypeStruct((SWA_KV_HEADS, n, SWA_HEAD_DIM), BF16),
                   jax.ShapeDtypeStruct((SWA_KV_WIDTH, n), BF16)],
        compiler_params=_params("parallel"),
        name="shared_kv",
    )(h, g, w, cos, sin)


def _b_in_kernel(prev_transposed, h_ref, *refs):
    prev, (g_ref, w_ref, cos_ref, sin_ref, mk_ref, mv_ref), outs = _split_refs(prev_transposed, refs, 6)
    qt_ref, gmt_ref, om_ref = outs
    first = _first_half_mask(IN_SUB_TILE)
    scale = SWA_HEAD_DIM ** -0.5 * LOG2E
    col_mq = 2 * SWA_WIDTH
    col_mg = col_mq + MEM_WIDTH

    for sub, xn in enumerate(_normed_sub_tiles(prev, h_ref, g_ref)):
        rows = slice(sub * IN_SUB_TILE, (sub + 1) * IN_SUB_TILE)

        def proj(lo, width):
            return _dot(xn, w_ref[:, lo:lo + width])

        mq = proj(col_mq, MEM_WIDTH)
        mem_gate = _silu(proj(col_mg, MEM_WIDTH))

        q = proj(0, SWA_WIDTH)
        probs, inv_sums = _memory_probs(mq, mk_ref)
        cos = cos_ref[rows, :]
        sin = sin_ref[rows, :]
        for cb in range(SWA_WIDTH // LANES):
            cs = slice(cb * LANES, (cb + 1) * LANES)
            qt_ref[cs, rows] = (_rope_block(q[:, cs], cos, sin, first) * scale).T.astype(BF16)

        gate = _silu(proj(SWA_WIDTH, SWA_WIDTH))
        _memory_readout(probs, inv_sums, mem_gate, mv_ref, om_ref, rows)
        for cb in range(SWA_WIDTH // LANES):
            cs = slice(cb * LANES, (cb + 1) * LANES)
            gmt_ref[cs, rows] = gate[:, cs].T.astype(BF16)


def _b_in_proj(layer, h, pending, g, w, cos, sin, mem_k, mem_v, tiles_per_seq):
    n = h.shape[0]
    row = lambda width: pl.BlockSpec((IN_TILE, width), lambda i: (i, 0))
    col = lambda width: pl.BlockSpec((width, IN_TILE), lambda i: (0, i))
    full = lambda a: pl.BlockSpec(a.shape, lambda i: (0,) * a.ndim)
    mem = pl.BlockSpec((None, None, MEM_TOKENS, MEM_WIDTH), lambda i: (layer, i // tiles_per_seq, 0, 0))
    out_specs = [col(SWA_WIDTH), col(SWA_WIDTH), row(MEM_WIDTH)]
    out_shape = [jax.ShapeDtypeStruct((SWA_WIDTH, n), BF16), jax.ShapeDtypeStruct((SWA_WIDTH, n), BF16),
                 jax.ShapeDtypeStruct((n, MEM_WIDTH), BF16)]
    if pending is not None:
        out_specs, out_shape = [row(D_MODEL)] + out_specs, [jax.ShapeDtypeStruct((n, D_MODEL), F32)] + out_shape
    pend_args, pend_specs = _pending_specs(pending, row, col, full)
    return pl.pallas_call(
        functools.partial(_b_in_kernel, None if pending is None else pending[-1]),
        grid=(n // IN_TILE,),
        in_specs=[row(D_MODEL)] + pend_specs
        + [full(g), _layer_weight(w, layer - N_A_LAYERS), row(LANES), row(LANES), mem, mem],
        out_specs=out_specs,
        out_shape=out_shape,
        compiler_params=_params("parallel"),
        name=f"b_in_proj_{layer}",
    )(h, *pend_args, g, w, cos, sin, mem_k, mem_v)


def _swa_kernel(sink_ref, qt_ref, kc_ref, kp_ref, vtc_ref, vtp_ref, gmt_ref, ot_ref):
    blk = WINDOW
    dh = SWA_HEAD_DIM
    ki = lax.broadcasted_iota(jnp.int32, (blk, blk), 0)
    qi = lax.broadcasted_iota(jnp.int32, (blk, blk), 1)
    from_prev = ki > qi
    from_prev_bf = from_prev.astype(BF16)
    prev_bias = jnp.where(pl.program_id(1) > 0, 0.0, MASK_VALUE)

    for sb in range(SWA_TILE // blk):
        cols = slice(sb * blk, (sb + 1) * blk)
        two = slice((sb - 1) * blk, (sb + 1) * blk)

        probs, inv_den = [], []
        for g in range(SWA_KV_HEADS):
            k2 = jnp.concatenate([kp_ref[g], kc_ref[g, cols, :]], axis=0) if sb == 0 else kc_ref[g, two, :]
            heads = range(g * SWA_GROUP, (g + 1) * SWA_GROUP)
            qt = jnp.concatenate([qt_ref[h * dh:(h + 1) * dh, cols] for h in heads], axis=1)
            st = _dot(k2, qt)
            ps = []
            for j, h in enumerate(heads):
                s_prev = st[:blk, j * blk:(j + 1) * blk]
                if sb == 0:
                    s_prev = s_prev + prev_bias
                s = jnp.where(from_prev, s_prev, st[blk:, j * blk:(j + 1) * blk])
                sink = sink_ref[h] * LOG2E
                m = jnp.maximum(jnp.max(s, axis=0, keepdims=True), sink)
                p = jnp.exp2(s - m)
                inv_den.append(1.0 / (jnp.sum(p, axis=0, keepdims=True) + jnp.exp2(sink - m)))
                p = p.astype(BF16)
                p_prev = p * from_prev_bf
                ps.append(jnp.concatenate([p_prev, p - p_prev], axis=0))
            probs.append(jnp.concatenate(ps, axis=1))

        for g in range(SWA_KV_HEADS):
            vrows = slice(g * dh, (g + 1) * dh)
            vt2 = (jnp.concatenate([vtp_ref[vrows, :], vtc_ref[vrows, cols]], axis=1) if sb == 0
                   else vtc_ref[vrows, two])
            ot = _dot(vt2, probs[g])
            for j, h in enumerate(range(g * SWA_GROUP, (g + 1) * SWA_GROUP)):
                rows = slice(h * dh, (h + 1) * dh)
                o = ot[:, j * blk:(j + 1) * blk] * inv_den[h]
                ot_ref[rows, cols] = (o * gmt_ref[rows, cols].astype(F32)).astype(BF16)


def _swa(sinks, qt, k, vt, gmt, batch):
    n = qt.shape[1]
    tiles = n // batch // SWA_TILE
    sub = SWA_TILE // WINDOW
    tile = lambda b, t: b * tiles + t
    prev = lambda b, t: jnp.maximum(tile(b, t) * sub - 1, 0)
    wide = pl.BlockSpec((SWA_WIDTH, SWA_TILE), lambda b, t: (0, tile(b, t)))
    return pl.pallas_call(
        _swa_kernel,
        grid=(batch, tiles),
        in_specs=[pl.BlockSpec(memory_space=pltpu.SMEM),
                  wide,
                  pl.BlockSpec((SWA_KV_HEADS, SWA_TILE, SWA_HEAD_DIM), lambda b, t: (0, tile(b, t), 0)),
                  pl.BlockSpec((SWA_KV_HEADS, WINDOW, SWA_HEAD_DIM), lambda b, t: (0, prev(b, t), 0)),
                  pl.BlockSpec((SWA_KV_WIDTH, SWA_TILE), lambda b, t: (0, tile(b, t))),
                  pl.BlockSpec((SWA_KV_WIDTH, WINDOW), lambda b, t: (0, prev(b, t))),
                  wide],
        out_specs=wide,
        out_shape=jax.ShapeDtypeStruct((SWA_WIDTH, n), BF16),
        compiler_params=_params("parallel", "arbitrary"),
        name="swa",
    )(sinks, qt, k, k, vt, vt, gmt)


def kernel(x, mem, positions, pre_norm_g, post_norm_g, mem_norm_g, w_mem_kv, a_w_in, a_lb_logits, a_out_norm_g, a_w_out, kv_norm_g, w_kv_shared, b_w_in, b_sinks, b_w_out):
    batch, seq, _ = x.shape
    n = batch * seq
    assert seq % max(IN_TILE, OUT_TILE, HG_TILE, SWA_TILE) == 0
    tiles_per_seq = seq // IN_TILE

    h = x.reshape(n, D_MODEL)
    mem_k, mem_v = _mem_kv(mem, mem_norm_g, w_mem_kv)
    cos, sin = _rope_tables(positions)
    a_w_in = a_w_in.astype(BF16)
    a_w_out = a_w_out.astype(BF16)
    b_w_in = b_w_in.astype(BF16)
    b_w_out = b_w_out.astype(BF16)
    k_sh = v_sh = None
    pending = None

    for layer in range(DEPTH):
        pre_g = pre_norm_g[layer].reshape(1, D_MODEL)
        post_g = post_norm_g[layer].reshape(1, D_MODEL)
        if layer < N_A_LAYERS:
            outs = _a_in_proj(layer, h, pending, pre_g, a_w_in, a_lb_logits, mem_k, mem_v, tiles_per_seq)
            if pending is not None:
                h, *outs = outs
            qs, lf, kk, v, gm, o_mem = outs
            o_main = _hgrn(qs, lf, kk, v, gm, a_out_norm_g[layer].reshape(1, HG_VAL_WIDTH), batch)
            pending = (o_main, o_mem, a_w_out, layer, post_g, False)
        else:
            j = layer - N_A_LAYERS
            outs = _b_in_proj(layer, h, pending, pre_g, b_w_in, cos, sin, mem_k, mem_v, tiles_per_seq)
            if pending is not None:
                h, *outs = outs
            qt, gmt, o_mem = outs
            if k_sh is None:
                k_sh, v_sh = _shared_kv(h, kv_norm_g.reshape(1, D_MODEL), w_kv_shared.astype(BF16), cos, sin)
            o_main_t = _swa(b_sinks[j], qt, k_sh, v_sh, gmt, batch)
            pending = (o_main_t, o_mem, b_w_out, j, post_g, True)
    o_main, o_mem, w_out, w_layer, post_g, transposed = pending
    h = _out_proj(o_main, o_mem, w_out, w_layer, post_g, h, "out_proj_last", main_transposed=transposed)
    return h.reshape(batch, seq, D_MODEL)
```

```python
import functools

import jax
import jax.numpy as jnp
from jax import lax
from jax.experimental import pallas as pl
from jax.experimental.pallas import tpu as pltpu

D_MODEL = 1024
DEPTH = 4
N_A_LAYERS = DEPTH // 2

HG_HEADS = 8
HG_DK = 128
HG_DV = D_MODEL // HG_HEADS
HG_KEY_WIDTH = HG_HEADS * HG_DK
HG_VAL_WIDTH = HG_HEADS * HG_DV
HG_CHUNK = 64

SWA_Q_HEADS = 16
SWA_KV_HEADS = 4
SWA_GROUP = SWA_Q_HEADS // SWA_KV_HEADS
SWA_HEAD_DIM = 64
SWA_WIDTH = SWA_Q_HEADS * SWA_HEAD_DIM
SWA_KV_WIDTH = SWA_KV_HEADS * SWA_HEAD_DIM
WINDOW = 128

MEM_TOKENS = 256
MEM_HEADS = 4
MEM_HEAD_DIM = 128
MEM_WIDTH = MEM_HEADS * MEM_HEAD_DIM

ROPE_THETA = 10000.0
NORM_EPS = 1e-6

LANES = 128
SUBLANES = 8
MASK_VALUE = -1e30
LOG2E = 1.4426950408889634

IN_TILE = 512
IN_SUB_TILE = 256
OUT_TILE = 1024
HG_TILE = 512
SWA_TILE = 1024
ROPE_TILE = 2048
VMEM_LIMIT = 56 * 1024 * 1024

F32 = jnp.float32
BF16 = jnp.bfloat16

_NT = (((1,), (1,)), ((), ()))
_TN = (((0,), (0,)), ((), ()))


def _dot(a, b):
    return jnp.dot(a, b, preferred_element_type=F32)


def _dot_nt(a, b):
    return lax.dot_general(a, b, _NT, preferred_element_type=F32)


def _dot_tn(a, b):
    return lax.dot_general(a, b, _TN, preferred_element_type=F32)


def _rms_norm(x, g):
    ms = jnp.mean(x * x, axis=-1, keepdims=True)
    return x * lax.rsqrt(ms + NORM_EPS) * g


def _silu(x):
    h = 0.5 * x
    return h + h * jnp.tanh(h)


def _params(*semantics):
    return pltpu.CompilerParams(dimension_semantics=semantics, vmem_limit_bytes=VMEM_LIMIT)


def _layer_weight(w, layer):
    return pl.BlockSpec((None,) + w.shape[1:], lambda *_: (layer, 0, 0), pipeline_mode=pl.Buffered(1))


def _layer_input(prev, h_ref, rows):
    if prev is None:
        return h_ref[rows, :]
    om_ref, ome_ref, wout_ref, pg_ref, hnew_ref, main_transposed = prev
    if main_transposed:
        y = _dot_tn(om_ref[:, rows], wout_ref[:HG_VAL_WIDTH, :])
    else:
        y = _dot(om_ref[rows, :], wout_ref[:HG_VAL_WIDTH, :])
    y = y + _dot(ome_ref[rows, :], wout_ref[HG_VAL_WIDTH:, :])
    h_new = h_ref[rows, :] + _rms_norm(y, pg_ref[...])
    hnew_ref[rows, :] = h_new
    return h_new


def _normed_sub_tiles(prev, h_ref, g_ref):
    return [_rms_norm(_layer_input(prev, h_ref, slice(s * IN_SUB_TILE, (s + 1) * IN_SUB_TILE)),
                      g_ref[...]).astype(BF16) for s in range(IN_TILE // IN_SUB_TILE)]


def _split_refs(prev_transposed, refs, n_in):
    if prev_transposed is None:
        return None, refs[:n_in], refs[n_in:]
    prev_in, ins, (hnew_ref, *outs) = refs[:4], refs[4:4 + n_in], refs[4 + n_in:]
    return (*prev_in, hnew_ref, prev_transposed), ins, outs


def _pending_specs(pending, row, col, full):
    if pending is None:
        return [], []
    o_main, o_mem, w_out, w_layer, post_g, transposed = pending
    main = col(HG_VAL_WIDTH) if transposed else row(HG_VAL_WIDTH)
    return [o_main, o_mem, w_out, post_g], [main, row(MEM_WIDTH), _layer_weight(w_out, w_layer), full(post_g)]


ROPE_FREQS = SWA_HEAD_DIM // 2
ROPE_PACK = LANES // ROPE_FREQS


def _rope_table_kernel(pos_ref, invf_ref, cos_ref, sin_ref):
    ang = pos_ref[...].astype(F32) * invf_ref[...]
    rows = ang.shape[0]
    src = lax.broadcasted_iota(jnp.int32, (LANES, LANES), 0)
    dst = lax.broadcasted_iota(jnp.int32, (LANES, LANES), 1)
    sin_sign = jnp.where((dst & ROPE_FREQS) == 0, -1.0, 1.0)
    for table, out_ref, sign in ((jnp.cos(ang), cos_ref, 1.0), (jnp.sin(ang), sin_ref, sin_sign)):
        hi = table.astype(BF16)
        rest = table - hi.astype(F32)
        mid = rest.astype(BF16)
        lo = (rest - mid.astype(F32)).astype(BF16)
        for i in range(ROPE_PACK):
            pick = src == i * ROPE_FREQS + (dst & (ROPE_FREQS - 1))
            spread = jnp.where(pick, sign, 0.0).astype(BF16)
            out_ref[pl.ds(i, rows, stride=ROPE_PACK), :] = _dot(hi, spread) + _dot(mid, spread) + _dot(lo, spread)


def _rope_tables(positions):
    n = positions.size
    inv_freq = ROPE_THETA ** (-jnp.arange(0, SWA_HEAD_DIM, 2, dtype=F32) / SWA_HEAD_DIM)
    invf = jnp.tile(inv_freq, ROPE_PACK).reshape(1, LANES)
    pos = jnp.repeat(positions.reshape(n // ROPE_PACK, ROPE_PACK), ROPE_FREQS, axis=1)
    row = pl.BlockSpec((ROPE_TILE, LANES), lambda i: (i, 0))
    return pl.pallas_call(
        _rope_table_kernel,
        grid=(n // ROPE_TILE,),
        in_specs=[pl.BlockSpec((ROPE_TILE // ROPE_PACK, LANES), lambda i: (i, 0)),
                  pl.BlockSpec((1, LANES), lambda i: (0, 0))],
        out_specs=[row, row],
        out_shape=[jax.ShapeDtypeStruct((n, LANES), F32)] * 2,
        compiler_params=_params("parallel"),
        name="rope_tables",
    )(pos, invf)


def _rope_block(xc, cos, sin, first_half):
    swapped = jnp.where(first_half,
                        pltpu.roll(xc, LANES - SWA_HEAD_DIM // 2, 1),
                        pltpu.roll(xc, SWA_HEAD_DIM // 2, 1))
    return xc * cos + swapped * sin


def _first_half_mask(rows):
    lane = lax.broadcasted_iota(jnp.int32, (rows, LANES), 1)
    return (lane & (SWA_HEAD_DIM // 2)) == 0


def _mem_kv_kernel(mem_ref, g_ref, w_ref, k_ref, v_ref):
    mn = _rms_norm(mem_ref[...], g_ref[...]).astype(BF16)
    kv = _dot(mn, w_ref[...].astype(BF16))
    k_ref[...] = kv[:, :MEM_WIDTH].astype(BF16)
    v_ref[...] = kv[:, MEM_WIDTH:].astype(BF16)


def _mem_kv(mem, mem_norm_g, w_mem_kv):
    b = mem.shape[0]
    out = pl.BlockSpec((None, None, MEM_TOKENS, MEM_WIDTH), lambda l, i: (l, i, 0, 0))
    return pl.pallas_call(
        _mem_kv_kernel,
        grid=(DEPTH, b),
        in_specs=[
            pl.BlockSpec((None, MEM_TOKENS, D_MODEL), lambda l, i: (i, 0, 0)),
            pl.BlockSpec((None, 1, D_MODEL), lambda l, i: (l, 0, 0)),
            pl.BlockSpec((None, D_MODEL, 2 * MEM_WIDTH), lambda l, i: (l, 0, 0)),
        ],
        out_specs=[out, out],
        out_shape=[jax.ShapeDtypeStruct((DEPTH, b, MEM_TOKENS, MEM_WIDTH), BF16)] * 2,
        compiler_params=_params("arbitrary", "arbitrary"),
        name="mem_kv",
    )(mem, mem_norm_g.reshape(DEPTH, 1, D_MODEL), w_mem_kv)


def _memory_probs(mq, mk_ref):
    mq = (mq * (MEM_HEAD_DIM ** -0.5 * LOG2E)).astype(BF16)
    probs, inv_sums = [], []
    for hh in range(MEM_HEADS):
        hs = slice(hh * MEM_HEAD_DIM, (hh + 1) * MEM_HEAD_DIM)
        s = _dot_nt(mq[:, hs], mk_ref[:, hs])
        p = jnp.exp2(s - jnp.max(s, axis=-1, keepdims=True))
        probs.append(p.astype(BF16))
        inv_sums.append(1.0 / jnp.sum(p, axis=-1, keepdims=True))
    return probs, inv_sums


def _memory_readout(probs, inv_sums, gate, mv_ref, out_ref, rows):
    for hh in range(MEM_HEADS):
        hs = slice(hh * MEM_HEAD_DIM, (hh + 1) * MEM_HEAD_DIM)
        o = _dot(probs[hh], mv_ref[:, hs]) * inv_sums[hh]
        out_ref[rows, hs] = (o * gate[:, hs]).astype(BF16)


def _a_in_kernel(layer, prev_transposed, h_ref, *refs):
    prev, (g_ref, w_ref, lbl_ref, mk_ref, mv_ref), outs = _split_refs(prev_transposed, refs, 5)
    qs_ref, lf_ref, kk_ref, v_ref, gm_ref, om_ref = outs

    logits = lbl_ref[...]
    e = jnp.exp(logits - jnp.max(logits, axis=0, keepdims=True))
    lb = jnp.sum(e[:layer + 1], axis=0, keepdims=True) / jnp.sum(e, axis=0, keepdims=True)
    mid = 0.5 * (1.0 + lb)
    half = 0.5 * (1.0 - lb)

    col_q, col_f, col_v = 0, HG_KEY_WIDTH, 2 * HG_KEY_WIDTH
    col_g = col_v + HG_VAL_WIDTH
    col_mq = col_g + HG_VAL_WIDTH
    col_mg = col_mq + MEM_WIDTH

    for sub, xn in enumerate(_normed_sub_tiles(prev, h_ref, g_ref)):
        rows = slice(sub * IN_SUB_TILE, (sub + 1) * IN_SUB_TILE)

        def proj(lo, width):
            return _dot(xn, w_ref[:, lo:lo + width])

        mq = proj(col_mq, MEM_WIDTH)
        mem_gate = _silu(proj(col_mg, MEM_WIDTH))
        qs_ref[rows, :] = _silu(proj(col_q, HG_KEY_WIDTH)).astype(BF16)
        probs, inv_sums = _memory_probs(mq, mk_ref)

        th = half * jnp.tanh(0.5 * proj(col_f, HG_KEY_WIDTH))
        lf_ref[rows, :] = jnp.log2(jnp.maximum(mid + th, lb))
        kk_ref[rows, :] = (half - th).astype(BF16)

        _memory_readout(probs, inv_sums, mem_gate, mv_ref, om_ref, rows)
        v_ref[rows, :] = proj(col_v, HG_VAL_WIDTH).astype(BF16)
        gm_ref[rows, :] = _silu(proj(col_g, HG_VAL_WIDTH)).astype(BF16)


def _a_in_proj(layer, h, pending, g, w, lb_logits, mem_k, mem_v, tiles_per_seq):
    n = h.shape[0]
    row = lambda width: pl.BlockSpec((IN_TILE, width), lambda i: (i, 0))
    col = lambda width: pl.BlockSpec((width, IN_TILE), lambda i: (0, i))
    full = lambda a: pl.BlockSpec(a.shape, lambda i: (0,) * a.ndim)
    mem = pl.BlockSpec((None, None, MEM_TOKENS, MEM_WIDTH), lambda i: (layer, i // tiles_per_seq, 0, 0))
    widths = (HG_KEY_WIDTH, HG_KEY_WIDTH, HG_KEY_WIDTH, HG_VAL_WIDTH, HG_VAL_WIDTH, MEM_WIDTH)
    dtypes = (BF16, F32, BF16, BF16, BF16, BF16)
    if pending is not None:
        widths, dtypes = (D_MODEL,) + widths, (F32,) + dtypes
    pend_args, pend_specs = _pending_specs(pending, row, col, full)
    return pl.pallas_call(
        functools.partial(_a_in_kernel, layer, None if pending is None else pending[-1]),
        grid=(n // IN_TILE,),
        in_specs=[row(D_MODEL)] + pend_specs + [full(g), _layer_weight(w, layer), full(lb_logits), mem, mem],
        out_specs=[row(wd) for wd in widths],
        out_shape=[jax.ShapeDtypeStruct((n, wd), dt) for wd, dt in zip(widths, dtypes)],
        compiler_params=_params("parallel"),
        name=f"a_in_proj_{layer}",
    )(h, *pend_args, g, w, lb_logits, mem_k, mem_v)


_LEVELS = (32, 16, 8, 4, 2, 1)
_SAFE_BLOCKS = (HG_CHUNK, 16)
SAFE_BLOCK_DECAY = 100.0


def _coarse_level(cum, qs, kk, m):
    c, w = cum.shape
    early = m % (2 * SUBLANES) == 0
    dt = BF16 if early else F32
    zeros = jnp.zeros((m, w), dt)
    qd, kd = [], []
    for blk in range(c // (2 * m)):
        lo, mid, hi = blk * 2 * m, blk * 2 * m + m, (blk + 1) * 2 * m
        ref = cum[mid - 1:mid, :]
        kd += [(kk[lo:mid] * jnp.exp2(ref - cum[lo:mid])).astype(dt), zeros]
        qd += [zeros, (qs[mid:hi] * jnp.exp2(cum[mid:hi] - ref)).astype(dt)]
    return jnp.concatenate(qd, axis=0).astype(BF16), jnp.concatenate(kd, axis=0).astype(BF16)


def _fine_level(cum3, qs3, kk3, m, upper_pen, lower_pen):
    groups, _, w = cum3.shape
    parts = [jnp.broadcast_to(cum3[:, blk * 2 * m + m - 1:blk * 2 * m + m, :], (groups, 2 * m, w))
             for blk in range(SUBLANES // (2 * m))]
    x = cum3 - (parts[0] if len(parts) == 1 else jnp.concatenate(parts, axis=1))
    qd = qs3 * jnp.exp2(x + upper_pen)
    kd = kk3 * jnp.exp2(lower_pen - x)
    c = groups * SUBLANES
    return qd.reshape(c, w).astype(BF16), kd.reshape(c, w).astype(BF16)


def _block_level(cum, qs, kk, block):
    c, _ = cum.shape
    local = [cum[:block]] + [cum[b:b + block] - cum[b - 1:b] for b in range(block, c, block)]
    local = local[0] if len(local) == 1 else jnp.concatenate(local, axis=0)
    return (qs * jnp.exp2(local)).astype(BF16), (kk * jnp.exp2(-local)).astype(BF16)


def _hgrn_constants():
    c = HG_CHUNK
    row = lax.broadcasted_iota(jnp.int32, (c, c), 0)
    col = lax.broadcasted_iota(jnp.int32, (c, c), 1)
    sub = lax.broadcasted_iota(jnp.int32, (1, SUBLANES, HG_DK), 1)
    fine = [m for m in _LEVELS if m < SUBLANES]
    same_block = {m: (row // (2 * m)) == (col // (2 * m)) for m in _LEVELS if 2 * m < c}
    return dict(
        same_block={m: mask.astype(F32) for m, mask in same_block.items()},
        diag=(row == col).astype(F32),
        causal_block={b: ((col <= row) if b == c else same_block[b // 2] & (col <= row)).astype(F32)
                      for b in _SAFE_BLOCKS},
        upper_pen={m: jnp.where((sub & m) != 0, 0.0, MASK_VALUE) for m in fine},
        lower_pen={m: jnp.where((sub & m) != 0, MASK_VALUE, 0.0) for m in fine},
        even_row=jnp.where((sub & 1) != 0, 0.0, 1.0),
    )


def _hgrn_tile(block, k, qs_ref, lf_ref, kk_ref, v_ref, gm_ref, gn, cum_s, o_ref, st_ref):
    c = HG_CHUNK
    heads = [slice(h * HG_DK, (h + 1) * HG_DK) for h in range(HG_HEADS)]
    shape3 = (c // SUBLANES, SUBLANES, HG_DK)
    n_chunks = HG_TILE // c

    def scores_pass(ci):
        rows = slice(ci * c, (ci + 1) * c)
        scores_bf, q_ins, k_outs, lasts = [], [], [], []
        for hs in heads:
            cum = cum_s[rows, hs]
            qs = qs_ref[rows, hs].astype(F32)
            kk = kk_ref[rows, hs].astype(F32)

            pairs = []
            for m in _LEVELS:
                if block is not None and 2 * m <= block:
                    continue
                if m >= SUBLANES:
                    qd, kd = _coarse_level(cum, qs, kk, m)
                elif m == 1:
                    qs3, kk3 = qs.reshape(shape3), kk.reshape(shape3)
                    qd3 = qs3 * jnp.exp2(lf_ref[rows, hs].reshape(shape3) + k["upper_pen"][m])
                    qd = qd3.reshape(c, HG_DK).astype(BF16)
                    kd = (kk3 * k["even_row"]).reshape(c, HG_DK).astype(BF16)
                else:
                    qd, kd = _fine_level(cum.reshape(shape3), qs.reshape(shape3), kk.reshape(shape3), m,
                                         k["upper_pen"][m], k["lower_pen"][m])
                pairs.append((qd, kd, k["same_block"].get(m)))
            last = cum[c - 1:c, :]
            q_in = (qs * jnp.exp2(cum)).astype(BF16)
            if block == c:
                k_side = kk * jnp.exp2(-cum)
                pairs.append((q_in, k_side.astype(BF16), k["causal_block"][block]))
                k_out = (k_side * jnp.exp2(last)).astype(BF16)
            else:
                if block is not None:
                    pairs.append(_block_level(cum, qs, kk, block) + (k["causal_block"][block],))
                else:
                    pairs.append((qs.astype(BF16), kk.astype(BF16), k["diag"]))
                k_out = (kk * jnp.exp2(last - cum)).astype(BF16)

            scores = None
            for qd, kd, mask in pairs:
                s = _dot_nt(qd, kd)
                if mask is not None:
                    s = s * mask
                scores = s if scores is None else scores + s

            scores_bf.append(scores.astype(BF16))
            q_ins.append(q_in)
            k_outs.append(k_out)
            lasts.append(last)
        return scores_bf, q_ins, k_outs, lasts

    def recurrence_pass(ci, scores_bf, q_ins, k_outs, lasts):
        rows = slice(ci * c, (ci + 1) * c)
        for h, hs in enumerate(heads):
            o = _dot_nt(q_ins[h], st_ref[h].astype(BF16)) + _dot(scores_bf[h], v_ref[rows, hs])
            o = o * lax.rsqrt(jnp.mean(o * o, axis=-1, keepdims=True) + NORM_EPS)
            o_ref[rows, hs] = (o * gn[:, hs] * gm_ref[rows, hs].astype(F32)).astype(BF16)

        for h, hs in enumerate(heads):
            st_ref[h] = st_ref[h] * jnp.exp2(lasts[h]) + _dot_tn(v_ref[rows, hs], k_outs[h])

    ahead = scores_pass(0)
    for ci in range(n_chunks):
        current, ahead = ahead, (scores_pass(ci + 1) if ci + 1 < n_chunks else None)
        recurrence_pass(ci, *current)


def _hgrn_kernel(qs_ref, lf_ref, kk_ref, v_ref, gm_ref, gn_ref, o_ref, cum_s, st_ref):
    c = HG_CHUNK

    @pl.when(pl.program_id(1) == 0)
    def _():
        st_ref[...] = jnp.zeros_like(st_ref)

    row = lax.broadcasted_iota(jnp.int32, (c, c), 0)
    col = lax.broadcasted_iota(jnp.int32, (c, c), 1)
    tri = (col <= row).astype(BF16)
    tri2 = jnp.concatenate([tri, tri], axis=1)

    worst = {b: None for b in _SAFE_BLOCKS}
    for ci in range(HG_TILE // c):
        rows = slice(ci * c, (ci + 1) * c)
        lf = lf_ref[rows, :]
        hi = lf.astype(BF16)
        lo = (lf - hi.astype(F32)).astype(BF16)
        cum = _dot(tri2, jnp.concatenate([hi, lo], axis=0))
        cum_s[rows, :] = cum
        for block in _SAFE_BLOCKS:
            ends = [cum[b - 1:b, :] for b in range(block, c + 1, block)]
            for b, end in enumerate(ends):
                decay = end if b == 0 else end - ends[b - 1]
                worst[block] = decay if worst[block] is None else jnp.minimum(worst[block], decay)
    safe = [jnp.min(worst[b]) >= -SAFE_BLOCK_DECAY for b in _SAFE_BLOCKS]

    consts = _hgrn_constants()
    args = (consts, qs_ref, lf_ref, kk_ref, v_ref, gm_ref, gn_ref[...], cum_s, o_ref, st_ref)

    not_yet = None
    for block, ok in zip(_SAFE_BLOCKS + (None,), safe + [True]):
        take = ok if not_yet is None else jnp.logical_and(not_yet, ok)

        @pl.when(take)
        def _(block=block):
            _hgrn_tile(block, *args)

        if block is not None:
            not_yet = jnp.logical_not(ok) if not_yet is None else jnp.logical_and(not_yet, jnp.logical_not(ok))


def _hgrn(qs, lf, kk, v, gm, gn, batch):
    n = qs.shape[0]
    tiles = n // batch // HG_TILE
    row = pl.BlockSpec((HG_TILE, HG_KEY_WIDTH), lambda b, t: (b * tiles + t, 0))
    return pl.pallas_call(
        _hgrn_kernel,
        grid=(batch, tiles),
        in_specs=[row, row, row, row, row, pl.BlockSpec((1, HG_VAL_WIDTH), lambda b, t: (0, 0))],
        out_specs=row,
        out_shape=jax.ShapeDtypeStruct((n, HG_VAL_WIDTH), BF16),
        scratch_shapes=[pltpu.VMEM((HG_TILE, HG_KEY_WIDTH), F32),
                        pltpu.VMEM((HG_HEADS, HG_DV, HG_DK), F32)],
        compiler_params=_params("arbitrary", "arbitrary"),
        name="hgrn2",
    )(qs, lf, kk, v, gm, gn)


def _out_kernel(main_transposed, om_ref, ome_ref, w_ref, g_ref, h_ref, o_ref):
    main_dot = _dot_tn if main_transposed else _dot
    y = main_dot(om_ref[...], w_ref[:HG_VAL_WIDTH, :]) + _dot(ome_ref[...], w_ref[HG_VAL_WIDTH:, :])
    o_ref[...] = h_ref[...] + _rms_norm(y, g_ref[...])


def _out_proj(o_main, o_mem, w, layer, g, h, name, main_transposed=False):
    n = h.shape[0]
    row = lambda width: pl.BlockSpec((OUT_TILE, width), lambda i: (i, 0))
    col = lambda width: pl.BlockSpec((width, OUT_TILE), lambda i: (0, i))
    full = lambda a: pl.BlockSpec(a.shape, lambda i: (0,) * a.ndim)
    main = col(HG_VAL_WIDTH) if main_transposed else row(HG_VAL_WIDTH)
    return pl.pallas_call(
        functools.partial(_out_kernel, main_transposed),
        grid=(n // OUT_TILE,),
        in_specs=[main, row(MEM_WIDTH), _layer_weight(w, layer), full(g), row(D_MODEL)],
        out_specs=row(D_MODEL),
        out_shape=jax.ShapeDtypeStruct((n, D_MODEL), F32),
        compiler_params=_params("parallel"),
        name=name,
    )(o_main, o_mem, w, g, h)


def _shared_kv_kernel(x_ref, g_ref, w_ref, cos_ref, sin_ref, k_ref, v_ref):
    xn = _rms_norm(x_ref[...], g_ref[...]).astype(BF16)
    kv = _dot(xn, w_ref[...])
    cos = cos_ref[...]
    sin = sin_ref[...]
    first = _first_half_mask(IN_TILE)
    heads_per_block = LANES // SWA_HEAD_DIM
    for cb in range(SWA_KV_WIDTH // LANES):
        kr = _rope_block(kv[:, cb * LANES:(cb + 1) * LANES], cos, sin, first).astype(BF16)
        for j in range(heads_per_block):
            k_ref[cb * heads_per_block + j] = kr[:, j * SWA_HEAD_DIM:(j + 1) * SWA_HEAD_DIM]
    v_ref[...] = kv[:, SWA_KV_WIDTH:].T.astype(BF16)


def _shared_kv(h, g, w, cos, sin):
    n = h.shape[0]
    row = lambda width: pl.BlockSpec((IN_TILE, width), lambda i: (i, 0))
    full = lambda a: pl.BlockSpec(a.shape, lambda i: (0,) * a.ndim)
    return pl.pallas_call(
        _shared_kv_kernel,
        grid=(n // IN_TILE,),
        in_specs=[row(D_MODEL), full(g), full(w), row(LANES), row(LANES)],
        out_specs=[pl.BlockSpec((SWA_KV_HEADS, IN_TILE, SWA_HEAD_DIM), lambda i: (0, i, 0)),
                   pl.BlockSpec((SWA_KV_WIDTH, IN_TILE), lambda i: (0, i))],
        out_shape=[jax.ShapeDtypeStruct((SWA_KV_HEADS, n, SWA_HEAD_DIM), BF16),
                   jax.ShapeDtypeStruct((SWA_KV_WIDTH, n), BF16)],
        compiler_params=_params("parallel"),
        name="shared_kv",
    )(h, g, w, cos, sin)


def _b_in_kernel(prev_transposed, h_ref, *refs):
    prev, (g_ref, w_ref, cos_ref, sin_ref, mk_ref, mv_ref), outs = _split_refs(prev_transposed, refs, 6)
    qt_ref, gmt_ref, om_ref = outs
    first = _first_half_mask(IN_SUB_TILE)
    scale = SWA_HEAD_DIM ** -0.5 * LOG2E
    col_mq = 2 * SWA_WIDTH
    col_mg = col_mq + MEM_WIDTH

    for sub, xn in enumerate(_normed_sub_tiles(prev, h_ref, g_ref)):
        rows = slice(sub * IN_SUB_TILE, (sub + 1) * IN_SUB_TILE)

        def proj(lo, width):
            return _dot(xn, w_ref[:, lo:lo + width])

        mq = proj(col_mq, MEM_WIDTH)
        mem_gate = _silu(proj(col_mg, MEM_WIDTH))

        q = proj(0, SWA_WIDTH)
        probs, inv_sums = _memory_probs(mq, mk_ref)
        cos = cos_ref[rows, :]
        sin = sin_ref[rows, :]
        for cb in range(SWA_WIDTH // LANES):
            cs = slice(cb * LANES, (cb + 1) * LANES)
            qt_ref[cs, rows] = (_rope_block(q[:, cs], cos, sin, first) * scale).T.astype(BF16)

        gate = _silu(proj(SWA_WIDTH, SWA_WIDTH))
        _memory_readout(probs, inv_sums, mem_gate, mv_ref, om_ref, rows)
        for cb in range(SWA_WIDTH // LANES):
            cs = slice(cb * LANES, (cb + 1) * LANES)
            gmt_ref[cs, rows] = gate[:, cs].T.astype(BF16)


def _b_in_proj(layer, h, pending, g, w, cos, sin, mem_k, mem_v, tiles_per_seq):
    n = h.shape[0]
    row = lambda width: pl.BlockSpec((IN_TILE, width), lambda i: (i, 0))
    col = lambda width: pl.BlockSpec((width, IN_TILE), lambda i: (0, i))
    full = lambda a: pl.BlockSpec(a.shape, lambda i: (0,) * a.ndim)
    mem = pl.BlockSpec((None, None, MEM_TOKENS, MEM_WIDTH), lambda i: (layer, i // tiles_per_seq, 0, 0))
    out_specs = [col(SWA_WIDTH), col(SWA_WIDTH), row(MEM_WIDTH)]
    out_shape = [jax.ShapeDtypeStruct((SWA_WIDTH, n), BF16), jax.ShapeDtypeStruct((SWA_WIDTH, n), BF16),
                 jax.ShapeDtypeStruct((n, MEM_WIDTH), BF16)]
    if pending is not None:
        out_specs, out_shape = [row(D_MODEL)] + out_specs, [jax.ShapeDtypeStruct((n, D_MODEL), F32)] + out_shape
    pend_args, pend_specs = _pending_specs(pending, row, col, full)
    return pl.pallas_call(
        functools.partial(_b_in_kernel, None if pending is None else pending[-1]),
        grid=(n // IN_TILE,),
        in_specs=[row(D_MODEL)] + pend_specs
        + [full(g), _layer_weight(w, layer - N_A_LAYERS), row(LANES), row(LANES), mem, mem],
        out_specs=out_specs,
        out_shape=out_shape,
        compiler_params=_params("parallel"),
        name=f"b_in_proj_{layer}",
    )(h, *pend_args, g, w, cos, sin, mem_k, mem_v)


def _swa_kernel(sink_ref, qt_ref, kc_ref, kp_ref, vtc_ref, vtp_ref, gmt_ref, ot_ref):
    blk = WINDOW
    dh = SWA_HEAD_DIM
    ki = lax.broadcasted_iota(jnp.int32, (blk, blk), 0)
    qi = lax.broadcasted_iota(jnp.int32, (blk, blk), 1)
    from_prev = ki > qi
    from_prev_bf = from_prev.astype(BF16)
    prev_bias = jnp.where(pl.program_id(1) > 0, 0.0, MASK_VALUE)

    for sb in range(SWA_TILE // blk):
        cols = slice(sb * blk, (sb + 1) * blk)
        two = slice((sb - 1) * blk, (sb + 1) * blk)

        probs, inv_den = [], []
        for g in range(SWA_KV_HEADS):
            k2 = jnp.concatenate([kp_ref[g], kc_ref[g, cols, :]], axis=0) if sb == 0 else kc_ref[g, two, :]
            heads = range(g * SWA_GROUP, (g + 1) * SWA_GROUP)
            qt = jnp.concatenate([qt_ref[h * dh:(h + 1) * dh, cols] for h in heads], axis=1)
            st = _dot(k2, qt)
            ps = []
            for j, h in enumerate(heads):
                s_prev = st[:blk, j * blk:(j + 1) * blk]
                if sb == 0:
                    s_prev = s_prev + prev_bias
                s = jnp.where(from_prev, s_prev, st[blk:, j * blk:(j + 1) * blk])
                sink = sink_ref[h] * LOG2E
                m = jnp.maximum(jnp.max(s, axis=0, keepdims=True), sink)
                p = jnp.exp2(s - m)
                inv_den.append(1.0 / (jnp.sum(p, axis=0, keepdims=True) + jnp.exp2(sink - m)))
                p = p.astype(BF16)
                p_prev = p * from_prev_bf
                ps.append(jnp.concatenate([p_prev, p - p_prev], axis=0))
            probs.append(jnp.concatenate(ps, axis=1))

        for g in range(SWA_KV_HEADS):
            vrows = slice(g * dh, (g + 1) * dh)
            vt2 = (jnp.concatenate([vtp_ref[vrows, :], vtc_ref[vrows, cols]], axis=1) if sb == 0
                   else vtc_ref[vrows, two])
            ot = _dot(vt2, probs[g])
            for j, h in enumerate(range(g * SWA_GROUP, (g + 1) * SWA_GROUP)):
                rows = slice(h * dh, (h + 1) * dh)
                o = ot[:, j * blk:(j + 1) * blk] * inv_den[h]
                ot_ref[rows, cols] = (o * gmt_ref[rows, cols].astype(F32)).astype(BF16)


def _swa(sinks, qt, k, vt, gmt, batch):
    n = qt.shape[1]
    tiles = n // batch // SWA_TILE
    sub = SWA_TILE // WINDOW
    tile = lambda b, t: b * tiles + t
    prev = lambda b, t: jnp.maximum(tile(b, t) * sub - 1, 0)
    wide = pl.BlockSpec((SWA_WIDTH, SWA_TILE), lambda b, t: (0, tile(b, t)))
    return pl.pallas_call(
        _swa_kernel,
        grid=(batch, tiles),
        in_specs=[pl.BlockSpec(memory_space=pltpu.SMEM),
                  wide,
                  pl.BlockSpec((SWA_KV_HEADS, SWA_TILE, SWA_HEAD_DIM), lambda b, t: (0, tile(b, t), 0)),
                  pl.BlockSpec((SWA_KV_HEADS, WINDOW, SWA_HEAD_DIM), lambda b, t: (0, prev(b, t), 0)),
                  pl.BlockSpec((SWA_KV_WIDTH, SWA_TILE), lambda b, t: (0, tile(b, t))),
                  pl.BlockSpec((SWA_KV_WIDTH, WINDOW), lambda b, t: (0, prev(b, t))),
                  wide],
        out_specs=wide,
        out_shape=jax.ShapeDtypeStruct((SWA_WIDTH, n), BF16),
        compiler_params=_params("parallel", "arbitrary"),
        name="swa",
    )(sinks, qt, k, k, vt, vt, gmt)


def kernel(x, mem, positions, pre_norm_g, post_norm_g, mem_norm_g, w_mem_kv, a_w_in, a_lb_logits, a_out_norm_g, a_w_out, kv_norm_g, w_kv_shared, b_w_in, b_sinks, b_w_out):
    batch, seq, _ = x.shape
    n = batch * seq
    assert seq % max(IN_TILE, OUT_TILE, HG_TILE, SWA_TILE) == 0 and n % ROPE_TILE == 0
    tiles_per_seq = seq // IN_TILE

    h = x.reshape(n, D_MODEL)
    mem_k, mem_v = _mem_kv(mem, mem_norm_g, w_mem_kv)
    cos, sin = _rope_tables(positions)
    a_w_in = a_w_in.astype(BF16)
    a_w_out = a_w_out.astype(BF16)
    b_w_in = b_w_in.astype(BF16)
    b_w_out = b_w_out.astype(BF16)
    k_sh = v_sh = None
    pending = None

    for layer in range(DEPTH):
        pre_g = pre_norm_g[layer].reshape(1, D_MODEL)
        post_g = post_norm_g[layer].reshape(1, D_MODEL)
        if layer < N_A_LAYERS:
            outs = _a_in_proj(layer, h, pending, pre_g, a_w_in, a_lb_logits, mem_k, mem_v, tiles_per_seq)
            if pending is not None:
                h, *outs = outs
            qs, lf, kk, v, gm, o_mem = outs
            o_main = _hgrn(qs, lf, kk, v, gm, a_out_norm_g[layer].reshape(1, HG_VAL_WIDTH), batch)
            pending = (o_main, o_mem, a_w_out, layer, post_g, False)
        else:
            j = layer - N_A_LAYERS
            outs = _b_in_proj(layer, h, pending, pre_g, b_w_in, cos, sin, mem_k, mem_v, tiles_per_seq)
            if pending is not None:
                h, *outs = outs
            qt, gmt, o_mem = outs
            if k_sh is None:
                k_sh, v_sh = _shared_kv(h, kv_norm_g.reshape(1, D_MODEL), w_kv_shared.astype(BF16), cos, sin)
            o_main_t = _swa(b_sinks[j], qt, k_sh, v_sh, gmt, batch)
            pending = (o_main_t, o_mem, b_w_out, j, post_g, True)
    o_main, o_mem, w_out, w_layer, post_g, transposed = pending
    h = _out_proj(o_main, o_mem, w_out, w_layer, post_g, h, "out_proj_last", main_transposed=transposed)
    return h.reshape(batch, seq, D_MODEL)
```

```python
import functools

import jax
import jax.numpy as jnp
from jax import lax
from jax.experimental import pallas as pl
from jax.experimental.pallas import tpu as pltpu

D_MODEL = 1024
DEPTH = 4
N_A_LAYERS = DEPTH // 2

HG_HEADS = 8
HG_DK = 128
HG_DV = D_MODEL // HG_HEADS
HG_KEY_WIDTH = HG_HEADS * HG_DK
HG_VAL_WIDTH = HG_HEADS * HG_DV
HG_CHUNK = 64

SWA_Q_HEADS = 16
SWA_KV_HEADS = 4
SWA_GROUP = SWA_Q_HEADS // SWA_KV_HEADS
SWA_HEAD_DIM = 64
SWA_WIDTH = SWA_Q_HEADS * SWA_HEAD_DIM
SWA_KV_WIDTH = SWA_KV_HEADS * SWA_HEAD_DIM
WINDOW = 128

MEM_TOKENS = 256
MEM_HEADS = 4
MEM_HEAD_DIM = 128
MEM_WIDTH = MEM_HEADS * MEM_HEAD_DIM

ROPE_THETA = 10000.0
NORM_EPS = 1e-6

LANES = 128
SUBLANES = 8
MASK_VALUE = -1e30
LOG2E = 1.4426950408889634

IN_TILE = 512
IN_SUB_TILE = 256
OUT_TILE = 1024
HG_TILE = 512
SWA_TILE = 1024
ROPE_TILE = 2048
VMEM_LIMIT = 56 * 1024 * 1024

F32 = jnp.float32
BF16 = jnp.bfloat16

_NT = (((1,), (1,)), ((), ()))
_TN = (((0,), (0,)), ((), ()))


def _dot(a, b):
    return jnp.dot(a, b, preferred_element_type=F32)


def _dot_nt(a, b):
    return lax.dot_general(a, b, _NT, preferred_element_type=F32)


def _dot_tn(a, b):
    return lax.dot_general(a, b, _TN, preferred_element_type=F32)


def _rms_norm(x, g):
    ms = jnp.mean(x * x, axis=-1, keepdims=True)
    return x * lax.rsqrt(ms + NORM_EPS) * g


def _silu(x):
    h = 0.5 * x
    return h + h * jnp.tanh(h)


def _params(*semantics):
    return pltpu.CompilerParams(dimension_semantics=semantics, vmem_limit_bytes=VMEM_LIMIT)


def _layer_weight(w, layer):
    return pl.BlockSpec((None,) + w.shape[1:], lambda *_: (layer, 0, 0), pipeline_mode=pl.Buffered(1))


def _layer_input(prev, h_ref, rows):
    if prev is None:
        return h_ref[rows, :]
    om_ref, ome_ref, wout_ref, pg_ref, hnew_ref, main_transposed = prev
    if main_transposed:
        y = _dot_tn(om_ref[:, rows], wout_ref[:HG_VAL_WIDTH, :])
    else:
        y = _dot(om_ref[rows, :], wout_ref[:HG_VAL_WIDTH, :])
    y = y + _dot(ome_ref[rows, :], wout_ref[HG_VAL_WIDTH:, :])
    h_new = h_ref[rows, :] + _rms_norm(y, pg_ref[...])
    hnew_ref[rows, :] = h_new
    return h_new


def _normed_sub_tiles(prev, h_ref, g_ref):
    return [_rms_norm(_layer_input(prev, h_ref, slice(s * IN_SUB_TILE, (s + 1) * IN_SUB_TILE)),
                      g_ref[...]).astype(BF16) for s in range(IN_TILE // IN_SUB_TILE)]


def _split_refs(prev_transposed, refs, n_in):
    if prev_transposed is None:
        return None, refs[:n_in], refs[n_in:]
    prev_in, ins, (hnew_ref, *outs) = refs[:4], refs[4:4 + n_in], refs[4 + n_in:]
    return (*prev_in, hnew_ref, prev_transposed), ins, outs


def _pending_specs(pending, row, col, full):
    if pending is None:
        return [], []
    o_main, o_mem, w_out, w_layer, post_g, transposed = pending
    main = col(HG_VAL_WIDTH) if transposed else row(HG_VAL_WIDTH)
    return [o_main, o_mem, w_out, post_g], [main, row(MEM_WIDTH), _layer_weight(w_out, w_layer), full(post_g)]


ROPE_FREQS = SWA_HEAD_DIM // 2
ROPE_PACK = LANES // ROPE_FREQS


def _rope_table_kernel(pos_ref, invf_ref, cos_ref, sin_ref):
    ang = pos_ref[...].astype(F32) * invf_ref[...]
    rows = ang.shape[0]
    src = lax.broadcasted_iota(jnp.int32, (LANES, LANES), 0)
    dst = lax.broadcasted_iota(jnp.int32, (LANES, LANES), 1)
    sin_sign = jnp.where((dst & ROPE_FREQS) == 0, -1.0, 1.0)
    for table, out_ref, sign in ((jnp.cos(ang), cos_ref, 1.0), (jnp.sin(ang), sin_ref, sin_sign)):
        hi = table.astype(BF16)
        rest = table - hi.astype(F32)
        mid = rest.astype(BF16)
        lo = (rest - mid.astype(F32)).astype(BF16)
        for i in range(ROPE_PACK):
            pick = src == i * ROPE_FREQS + (dst & (ROPE_FREQS - 1))
            spread = jnp.where(pick, sign, 0.0).astype(BF16)
            out_ref[pl.ds(i, rows, stride=ROPE_PACK), :] = _dot(hi, spread) + _dot(mid, spread) + _dot(lo, spread)


def _rope_tables(positions):
    n = positions.size
    inv_freq = ROPE_THETA ** (-jnp.arange(0, SWA_HEAD_DIM, 2, dtype=F32) / SWA_HEAD_DIM)
    invf = jnp.tile(inv_freq, ROPE_PACK).reshape(1, LANES)
    pos = jnp.repeat(positions.reshape(n // ROPE_PACK, ROPE_PACK), ROPE_FREQS, axis=1)
    row = pl.BlockSpec((ROPE_TILE, LANES), lambda i: (i, 0))
    return pl.pallas_call(
        _rope_table_kernel,
        grid=(n // ROPE_TILE,),
        in_specs=[pl.BlockSpec((ROPE_TILE // ROPE_PACK, LANES), lambda i: (i, 0)),
                  pl.BlockSpec((1, LANES), lambda i: (0, 0))],
        out_specs=[row, row],
        out_shape=[jax.ShapeDtypeStruct((n, LANES), F32)] * 2,
        compiler_params=_params("parallel"),
        name="rope_tables",
    )(pos, invf)


def _rope_block(xc, cos, sin, first_half):
    swapped = jnp.where(first_half,
                        pltpu.roll(xc, LANES - SWA_HEAD_DIM // 2, 1),
                        pltpu.roll(xc, SWA_HEAD_DIM // 2, 1))
    return xc * cos + swapped * sin


def _first_half_mask(rows):
    lane = lax.broadcasted_iota(jnp.int32, (rows, LANES), 1)
    return (lane & (SWA_HEAD_DIM // 2)) == 0


def _mem_kv_kernel(mem_ref, g_ref, w_ref, k_ref, v_ref):
    mn = _rms_norm(mem_ref[...], g_ref[...]).astype(BF16)
    kv = _dot(mn, w_ref[...].astype(BF16))
    k_ref[...] = kv[:, :MEM_WIDTH].astype(BF16)
    v_ref[...] = kv[:, MEM_WIDTH:].astype(BF16)


def _mem_kv(mem, mem_norm_g, w_mem_kv):
    b = mem.shape[0]
    out = pl.BlockSpec((None, None, MEM_TOKENS, MEM_WIDTH), lambda l, i: (l, i, 0, 0))
    return pl.pallas_call(
        _mem_kv_kernel,
        grid=(DEPTH, b),
        in_specs=[
            pl.BlockSpec((None, MEM_TOKENS, D_MODEL), lambda l, i: (i, 0, 0)),
            pl.BlockSpec((None, 1, D_MODEL), lambda l, i: (l, 0, 0)),
            pl.BlockSpec((None, D_MODEL, 2 * MEM_WIDTH), lambda l, i: (l, 0, 0)),
        ],
        out_specs=[out, out],
        out_shape=[jax.ShapeDtypeStruct((DEPTH, b, MEM_TOKENS, MEM_WIDTH), BF16)] * 2,
        compiler_params=_params("arbitrary", "arbitrary"),
        name="mem_kv",
    )(mem, mem_norm_g.reshape(DEPTH, 1, D_MODEL), w_mem_kv)


def _memory_probs(mq, mk_ref):
    mq = (mq * (MEM_HEAD_DIM ** -0.5 * LOG2E)).astype(BF16)
    probs, inv_sums = [], []
    for hh in range(MEM_HEADS):
        hs = slice(hh * MEM_HEAD_DIM, (hh + 1) * MEM_HEAD_DIM)
        s = _dot_nt(mq[:, hs], mk_ref[:, hs])
        p = jnp.exp2(s - jnp.max(s, axis=-1, keepdims=True))
        probs.append(p.astype(BF16))
        inv_sums.append(1.0 / jnp.sum(p, axis=-1, keepdims=True))
    return probs, inv_sums


def _memory_readout(probs, inv_sums, gate, mv_ref, out_ref, rows):
    for hh in range(MEM_HEADS):
        hs = slice(hh * MEM_HEAD_DIM, (hh + 1) * MEM_HEAD_DIM)
        o = _dot(probs[hh], mv_ref[:, hs]) * inv_sums[hh]
        out_ref[rows, hs] = (o * gate[:, hs]).astype(BF16)


def _a_in_kernel(layer, prev_transposed, h_ref, *refs):
    prev, (g_ref, w_ref, lbl_ref, mk_ref, mv_ref), outs = _split_refs(prev_transposed, refs, 5)
    qs_ref, lf_ref, kk_ref, v_ref, gm_ref, om_ref = outs

    logits = lbl_ref[...]
    e = jnp.exp(logits - jnp.max(logits, axis=0, keepdims=True))
    lb = jnp.sum(e[:layer + 1], axis=0, keepdims=True) / jnp.sum(e, axis=0, keepdims=True)
    mid = 0.5 * (1.0 + lb)
    half = 0.5 * (1.0 - lb)

    col_q, col_f, col_v = 0, HG_KEY_WIDTH, 2 * HG_KEY_WIDTH
    col_g = col_v + HG_VAL_WIDTH
    col_mq = col_g + HG_VAL_WIDTH
    col_mg = col_mq + MEM_WIDTH

    for sub, xn in enumerate(_normed_sub_tiles(prev, h_ref, g_ref)):
        rows = slice(sub * IN_SUB_TILE, (sub + 1) * IN_SUB_TILE)

        def proj(lo, width):
            return _dot(xn, w_ref[:, lo:lo + width])

        mq = proj(col_mq, MEM_WIDTH)
        mem_gate = _silu(proj(col_mg, MEM_WIDTH))
        qs_ref[rows, :] = _silu(proj(col_q, HG_KEY_WIDTH)).astype(BF16)
        probs, inv_sums = _memory_probs(mq, mk_ref)

        th = half * jnp.tanh(0.5 * proj(col_f, HG_KEY_WIDTH))
        lf_ref[rows, :] = jnp.log2(jnp.maximum(mid + th, lb))
        kk_ref[rows, :] = (half - th).astype(BF16)

        _memory_readout(probs, inv_sums, mem_gate, mv_ref, om_ref, rows)
        v_ref[rows, :] = proj(col_v, HG_VAL_WIDTH).astype(BF16)
        gm_ref[rows, :] = _silu(proj(col_g, HG_VAL_WIDTH)).astype(BF16)


def _a_in_proj(layer, h, pending, g, w, lb_logits, mem_k, mem_v, tiles_per_seq):
    n = h.shape[0]
    row = lambda width: pl.BlockSpec((IN_TILE, width), lambda i: (i, 0))
    col = lambda width: pl.BlockSpec((width, IN_TILE), lambda i: (0, i))
    full = lambda a: pl.BlockSpec(a.shape, lambda i: (0,) * a.ndim)
    mem = pl.BlockSpec((None, None, MEM_TOKENS, MEM_WIDTH), lambda i: (layer, i // tiles_per_seq, 0, 0))
    widths = (HG_KEY_WIDTH, HG_KEY_WIDTH, HG_KEY_WIDTH, HG_VAL_WIDTH, HG_VAL_WIDTH, MEM_WIDTH)
    dtypes = (BF16, F32, BF16, BF16, BF16, BF16)
    if pending is not None:
        widths, dtypes = (D_MODEL,) + widths, (F32,) + dtypes
    pend_args, pend_specs = _pending_specs(pending, row, col, full)
    return pl.pallas_call(
        functools.partial(_a_in_kernel, layer, None if pending is None else pending[-1]),
        grid=(n // IN_TILE,),
        in_specs=[row(D_MODEL)] + pend_specs + [full(g), _layer_weight(w, layer), full(lb_logits), mem, mem],
        out_specs=[row(wd) for wd in widths],
        out_shape=[jax.ShapeDtypeStruct((n, wd), dt) for wd, dt in zip(widths, dtypes)],
        compiler_params=_params("parallel"),
        name=f"a_in_proj_{layer}",
    )(h, *pend_args, g, w, lb_logits, mem_k, mem_v)


_LEVELS = (32, 16, 8, 4, 2, 1)
_SAFE_BLOCKS = (HG_CHUNK, 16)
SAFE_BLOCK_DECAY = 100.0


def _coarse_level(cum, qs, kk, m):
    c, w = cum.shape
    early = m % (2 * SUBLANES) == 0
    dt = BF16 if early else F32
    zeros = jnp.zeros((m, w), dt)
    qd, kd = [], []
    for blk in range(c // (2 * m)):
        lo, mid, hi = blk * 2 * m, blk * 2 * m + m, (blk + 1) * 2 * m
        ref = cum[mid - 1:mid, :]
        kd += [(kk[lo:mid] * jnp.exp2(ref - cum[lo:mid])).astype(dt), zeros]
        qd += [zeros, (qs[mid:hi] * jnp.exp2(cum[mid:hi] - ref)).astype(dt)]
    return jnp.concatenate(qd, axis=0).astype(BF16), jnp.concatenate(kd, axis=0).astype(BF16)


def _fine_level(cum3, qs3, kk3, m, upper_pen, lower_pen):
    groups, _, w = cum3.shape
    parts = [jnp.broadcast_to(cum3[:, blk * 2 * m + m - 1:blk * 2 * m + m, :], (groups, 2 * m, w))
             for blk in range(SUBLANES // (2 * m))]
    x = cum3 - (parts[0] if len(parts) == 1 else jnp.concatenate(parts, axis=1))
    qd = qs3 * jnp.exp2(x + upper_pen)
    kd = kk3 * jnp.exp2(lower_pen - x)
    c = groups * SUBLANES
    return qd.reshape(c, w).astype(BF16), kd.reshape(c, w).astype(BF16)


def _block_level(cum, qs, kk, block):
    c, _ = cum.shape
    local = [cum[:block]] + [cum[b:b + block] - cum[b - 1:b] for b in range(block, c, block)]
    local = local[0] if len(local) == 1 else jnp.concatenate(local, axis=0)
    return (qs * jnp.exp2(local)).astype(BF16), (kk * jnp.exp2(-local)).astype(BF16)


def _hgrn_constants():
    c = HG_CHUNK
    row = lax.broadcasted_iota(jnp.int32, (c, c), 0)
    col = lax.broadcasted_iota(jnp.int32, (c, c), 1)
    sub = lax.broadcasted_iota(jnp.int32, (1, SUBLANES, HG_DK), 1)
    fine = [m for m in _LEVELS if m < SUBLANES]
    same_block = {m: (row // (2 * m)) == (col // (2 * m)) for m in _LEVELS if 2 * m < c}
    return dict(
        same_block={m: mask.astype(F32) for m, mask in same_block.items()},
        diag=(row == col).astype(F32),
        causal_block={b: ((col <= row) if b == c else same_block[b // 2] & (col <= row)).astype(F32)
                      for b in _SAFE_BLOCKS},
        upper_pen={m: jnp.where((sub & m) != 0, 0.0, MASK_VALUE) for m in fine},
        lower_pen={m: jnp.where((sub & m) != 0, MASK_VALUE, 0.0) for m in fine},
        even_row=jnp.where((sub & 1) != 0, 0.0, 1.0),
    )


def _hgrn_tile(block, k, qs_ref, lf_ref, kk_ref, v_ref, gm_ref, gn, cum_s, o_ref, st_ref):
    c = HG_CHUNK
    heads = [slice(h * HG_DK, (h + 1) * HG_DK) for h in range(HG_HEADS)]
    shape3 = (c // SUBLANES, SUBLANES, HG_DK)
    n_chunks = HG_TILE // c

    def scores_pass(ci):
        rows = slice(ci * c, (ci + 1) * c)
        scores_bf, q_ins, k_outs, lasts = [], [], [], []
        for hs in heads:
            cum = cum_s[rows, hs]
            qs = qs_ref[rows, hs].astype(F32)
            kk = kk_ref[rows, hs].astype(F32)

            pairs = []
            for m in _LEVELS:
                if block is not None and 2 * m <= block:
                    continue
                if m >= SUBLANES:
                    qd, kd = _coarse_level(cum, qs, kk, m)
                elif m == 1:
                    qs3, kk3 = qs.reshape(shape3), kk.reshape(shape3)
                    qd3 = qs3 * jnp.exp2(lf_ref[rows, hs].reshape(shape3) + k["upper_pen"][m])
                    qd = qd3.reshape(c, HG_DK).astype(BF16)
                    kd = (kk3 * k["even_row"]).reshape(c, HG_DK).astype(BF16)
                else:
                    qd, kd = _fine_level(cum.reshape(shape3), qs.reshape(shape3), kk.reshape(shape3), m,
                                         k["upper_pen"][m], k["lower_pen"][m])
                pairs.append((qd, kd, k["same_block"].get(m)))
            last = cum[c - 1:c, :]
            q_in = (qs * jnp.exp2(cum)).astype(BF16)
            if block == c:
                k_side = kk * jnp.exp2(-cum)
                pairs.append((q_in, k_side.astype(BF16), k["causal_block"][block]))
                k_out = (k_side * jnp.exp2(last)).astype(BF16)
            else:
                if block is not None:
                    pairs.append(_block_level(cum, qs, kk, block) + (k["causal_block"][block],))
                else:
                    pairs.append((qs.astype(BF16), kk.astype(BF16), k["diag"]))
                k_out = (kk * jnp.exp2(last - cum)).astype(BF16)

            scores = None
            for qd, kd, mask in pairs:
                s = _dot_nt(qd, kd)
                if mask is not None:
                    s = s * mask
                scores = s if scores is None else scores + s

            scores_bf.append(scores.astype(BF16))
            q_ins.append(q_in)
            k_outs.append(k_out)
            lasts.append(last)
        return scores_bf, q_ins, k_outs, lasts

    def recurrence_pass(ci, scores_bf, q_ins, k_outs, lasts):
        rows = slice(ci * c, (ci + 1) * c)
        for h, hs in enumerate(heads):
            o = _dot_nt(q_ins[h], st_ref[h].astype(BF16)) + _dot(scores_bf[h], v_ref[rows, hs])
            o = o * lax.rsqrt(jnp.mean(o * o, axis=-1, keepdims=True) + NORM_EPS)
            o_ref[rows, hs] = (o * gn[:, hs] * gm_ref[rows, hs].astype(F32)).astype(BF16)

        for h, hs in enumerate(heads):
            st_ref[h] = st_ref[h] * jnp.exp2(lasts[h]) + _dot_tn(v_ref[rows, hs], k_outs[h])

    ahead = scores_pass(0)
    for ci in range(n_chunks):
        current, ahead = ahead, (scores_pass(ci + 1) if ci + 1 < n_chunks else None)
        recurrence_pass(ci, *current)


def _hgrn_kernel(qs_ref, lf_ref, kk_ref, v_ref, gm_ref, gn_ref, o_ref, cum_s, st_ref):
    c = HG_CHUNK

    @pl.when(pl.program_id(1) == 0)
    def _():
        st_ref[...] = jnp.zeros_like(st_ref)

    row = lax.broadcasted_iota(jnp.int32, (c, c), 0)
    col = lax.broadcasted_iota(jnp.int32, (c, c), 1)
    tri = (col <= row).astype(BF16)
    tri2 = jnp.concatenate([tri, tri], axis=1)

    worst = {b: None for b in _SAFE_BLOCKS}
    for ci in range(HG_TILE // c):
        rows = slice(ci * c, (ci + 1) * c)
        lf = lf_ref[rows, :]
        hi = lf.astype(BF16)
        lo = (lf - hi.astype(F32)).astype(BF16)
        cum = _dot(tri2, jnp.concatenate([hi, lo], axis=0))
        cum_s[rows, :] = cum
        for block in _SAFE_BLOCKS:
            ends = [cum[b - 1:b, :] for b in range(block, c + 1, block)]
            for b, end in enumerate(ends):
                decay = end if b == 0 else end - ends[b - 1]
                worst[block] = decay if worst[block] is None else jnp.minimum(worst[block], decay)
    safe = [jnp.min(worst[b]) >= -SAFE_BLOCK_DECAY for b in _SAFE_BLOCKS]

    consts = _hgrn_constants()
    args = (consts, qs_ref, lf_ref, kk_ref, v_ref, gm_ref, gn_ref[...], cum_s, o_ref, st_ref)

    not_yet = None
    for block, ok in zip(_SAFE_BLOCKS + (None,), safe + [True]):
        take = ok if not_yet is None else jnp.logical_and(not_yet, ok)

        @pl.when(take)
        def _(block=block):
            _hgrn_tile(block, *args)

        if block is not None:
            not_yet = jnp.logical_not(ok) if not_yet is None else jnp.logical_and(not_yet, jnp.logical_not(ok))


def _hgrn(qs, lf, kk, v, gm, gn, batch):
    n = qs.shape[0]
    tiles = n // batch // HG_TILE
    row = pl.BlockSpec((HG_TILE, HG_KEY_WIDTH), lambda b, t: (b * tiles + t, 0))
    return pl.pallas_call(
        _hgrn_kernel,
        grid=(batch, tiles),
        in_specs=[row, row, row, row, row, pl.BlockSpec((1, HG_VAL_WIDTH), lambda b, t: (0, 0))],
        out_specs=row,
        out_shape=jax.ShapeDtypeStruct((n, HG_VAL_WIDTH), BF16),
        scratch_shapes=[pltpu.VMEM((HG_TILE, HG_KEY_WIDTH), F32),
                        pltpu.VMEM((HG_HEADS, HG_DV, HG_DK), F32)],
        compiler_params=_params("arbitrary", "arbitrary"),
        name="hgrn2",
    )(qs, lf, kk, v, gm, gn)


def _out_kernel(main_transposed, om_ref, ome_ref, w_ref, g_ref, h_ref, o_ref):
    main_dot = _dot_tn if main_transposed else _dot
    y = main_dot(om_ref[...], w_ref[:HG_VAL_WIDTH, :]) + _dot(ome_ref[...], w_ref[HG_VAL_WIDTH:, :])
    o_ref[...] = h_ref[...] + _rms_norm(y, g_ref[...])


def _out_proj(o_main, o_mem, w, layer, g, h, name, main_transposed=False):
    n = h.shape[0]
    row = lambda width: pl.BlockSpec((OUT_TILE, width), lambda i: (i, 0))
    col = lambda width: pl.BlockSpec((width, OUT_TILE), lambda i: (0, i))
    full = lambda a: pl.BlockSpec(a.shape, lambda i: (0,) * a.ndim)
    main = col(HG_VAL_WIDTH) if main_transposed else row(HG_VAL_WIDTH)
    return pl.pallas_call(
        functools.partial(_out_kernel, main_transposed),
        grid=(n // OUT_TILE,),
        in_specs=[main, row(MEM_WIDTH), _layer_weight(w, layer), full(g), row(D_MODEL)],
        out_specs=row(D_MODEL),
        out_shape=jax.ShapeDtypeStruct((n, D_MODEL), F32),
        compiler_params=_params("parallel"),
        name=name,
    )(o_main, o_mem, w, g, h)


def _shared_kv_kernel(x_ref, g_ref, w_ref, cos_ref, sin_ref, k_ref, v_ref):
    xn = _rms_norm(x_ref[...], g_ref[...]).astype(BF16)
    kv = _dot(xn, w_ref[...])
    cos = cos_ref[...]
    sin = sin_ref[...]
    first = _first_half_mask(IN_TILE)
    heads_per_block = LANES // SWA_HEAD_DIM
    for cb in range(SWA_KV_WIDTH // LANES):
        kr = _rope_block(kv[:, cb * LANES:(cb + 1) * LANES], cos, sin, first).astype(BF16)
        for j in range(heads_per_block):
            k_ref[cb * heads_per_block + j] = kr[:, j * SWA_HEAD_DIM:(j + 1) * SWA_HEAD_DIM]
    v_ref[...] = kv[:, SWA_KV_WIDTH:].T.astype(BF16)


def _shared_kv(h, g, w, cos, sin):
    n = h.shape[0]
    row = lambda width: pl.BlockSpec((IN_TILE, width), lambda i: (i, 0))
    full = lambda a: pl.BlockSpec(a.shape, lambda i: (0,) * a.ndim)
    return pl.pallas_call(
        _shared_kv_kernel,
        grid=(n // IN_TILE,),
        in_specs=[row(D_MODEL), full(g), full(w), row(LANES), row(LANES)],
        out_specs=[pl.BlockSpec((SWA_KV_HEADS, IN_TILE, SWA_HEAD_DIM), lambda i: (0, i, 0)),
                   pl.BlockSpec((SWA_KV_WIDTH, IN_TILE), lambda i: (0, i))],
        out_shape=[jax.ShapeDtypeStruct((SWA_KV_HEADS, n, SWA_HEAD_DIM), BF16),
                   jax.ShapeDtypeStruct((SWA_KV_WIDTH, n), BF16)],
        compiler_params=_params("parallel"),
        name="shared_kv",
    )(h, g, w, cos, sin)


def _b_in_kernel(prev_transposed, h_ref, *refs):
    prev, (g_ref, w_ref, cos_ref, sin_ref, mk_ref, mv_ref), outs = _split_refs(prev_transposed, refs, 6)
    qt_ref, gmt_ref, om_ref = outs
    first = _first_half_mask(IN_SUB_TILE)
    scale = SWA_HEAD_DIM ** -0.5 * LOG2E
    col_mq = 2 * SWA_WIDTH
    col_mg = col_mq + MEM_WIDTH

    for sub, xn in enumerate(_normed_sub_tiles(prev, h_ref, g_ref)):
        rows = slice(sub * IN_SUB_TILE, (sub + 1) * IN_SUB_TILE)

        def proj(lo, width):
            return _dot(xn, w_ref[:, lo:lo + width])

        mq = proj(col_mq, MEM_WIDTH)
        mem_gate = _silu(proj(col_mg, MEM_WIDTH))

        q = proj(0, SWA_WIDTH)
        probs, inv_sums = _memory_probs(mq, mk_ref)
        cos = cos_ref[rows, :]
        sin = sin_ref[rows, :]
        for cb in range(SWA_WIDTH // LANES):
            cs = slice(cb * LANES, (cb + 1) * LANES)
            qt_ref[cs, rows] = (_rope_block(q[:, cs], cos, sin, first) * scale).T.astype(BF16)

        gate = _silu(proj(SWA_WIDTH, SWA_WIDTH))
        _memory_readout(probs, inv_sums, mem_gate, mv_ref, om_ref, rows)
        for cb in range(SWA_WIDTH // LANES):
            cs = slice(cb * LANES, (cb + 1) * LANES)
            gmt_ref[cs, rows] = gate[:, cs].T.astype(BF16)


def _b_in_proj(layer, h, pending, g, w, cos, sin, mem_k, mem_v, tiles_per_seq):
    n = h.shape[0]
    row = lambda width: pl.BlockSpec((IN_TILE, width), lambda i: (i, 0))
    col = lambda width: pl.BlockSpec((width, IN_TILE), lambda i: (0, i))
    full = lambda a: pl.BlockSpec(a.shape, lambda i: (0,) * a.ndim)
    mem = pl.BlockSpec((None, None, MEM_TOKENS, MEM_WIDTH), lambda i: (layer, i // tiles_per_seq, 0, 0))
    out_specs = [col(SWA_WIDTH), col(SWA_WIDTH), row(MEM_WIDTH)]
    out_shape = [jax.ShapeDtypeStruct((SWA_WIDTH, n), BF16), jax.ShapeDtypeStruct((SWA_WIDTH, n), BF16),
                 jax.ShapeDtypeStruct((n, MEM_WIDTH), BF16)]
    if pending is not None:
        out_specs, out_shape = [row(D_MODEL)] + out_specs, [jax.ShapeDtypeStruct((n, D_MODEL), F32)] + out_shape
    pend_args, pend_specs = _pending_specs(pending, row, col, full)
    return pl.pallas_call(
        functools.partial(_b_in_kernel, None if pending is None else pending[-1]),
        grid=(n // IN_TILE,),
        in_specs=[row(D_MODEL)] + pend_specs
        + [full(g), _layer_weight(w, layer - N_A_LAYERS), row(LANES), row(LANES), mem, mem],
        out_specs=out_specs,
        out_shape=out_shape,
        compiler_params=_params("parallel"),
        name=f"b_in_proj_{layer}",
    )(h, *pend_args, g, w, cos, sin, mem_k, mem_v)


def _swa_kernel(sink_ref, qt_ref, kc_ref, kp_ref, vtc_ref, vtp_ref, gmt_ref, ot_ref):
    blk = WINDOW
    dh = SWA_HEAD_DIM
    ki = lax.broadcasted_iota(jnp.int32, (blk, blk), 0)
    qi = lax.broadcasted_iota(jnp.int32, (blk, blk), 1)
    from_prev = ki > qi
    from_prev_bf = from_prev.astype(BF16)
    prev_bias = jnp.where(pl.program_id(1) > 0, 0.0, MASK_VALUE)

    def scores(sb, g):
        cols = slice(sb * blk, (sb + 1) * blk)
        k2 = (jnp.concatenate([kp_ref[g], kc_ref[g, cols, :]], axis=0) if sb == 0
              else kc_ref[g, (sb - 1) * blk:(sb + 1) * blk, :])
        heads = range(g * SWA_GROUP, (g + 1) * SWA_GROUP)
        qt = jnp.concatenate([qt_ref[h * dh:(h + 1) * dh, cols] for h in heads], axis=1)
        return _dot(k2, qt)

    def attend(sb, g, st):
        cols = slice(sb * blk, (sb + 1) * blk)
        heads = range(g * SWA_GROUP, (g + 1) * SWA_GROUP)
        ps, inv_den = [], []
        for j in range(SWA_GROUP):
            s_prev = st[:blk, j * blk:(j + 1) * blk]
            if sb == 0:
                s_prev = s_prev + prev_bias
            s = jnp.where(from_prev, s_prev, st[blk:, j * blk:(j + 1) * blk])
            sink = sink_ref[heads[j]] * LOG2E
            m = jnp.maximum(jnp.max(s, axis=0, keepdims=True), sink)
            p = jnp.exp2(s - m)
            inv_den.append(1.0 / (jnp.sum(p, axis=0, keepdims=True) + jnp.exp2(sink - m)))
            p = p.astype(BF16)
            p_prev = p * from_prev_bf
            ps.append(jnp.concatenate([p_prev, p - p_prev], axis=0))
        vrows = slice(g * dh, (g + 1) * dh)
        vt2 = (jnp.concatenate([vtp_ref[vrows, :], vtc_ref[vrows, cols]], axis=1) if sb == 0
               else vtc_ref[vrows, (sb - 1) * blk:(sb + 1) * blk])
        ot = _dot(vt2, jnp.concatenate(ps, axis=1))
        for j, h in enumerate(heads):
            rows = slice(h * dh, (h + 1) * dh)
            o = ot[:, j * blk:(j + 1) * blk] * inv_den[j]
            ot_ref[rows, cols] = (o * gmt_ref[rows, cols].astype(F32)).astype(BF16)

    kv_heads = range(SWA_KV_HEADS)
    n_blocks = SWA_TILE // blk
    ahead = [scores(0, g) for g in kv_heads]
    for sb in range(n_blocks):
        current, ahead = ahead, ([scores(sb + 1, g) for g in kv_heads] if sb + 1 < n_blocks else None)
        for g in kv_heads:
            attend(sb, g, current[g])


def _swa(sinks, qt, k, vt, gmt, batch):
    n = qt.shape[1]
    tiles = n // batch // SWA_TILE
    sub = SWA_TILE // WINDOW
    tile = lambda b, t: b * tiles + t
    prev = lambda b, t: jnp.maximum(tile(b, t) * sub - 1, 0)
    wide = pl.BlockSpec((SWA_WIDTH, SWA_TILE), lambda b, t: (0, tile(b, t)))
    return pl.pallas_call(
        _swa_kernel,
        grid=(batch, tiles),
        in_specs=[pl.BlockSpec(memory_space=pltpu.SMEM),
                  wide,
                  pl.BlockSpec((SWA_KV_HEADS, SWA_TILE, SWA_HEAD_DIM), lambda b, t: (0, tile(b, t), 0)),
                  pl.BlockSpec((SWA_KV_HEADS, WINDOW, SWA_HEAD_DIM), lambda b, t: (0, prev(b, t), 0)),
                  pl.BlockSpec((SWA_KV_WIDTH, SWA_TILE), lambda b, t: (0, tile(b, t))),
                  pl.BlockSpec((SWA_KV_WIDTH, WINDOW), lambda b, t: (0, prev(b, t))),
                  wide],
        out_specs=wide,
        out_shape=jax.ShapeDtypeStruct((SWA_WIDTH, n), BF16),
        compiler_params=_params("parallel", "arbitrary"),
        name="swa",
    )(sinks, qt, k, k, vt, vt, gmt)


def kernel(x, mem, positions, pre_norm_g, post_norm_g, mem_norm_g, w_mem_kv, a_w_in, a_lb_logits, a_out_norm_g, a_w_out, kv_norm_g, w_kv_shared, b_w_in, b_sinks, b_w_out):
    batch, seq, _ = x.shape
    n = batch * seq
    assert seq % max(IN_TILE, OUT_TILE, HG_TILE, SWA_TILE) == 0 and n % ROPE_TILE == 0
    tiles_per_seq = seq // IN_TILE

    h = x.reshape(n, D_MODEL)
    mem_k, mem_v = _mem_kv(mem, mem_norm_g, w_mem_kv)
    cos, sin = _rope_tables(positions)
    a_w_in = a_w_in.astype(BF16)
    a_w_out = a_w_out.astype(BF16)
    b_w_in = b_w_in.astype(BF16)
    b_w_out = b_w_out.astype(BF16)
    k_sh = v_sh = None
    pending = None

    for layer in range(DEPTH):
        pre_g = pre_norm_g[layer].reshape(1, D_MODEL)
        post_g = post_norm_g[layer].reshape(1, D_MODEL)
        if layer < N_A_LAYERS:
            outs = _a_in_proj(layer, h, pending, pre_g, a_w_in, a_lb_logits, mem_k, mem_v, tiles_per_seq)
            if pending is not None:
                h, *outs = outs
            qs, lf, kk, v, gm, o_mem = outs
            o_main = _hgrn(qs, lf, kk, v, gm, a_out_norm_g[layer].reshape(1, HG_VAL_WIDTH), batch)
            pending = (o_main, o_mem, a_w_out, layer, post_g, False)
        else:
            j = layer - N_A_LAYERS
            outs = _b_in_proj(layer, h, pending, pre_g, b_w_in, cos, sin, mem_k, mem_v, tiles_per_seq)
            if pending is not None:
                h, *outs = outs
            qt, gmt, o_mem = outs
            if k_sh is None:
                k_sh, v_sh = _shared_kv(h, kv_norm_g.reshape(1, D_MODEL), w_kv_shared.astype(BF16), cos, sin)
            o_main_t = _swa(b_sinks[j], qt, k_sh, v_sh, gmt, batch)
            pending = (o_main_t, o_mem, b_w_out, j, post_g, True)
    o_main, o_mem, w_out, w_layer, post_g, transposed = pending
    h = _out_proj(o_main, o_mem, w_out, w_layer, post_g, h, "out_proj_last", main_transposed=transposed)
    return h.reshape(batch, seq, D_MODEL)
```

```python
import functools

import jax
import jax.numpy as jnp
from jax import lax
from jax.experimental import pallas as pl
from jax.experimental.pallas import tpu as pltpu

D_MODEL = 1024
DEPTH = 4
N_A_LAYERS = DEPTH // 2

HG_HEADS = 8
HG_DK = 128
HG_DV = D_MODEL // HG_HEADS
HG_KEY_WIDTH = HG_HEADS * HG_DK
HG_VAL_WIDTH = HG_HEADS * HG_DV
HG_CHUNK = 64

SWA_Q_HEADS = 16
SWA_KV_HEADS = 4
SWA_GROUP = SWA_Q_HEADS // SWA_KV_HEADS
SWA_HEAD_DIM = 64
SWA_WIDTH = SWA_Q_HEADS * SWA_HEAD_DIM
SWA_KV_WIDTH = SWA_KV_HEADS * SWA_HEAD_DIM
WINDOW = 128

MEM_TOKENS = 256
MEM_HEADS = 4
MEM_HEAD_DIM = 128
MEM_WIDTH = MEM_HEADS * MEM_HEAD_DIM

ROPE_THETA = 10000.0
NORM_EPS = 1e-6

LANES = 128
SUBLANES = 8
MASK_VALUE = -1e30
LOG2E = 1.4426950408889634

IN_TILE = 512
IN_SUB_TILE = 256
OUT_TILE = 1024
HG_TILE = 512
SWA_TILE = 1024
ROPE_TILE = 2048
VMEM_LIMIT = 56 * 1024 * 1024

F32 = jnp.float32
BF16 = jnp.bfloat16

_NT = (((1,), (1,)), ((), ()))
_TN = (((0,), (0,)), ((), ()))


def _dot(a, b):
    return jnp.dot(a, b, preferred_element_type=F32)


def _dot_nt(a, b):
    return lax.dot_general(a, b, _NT, preferred_element_type=F32)


def _dot_tn(a, b):
    return lax.dot_general(a, b, _TN, preferred_element_type=F32)


def _rms_norm(x, g):
    ms = jnp.mean(x * x, axis=-1, keepdims=True)
    return x * lax.rsqrt(ms + NORM_EPS) * g


def _silu(x):
    h = 0.5 * x
    return h + h * jnp.tanh(h)


def _params(*semantics):
    return pltpu.CompilerParams(dimension_semantics=semantics, vmem_limit_bytes=VMEM_LIMIT)


def _layer_weight(w, layer):
    return pl.BlockSpec((None,) + w.shape[1:], lambda *_: (layer, 0, 0), pipeline_mode=pl.Buffered(1))


def _layer_input(prev, h_ref, rows):
    if prev is None:
        return h_ref[rows, :]
    om_ref, ome_ref, wout_ref, pg_ref, hnew_ref, main_transposed = prev
    if main_transposed:
        y = _dot_tn(om_ref[:, rows], wout_ref[:HG_VAL_WIDTH, :])
    else:
        y = _dot(om_ref[rows, :], wout_ref[:HG_VAL_WIDTH, :])
    y = y + _dot(ome_ref[rows, :], wout_ref[HG_VAL_WIDTH:, :])
    h_new = h_ref[rows, :] + _rms_norm(y, pg_ref[...])
    hnew_ref[rows, :] = h_new
    return h_new


def _normed_sub_tiles(prev, h_ref, g_ref):
    return [_rms_norm(_layer_input(prev, h_ref, slice(s * IN_SUB_TILE, (s + 1) * IN_SUB_TILE)),
                      g_ref[...]).astype(BF16) for s in range(IN_TILE // IN_SUB_TILE)]


def _split_refs(prev_transposed, refs, n_in):
    if prev_transposed is None:
        return None, refs[:n_in], refs[n_in:]
    prev_in, ins, (hnew_ref, *outs) = refs[:4], refs[4:4 + n_in], refs[4 + n_in:]
    return (*prev_in, hnew_ref, prev_transposed), ins, outs


def _pending_specs(pending, row, col, full):
    if pending is None:
        return [], []
    o_main, o_mem, w_out, w_layer, post_g, transposed = pending
    main = col(HG_VAL_WIDTH) if transposed else row(HG_VAL_WIDTH)
    return [o_main, o_mem, w_out, post_g], [main, row(MEM_WIDTH), _layer_weight(w_out, w_layer), full(post_g)]


ROPE_FREQS = SWA_HEAD_DIM // 2
ROPE_PACK = LANES // ROPE_FREQS


def _rope_table_kernel(pos_ref, invf_ref, cos_ref, sin_ref):
    ang = pos_ref[...].astype(F32) * invf_ref[...]
    rows = ang.shape[0]
    src = lax.broadcasted_iota(jnp.int32, (LANES, LANES), 0)
    dst = lax.broadcasted_iota(jnp.int32, (LANES, LANES), 1)
    sin_sign = jnp.where((dst & ROPE_FREQS) == 0, -1.0, 1.0)
    for table, out_ref, sign in ((jnp.cos(ang), cos_ref, 1.0), (jnp.sin(ang), sin_ref, sin_sign)):
        hi = table.astype(BF16)
        rest = table - hi.astype(F32)
        mid = rest.astype(BF16)
        lo = (rest - mid.astype(F32)).astype(BF16)
        for i in range(ROPE_PACK):
            pick = src == i * ROPE_FREQS + (dst & (ROPE_FREQS - 1))
            spread = jnp.where(pick, sign, 0.0).astype(BF16)
            out_ref[pl.ds(i, rows, stride=ROPE_PACK), :] = _dot(hi, spread) + _dot(mid, spread) + _dot(lo, spread)


def _rope_tables(positions):
    n = positions.size
    inv_freq = ROPE_THETA ** (-jnp.arange(0, SWA_HEAD_DIM, 2, dtype=F32) / SWA_HEAD_DIM)
    invf = jnp.tile(inv_freq, ROPE_PACK).reshape(1, LANES)
    pos = jnp.repeat(positions.reshape(n // ROPE_PACK, ROPE_PACK), ROPE_FREQS, axis=1)
    row = pl.BlockSpec((ROPE_TILE, LANES), lambda i: (i, 0))
    return pl.pallas_call(
        _rope_table_kernel,
        grid=(n // ROPE_TILE,),
        in_specs=[pl.BlockSpec((ROPE_TILE // ROPE_PACK, LANES), lambda i: (i, 0)),
                  pl.BlockSpec((1, LANES), lambda i: (0, 0))],
        out_specs=[row, row],
        out_shape=[jax.ShapeDtypeStruct((n, LANES), F32)] * 2,
        compiler_params=_params("parallel"),
        name="rope_tables",
    )(pos, invf)


def _rope_block(xc, cos, sin, first_half):
    swapped = jnp.where(first_half,
                        pltpu.roll(xc, LANES - SWA_HEAD_DIM // 2, 1),
                        pltpu.roll(xc, SWA_HEAD_DIM // 2, 1))
    return xc * cos + swapped * sin


def _first_half_mask(rows):
    lane = lax.broadcasted_iota(jnp.int32, (rows, LANES), 1)
    return (lane & (SWA_HEAD_DIM // 2)) == 0


def _mem_kv_kernel(mem_ref, g_ref, w_ref, k_ref, v_ref):
    mn = _rms_norm(mem_ref[...], g_ref[...]).astype(BF16)
    kv = _dot(mn, w_ref[...].astype(BF16))
    k_ref[...] = kv[:, :MEM_WIDTH].astype(BF16)
    v_ref[...] = kv[:, MEM_WIDTH:].astype(BF16)


def _mem_kv(mem, mem_norm_g, w_mem_kv):
    b = mem.shape[0]
    out = pl.BlockSpec((None, None, MEM_TOKENS, MEM_WIDTH), lambda l, i: (l, i, 0, 0))
    return pl.pallas_call(
        _mem_kv_kernel,
        grid=(DEPTH, b),
        in_specs=[
            pl.BlockSpec((None, MEM_TOKENS, D_MODEL), lambda l, i: (i, 0, 0)),
            pl.BlockSpec((None, 1, D_MODEL), lambda l, i: (l, 0, 0)),
            pl.BlockSpec((None, D_MODEL, 2 * MEM_WIDTH), lambda l, i: (l, 0, 0)),
        ],
        out_specs=[out, out],
        out_shape=[jax.ShapeDtypeStruct((DEPTH, b, MEM_TOKENS, MEM_WIDTH), BF16)] * 2,
        compiler_params=_params("arbitrary", "arbitrary"),
        name="mem_kv",
    )(mem, mem_norm_g.reshape(DEPTH, 1, D_MODEL), w_mem_kv)


def _memory_probs(mq, mk_ref):
    mq = (mq * (MEM_HEAD_DIM ** -0.5 * LOG2E)).astype(BF16)
    probs, inv_sums = [], []
    for hh in range(MEM_HEADS):
        hs = slice(hh * MEM_HEAD_DIM, (hh + 1) * MEM_HEAD_DIM)
        s = _dot_nt(mq[:, hs], mk_ref[:, hs])
        p = jnp.exp2(s - jnp.max(s, axis=-1, keepdims=True))
        probs.append(p.astype(BF16))
        inv_sums.append(1.0 / jnp.sum(p, axis=-1, keepdims=True))
    return probs, inv_sums


def _memory_readout(probs, inv_sums, gate, mv_ref, out_ref, rows):
    for hh in range(MEM_HEADS):
        hs = slice(hh * MEM_HEAD_DIM, (hh + 1) * MEM_HEAD_DIM)
        o = _dot(probs[hh], mv_ref[:, hs]) * inv_sums[hh]
        out_ref[rows, hs] = (o * gate[:, hs]).astype(BF16)


def _a_in_kernel(layer, prev_transposed, h_ref, *refs):
    prev, (g_ref, w_ref, lbl_ref, mk_ref, mv_ref), outs = _split_refs(prev_transposed, refs, 5)
    qs_ref, lf_ref, kk_ref, v_ref, gm_ref, om_ref = outs

    logits = lbl_ref[...]
    e = jnp.exp(logits - jnp.max(logits, axis=0, keepdims=True))
    lb = jnp.sum(e[:layer + 1], axis=0, keepdims=True) / jnp.sum(e, axis=0, keepdims=True)
    mid = 0.5 * (1.0 + lb)
    half = 0.5 * (1.0 - lb)

    col_q, col_f, col_v = 0, HG_KEY_WIDTH, 2 * HG_KEY_WIDTH
    col_g = col_v + HG_VAL_WIDTH
    col_mq = col_g + HG_VAL_WIDTH
    col_mg = col_mq + MEM_WIDTH

    for sub, xn in enumerate(_normed_sub_tiles(prev, h_ref, g_ref)):
        rows = slice(sub * IN_SUB_TILE, (sub + 1) * IN_SUB_TILE)

        def proj(lo, width):
            return _dot(xn, w_ref[:, lo:lo + width])

        mq = proj(col_mq, MEM_WIDTH)
        mem_gate = _silu(proj(col_mg, MEM_WIDTH))
        qs_ref[rows, :] = _silu(proj(col_q, HG_KEY_WIDTH)).astype(BF16)
        probs, inv_sums = _memory_probs(mq, mk_ref)

        th = half * jnp.tanh(0.5 * proj(col_f, HG_KEY_WIDTH))
        lf_ref[rows, :] = jnp.log2(jnp.maximum(mid + th, lb))
        kk_ref[rows, :] = (half - th).astype(BF16)

        _memory_readout(probs, inv_sums, mem_gate, mv_ref, om_ref, rows)
        v_ref[rows, :] = proj(col_v, HG_VAL_WIDTH).astype(BF16)
        gm_ref[rows, :] = _silu(proj(col_g, HG_VAL_WIDTH)).astype(BF16)


def _a_in_proj(layer, h, pending, g, w, lb_logits, mem_k, mem_v, tiles_per_seq):
    n = h.shape[0]
    row = lambda width: pl.BlockSpec((IN_TILE, width), lambda i: (i, 0))
    col = lambda width: pl.BlockSpec((width, IN_TILE), lambda i: (0, i))
    full = lambda a: pl.BlockSpec(a.shape, lambda i: (0,) * a.ndim)
    mem = pl.BlockSpec((None, None, MEM_TOKENS, MEM_WIDTH), lambda i: (layer, i // tiles_per_seq, 0, 0))
    widths = (HG_KEY_WIDTH, HG_KEY_WIDTH, HG_KEY_WIDTH, HG_VAL_WIDTH, HG_VAL_WIDTH, MEM_WIDTH)
    dtypes = (BF16, F32, BF16, BF16, BF16, BF16)
    if pending is not None:
        widths, dtypes = (D_MODEL,) + widths, (F32,) + dtypes
    pend_args, pend_specs = _pending_specs(pending, row, col, full)
    return pl.pallas_call(
        functools.partial(_a_in_kernel, layer, None if pending is None else pending[-1]),
        grid=(n // IN_TILE,),
        in_specs=[row(D_MODEL)] + pend_specs + [full(g), _layer_weight(w, layer), full(lb_logits), mem, mem],
        out_specs=[row(wd) for wd in widths],
        out_shape=[jax.ShapeDtypeStruct((n, wd), dt) for wd, dt in zip(widths, dtypes)],
        compiler_params=_params("parallel"),
        name=f"a_in_proj_{layer}",
    )(h, *pend_args, g, w, lb_logits, mem_k, mem_v)


_LEVELS = (32, 16, 8, 4, 2, 1)
_SAFE_BLOCKS = (HG_CHUNK, 16)
SAFE_BLOCK_DECAY = 100.0


def _coarse_level(cum, qs, kk, m):
    c, w = cum.shape
    early = m % (2 * SUBLANES) == 0
    dt = BF16 if early else F32
    zeros = jnp.zeros((m, w), dt)
    qd, kd = [], []
    for blk in range(c // (2 * m)):
        lo, mid, hi = blk * 2 * m, blk * 2 * m + m, (blk + 1) * 2 * m
        ref = cum[mid - 1:mid, :]
        kd += [(kk[lo:mid] * jnp.exp2(ref - cum[lo:mid])).astype(dt), zeros]
        qd += [zeros, (qs[mid:hi] * jnp.exp2(cum[mid:hi] - ref)).astype(dt)]
    return jnp.concatenate(qd, axis=0).astype(BF16), jnp.concatenate(kd, axis=0).astype(BF16)


def _fine_level(cum3, qs3, kk3, m, upper_pen, lower_pen):
    groups, _, w = cum3.shape
    parts = [jnp.broadcast_to(cum3[:, blk * 2 * m + m - 1:blk * 2 * m + m, :], (groups, 2 * m, w))
             for blk in range(SUBLANES // (2 * m))]
    x = cum3 - (parts[0] if len(parts) == 1 else jnp.concatenate(parts, axis=1))
    qd = qs3 * jnp.exp2(x + upper_pen)
    kd = kk3 * jnp.exp2(lower_pen - x)
    c = groups * SUBLANES
    return qd.reshape(c, w).astype(BF16), kd.reshape(c, w).astype(BF16)


def _block_level(cum, qs, kk, block):
    c, _ = cum.shape
    local = [cum[:block]] + [cum[b:b + block] - cum[b - 1:b] for b in range(block, c, block)]
    local = local[0] if len(local) == 1 else jnp.concatenate(local, axis=0)
    return (qs * jnp.exp2(local)).astype(BF16), (kk * jnp.exp2(-local)).astype(BF16)


def _hgrn_constants():
    c = HG_CHUNK
    row = lax.broadcasted_iota(jnp.int32, (c, c), 0)
    col = lax.broadcasted_iota(jnp.int32, (c, c), 1)
    sub = lax.broadcasted_iota(jnp.int32, (1, SUBLANES, HG_DK), 1)
    fine = [m for m in _LEVELS if m < SUBLANES]
    same_block = {m: (row // (2 * m)) == (col // (2 * m)) for m in _LEVELS if 2 * m < c}
    return dict(
        same_block={m: mask.astype(F32) for m, mask in same_block.items()},
        diag=(row == col).astype(F32),
        causal_block={b: ((col <= row) if b == c else same_block[b // 2] & (col <= row)).astype(F32)
                      for b in _SAFE_BLOCKS},
        upper_pen={m: jnp.where((sub & m) != 0, 0.0, MASK_VALUE) for m in fine},
        lower_pen={m: jnp.where((sub & m) != 0, MASK_VALUE, 0.0) for m in fine},
        even_row=jnp.where((sub & 1) != 0, 0.0, 1.0),
    )


def _hgrn_tile(block, k, qs_ref, lf_ref, kk_ref, v_ref, gm_ref, gn, cum_s, o_ref, st_ref):
    c = HG_CHUNK
    heads = [slice(h * HG_DK, (h + 1) * HG_DK) for h in range(HG_HEADS)]
    shape3 = (c // SUBLANES, SUBLANES, HG_DK)
    n_chunks = HG_TILE // c

    def scores_pass(ci):
        rows = slice(ci * c, (ci + 1) * c)
        scores_bf, q_ins, k_outs, lasts = [], [], [], []
        for hs in heads:
            cum = cum_s[rows, hs]
            qs = qs_ref[rows, hs].astype(F32)
            kk = kk_ref[rows, hs].astype(F32)

            pairs = []
            for m in _LEVELS:
                if block is not None and 2 * m <= block:
                    continue
                if m >= SUBLANES:
                    qd, kd = _coarse_level(cum, qs, kk, m)
                elif m == 1:
                    qs3, kk3 = qs.reshape(shape3), kk.reshape(shape3)
                    qd3 = qs3 * jnp.exp2(lf_ref[rows, hs].reshape(shape3) + k["upper_pen"][m])
                    qd = qd3.reshape(c, HG_DK).astype(BF16)
                    kd = (kk3 * k["even_row"]).reshape(c, HG_DK).astype(BF16)
                else:
                    qd, kd = _fine_level(cum.reshape(shape3), qs.reshape(shape3), kk.reshape(shape3), m,
                                         k["upper_pen"][m], k["lower_pen"][m])
                pairs.append((qd, kd, k["same_block"].get(m)))
            last = cum[c - 1:c, :]
            q_in = (qs * jnp.exp2(cum)).astype(BF16)
            if block == c:
                k_side = kk * jnp.exp2(-cum)
                pairs.append((q_in, k_side.astype(BF16), k["causal_block"][block]))
                k_out = (k_side * jnp.exp2(last)).astype(BF16)
            else:
                if block is not None:
                    pairs.append(_block_level(cum, qs, kk, block) + (k["causal_block"][block],))
                else:
                    pairs.append((qs.astype(BF16), kk.astype(BF16), k["diag"]))
                k_out = (kk * jnp.exp2(last - cum)).astype(BF16)

            scores = None
            for qd, kd, mask in pairs:
                s = _dot_nt(qd, kd)
                if mask is not None:
                    s = s * mask
                scores = s if scores is None else scores + s

            scores_bf.append(scores.astype(BF16))
            q_ins.append(q_in)
            k_outs.append(k_out)
            lasts.append(last)
        return scores_bf, q_ins, k_outs, lasts

    def recurrence_pass(ci, scores_bf, q_ins, k_outs, lasts):
        rows = slice(ci * c, (ci + 1) * c)
        for h, hs in enumerate(heads):
            o = _dot_nt(q_ins[h], st_ref[h].astype(BF16)) + _dot(scores_bf[h], v_ref[rows, hs])
            o = o * lax.rsqrt(jnp.mean(o * o, axis=-1, keepdims=True) + NORM_EPS)
            o_ref[rows, hs] = (o * gn[:, hs] * gm_ref[rows, hs].astype(F32)).astype(BF16)

        for h, hs in enumerate(heads):
            st_ref[h] = st_ref[h] * jnp.exp2(lasts[h]) + _dot_tn(v_ref[rows, hs], k_outs[h])

    ahead = scores_pass(0)
    for ci in range(n_chunks):
        current, ahead = ahead, (scores_pass(ci + 1) if ci + 1 < n_chunks else None)
        recurrence_pass(ci, *current)


def _hgrn_kernel(qs_ref, lf_ref, kk_ref, v_ref, gm_ref, gn_ref, o_ref, cum_s, st_ref):
    c = HG_CHUNK

    @pl.when(pl.program_id(1) == 0)
    def _():
        st_ref[...] = jnp.zeros_like(st_ref)

    row = lax.broadcasted_iota(jnp.int32, (c, c), 0)
    col = lax.broadcasted_iota(jnp.int32, (c, c), 1)
    tri = (col <= row).astype(BF16)
    tri2 = jnp.concatenate([tri, tri], axis=1)

    worst = {b: None for b in _SAFE_BLOCKS}
    for ci in range(HG_TILE // c):
        rows = slice(ci * c, (ci + 1) * c)
        lf = lf_ref[rows, :]
        hi = lf.astype(BF16)
        lo = (lf - hi.astype(F32)).astype(BF16)
        cum = _dot(tri2, jnp.concatenate([hi, lo], axis=0))
        cum_s[rows, :] = cum
        for block in _SAFE_BLOCKS:
            ends = [cum[b - 1:b, :] for b in range(block, c + 1, block)]
            for b, end in enumerate(ends):
                decay = end if b == 0 else end - ends[b - 1]
                worst[block] = decay if worst[block] is None else jnp.minimum(worst[block], decay)
    safe = [jnp.min(worst[b]) >= -SAFE_BLOCK_DECAY for b in _SAFE_BLOCKS]

    consts = _hgrn_constants()
    args = (consts, qs_ref, lf_ref, kk_ref, v_ref, gm_ref, gn_ref[...], cum_s, o_ref, st_ref)

    not_yet = None
    for block, ok in zip(_SAFE_BLOCKS + (None,), safe + [True]):
        take = ok if not_yet is None else jnp.logical_and(not_yet, ok)

        @pl.when(take)
        def _(block=block):
            _hgrn_tile(block, *args)

        if block is not None:
            not_yet = jnp.logical_not(ok) if not_yet is None else jnp.logical_and(not_yet, jnp.logical_not(ok))


def _hgrn(qs, lf, kk, v, gm, gn, batch):
    n = qs.shape[0]
    tiles = n // batch // HG_TILE
    row = pl.BlockSpec((HG_TILE, HG_KEY_WIDTH), lambda b, t: (b * tiles + t, 0))
    return pl.pallas_call(
        _hgrn_kernel,
        grid=(batch, tiles),
        in_specs=[row, row, row, row, row, pl.BlockSpec((1, HG_VAL_WIDTH), lambda b, t: (0, 0))],
        out_specs=row,
        out_shape=jax.ShapeDtypeStruct((n, HG_VAL_WIDTH), BF16),
        scratch_shapes=[pltpu.VMEM((HG_TILE, HG_KEY_WIDTH), F32),
                        pltpu.VMEM((HG_HEADS, HG_DV, HG_DK), F32)],
        compiler_params=_params("arbitrary", "arbitrary"),
        name="hgrn2",
    )(qs, lf, kk, v, gm, gn)


def _out_kernel(main_transposed, om_ref, ome_ref, w_ref, g_ref, h_ref, o_ref):
    main_dot = _dot_tn if main_transposed else _dot
    y = main_dot(om_ref[...], w_ref[:HG_VAL_WIDTH, :]) + _dot(ome_ref[...], w_ref[HG_VAL_WIDTH:, :])
    o_ref[...] = h_ref[...] + _rms_norm(y, g_ref[...])


def _out_proj(o_main, o_mem, w, layer, g, h, name, main_transposed=False):
    n = h.shape[0]
    row = lambda width: pl.BlockSpec((OUT_TILE, width), lambda i: (i, 0))
    col = lambda width: pl.BlockSpec((width, OUT_TILE), lambda i: (0, i))
    full = lambda a: pl.BlockSpec(a.shape, lambda i: (0,) * a.ndim)
    main = col(HG_VAL_WIDTH) if main_transposed else row(HG_VAL_WIDTH)
    return pl.pallas_call(
        functools.partial(_out_kernel, main_transposed),
        grid=(n // OUT_TILE,),
        in_specs=[main, row(MEM_WIDTH), _layer_weight(w, layer), full(g), row(D_MODEL)],
        out_specs=row(D_MODEL),
        out_shape=jax.ShapeDtypeStruct((n, D_MODEL), F32),
        compiler_params=_params("parallel"),
        name=name,
    )(o_main, o_mem, w, g, h)


def _shared_kv_kernel(x_ref, g_ref, w_ref, cos_ref, sin_ref, k_ref, v_ref):
    first = _first_half_mask(IN_SUB_TILE)
    heads_per_block = LANES // SWA_HEAD_DIM
    for sub, xn in enumerate(_normed_sub_tiles(None, x_ref, g_ref)):
        rows = slice(sub * IN_SUB_TILE, (sub + 1) * IN_SUB_TILE)
        kv = _dot(xn, w_ref[...])
        cos = cos_ref[rows, :]
        sin = sin_ref[rows, :]
        for cb in range(SWA_KV_WIDTH // LANES):
            kr = _rope_block(kv[:, cb * LANES:(cb + 1) * LANES], cos, sin, first).astype(BF16)
            for j in range(heads_per_block):
                k_ref[cb * heads_per_block + j, rows, :] = kr[:, j * SWA_HEAD_DIM:(j + 1) * SWA_HEAD_DIM]
        v_ref[:, rows] = kv[:, SWA_KV_WIDTH:].T.astype(BF16)


def _shared_kv(h, g, w, cos, sin):
    n = h.shape[0]
    row = lambda width: pl.BlockSpec((IN_TILE, width), lambda i: (i, 0))
    full = lambda a: pl.BlockSpec(a.shape, lambda i: (0,) * a.ndim)
    return pl.pallas_call(
        _shared_kv_kernel,
        grid=(n // IN_TILE,),
        in_specs=[row(D_MODEL), full(g), full(w), row(LANES), row(LANES)],
        out_specs=[pl.BlockSpec((SWA_KV_HEADS, IN_TILE, SWA_HEAD_DIM), lambda i: (0, i, 0)),
                   pl.BlockSpec((SWA_KV_WIDTH, IN_TILE), lambda i: (0, i))],
        out_shape=[jax.ShapeDtypeStruct((SWA_KV_HEADS, n, SWA_HEAD_DIM), BF16),
                   jax.ShapeDtypeStruct((SWA_KV_WIDTH, n), BF16)],
        compiler_params=_params("parallel"),
        name="shared_kv",
    )(h, g, w, cos, sin)


def _b_in_kernel(prev_transposed, h_ref, *refs):
    prev, (g_ref, w_ref, cos_ref, sin_ref, mk_ref, mv_ref), outs = _split_refs(prev_transposed, refs, 6)
    qt_ref, gmt_ref, om_ref = outs
    first = _first_half_mask(IN_SUB_TILE)
    scale = SWA_HEAD_DIM ** -0.5 * LOG2E
    col_mq = 2 * SWA_WIDTH
    col_mg = col_mq + MEM_WIDTH

    for sub, xn in enumerate(_normed_sub_tiles(prev, h_ref, g_ref)):
        rows = slice(sub * IN_SUB_TILE, (sub + 1) * IN_SUB_TILE)

        def proj(lo, width):
            return _dot(xn, w_ref[:, lo:lo + width])

        mq = proj(col_mq, MEM_WIDTH)
        mem_gate = _silu(proj(col_mg, MEM_WIDTH))

        q = proj(0, SWA_WIDTH)
        probs, inv_sums = _memory_probs(mq, mk_ref)
        cos = cos_ref[rows, :]
        sin = sin_ref[rows, :]
        for cb in range(SWA_WIDTH // LANES):
            cs = slice(cb * LANES, (cb + 1) * LANES)
            qt_ref[cs, rows] = (_rope_block(q[:, cs], cos, sin, first) * scale).T.astype(BF16)

        gate = _silu(proj(SWA_WIDTH, SWA_WIDTH))
        _memory_readout(probs, inv_sums, mem_gate, mv_ref, om_ref, rows)
        for cb in range(SWA_WIDTH // LANES):
            cs = slice(cb * LANES, (cb + 1) * LANES)
            gmt_ref[cs, rows] = gate[:, cs].T.astype(BF16)


def _b_in_proj(layer, h, pending, g, w, cos, sin, mem_k, mem_v, tiles_per_seq):
    n = h.shape[0]
    row = lambda width: pl.BlockSpec((IN_TILE, width), lambda i: (i, 0))
    col = lambda width: pl.BlockSpec((width, IN_TILE), lambda i: (0, i))
    full = lambda a: pl.BlockSpec(a.shape, lambda i: (0,) * a.ndim)
    mem = pl.BlockSpec((None, None, MEM_TOKENS, MEM_WIDTH), lambda i: (layer, i // tiles_per_seq, 0, 0))
    out_specs = [col(SWA_WIDTH), col(SWA_WIDTH), row(MEM_WIDTH)]
    out_shape = [jax.ShapeDtypeStruct((SWA_WIDTH, n), BF16), jax.ShapeDtypeStruct((SWA_WIDTH, n), BF16),
                 jax.ShapeDtypeStruct((n, MEM_WIDTH), BF16)]
    if pending is not None:
        out_specs, out_shape = [row(D_MODEL)] + out_specs, [jax.ShapeDtypeStruct((n, D_MODEL), F32)] + out_shape
    pend_args, pend_specs = _pending_specs(pending, row, col, full)
    return pl.pallas_call(
        functools.partial(_b_in_kernel, None if pending is None else pending[-1]),
        grid=(n // IN_TILE,),
        in_specs=[row(D_MODEL)] + pend_specs
        + [full(g), _layer_weight(w, layer - N_A_LAYERS), row(LANES), row(LANES), mem, mem],
        out_specs=out_specs,
        out_shape=out_shape,
        compiler_params=_params("parallel"),
        name=f"b_in_proj_{layer}",
    )(h, *pend_args, g, w, cos, sin, mem_k, mem_v)


def _swa_kernel(sink_ref, qt_ref, kc_ref, kp_ref, vtc_ref, vtp_ref, gmt_ref, ot_ref):
    blk = WINDOW
    dh = SWA_HEAD_DIM
    ki = lax.broadcasted_iota(jnp.int32, (blk, blk), 0)
    qi = lax.broadcasted_iota(jnp.int32, (blk, blk), 1)
    from_prev = ki > qi
    from_prev_bf = from_prev.astype(BF16)
    prev_bias = jnp.where(pl.program_id(1) > 0, 0.0, MASK_VALUE)

    def scores(sb, g):
        cols = slice(sb * blk, (sb + 1) * blk)
        k2 = (jnp.concatenate([kp_ref[g], kc_ref[g, cols, :]], axis=0) if sb == 0
              else kc_ref[g, (sb - 1) * blk:(sb + 1) * blk, :])
        heads = range(g * SWA_GROUP, (g + 1) * SWA_GROUP)
        qt = jnp.concatenate([qt_ref[h * dh:(h + 1) * dh, cols] for h in heads], axis=1)
        return _dot(k2, qt)

    def attend(sb, g, st):
        cols = slice(sb * blk, (sb + 1) * blk)
        heads = range(g * SWA_GROUP, (g + 1) * SWA_GROUP)
        ps, inv_den = [], []
        for j in range(SWA_GROUP):
            s_prev = st[:blk, j * blk:(j + 1) * blk]
            if sb == 0:
                s_prev = s_prev + prev_bias
            s = jnp.where(from_prev, s_prev, st[blk:, j * blk:(j + 1) * blk])
            sink = sink_ref[heads[j]] * LOG2E
            m = jnp.maximum(jnp.max(s, axis=0, keepdims=True), sink)
            p = jnp.exp2(s - m)
            inv_den.append(1.0 / (jnp.sum(p, axis=0, keepdims=True) + jnp.exp2(sink - m)))
            p = p.astype(BF16)
            p_prev = p * from_prev_bf
            ps.append(jnp.concatenate([p_prev, p - p_prev], axis=0))
        vrows = slice(g * dh, (g + 1) * dh)
        vt2 = (jnp.concatenate([vtp_ref[vrows, :], vtc_ref[vrows, cols]], axis=1) if sb == 0
               else vtc_ref[vrows, (sb - 1) * blk:(sb + 1) * blk])
        ot = _dot(vt2, jnp.concatenate(ps, axis=1))
        for j, h in enumerate(heads):
            rows = slice(h * dh, (h + 1) * dh)
            o = ot[:, j * blk:(j + 1) * blk] * inv_den[j]
            ot_ref[rows, cols] = (o * gmt_ref[rows, cols].astype(F32)).astype(BF16)

    kv_heads = range(SWA_KV_HEADS)
    n_blocks = SWA_TILE // blk
    ahead = [scores(0, g) for g in kv_heads]
    for sb in range(n_blocks):
        current, ahead = ahead, ([scores(sb + 1, g) for g in kv_heads] if sb + 1 < n_blocks else None)
        for g in kv_heads:
            attend(sb, g, current[g])


def _swa(sinks, qt, k, vt, gmt, batch):
    n = qt.shape[1]
    tiles = n // batch // SWA_TILE
    sub = SWA_TILE // WINDOW
    tile = lambda b, t: b * tiles + t
    prev = lambda b, t: jnp.maximum(tile(b, t) * sub - 1, 0)
    wide = pl.BlockSpec((SWA_WIDTH, SWA_TILE), lambda b, t: (0, tile(b, t)))
    return pl.pallas_call(
        _swa_kernel,
        grid=(batch, tiles),
        in_specs=[pl.BlockSpec(memory_space=pltpu.SMEM),
                  wide,
                  pl.BlockSpec((SWA_KV_HEADS, SWA_TILE, SWA_HEAD_DIM), lambda b, t: (0, tile(b, t), 0)),
                  pl.BlockSpec((SWA_KV_HEADS, WINDOW, SWA_HEAD_DIM), lambda b, t: (0, prev(b, t), 0)),
                  pl.BlockSpec((SWA_KV_WIDTH, SWA_TILE), lambda b, t: (0, tile(b, t))),
                  pl.BlockSpec((SWA_KV_WIDTH, WINDOW), lambda b, t: (0, prev(b, t))),
                  wide],
        out_specs=wide,
        out_shape=jax.ShapeDtypeStruct((SWA_WIDTH, n), BF16),
        compiler_params=_params("parallel", "arbitrary"),
        name="swa",
    )(sinks, qt, k, k, vt, vt, gmt)


def kernel(x, mem, positions, pre_norm_g, post_norm_g, mem_norm_g, w_mem_kv, a_w_in, a_lb_logits, a_out_norm_g, a_w_out, kv_norm_g, w_kv_shared, b_w_in, b_sinks, b_w_out):
    batch, seq, _ = x.shape
    n = batch * seq
    assert seq % max(IN_TILE, OUT_TILE, HG_TILE, SWA_TILE) == 0 and n % ROPE_TILE == 0
    tiles_per_seq = seq // IN_TILE

    h = x.reshape(n, D_MODEL)
    mem_k, mem_v = _mem_kv(mem, mem_norm_g, w_mem_kv)
    cos, sin = _rope_tables(positions)
    a_w_in = a_w_in.astype(BF16)
    a_w_out = a_w_out.astype(BF16)
    b_w_in = b_w_in.astype(BF16)
    b_w_out = b_w_out.astype(BF16)
    k_sh = v_sh = None
    pending = None

    for layer in range(DEPTH):
        pre_g = pre_norm_g[layer].reshape(1, D_MODEL)
        post_g = post_norm_g[layer].reshape(1, D_MODEL)
        if layer < N_A_LAYERS:
            outs = _a_in_proj(layer, h, pending, pre_g, a_w_in, a_lb_logits, mem_k, mem_v, tiles_per_seq)
            if pending is not None:
                h, *outs = outs
            qs, lf, kk, v, gm, o_mem = outs
            o_main = _hgrn(qs, lf, kk, v, gm, a_out_norm_g[layer].reshape(1, HG_VAL_WIDTH), batch)
            pending = (o_main, o_mem, a_w_out, layer, post_g, False)
        else:
            j = layer - N_A_LAYERS
            outs = _b_in_proj(layer, h, pending, pre_g, b_w_in, cos, sin, mem_k, mem_v, tiles_per_seq)
            if pending is not None:
                h, *outs = outs
            qt, gmt, o_mem = outs
            if k_sh is None:
                k_sh, v_sh = _shared_kv(h, kv_norm_g.reshape(1, D_MODEL), w_kv_shared.astype(BF16), cos, sin)
            o_main_t = _swa(b_sinks[j], qt, k_sh, v_sh, gmt, batch)
            pending = (o_main_t, o_mem, b_w_out, j, post_g, True)
    o_main, o_mem, w_out, w_layer, post_g, transposed = pending
    h = _out_proj(o_main, o_mem, w_out, w_layer, post_g, h, "out_proj_last", main_transposed=transposed)
    return h.reshape(batch, seq, D_MODEL)
```

```python
import functools

import jax
import jax.numpy as jnp
from jax import lax
from jax.experimental import pallas as pl
from jax.experimental.pallas import tpu as pltpu

D_MODEL = 1024
DEPTH = 4
N_A_LAYERS = DEPTH // 2

HG_HEADS = 8
HG_DK = 128
HG_DV = D_MODEL // HG_HEADS
HG_KEY_WIDTH = HG_HEADS * HG_DK
HG_VAL_WIDTH = HG_HEADS * HG_DV
HG_CHUNK = 64

SWA_Q_HEADS = 16
SWA_KV_HEADS = 4
SWA_GROUP = SWA_Q_HEADS // SWA_KV_HEADS
SWA_HEAD_DIM = 64
SWA_WIDTH = SWA_Q_HEADS * SWA_HEAD_DIM
SWA_KV_WIDTH = SWA_KV_HEADS * SWA_HEAD_DIM
WINDOW = 128

MEM_TOKENS = 256
MEM_HEADS = 4
MEM_HEAD_DIM = 128
MEM_WIDTH = MEM_HEADS * MEM_HEAD_DIM

ROPE_THETA = 10000.0
NORM_EPS = 1e-6

LANES = 128
SUBLANES = 8
MASK_VALUE = -1e30
LOG2E = 1.4426950408889634

IN_TILE = 512
IN_SUB_TILE = 256
OUT_TILE = 1024
HG_TILE = 512
SWA_TILE = 1024
ROPE_TILE = 2048
VMEM_LIMIT = 56 * 1024 * 1024

F32 = jnp.float32
BF16 = jnp.bfloat16

_NT = (((1,), (1,)), ((), ()))
_TN = (((0,), (0,)), ((), ()))


def _dot(a, b):
    return jnp.dot(a, b, preferred_element_type=F32)


def _dot_nt(a, b):
    return lax.dot_general(a, b, _NT, preferred_element_type=F32)


def _dot_tn(a, b):
    return lax.dot_general(a, b, _TN, preferred_element_type=F32)


def _rms_norm(x, g):
    ms = jnp.mean(x * x, axis=-1, keepdims=True)
    return x * lax.rsqrt(ms + NORM_EPS) * g


def _silu(x):
    h = 0.5 * x
    return h + h * jnp.tanh(h)


def _params(*semantics):
    return pltpu.CompilerParams(dimension_semantics=semantics, vmem_limit_bytes=VMEM_LIMIT)


def _layer_weight(w, layer):
    return pl.BlockSpec((None,) + w.shape[1:], lambda *_: (layer, 0, 0), pipeline_mode=pl.Buffered(1))


def _layer_input(prev, h_ref, rows):
    if prev is None:
        return h_ref[rows, :]
    om_ref, ome_ref, wout_ref, pg_ref, hnew_ref, main_transposed = prev
    if main_transposed:
        y = _dot_tn(om_ref[:, rows], wout_ref[:HG_VAL_WIDTH, :])
    else:
        y = _dot(om_ref[rows, :], wout_ref[:HG_VAL_WIDTH, :])
    y = y + _dot(ome_ref[rows, :], wout_ref[HG_VAL_WIDTH:, :])
    h_new = h_ref[rows, :] + _rms_norm(y, pg_ref[...])
    hnew_ref[rows, :] = h_new
    return h_new


def _normed_sub_tiles(prev, h_ref, g_ref):
    return [_rms_norm(_layer_input(prev, h_ref, slice(s * IN_SUB_TILE, (s + 1) * IN_SUB_TILE)),
                      g_ref[...]).astype(BF16) for s in range(IN_TILE // IN_SUB_TILE)]


def _split_refs(prev_transposed, refs, n_in):
    if prev_transposed is None:
        return None, refs[:n_in], refs[n_in:]
    prev_in, ins, (hnew_ref, *outs) = refs[:4], refs[4:4 + n_in], refs[4 + n_in:]
    return (*prev_in, hnew_ref, prev_transposed), ins, outs


def _pending_specs(pending, row, col, full):
    if pending is None:
        return [], []
    o_main, o_mem, w_out, w_layer, post_g, transposed = pending
    main = col(HG_VAL_WIDTH) if transposed else row(HG_VAL_WIDTH)
    return [o_main, o_mem, w_out, post_g], [main, row(MEM_WIDTH), _layer_weight(w_out, w_layer), full(post_g)]


ROPE_FREQS = SWA_HEAD_DIM // 2
ROPE_PACK = LANES // ROPE_FREQS


def _rope_table_kernel(pos_ref, invf_ref, cos_ref, sin_ref):
    ang = pos_ref[...].astype(F32) * invf_ref[...]
    rows = ang.shape[0]
    src = lax.broadcasted_iota(jnp.int32, (LANES, LANES), 0)
    dst = lax.broadcasted_iota(jnp.int32, (LANES, LANES), 1)
    sin_sign = jnp.where((dst & ROPE_FREQS) == 0, -1.0, 1.0)
    for table, out_ref, sign in ((jnp.cos(ang), cos_ref, 1.0), (jnp.sin(ang), sin_ref, sin_sign)):
        hi = table.astype(BF16)
        rest = table - hi.astype(F32)
        mid = rest.astype(BF16)
        lo = (rest - mid.astype(F32)).astype(BF16)
        for i in range(ROPE_PACK):
            pick = src == i * ROPE_FREQS + (dst & (ROPE_FREQS - 1))
            spread = jnp.where(pick, sign, 0.0).astype(BF16)
            out_ref[pl.ds(i, rows, stride=ROPE_PACK), :] = _dot(hi, spread) + _dot(mid, spread) + _dot(lo, spread)


def _rope_tables(positions):
    n = positions.size
    inv_freq = ROPE_THETA ** (-jnp.arange(0, SWA_HEAD_DIM, 2, dtype=F32) / SWA_HEAD_DIM)
    invf = jnp.tile(inv_freq, ROPE_PACK).reshape(1, LANES)
    pos = jnp.repeat(positions.reshape(n // ROPE_PACK, ROPE_PACK), ROPE_FREQS, axis=1)
    row = pl.BlockSpec((ROPE_TILE, LANES), lambda i: (i, 0))
    return pl.pallas_call(
        _rope_table_kernel,
        grid=(n // ROPE_TILE,),
        in_specs=[pl.BlockSpec((ROPE_TILE // ROPE_PACK, LANES), lambda i: (i, 0)),
                  pl.BlockSpec((1, LANES), lambda i: (0, 0))],
        out_specs=[row, row],
        out_shape=[jax.ShapeDtypeStruct((n, LANES), F32)] * 2,
        compiler_params=_params("parallel"),
        name="rope_tables",
    )(pos, invf)


def _rope_block(xc, cos, sin, first_half):
    swapped = jnp.where(first_half,
                        pltpu.roll(xc, LANES - SWA_HEAD_DIM // 2, 1),
                        pltpu.roll(xc, SWA_HEAD_DIM // 2, 1))
    return xc * cos + swapped * sin


def _first_half_mask(rows):
    lane = lax.broadcasted_iota(jnp.int32, (rows, LANES), 1)
    return (lane & (SWA_HEAD_DIM // 2)) == 0


def _mem_kv_kernel(mem_ref, g_ref, w_ref, k_ref, v_ref):
    x = mem_ref[...]
    xhat = x * lax.rsqrt(jnp.mean(x * x, axis=-1, keepdims=True) + NORM_EPS)
    for layer in range(DEPTH):
        kv = _dot((xhat * g_ref[layer]).astype(BF16), w_ref[layer].astype(BF16))
        k_ref[layer] = kv[:, :MEM_WIDTH].astype(BF16)
        v_ref[layer] = kv[:, MEM_WIDTH:].astype(BF16)


def _mem_kv(mem, mem_norm_g, w_mem_kv):
    b = mem.shape[0]
    out = pl.BlockSpec((DEPTH, None, MEM_TOKENS, MEM_WIDTH), lambda i: (0, i, 0, 0))
    return pl.pallas_call(
        _mem_kv_kernel,
        grid=(b,),
        in_specs=[
            pl.BlockSpec((None, MEM_TOKENS, D_MODEL), lambda i: (i, 0, 0)),
            pl.BlockSpec((DEPTH, 1, D_MODEL), lambda i: (0, 0, 0)),
            pl.BlockSpec(w_mem_kv.shape, lambda i: (0, 0, 0), pipeline_mode=pl.Buffered(1)),
        ],
        out_specs=[out, out],
        out_shape=[jax.ShapeDtypeStruct((DEPTH, b, MEM_TOKENS, MEM_WIDTH), BF16)] * 2,
        compiler_params=_params("parallel"),
        name="mem_kv",
    )(mem, mem_norm_g.reshape(DEPTH, 1, D_MODEL), w_mem_kv)


def _memory_probs(mq, mk_ref):
    mq = (mq * (MEM_HEAD_DIM ** -0.5 * LOG2E)).astype(BF16)
    probs, inv_sums = [], []
    for hh in range(MEM_HEADS):
        hs = slice(hh * MEM_HEAD_DIM, (hh + 1) * MEM_HEAD_DIM)
        s = _dot_nt(mq[:, hs], mk_ref[:, hs])
        p = jnp.exp2(s - jnp.max(s, axis=-1, keepdims=True))
        probs.append(p.astype(BF16))
        inv_sums.append(1.0 / jnp.sum(p, axis=-1, keepdims=True))
    return probs, inv_sums


def _memory_readout(probs, inv_sums, gate, mv_ref, out_ref, rows):
    for hh in range(MEM_HEADS):
        hs = slice(hh * MEM_HEAD_DIM, (hh + 1) * MEM_HEAD_DIM)
        o = _dot(probs[hh], mv_ref[:, hs]) * inv_sums[hh]
        out_ref[rows, hs] = (o * gate[:, hs]).astype(BF16)


def _a_in_kernel(layer, prev_transposed, h_ref, *refs):
    prev, (g_ref, w_ref, lbl_ref, mk_ref, mv_ref), outs = _split_refs(prev_transposed, refs, 5)
    qs_ref, lf_ref, kk_ref, v_ref, gm_ref, om_ref = outs

    logits = lbl_ref[...]
    e = jnp.exp(logits - jnp.max(logits, axis=0, keepdims=True))
    lb = jnp.sum(e[:layer + 1], axis=0, keepdims=True) / jnp.sum(e, axis=0, keepdims=True)
    mid = 0.5 * (1.0 + lb)
    half = 0.5 * (1.0 - lb)

    col_q, col_f, col_v = 0, HG_KEY_WIDTH, 2 * HG_KEY_WIDTH
    col_g = col_v + HG_VAL_WIDTH
    col_mq = col_g + HG_VAL_WIDTH
    col_mg = col_mq + MEM_WIDTH

    for sub, xn in enumerate(_normed_sub_tiles(prev, h_ref, g_ref)):
        rows = slice(sub * IN_SUB_TILE, (sub + 1) * IN_SUB_TILE)

        def proj(lo, width):
            return _dot(xn, w_ref[:, lo:lo + width])

        mq = proj(col_mq, MEM_WIDTH)
        mem_gate = _silu(proj(col_mg, MEM_WIDTH))
        qs_ref[rows, :] = _silu(proj(col_q, HG_KEY_WIDTH)).astype(BF16)
        probs, inv_sums = _memory_probs(mq, mk_ref)

        th = half * jnp.tanh(0.5 * proj(col_f, HG_KEY_WIDTH))
        lf_ref[rows, :] = jnp.log2(jnp.maximum(mid + th, lb))
        kk_ref[rows, :] = (half - th).astype(BF16)

        _memory_readout(probs, inv_sums, mem_gate, mv_ref, om_ref, rows)
        v_ref[rows, :] = proj(col_v, HG_VAL_WIDTH).astype(BF16)
        gm_ref[rows, :] = _silu(proj(col_g, HG_VAL_WIDTH)).astype(BF16)


def _a_in_proj(layer, h, pending, g, w, lb_logits, mem_k, mem_v, tiles_per_seq):
    n = h.shape[0]
    row = lambda width: pl.BlockSpec((IN_TILE, width), lambda i: (i, 0))
    col = lambda width: pl.BlockSpec((width, IN_TILE), lambda i: (0, i))
    full = lambda a: pl.BlockSpec(a.shape, lambda i: (0,) * a.ndim)
    mem = pl.BlockSpec((None, None, MEM_TOKENS, MEM_WIDTH), lambda i: (layer, i // tiles_per_seq, 0, 0))
    widths = (HG_KEY_WIDTH, HG_KEY_WIDTH, HG_KEY_WIDTH, HG_VAL_WIDTH, HG_VAL_WIDTH, MEM_WIDTH)
    dtypes = (BF16, F32, BF16, BF16, BF16, BF16)
    if pending is not None:
        widths, dtypes = (D_MODEL,) + widths, (F32,) + dtypes
    pend_args, pend_specs = _pending_specs(pending, row, col, full)
    return pl.pallas_call(
        functools.partial(_a_in_kernel, layer, None if pending is None else pending[-1]),
        grid=(n // IN_TILE,),
        in_specs=[row(D_MODEL)] + pend_specs + [full(g), _layer_weight(w, layer), full(lb_logits), mem, mem],
        out_specs=[row(wd) for wd in widths],
        out_shape=[jax.ShapeDtypeStruct((n, wd), dt) for wd, dt in zip(widths, dtypes)],
        compiler_params=_params("parallel"),
        name=f"a_in_proj_{layer}",
    )(h, *pend_args, g, w, lb_logits, mem_k, mem_v)


_LEVELS = (32, 16, 8, 4, 2, 1)
_SAFE_BLOCKS = (HG_CHUNK, 16)
SAFE_BLOCK_DECAY = 100.0


def _coarse_level(cum, qs, kk, m):
    c, w = cum.shape
    early = m % (2 * SUBLANES) == 0
    dt = BF16 if early else F32
    zeros = jnp.zeros((m, w), dt)
    qd, kd = [], []
    for blk in range(c // (2 * m)):
        lo, mid, hi = blk * 2 * m, blk * 2 * m + m, (blk + 1) * 2 * m
        ref = cum[mid - 1:mid, :]
        kd += [(kk[lo:mid] * jnp.exp2(ref - cum[lo:mid])).astype(dt), zeros]
        qd += [zeros, (qs[mid:hi] * jnp.exp2(cum[mid:hi] - ref)).astype(dt)]
    return jnp.concatenate(qd, axis=0).astype(BF16), jnp.concatenate(kd, axis=0).astype(BF16)


def _fine_level(cum3, qs3, kk3, m, upper_pen, lower_pen):
    groups, _, w = cum3.shape
    parts = [jnp.broadcast_to(cum3[:, blk * 2 * m + m - 1:blk * 2 * m + m, :], (groups, 2 * m, w))
             for blk in range(SUBLANES // (2 * m))]
    x = cum3 - (parts[0] if len(parts) == 1 else jnp.concatenate(parts, axis=1))
    qd = qs3 * jnp.exp2(x + upper_pen)
    kd = kk3 * jnp.exp2(lower_pen - x)
    c = groups * SUBLANES
    return qd.reshape(c, w).astype(BF16), kd.reshape(c, w).astype(BF16)


def _block_level(cum, qs, kk, block):
    c, _ = cum.shape
    local = [cum[:block]] + [cum[b:b + block] - cum[b - 1:b] for b in range(block, c, block)]
    local = local[0] if len(local) == 1 else jnp.concatenate(local, axis=0)
    return (qs * jnp.exp2(local)).astype(BF16), (kk * jnp.exp2(-local)).astype(BF16)


def _hgrn_constants():
    c = HG_CHUNK
    row = lax.broadcasted_iota(jnp.int32, (c, c), 0)
    col = lax.broadcasted_iota(jnp.int32, (c, c), 1)
    sub = lax.broadcasted_iota(jnp.int32, (1, SUBLANES, HG_DK), 1)
    fine = [m for m in _LEVELS if m < SUBLANES]
    same_block = {m: (row // (2 * m)) == (col // (2 * m)) for m in _LEVELS if 2 * m < c}
    return dict(
        same_block={m: mask.astype(F32) for m, mask in same_block.items()},
        diag=(row == col).astype(F32),
        causal_block={b: ((col <= row) if b == c else same_block[b // 2] & (col <= row)).astype(F32)
                      for b in _SAFE_BLOCKS},
        upper_pen={m: jnp.where((sub & m) != 0, 0.0, MASK_VALUE) for m in fine},
        lower_pen={m: jnp.where((sub & m) != 0, MASK_VALUE, 0.0) for m in fine},
        even_row=jnp.where((sub & 1) != 0, 0.0, 1.0),
    )


def _hgrn_tile(block, k, qs_ref, lf_ref, kk_ref, v_ref, gm_ref, gn, cum_s, o_ref, st_ref):
    c = HG_CHUNK
    heads = [slice(h * HG_DK, (h + 1) * HG_DK) for h in range(HG_HEADS)]
    shape3 = (c // SUBLANES, SUBLANES, HG_DK)
    n_chunks = HG_TILE // c

    def scores_pass(ci):
        rows = slice(ci * c, (ci + 1) * c)
        scores_bf, q_ins, k_outs, lasts = [], [], [], []
        for hs in heads:
            cum = cum_s[rows, hs]
            qs = qs_ref[rows, hs].astype(F32)
            kk = kk_ref[rows, hs].astype(F32)

            pairs = []
            for m in _LEVELS:
                if block is not None and 2 * m <= block:
                    continue
                if m >= SUBLANES:
                    qd, kd = _coarse_level(cum, qs, kk, m)
                elif m == 1:
                    qs3, kk3 = qs.reshape(shape3), kk.reshape(shape3)
                    qd3 = qs3 * jnp.exp2(lf_ref[rows, hs].reshape(shape3) + k["upper_pen"][m])
                    qd = qd3.reshape(c, HG_DK).astype(BF16)
                    kd = (kk3 * k["even_row"]).reshape(c, HG_DK).astype(BF16)
                else:
                    qd, kd = _fine_level(cum.reshape(shape3), qs.reshape(shape3), kk.reshape(shape3), m,
                                         k["upper_pen"][m], k["lower_pen"][m])
                pairs.append((qd, kd, k["same_block"].get(m)))
            last = cum[c - 1:c, :]
            q_in = (qs * jnp.exp2(cum)).astype(BF16)
            if block == c:
                k_side = kk * jnp.exp2(-cum)
                pairs.append((q_in, k_side.astype(BF16), k["causal_block"][block]))
                k_out = (k_side * jnp.exp2(last)).astype(BF16)
            else:
                if block is not None:
                    pairs.append(_block_level(cum, qs, kk, block) + (k["causal_block"][block],))
                else:
                    pairs.append((qs.astype(BF16), kk.astype(BF16), k["diag"]))
                k_out = (kk * jnp.exp2(last - cum)).astype(BF16)

            scores = None
            for qd, kd, mask in pairs:
                s = _dot_nt(qd, kd)
                if mask is not None:
                    s = s * mask
                scores = s if scores is None else scores + s

            scores_bf.append(scores.astype(BF16))
            q_ins.append(q_in)
            k_outs.append(k_out)
            lasts.append(last)
        return scores_bf, q_ins, k_outs, lasts

    def recurrence_pass(ci, scores_bf, q_ins, k_outs, lasts):
        rows = slice(ci * c, (ci + 1) * c)
        for h, hs in enumerate(heads):
            o = _dot_nt(q_ins[h], st_ref[h].astype(BF16)) + _dot(scores_bf[h], v_ref[rows, hs])
            o = o * lax.rsqrt(jnp.mean(o * o, axis=-1, keepdims=True) + NORM_EPS)
            o_ref[rows, hs] = (o * gn[:, hs] * gm_ref[rows, hs].astype(F32)).astype(BF16)

        for h, hs in enumerate(heads):
            st_ref[h] = st_ref[h] * jnp.exp2(lasts[h]) + _dot_tn(v_ref[rows, hs], k_outs[h])

    ahead = scores_pass(0)
    for ci in range(n_chunks):
        current, ahead = ahead, (scores_pass(ci + 1) if ci + 1 < n_chunks else None)
        recurrence_pass(ci, *current)


def _hgrn_kernel(qs_ref, lf_ref, kk_ref, v_ref, gm_ref, gn_ref, o_ref, cum_s, st_ref):
    c = HG_CHUNK

    @pl.when(pl.program_id(1) == 0)
    def _():
        st_ref[...] = jnp.zeros_like(st_ref)

    row = lax.broadcasted_iota(jnp.int32, (c, c), 0)
    col = lax.broadcasted_iota(jnp.int32, (c, c), 1)
    tri = (col <= row).astype(BF16)
    tri2 = jnp.concatenate([tri, tri], axis=1)

    worst = {b: None for b in _SAFE_BLOCKS}
    for ci in range(HG_TILE // c):
        rows = slice(ci * c, (ci + 1) * c)
        lf = lf_ref[rows, :]
        hi = lf.astype(BF16)
        lo = (lf - hi.astype(F32)).astype(BF16)
        cum = _dot(tri2, jnp.concatenate([hi, lo], axis=0))
        cum_s[rows, :] = cum
        for block in _SAFE_BLOCKS:
            ends = [cum[b - 1:b, :] for b in range(block, c + 1, block)]
            for b, end in enumerate(ends):
                decay = end if b == 0 else end - ends[b - 1]
                worst[block] = decay if worst[block] is None else jnp.minimum(worst[block], decay)
    safe = [jnp.min(worst[b]) >= -SAFE_BLOCK_DECAY for b in _SAFE_BLOCKS]

    consts = _hgrn_constants()
    args = (consts, qs_ref, lf_ref, kk_ref, v_ref, gm_ref, gn_ref[...], cum_s, o_ref, st_ref)

    not_yet = None
    for block, ok in zip(_SAFE_BLOCKS + (None,), safe + [True]):
        take = ok if not_yet is None else jnp.logical_and(not_yet, ok)

        @pl.when(take)
        def _(block=block):
            _hgrn_tile(block, *args)

        if block is not None:
            not_yet = jnp.logical_not(ok) if not_yet is None else jnp.logical_and(not_yet, jnp.logical_not(ok))


def _hgrn(qs, lf, kk, v, gm, gn, batch):
    n = qs.shape[0]
    tiles = n // batch // HG_TILE
    row = pl.BlockSpec((HG_TILE, HG_KEY_WIDTH), lambda b, t: (b * tiles + t, 0))
    return pl.pallas_call(
        _hgrn_kernel,
        grid=(batch, tiles),
        in_specs=[row, row, row, row, row, pl.BlockSpec((1, HG_VAL_WIDTH), lambda b, t: (0, 0))],
        out_specs=row,
        out_shape=jax.ShapeDtypeStruct((n, HG_VAL_WIDTH), BF16),
        scratch_shapes=[pltpu.VMEM((HG_TILE, HG_KEY_WIDTH), F32),
                        pltpu.VMEM((HG_HEADS, HG_DV, HG_DK), F32)],
        compiler_params=_params("arbitrary", "arbitrary"),
        name="hgrn2",
    )(qs, lf, kk, v, gm, gn)


def _out_kernel(main_transposed, om_ref, ome_ref, w_ref, g_ref, h_ref, o_ref):
    main_dot = _dot_tn if main_transposed else _dot
    y = main_dot(om_ref[...], w_ref[:HG_VAL_WIDTH, :]) + _dot(ome_ref[...], w_ref[HG_VAL_WIDTH:, :])
    o_ref[...] = h_ref[...] + _rms_norm(y, g_ref[...])


def _out_proj(o_main, o_mem, w, layer, g, h, name, main_transposed=False):
    n = h.shape[0]
    row = lambda width: pl.BlockSpec((OUT_TILE, width), lambda i: (i, 0))
    col = lambda width: pl.BlockSpec((width, OUT_TILE), lambda i: (0, i))
    full = lambda a: pl.BlockSpec(a.shape, lambda i: (0,) * a.ndim)
    main = col(HG_VAL_WIDTH) if main_transposed else row(HG_VAL_WIDTH)
    return pl.pallas_call(
        functools.partial(_out_kernel, main_transposed),
        grid=(n // OUT_TILE,),
        in_specs=[main, row(MEM_WIDTH), _layer_weight(w, layer), full(g), row(D_MODEL)],
        out_specs=row(D_MODEL),
        out_shape=jax.ShapeDtypeStruct((n, D_MODEL), F32),
        compiler_params=_params("parallel"),
        name=name,
    )(o_main, o_mem, w, g, h)


def _shared_kv_kernel(x_ref, g_ref, w_ref, cos_ref, sin_ref, k_ref, v_ref):
    xn = _rms_norm(x_ref[...], g_ref[...]).astype(BF16)
    kv = _dot(xn, w_ref[...])
    cos = cos_ref[...]
    sin = sin_ref[...]
    first = _first_half_mask(IN_TILE)
    heads_per_block = LANES // SWA_HEAD_DIM
    for cb in range(SWA_KV_WIDTH // LANES):
        kr = _rope_block(kv[:, cb * LANES:(cb + 1) * LANES], cos, sin, first).astype(BF16)
        for j in range(heads_per_block):
            k_ref[cb * heads_per_block + j] = kr[:, j * SWA_HEAD_DIM:(j + 1) * SWA_HEAD_DIM]
    v_ref[...] = kv[:, SWA_KV_WIDTH:].T.astype(BF16)


def _shared_kv(h, g, w, cos, sin):
    n = h.shape[0]
    row = lambda width: pl.BlockSpec((IN_TILE, width), lambda i: (i, 0))
    full = lambda a: pl.BlockSpec(a.shape, lambda i: (0,) * a.ndim)
    return pl.pallas_call(
        _shared_kv_kernel,
        grid=(n // IN_TILE,),
        in_specs=[row(D_MODEL), full(g), full(w), row(LANES), row(LANES)],
        out_specs=[pl.BlockSpec((SWA_KV_HEADS, IN_TILE, SWA_HEAD_DIM), lambda i: (0, i, 0)),
                   pl.BlockSpec((SWA_KV_WIDTH, IN_TILE), lambda i: (0, i))],
        out_shape=[jax.ShapeDtypeStruct((SWA_KV_HEADS, n, SWA_HEAD_DIM), BF16),
                   jax.ShapeDtypeStruct((SWA_KV_WIDTH, n), BF16)],
        compiler_params=_params("parallel"),
        name="shared_kv",
    )(h, g, w, cos, sin)


def _b_in_kernel(prev_transposed, h_ref, *refs):
    prev, (g_ref, w_ref, cos_ref, sin_ref, mk_ref, mv_ref), outs = _split_refs(prev_transposed, refs, 6)
    qt_ref, gmt_ref, om_ref = outs
    first = _first_half_mask(IN_SUB_TILE)
    scale = SWA_HEAD_DIM ** -0.5 * LOG2E
    col_mq = 2 * SWA_WIDTH
    col_mg = col_mq + MEM_WIDTH

    for sub, xn in enumerate(_normed_sub_tiles(prev, h_ref, g_ref)):
        rows = slice(sub * IN_SUB_TILE, (sub + 1) * IN_SUB_TILE)

        def proj(lo, width):
            return _dot(xn, w_ref[:, lo:lo + width])

        mq = proj(col_mq, MEM_WIDTH)
        mem_gate = _silu(proj(col_mg, MEM_WIDTH))

        q = proj(0, SWA_WIDTH)
        probs, inv_sums = _memory_probs(mq, mk_ref)
        cos = cos_ref[rows, :]
        sin = sin_ref[rows, :]
        for cb in range(SWA_WIDTH // LANES):
            cs = slice(cb * LANES, (cb + 1) * LANES)
            qt_ref[cs, rows] = (_rope_block(q[:, cs], cos, sin, first) * scale).T.astype(BF16)

        gate = _silu(proj(SWA_WIDTH, SWA_WIDTH))
        _memory_readout(probs, inv_sums, mem_gate, mv_ref, om_ref, rows)
        for cb in range(SWA_WIDTH // LANES):
            cs = slice(cb * LANES, (cb + 1) * LANES)
            gmt_ref[cs, rows] = gate[:, cs].T.astype(BF16)


def _b_in_proj(layer, h, pending, g, w, cos, sin, mem_k, mem_v, tiles_per_seq):
    n = h.shape[0]
    row = lambda width: pl.BlockSpec((IN_TILE, width), lambda i: (i, 0))
    col = lambda width: pl.BlockSpec((width, IN_TILE), lambda i: (0, i))
    full = lambda a: pl.BlockSpec(a.shape, lambda i: (0,) * a.ndim)
    mem = pl.BlockSpec((None, None, MEM_TOKENS, MEM_WIDTH), lambda i: (layer, i // tiles_per_seq, 0, 0))
    out_specs = [col(SWA_WIDTH), col(SWA_WIDTH), row(MEM_WIDTH)]
    out_shape = [jax.ShapeDtypeStruct((SWA_WIDTH, n), BF16), jax.ShapeDtypeStruct((SWA_WIDTH, n), BF16),
                 jax.ShapeDtypeStruct((n, MEM_WIDTH), BF16)]
    if pending is not None:
        out_specs, out_shape = [row(D_MODEL)] + out_specs, [jax.ShapeDtypeStruct((n, D_MODEL), F32)] + out_shape
    pend_args, pend_specs = _pending_specs(pending, row, col, full)
    return pl.pallas_call(
        functools.partial(_b_in_kernel, None if pending is None else pending[-1]),
        grid=(n // IN_TILE,),
        in_specs=[row(D_MODEL)] + pend_specs
        + [full(g), _layer_weight(w, layer - N_A_LAYERS), row(LANES), row(LANES), mem, mem],
        out_specs=out_specs,
        out_shape=out_shape,
        compiler_params=_params("parallel"),
        name=f"b_in_proj_{layer}",
    )(h, *pend_args, g, w, cos, sin, mem_k, mem_v)


def _swa_kernel(sink_ref, qt_ref, kc_ref, kp_ref, vtc_ref, vtp_ref, gmt_ref, ot_ref):
    blk = WINDOW
    dh = SWA_HEAD_DIM
    ki = lax.broadcasted_iota(jnp.int32, (blk, blk), 0)
    qi = lax.broadcasted_iota(jnp.int32, (blk, blk), 1)
    from_prev = ki > qi
    from_prev_bf = from_prev.astype(BF16)
    prev_bias = jnp.where(pl.program_id(1) > 0, 0.0, MASK_VALUE)

    def scores(sb, g):
        cols = slice(sb * blk, (sb + 1) * blk)
        k2 = (jnp.concatenate([kp_ref[g], kc_ref[g, cols, :]], axis=0) if sb == 0
              else kc_ref[g, (sb - 1) * blk:(sb + 1) * blk, :])
        heads = range(g * SWA_GROUP, (g + 1) * SWA_GROUP)
        qt = jnp.concatenate([qt_ref[h * dh:(h + 1) * dh, cols] for h in heads], axis=1)
        return _dot(k2, qt)

    def attend(sb, g, st):
        cols = slice(sb * blk, (sb + 1) * blk)
        heads = range(g * SWA_GROUP, (g + 1) * SWA_GROUP)
        ps, inv_den = [], []
        for j in range(SWA_GROUP):
            s_prev = st[:blk, j * blk:(j + 1) * blk]
            if sb == 0:
                s_prev = s_prev + prev_bias
            s = jnp.where(from_prev, s_prev, st[blk:, j * blk:(j + 1) * blk])
            sink = sink_ref[heads[j]] * LOG2E
            m = jnp.maximum(jnp.max(s, axis=0, keepdims=True), sink)
            p = jnp.exp2(s - m)
            inv_den.append(1.0 / (jnp.sum(p, axis=0, keepdims=True) + jnp.exp2(sink - m)))
            p = p.astype(BF16)
            p_prev = p * from_prev_bf
            ps.append(jnp.concatenate([p_prev, p - p_prev], axis=0))
        vrows = slice(g * dh, (g + 1) * dh)
        vt2 = (jnp.concatenate([vtp_ref[vrows, :], vtc_ref[vrows, cols]], axis=1) if sb == 0
               else vtc_ref[vrows, (sb - 1) * blk:(sb + 1) * blk])
        ot = _dot(vt2, jnp.concatenate(ps, axis=1))
        for j, h in enumerate(heads):
            rows = slice(h * dh, (h + 1) * dh)
            o = ot[:, j * blk:(j + 1) * blk] * inv_den[j]
            ot_ref[rows, cols] = (o * gmt_ref[rows, cols].astype(F32)).astype(BF16)

    kv_heads = range(SWA_KV_HEADS)
    n_blocks = SWA_TILE // blk
    ahead = [scores(0, g) for g in kv_heads]
    for sb in range(n_blocks):
        current, ahead = ahead, ([scores(sb + 1, g) for g in kv_heads] if sb + 1 < n_blocks else None)
        for g in kv_heads:
            attend(sb, g, current[g])


def _swa(sinks, qt, k, vt, gmt, batch):
    n = qt.shape[1]
    tiles = n // batch // SWA_TILE
    sub = SWA_TILE // WINDOW
    tile = lambda b, t: b * tiles + t
    prev = lambda b, t: jnp.maximum(tile(b, t) * sub - 1, 0)
    wide = pl.BlockSpec((SWA_WIDTH, SWA_TILE), lambda b, t: (0, tile(b, t)))
    return pl.pallas_call(
        _swa_kernel,
        grid=(batch, tiles),
        in_specs=[pl.BlockSpec(memory_space=pltpu.SMEM),
                  wide,
                  pl.BlockSpec((SWA_KV_HEADS, SWA_TILE, SWA_HEAD_DIM), lambda b, t: (0, tile(b, t), 0)),
                  pl.BlockSpec((SWA_KV_HEADS, WINDOW, SWA_HEAD_DIM), lambda b, t: (0, prev(b, t), 0)),
                  pl.BlockSpec((SWA_KV_WIDTH, SWA_TILE), lambda b, t: (0, tile(b, t))),
                  pl.BlockSpec((SWA_KV_WIDTH, WINDOW), lambda b, t: (0, prev(b, t))),
                  wide],
        out_specs=wide,
        out_shape=jax.ShapeDtypeStruct((SWA_WIDTH, n), BF16),
        compiler_params=_params("parallel", "arbitrary"),
        name="swa",
    )(sinks, qt, k, k, vt, vt, gmt)


def kernel(x, mem, positions, pre_norm_g, post_norm_g, mem_norm_g, w_mem_kv, a_w_in, a_lb_logits, a_out_norm_g, a_w_out, kv_norm_g, w_kv_shared, b_w_in, b_sinks, b_w_out):
    batch, seq, _ = x.shape
    n = batch * seq
    assert seq % max(IN_TILE, OUT_TILE, HG_TILE, SWA_TILE) == 0 and n % ROPE_TILE == 0
    tiles_per_seq = seq // IN_TILE

    h = x.reshape(n, D_MODEL)
    mem_k, mem_v = _mem_kv(mem, mem_norm_g, w_mem_kv)
    cos, sin = _rope_tables(positions)
    a_w_in = a_w_in.astype(BF16)
    a_w_out = a_w_out.astype(BF16)
    b_w_in = b_w_in.astype(BF16)
    b_w_out = b_w_out.astype(BF16)
    k_sh = v_sh = None
    pending = None

    for layer in range(DEPTH):
        pre_g = pre_norm_g[layer].reshape(1, D_MODEL)
        post_g = post_norm_g[layer].reshape(1, D_MODEL)
        if layer < N_A_LAYERS:
            outs = _a_in_proj(layer, h, pending, pre_g, a_w_in, a_lb_logits, mem_k, mem_v, tiles_per_seq)
            if pending is not None:
                h, *outs = outs
            qs, lf, kk, v, gm, o_mem = outs
            o_main = _hgrn(qs, lf, kk, v, gm, a_out_norm_g[layer].reshape(1, HG_VAL_WIDTH), batch)
            pending = (o_main, o_mem, a_w_out, layer, post_g, False)
        else:
            j = layer - N_A_LAYERS
            outs = _b_in_proj(layer, h, pending, pre_g, b_w_in, cos, sin, mem_k, mem_v, tiles_per_seq)
            if pending is not None:
                h, *outs = outs
            qt, gmt, o_mem = outs
            if k_sh is None:
                k_sh, v_sh = _shared_kv(h, kv_norm_g.reshape(1, D_MODEL), w_kv_shared.astype(BF16), cos, sin)
            o_main_t = _swa(b_sinks[j], qt, k_sh, v_sh, gmt, batch)
            pending = (o_main_t, o_mem, b_w_out, j, post_g, True)
    o_main, o_mem, w_out, w_layer, post_g, transposed = pending
    h = _out_proj(o_main, o_mem, w_out, w_layer, post_g, h, "out_proj_last", main_transposed=transposed)
    return h.reshape(batch, seq, D_MODEL)
```

```python
import functools

import jax
import jax.numpy as jnp
from jax import lax
from jax.experimental import pallas as pl
from jax.experimental.pallas import tpu as pltpu

D_MODEL = 1024
DEPTH = 4
N_A_LAYERS = DEPTH // 2

HG_HEADS = 8
HG_DK = 128
HG_DV = D_MODEL // HG_HEADS
HG_KEY_WIDTH = HG_HEADS * HG_DK
HG_VAL_WIDTH = HG_HEADS * HG_DV
HG_CHUNK = 64

SWA_Q_HEADS = 16
SWA_KV_HEADS = 4
SWA_GROUP = SWA_Q_HEADS // SWA_KV_HEADS
SWA_HEAD_DIM = 64
SWA_WIDTH = SWA_Q_HEADS * SWA_HEAD_DIM
SWA_KV_WIDTH = SWA_KV_HEADS * SWA_HEAD_DIM
WINDOW = 128

MEM_TOKENS = 256
MEM_HEADS = 4
MEM_HEAD_DIM = 128
MEM_WIDTH = MEM_HEADS * MEM_HEAD_DIM

ROPE_THETA = 10000.0
NORM_EPS = 1e-6

LANES = 128
SUBLANES = 8
MASK_VALUE = -1e30
LOG2E = 1.4426950408889634

IN_TILE = 512
IN_SUB_TILE = 256
OUT_TILE = 1024
HG_TILE = 512
SWA_TILE = 1024
ROPE_TILE = 2048
VMEM_LIMIT = 56 * 1024 * 1024

F32 = jnp.float32
BF16 = jnp.bfloat16

_NT = (((1,), (1,)), ((), ()))
_TN = (((0,), (0,)), ((), ()))


def _dot(a, b):
    return jnp.dot(a, b, preferred_element_type=F32)


def _dot_nt(a, b):
    return lax.dot_general(a, b, _NT, preferred_element_type=F32)


def _dot_tn(a, b):
    return lax.dot_general(a, b, _TN, preferred_element_type=F32)


def _rms_norm(x, g):
    ms = jnp.mean(x * x, axis=-1, keepdims=True)
    return x * lax.rsqrt(ms + NORM_EPS) * g


def _silu(x):
    h = 0.5 * x
    return h + h * jnp.tanh(h)


def _params(*semantics):
    return pltpu.CompilerParams(dimension_semantics=semantics, vmem_limit_bytes=VMEM_LIMIT)


def _layer_weight(w, layer):
    return pl.BlockSpec((None,) + w.shape[1:], lambda *_: (layer, 0, 0), pipeline_mode=pl.Buffered(1))


def _layer_input(prev, h_ref, rows):
    if prev is None:
        return h_ref[rows, :]
    om_ref, ome_ref, wout_ref, pg_ref, hnew_ref, main_transposed = prev
    if main_transposed:
        y = _dot_tn(om_ref[:, rows], wout_ref[:HG_VAL_WIDTH, :])
    else:
        y = _dot(om_ref[rows, :], wout_ref[:HG_VAL_WIDTH, :])
    y = y + _dot(ome_ref[rows, :], wout_ref[HG_VAL_WIDTH:, :])
    h_new = h_ref[rows, :] + _rms_norm(y, pg_ref[...])
    hnew_ref[rows, :] = h_new
    return h_new


def _normed_sub_tiles(prev, h_ref, g_ref):
    return [_rms_norm(_layer_input(prev, h_ref, slice(s * IN_SUB_TILE, (s + 1) * IN_SUB_TILE)),
                      g_ref[...]).astype(BF16) for s in range(IN_TILE // IN_SUB_TILE)]


def _split_refs(prev_transposed, refs, n_in):
    if prev_transposed is None:
        return None, refs[:n_in], refs[n_in:]
    prev_in, ins, (hnew_ref, *outs) = refs[:4], refs[4:4 + n_in], refs[4 + n_in:]
    return (*prev_in, hnew_ref, prev_transposed), ins, outs


def _pending_specs(pending, row, col, full):
    if pending is None:
        return [], []
    o_main, o_mem, w_out, w_layer, post_g, transposed = pending
    main = col(HG_VAL_WIDTH) if transposed else row(HG_VAL_WIDTH)
    return [o_main, o_mem, w_out, post_g], [main, row(MEM_WIDTH), _layer_weight(w_out, w_layer), full(post_g)]


ROPE_FREQS = SWA_HEAD_DIM // 2
ROPE_PACK = LANES // ROPE_FREQS


def _rope_table_kernel(pos_ref, invf_ref, cos_ref, sin_ref):
    ang = pos_ref[...].astype(F32) * invf_ref[...]
    rows = ang.shape[0]
    src = lax.broadcasted_iota(jnp.int32, (LANES, LANES), 0)
    dst = lax.broadcasted_iota(jnp.int32, (LANES, LANES), 1)
    sin_sign = jnp.where((dst & ROPE_FREQS) == 0, -1.0, 1.0)
    for table, out_ref, sign in ((jnp.cos(ang), cos_ref, 1.0), (jnp.sin(ang), sin_ref, sin_sign)):
        hi = table.astype(BF16)
        rest = table - hi.astype(F32)
        mid = rest.astype(BF16)
        lo = (rest - mid.astype(F32)).astype(BF16)
        for i in range(ROPE_PACK):
            pick = src == i * ROPE_FREQS + (dst & (ROPE_FREQS - 1))
            spread = jnp.where(pick, sign, 0.0).astype(BF16)
            out_ref[pl.ds(i, rows, stride=ROPE_PACK), :] = _dot(hi, spread) + _dot(mid, spread) + _dot(lo, spread)


def _rope_tables(positions):
    n = positions.size
    inv_freq = ROPE_THETA ** (-jnp.arange(0, SWA_HEAD_DIM, 2, dtype=F32) / SWA_HEAD_DIM)
    invf = jnp.tile(inv_freq, ROPE_PACK).reshape(1, LANES)
    pos = jnp.repeat(positions.reshape(n // ROPE_PACK, ROPE_PACK), ROPE_FREQS, axis=1)
    row = pl.BlockSpec((ROPE_TILE, LANES), lambda i: (i, 0))
    return pl.pallas_call(
        _rope_table_kernel,
        grid=(n // ROPE_TILE,),
        in_specs=[pl.BlockSpec((ROPE_TILE // ROPE_PACK, LANES), lambda i: (i, 0)),
                  pl.BlockSpec((1, LANES), lambda i: (0, 0))],
        out_specs=[row, row],
        out_shape=[jax.ShapeDtypeStruct((n, LANES), F32)] * 2,
        compiler_params=_params("parallel"),
        name="rope_tables",
    )(pos, invf)


def _rope_block(xc, cos, sin, first_half):
    swapped = jnp.where(first_half,
                        pltpu.roll(xc, LANES - SWA_HEAD_DIM // 2, 1),
                        pltpu.roll(xc, SWA_HEAD_DIM // 2, 1))
    return xc * cos + swapped * sin


def _first_half_mask(rows):
    lane = lax.broadcasted_iota(jnp.int32, (rows, LANES), 1)
    return (lane & (SWA_HEAD_DIM // 2)) == 0


def _mem_kv_kernel(mem_ref, g_ref, w_ref, k_ref, v_ref):
    x = mem_ref[...]
    xhat = x * lax.rsqrt(jnp.mean(x * x, axis=-1, keepdims=True) + NORM_EPS)
    for layer in range(DEPTH):
        kv = _dot((xhat * g_ref[layer]).astype(BF16), w_ref[layer].astype(BF16))
        k_ref[layer] = kv[:, :MEM_WIDTH].astype(BF16)
        v_ref[layer] = kv[:, MEM_WIDTH:].astype(BF16)


def _mem_kv(mem, mem_norm_g, w_mem_kv):
    b = mem.shape[0]
    out = pl.BlockSpec((DEPTH, None, MEM_TOKENS, MEM_WIDTH), lambda i: (0, i, 0, 0))
    return pl.pallas_call(
        _mem_kv_kernel,
        grid=(b,),
        in_specs=[
            pl.BlockSpec((None, MEM_TOKENS, D_MODEL), lambda i: (i, 0, 0)),
            pl.BlockSpec((DEPTH, 1, D_MODEL), lambda i: (0, 0, 0)),
            pl.BlockSpec(w_mem_kv.shape, lambda i: (0, 0, 0), pipeline_mode=pl.Buffered(1)),
        ],
        out_specs=[out, out],
        out_shape=[jax.ShapeDtypeStruct((DEPTH, b, MEM_TOKENS, MEM_WIDTH), BF16)] * 2,
        compiler_params=_params("parallel"),
        name="mem_kv",
    )(mem, mem_norm_g.reshape(DEPTH, 1, D_MODEL), w_mem_kv)


def _memory_probs(mq, mk_ref):
    mq = (mq * (MEM_HEAD_DIM ** -0.5 * LOG2E)).astype(BF16)
    probs, inv_sums = [], []
    for hh in range(MEM_HEADS):
        hs = slice(hh * MEM_HEAD_DIM, (hh + 1) * MEM_HEAD_DIM)
        s = _dot_nt(mq[:, hs], mk_ref[:, hs])
        p = jnp.exp2(s - jnp.max(s, axis=-1, keepdims=True))
        probs.append(p.astype(BF16))
        inv_sums.append(1.0 / jnp.sum(p, axis=-1, keepdims=True))
    return probs, inv_sums


def _memory_readout(probs, inv_sums, gate, mv_ref, out_ref, rows):
    for hh in range(MEM_HEADS):
        hs = slice(hh * MEM_HEAD_DIM, (hh + 1) * MEM_HEAD_DIM)
        o = _dot(probs[hh], mv_ref[:, hs]) * inv_sums[hh]
        out_ref[rows, hs] = (o * gate[:, hs]).astype(BF16)


def _a_in_kernel(layer, prev_transposed, h_ref, *refs):
    prev, (g_ref, w_ref, lbl_ref, mk_ref, mv_ref), outs = _split_refs(prev_transposed, refs, 5)
    qs_ref, lf_ref, kk_ref, v_ref, gm_ref, om_ref = outs

    logits = lbl_ref[...]
    e = jnp.exp(logits - jnp.max(logits, axis=0, keepdims=True))
    lb = jnp.sum(e[:layer + 1], axis=0, keepdims=True) / jnp.sum(e, axis=0, keepdims=True)
    mid = 0.5 * (1.0 + lb)
    half = 0.5 * (1.0 - lb)

    col_q, col_f, col_v = 0, HG_KEY_WIDTH, 2 * HG_KEY_WIDTH
    col_g = col_v + HG_VAL_WIDTH
    col_mq = col_g + HG_VAL_WIDTH
    col_mg = col_mq + MEM_WIDTH

    for sub, xn in enumerate(_normed_sub_tiles(prev, h_ref, g_ref)):
        rows = slice(sub * IN_SUB_TILE, (sub + 1) * IN_SUB_TILE)

        def proj(lo, width):
            return _dot(xn, w_ref[:, lo:lo + width])

        mq = proj(col_mq, MEM_WIDTH)
        mem_gate = _silu(proj(col_mg, MEM_WIDTH))
        qs_ref[rows, :] = _silu(proj(col_q, HG_KEY_WIDTH)).astype(BF16)
        probs, inv_sums = _memory_probs(mq, mk_ref)

        th = half * jnp.tanh(0.5 * proj(col_f, HG_KEY_WIDTH))
        lf_ref[rows, :] = jnp.log2(jnp.maximum(mid + th, lb))
        kk_ref[rows, :] = (half - th).astype(BF16)

        _memory_readout(probs, inv_sums, mem_gate, mv_ref, om_ref, rows)
        v_ref[rows, :] = proj(col_v, HG_VAL_WIDTH).astype(BF16)
        gm_ref[rows, :] = _silu(proj(col_g, HG_VAL_WIDTH)).astype(BF16)


def _a_in_proj(layer, h, pending, g, w, lb_logits, mem_k, mem_v, tiles_per_seq):
    n = h.shape[0]
    row = lambda width: pl.BlockSpec((IN_TILE, width), lambda i: (i, 0))
    col = lambda width: pl.BlockSpec((width, IN_TILE), lambda i: (0, i))
    full = lambda a: pl.BlockSpec(a.shape, lambda i: (0,) * a.ndim)
    mem = pl.BlockSpec((None, None, MEM_TOKENS, MEM_WIDTH), lambda i: (layer, i // tiles_per_seq, 0, 0))
    widths = (HG_KEY_WIDTH, HG_KEY_WIDTH, HG_KEY_WIDTH, HG_VAL_WIDTH, HG_VAL_WIDTH, MEM_WIDTH)
    dtypes = (BF16, F32, BF16, BF16, BF16, BF16)
    if pending is not None:
        widths, dtypes = (D_MODEL,) + widths, (F32,) + dtypes
    pend_args, pend_specs = _pending_specs(pending, row, col, full)
    return pl.pallas_call(
        functools.partial(_a_in_kernel, layer, None if pending is None else pending[-1]),
        grid=(n // IN_TILE,),
        in_specs=[row(D_MODEL)] + pend_specs + [full(g), _layer_weight(w, layer), full(lb_logits), mem, mem],
        out_specs=[row(wd) for wd in widths],
        out_shape=[jax.ShapeDtypeStruct((n, wd), dt) for wd, dt in zip(widths, dtypes)],
        compiler_params=_params("parallel"),
        name=f"a_in_proj_{layer}",
    )(h, *pend_args, g, w, lb_logits, mem_k, mem_v)


_LEVELS = (32, 16, 8, 4, 2, 1)
_SAFE_BLOCKS = (HG_CHUNK, 16)
SAFE_BLOCK_DECAY = 100.0


def _coarse_level(cum, qs, kk, m):
    c, w = cum.shape
    early = m % (2 * SUBLANES) == 0
    dt = BF16 if early else F32
    zeros = jnp.zeros((m, w), dt)
    qd, kd = [], []
    for blk in range(c // (2 * m)):
        lo, mid, hi = blk * 2 * m, blk * 2 * m + m, (blk + 1) * 2 * m
        ref = cum[mid - 1:mid, :]
        kd += [(kk[lo:mid] * jnp.exp2(ref - cum[lo:mid])).astype(dt), zeros]
        qd += [zeros, (qs[mid:hi] * jnp.exp2(cum[mid:hi] - ref)).astype(dt)]
    return jnp.concatenate(qd, axis=0).astype(BF16), jnp.concatenate(kd, axis=0).astype(BF16)


def _fine_level(cum3, qs3, kk3, m, upper_pen, lower_pen):
    groups, _, w = cum3.shape
    parts = [jnp.broadcast_to(cum3[:, blk * 2 * m + m - 1:blk * 2 * m + m, :], (groups, 2 * m, w))
             for blk in range(SUBLANES // (2 * m))]
    x = cum3 - (parts[0] if len(parts) == 1 else jnp.concatenate(parts, axis=1))
    qd = qs3 * jnp.exp2(x + upper_pen)
    kd = kk3 * jnp.exp2(lower_pen - x)
    c = groups * SUBLANES
    return qd.reshape(c, w).astype(BF16), kd.reshape(c, w).astype(BF16)


def _block_level(cum, qs, kk, block):
    c, _ = cum.shape
    local = [cum[:block]] + [cum[b:b + block] - cum[b - 1:b] for b in range(block, c, block)]
    local = local[0] if len(local) == 1 else jnp.concatenate(local, axis=0)
    return (qs * jnp.exp2(local)).astype(BF16), (kk * jnp.exp2(-local)).astype(BF16)


def _hgrn_constants():
    c = HG_CHUNK
    row = lax.broadcasted_iota(jnp.int32, (c, c), 0)
    col = lax.broadcasted_iota(jnp.int32, (c, c), 1)
    sub = lax.broadcasted_iota(jnp.int32, (1, SUBLANES, HG_DK), 1)
    fine = [m for m in _LEVELS if m < SUBLANES]
    same_block = {m: (row // (2 * m)) == (col // (2 * m)) for m in _LEVELS if 2 * m < c}
    return dict(
        same_block={m: mask.astype(F32) for m, mask in same_block.items()},
        diag=(row == col).astype(F32),
        causal_block={b: ((col <= row) if b == c else same_block[b // 2] & (col <= row)).astype(F32)
                      for b in _SAFE_BLOCKS},
        upper_pen={m: jnp.where((sub & m) != 0, 0.0, MASK_VALUE) for m in fine},
        lower_pen={m: jnp.where((sub & m) != 0, MASK_VALUE, 0.0) for m in fine},
        even_row=jnp.where((sub & 1) != 0, 0.0, 1.0),
    )


def _hgrn_tile(block, k, qs_ref, lf_ref, kk_ref, v_ref, gm_ref, gn, cum_s, o_ref, st_ref):
    c = HG_CHUNK
    heads = [slice(h * HG_DK, (h + 1) * HG_DK) for h in range(HG_HEADS)]
    shape3 = (c // SUBLANES, SUBLANES, HG_DK)
    n_chunks = HG_TILE // c

    def scores_pass(ci):
        rows = slice(ci * c, (ci + 1) * c)
        scores_bf, q_ins, k_outs, lasts = [], [], [], []
        for hs in heads:
            cum = cum_s[rows, hs]
            qs = qs_ref[rows, hs].astype(F32)
            kk = kk_ref[rows, hs].astype(F32)

            pairs = []
            for m in _LEVELS:
                if block is not None and 2 * m <= block:
                    continue
                if m >= SUBLANES:
                    qd, kd = _coarse_level(cum, qs, kk, m)
                elif m == 1:
                    qs3, kk3 = qs.reshape(shape3), kk.reshape(shape3)
                    qd3 = qs3 * jnp.exp2(lf_ref[rows, hs].reshape(shape3) + k["upper_pen"][m])
                    qd = qd3.reshape(c, HG_DK).astype(BF16)
                    kd = (kk3 * k["even_row"]).reshape(c, HG_DK).astype(BF16)
                else:
                    qd, kd = _fine_level(cum.reshape(shape3), qs.reshape(shape3), kk.reshape(shape3), m,
                                         k["upper_pen"][m], k["lower_pen"][m])
                pairs.append((qd, kd, k["same_block"].get(m)))
            last = cum[c - 1:c, :]
            q_in = (qs * jnp.exp2(cum)).astype(BF16)
            if block == c:
                k_side = kk * jnp.exp2(-cum)
                pairs.append((q_in, k_side.astype(BF16), k["causal_block"][block]))
                k_out = (k_side * jnp.exp2(last)).astype(BF16)
            else:
                if block is not None:
                    pairs.append(_block_level(cum, qs, kk, block) + (k["causal_block"][block],))
                else:
                    pairs.append((qs.astype(BF16), kk.astype(BF16), k["diag"]))
                k_out = (kk * jnp.exp2(last - cum)).astype(BF16)

            scores = None
            for qd, kd, mask in pairs:
                s = _dot_nt(qd, kd)
                if mask is not None:
                    s = s * mask
                scores = s if scores is None else scores + s

            scores_bf.append(scores.astype(BF16))
            q_ins.append(q_in)
            k_outs.append(k_out)
            lasts.append(last)
        return scores_bf, q_ins, k_outs, lasts

    def recurrence_pass(ci, scores_bf, q_ins, k_outs, lasts):
        rows = slice(ci * c, (ci + 1) * c)
        for h, hs in enumerate(heads):
            o = _dot_nt(q_ins[h], st_ref[h].astype(BF16)) + _dot(scores_bf[h], v_ref[rows, hs])
            o = o * lax.rsqrt(jnp.mean(o * o, axis=-1, keepdims=True) + NORM_EPS)
            o_ref[rows, hs] = (o * gn[:, hs] * gm_ref[rows, hs].astype(F32)).astype(BF16)

        for h, hs in enumerate(heads):
            st_ref[h] = st_ref[h] * jnp.exp2(lasts[h]) + _dot_tn(v_ref[rows, hs], k_outs[h])

    ahead = scores_pass(0)
    for ci in range(n_chunks):
        current, ahead = ahead, (scores_pass(ci + 1) if ci + 1 < n_chunks else None)
        recurrence_pass(ci, *current)


def _hgrn_kernel(qs_ref, lf_ref, kk_ref, v_ref, gm_ref, gn_ref, o_ref, cum_s, st_ref):
    c = HG_CHUNK

    @pl.when(pl.program_id(1) == 0)
    def _():
        st_ref[...] = jnp.zeros_like(st_ref)

    row = lax.broadcasted_iota(jnp.int32, (c, c), 0)
    col = lax.broadcasted_iota(jnp.int32, (c, c), 1)
    tri = (col <= row).astype(BF16)
    tri2 = jnp.concatenate([tri, tri], axis=1)

    worst = {b: None for b in _SAFE_BLOCKS}
    for ci in range(HG_TILE // c):
        rows = slice(ci * c, (ci + 1) * c)
        lf = lf_ref[rows, :]
        hi = lf.astype(BF16)
        lo = (lf - hi.astype(F32)).astype(BF16)
        cum = _dot(tri2, jnp.concatenate([hi, lo], axis=0))
        cum_s[rows, :] = cum
        for block in _SAFE_BLOCKS:
            ends = [cum[b - 1:b, :] for b in range(block, c + 1, block)]
            for b, end in enumerate(ends):
                decay = end if b == 0 else end - ends[b - 1]
                worst[block] = decay if worst[block] is None else jnp.minimum(worst[block], decay)
    safe = [jnp.min(worst[b]) >= -SAFE_BLOCK_DECAY for b in _SAFE_BLOCKS]

    consts = _hgrn_constants()
    args = (consts, qs_ref, lf_ref, kk_ref, v_ref, gm_ref, gn_ref[...], cum_s, o_ref, st_ref)

    not_yet = None
    for block, ok in zip(_SAFE_BLOCKS + (None,), safe + [True]):
        take = ok if not_yet is None else jnp.logical_and(not_yet, ok)

        @pl.when(take)
        def _(block=block):
            _hgrn_tile(block, *args)

        if block is not None:
            not_yet = jnp.logical_not(ok) if not_yet is None else jnp.logical_and(not_yet, jnp.logical_not(ok))


def _hgrn(qs, lf, kk, v, gm, gn, batch):
    n = qs.shape[0]
    tiles = n // batch // HG_TILE
    row = pl.BlockSpec((HG_TILE, HG_KEY_WIDTH), lambda b, t: (b * tiles + t, 0))
    return pl.pallas_call(
        _hgrn_kernel,
        grid=(batch, tiles),
        in_specs=[row, row, row, row, row, pl.BlockSpec((1, HG_VAL_WIDTH), lambda b, t: (0, 0))],
        out_specs=row,
        out_shape=jax.ShapeDtypeStruct((n, HG_VAL_WIDTH), BF16),
        scratch_shapes=[pltpu.VMEM((HG_TILE, HG_KEY_WIDTH), F32),
                        pltpu.VMEM((HG_HEADS, HG_DV, HG_DK), F32)],
        compiler_params=_params("arbitrary", "arbitrary"),
        name="hgrn2",
    )(qs, lf, kk, v, gm, gn)


def _out_kernel(main_transposed, om_ref, ome_ref, w_ref, g_ref, h_ref, o_ref):
    main_dot = _dot_tn if main_transposed else _dot
    y = main_dot(om_ref[...], w_ref[:HG_VAL_WIDTH, :]) + _dot(ome_ref[...], w_ref[HG_VAL_WIDTH:, :])
    o_ref[...] = h_ref[...] + _rms_norm(y, g_ref[...])


def _out_proj(o_main, o_mem, w, layer, g, h, name, main_transposed=False):
    n = h.shape[0]
    deep = pl.Buffered(3)
    row = lambda width, mode=None: pl.BlockSpec((OUT_TILE, width), lambda i: (i, 0), pipeline_mode=mode)
    col = lambda width: pl.BlockSpec((width, OUT_TILE), lambda i: (0, i), pipeline_mode=deep)
    main = col(HG_VAL_WIDTH) if main_transposed else row(HG_VAL_WIDTH, deep)

    def outer(om_hbm, ome_hbm, w_ref, g_ref, h_hbm, o_hbm):
        def body(om_ref, ome_ref, h_ref, o_ref):
            _out_kernel(main_transposed, om_ref, ome_ref, w_ref, g_ref, h_ref, o_ref)

        pltpu.emit_pipeline(
            body,
            grid=(n // OUT_TILE,),
            in_specs=[main, row(MEM_WIDTH, deep), row(D_MODEL, deep)],
            out_specs=[row(D_MODEL)],
        )(om_hbm, ome_hbm, h_hbm, o_hbm)

    any_space = pl.BlockSpec(memory_space=pl.ANY)
    vmem = pl.BlockSpec(memory_space=pltpu.VMEM)
    return pl.pallas_call(
        outer,
        in_specs=[any_space, any_space, vmem, vmem, any_space],
        out_specs=any_space,
        out_shape=jax.ShapeDtypeStruct((n, D_MODEL), F32),
        compiler_params=pltpu.CompilerParams(vmem_limit_bytes=VMEM_LIMIT),
        name=name,
    )(o_main, o_mem, w[layer], g, h)


def _shared_kv_kernel(x_ref, g_ref, w_ref, cos_ref, sin_ref, k_ref, v_ref):
    xn = _rms_norm(x_ref[...], g_ref[...]).astype(BF16)
    kv = _dot(xn, w_ref[...])
    cos = cos_ref[...]
    sin = sin_ref[...]
    first = _first_half_mask(IN_TILE)
    heads_per_block = LANES // SWA_HEAD_DIM
    for cb in range(SWA_KV_WIDTH // LANES):
        kr = _rope_block(kv[:, cb * LANES:(cb + 1) * LANES], cos, sin, first).astype(BF16)
        for j in range(heads_per_block):
            k_ref[cb * heads_per_block + j] = kr[:, j * SWA_HEAD_DIM:(j + 1) * SWA_HEAD_DIM]
    v_ref[...] = kv[:, SWA_KV_WIDTH:].T.astype(BF16)


def _shared_kv(h, g, w, cos, sin):
    n = h.shape[0]
    row = lambda width: pl.BlockSpec((IN_TILE, width), lambda i: (i, 0))
    full = lambda a: pl.BlockSpec(a.shape, lambda i: (0,) * a.ndim)
    return pl.pallas_call(
        _shared_kv_kernel,
        grid=(n // IN_TILE,),
        in_specs=[row(D_MODEL), full(g), full(w), row(LANES), row(LANES)],
        out_specs=[pl.BlockSpec((SWA_KV_HEADS, IN_TILE, SWA_HEAD_DIM), lambda i: (0, i, 0)),
                   pl.BlockSpec((SWA_KV_WIDTH, IN_TILE), lambda i: (0, i))],
        out_shape=[jax.ShapeDtypeStruct((SWA_KV_HEADS, n, SWA_HEAD_DIM), BF16),
                   jax.ShapeDtypeStruct((SWA_KV_WIDTH, n), BF16)],
        compiler_params=_params("parallel"),
        name="shared_kv",
    )(h, g, w, cos, sin)


def _b_in_kernel(prev_transposed, h_ref, *refs):
    prev, (g_ref, w_ref, cos_ref, sin_ref, mk_ref, mv_ref), outs = _split_refs(prev_transposed, refs, 6)
    qt_ref, gmt_ref, om_ref = outs
    first = _first_half_mask(IN_SUB_TILE)
    scale = SWA_HEAD_DIM ** -0.5 * LOG2E
    col_mq = 2 * SWA_WIDTH
    col_mg = col_mq + MEM_WIDTH

    for sub, xn in enumerate(_normed_sub_tiles(prev, h_ref, g_ref)):
        rows = slice(sub * IN_SUB_TILE, (sub + 1) * IN_SUB_TILE)

        def proj(lo, width):
            return _dot(xn, w_ref[:, lo:lo + width])

        mq = proj(col_mq, MEM_WIDTH)
        mem_gate = _silu(proj(col_mg, MEM_WIDTH))

        q = proj(0, SWA_WIDTH)
        probs, inv_sums = _memory_probs(mq, mk_ref)
        cos = cos_ref[rows, :]
        sin = sin_ref[rows, :]
        for cb in range(SWA_WIDTH // LANES):
            cs = slice(cb * LANES, (cb + 1) * LANES)
            qt_ref[cs, rows] = (_rope_block(q[:, cs], cos, sin, first) * scale).T.astype(BF16)

        gate = _silu(proj(SWA_WIDTH, SWA_WIDTH))
        _memory_readout(probs, inv_sums, mem_gate, mv_ref, om_ref, rows)
        for cb in range(SWA_WIDTH // LANES):
            cs = slice(cb * LANES, (cb + 1) * LANES)
            gmt_ref[cs, rows] = gate[:, cs].T.astype(BF16)


def _b_in_proj(layer, h, pending, g, w, cos, sin, mem_k, mem_v, tiles_per_seq):
    n = h.shape[0]
    row = lambda width: pl.BlockSpec((IN_TILE, width), lambda i: (i, 0))
    col = lambda width: pl.BlockSpec((width, IN_TILE), lambda i: (0, i))
    full = lambda a: pl.BlockSpec(a.shape, lambda i: (0,) * a.ndim)
    mem = pl.BlockSpec((None, None, MEM_TOKENS, MEM_WIDTH), lambda i: (layer, i // tiles_per_seq, 0, 0))
    out_specs = [col(SWA_WIDTH), col(SWA_WIDTH), row(MEM_WIDTH)]
    out_shape = [jax.ShapeDtypeStruct((SWA_WIDTH, n), BF16), jax.ShapeDtypeStruct((SWA_WIDTH, n), BF16),
                 jax.ShapeDtypeStruct((n, MEM_WIDTH), BF16)]
    if pending is not None:
        out_specs, out_shape = [row(D_MODEL)] + out_specs, [jax.ShapeDtypeStruct((n, D_MODEL), F32)] + out_shape
    pend_args, pend_specs = _pending_specs(pending, row, col, full)
    return pl.pallas_call(
        functools.partial(_b_in_kernel, None if pending is None else pending[-1]),
        grid=(n // IN_TILE,),
        in_specs=[row(D_MODEL)] + pend_specs
        + [full(g), _layer_weight(w, layer - N_A_LAYERS), row(LANES), row(LANES), mem, mem],
        out_specs=out_specs,
        out_shape=out_shape,
        compiler_params=_params("parallel"),
        name=f"b_in_proj_{layer}",
    )(h, *pend_args, g, w, cos, sin, mem_k, mem_v)


def _swa_kernel(sink_ref, qt_ref, kc_ref, kp_ref, vtc_ref, vtp_ref, gmt_ref, ot_ref):
    blk = WINDOW
    dh = SWA_HEAD_DIM
    ki = lax.broadcasted_iota(jnp.int32, (blk, blk), 0)
    qi = lax.broadcasted_iota(jnp.int32, (blk, blk), 1)
    from_prev = ki > qi
    from_prev_bf = from_prev.astype(BF16)
    prev_bias = jnp.where(pl.program_id(1) > 0, 0.0, MASK_VALUE)

    def scores(sb, g):
        cols = slice(sb * blk, (sb + 1) * blk)
        k2 = (jnp.concatenate([kp_ref[g], kc_ref[g, cols, :]], axis=0) if sb == 0
              else kc_ref[g, (sb - 1) * blk:(sb + 1) * blk, :])
        heads = range(g * SWA_GROUP, (g + 1) * SWA_GROUP)
        qt = jnp.concatenate([qt_ref[h * dh:(h + 1) * dh, cols] for h in heads], axis=1)
        return _dot(k2, qt)

    def attend(sb, g, st):
        cols = slice(sb * blk, (sb + 1) * blk)
        heads = range(g * SWA_GROUP, (g + 1) * SWA_GROUP)
        ps, inv_den = [], []
        for j in range(SWA_GROUP):
            s_prev = st[:blk, j * blk:(j + 1) * blk]
            if sb == 0:
                s_prev = s_prev + prev_bias
            s = jnp.where(from_prev, s_prev, st[blk:, j * blk:(j + 1) * blk])
            sink = sink_ref[heads[j]] * LOG2E
            m = jnp.maximum(jnp.max(s, axis=0, keepdims=True), sink)
            p = jnp.exp2(s - m)
            inv_den.append(1.0 / (jnp.sum(p, axis=0, keepdims=True) + jnp.exp2(sink - m)))
            p = p.astype(BF16)
            p_prev = p * from_prev_bf
            ps.append(jnp.concatenate([p_prev, p - p_prev], axis=0))
        vrows = slice(g * dh, (g + 1) * dh)
        vt2 = (jnp.concatenate([vtp_ref[vrows, :], vtc_ref[vrows, cols]], axis=1) if sb == 0
               else vtc_ref[vrows, (sb - 1) * blk:(sb + 1) * blk])
        ot = _dot(vt2, jnp.concatenate(ps, axis=1))
        for j, h in enumerate(heads):
            rows = slice(h * dh, (h + 1) * dh)
            o = ot[:, j * blk:(j + 1) * blk] * inv_den[j]
            ot_ref[rows, cols] = (o * gmt_ref[rows, cols].astype(F32)).astype(BF16)

    kv_heads = range(SWA_KV_HEADS)
    n_blocks = SWA_TILE // blk
    ahead = [scores(0, g) for g in kv_heads]
    for sb in range(n_blocks):
        current, ahead = ahead, ([scores(sb + 1, g) for g in kv_heads] if sb + 1 < n_blocks else None)
        for g in kv_heads:
            attend(sb, g, current[g])


def _swa(sinks, qt, k, vt, gmt, batch):
    n = qt.shape[1]
    tiles = n // batch // SWA_TILE
    sub = SWA_TILE // WINDOW
    tile = lambda b, t: b * tiles + t
    prev = lambda b, t: jnp.maximum(tile(b, t) * sub - 1, 0)
    wide = pl.BlockSpec((SWA_WIDTH, SWA_TILE), lambda b, t: (0, tile(b, t)))
    return pl.pallas_call(
        _swa_kernel,
        grid=(batch, tiles),
        in_specs=[pl.BlockSpec(memory_space=pltpu.SMEM),
                  wide,
                  pl.BlockSpec((SWA_KV_HEADS, SWA_TILE, SWA_HEAD_DIM), lambda b, t: (0, tile(b, t), 0)),
                  pl.BlockSpec((SWA_KV_HEADS, WINDOW, SWA_HEAD_DIM), lambda b, t: (0, prev(b, t), 0)),
                  pl.BlockSpec((SWA_KV_WIDTH, SWA_TILE), lambda b, t: (0, tile(b, t))),
                  pl.BlockSpec((SWA_KV_WIDTH, WINDOW), lambda b, t: (0, prev(b, t))),
                  wide],
        out_specs=wide,
        out_shape=jax.ShapeDtypeStruct((SWA_WIDTH, n), BF16),
        compiler_params=_params("parallel", "arbitrary"),
        name="swa",
    )(sinks, qt, k, k, vt, vt, gmt)


def kernel(x, mem, positions, pre_norm_g, post_norm_g, mem_norm_g, w_mem_kv, a_w_in, a_lb_logits, a_out_norm_g, a_w_out, kv_norm_g, w_kv_shared, b_w_in, b_sinks, b_w_out):
    batch, seq, _ = x.shape
    n = batch * seq
    assert seq % max(IN_TILE, OUT_TILE, HG_TILE, SWA_TILE) == 0 and n % ROPE_TILE == 0
    tiles_per_seq = seq // IN_TILE

    h = x.reshape(n, D_MODEL)
    mem_k, mem_v = _mem_kv(mem, mem_norm_g, w_mem_kv)
    cos, sin = _rope_tables(positions)
    a_w_in = a_w_in.astype(BF16)
    a_w_out = a_w_out.astype(BF16)
    b_w_in = b_w_in.astype(BF16)
    b_w_out = b_w_out.astype(BF16)
    k_sh = v_sh = None
    pending = None

    for layer in range(DEPTH):
        pre_g = pre_norm_g[layer].reshape(1, D_MODEL)
        post_g = post_norm_g[layer].reshape(1, D_MODEL)
        if layer < N_A_LAYERS:
            outs = _a_in_proj(layer, h, pending, pre_g, a_w_in, a_lb_logits, mem_k, mem_v, tiles_per_seq)
            if pending is not None:
                h, *outs = outs
            qs, lf, kk, v, gm, o_mem = outs
            o_main = _hgrn(qs, lf, kk, v, gm, a_out_norm_g[layer].reshape(1, HG_VAL_WIDTH), batch)
            pending = (o_main, o_mem, a_w_out, layer, post_g, False)
        else:
            j = layer - N_A_LAYERS
            outs = _b_in_proj(layer, h, pending, pre_g, b_w_in, cos, sin, mem_k, mem_v, tiles_per_seq)
            if pending is not None:
                h, *outs = outs
            qt, gmt, o_mem = outs
            if k_sh is None:
                k_sh, v_sh = _shared_kv(h, kv_norm_g.reshape(1, D_MODEL), w_kv_shared.astype(BF16), cos, sin)
            o_main_t = _swa(b_sinks[j], qt, k_sh, v_sh, gmt, batch)
            pending = (o_main_t, o_mem, b_w_out, j, post_g, True)
    o_main, o_mem, w_out, w_layer, post_g, transposed = pending
    h = _out_proj(o_main, o_mem, w_out, w_layer, post_g, h, "out_proj_last", main_transposed=transposed)
    return h.reshape(batch, seq, D_MODEL)
```
